```python
import math
import jax, jax.numpy as jnp
from jax import lax
import numpy as np

D_MODEL = 2048
BATCH = 4
SEQ = 4096
DEPTH = 1

PLE_DIM = 256
N_HEADS = 16
HEAD_DIM = 128
ATTN_WIDTH = N_HEADS * HEAD_DIM
MOBA_BLOCK = 256
MOBA_TOPK = 3
QUERY_CHUNK = 16
N_BUCKETS = 32
MAX_DISTANCE = 128
CONV_WIDTH = D_MODEL
CONV_KERNEL = 31
N_GROUPS = 4
EXPERTS_PER_GROUP = 4
N_EXPERTS = N_GROUPS * EXPERTS_PER_GROUP
EXPERT_TOPK = 2
EXPERT_FF = 512
IN_COLS = 3 * ATTN_WIDTH + 2 * CONV_WIDTH + 2 * D_MODEL
EPS = 1e-6
NEG = -1e30

kernel_name = "hybrid_moba_conformer_hmoe_block"


def rmsnorm(x, g):
    x32 = x.astype(jnp.float32)
    y = x32 * lax.rsqrt(jnp.mean(x32 * x32, axis=-1, keepdims=True) + EPS)
    return (y * g.astype(jnp.float32)).astype(x.dtype)


def layernorm(x, g, b):
    x32 = x.astype(jnp.float32)
    mu = jnp.mean(x32, axis=-1, keepdims=True)
    xc = x32 - mu
    var = jnp.mean(xc * xc, axis=-1, keepdims=True)
    y = xc * lax.rsqrt(var + EPS) * g.astype(jnp.float32) + b.astype(jnp.float32)
    return y.astype(x.dtype)


def rel_bucket(dist):
    n = jnp.maximum(dist, 0)
    max_exact = N_BUCKETS // 2
    nf = jnp.maximum(n, 1).astype(jnp.float32)
    large = max_exact + (jnp.log(nf / max_exact) / math.log(MAX_DISTANCE / max_exact)
                         * (N_BUCKETS - max_exact)).astype(jnp.int32)
    large = jnp.minimum(large, N_BUCKETS - 1)
    return jnp.where(n < max_exact, n, large)


def moba_attention(q, k, v, rel_bias):
    B, S = q.shape[0], q.shape[1]
    S_pad = -(-S // MOBA_BLOCK) * MOBA_BLOCK
    pad = S_pad - S
    q, k, v = [jnp.pad(t, ((0, 0), (0, pad), (0, 0), (0, 0))).transpose(0, 2, 1, 3) for t in (q, k, v)]
    NB = S_pad // MOBA_BLOCK
    kb = k.reshape(B, N_HEADS, NB, MOBA_BLOCK, HEAD_DIM)
    vb = v.reshape(B, N_HEADS, NB, MOBA_BLOCK, HEAD_DIM)
    kmean = jnp.mean(kb.astype(jnp.float32), axis=3)
    gate = jnp.einsum('bhsd,bhnd->bhsn', q.astype(jnp.float32), kmean)
    pos = jnp.arange(S_pad)
    qblk = pos // MOBA_BLOCK
    past = jnp.arange(NB)[None, :] < qblk[:, None]
    gate = jnp.where(past, gate, NEG)
    n_sel = min(MOBA_TOPK, NB)
    _, sel_idx = lax.top_k(gate, n_sel)
    sel_valid = sel_idx < qblk[:, None]

    NC = S_pad // QUERY_CHUNK

    def chunks(t):
        return jnp.moveaxis(t.reshape(B, N_HEADS, NC, QUERY_CHUNK, *t.shape[3:]), 2, 0)

    bias_table = rel_bias.astype(jnp.float32).T
    hi = jnp.arange(N_HEADS)
    bi = jnp.arange(B)
    scale = HEAD_DIM ** -0.5
    blk_off = jnp.arange(MOBA_BLOCK)

    def chunk_attn(args):
        q_c, idx_c, valid_c, n = args
        t = n * QUERY_CHUNK + jnp.arange(QUERY_CHUNK)
        c = (n * QUERY_CHUNK) // MOBA_BLOCK
        k_own = lax.dynamic_index_in_dim(kb, c, axis=2, keepdims=False)
        v_own = lax.dynamic_index_in_dim(vb, c, axis=2, keepdims=False)
        rel_own = t[:, None] - (c * MOBA_BLOCK + blk_off)[None, :]
        l_own = (jnp.einsum('bhqd,bhkd->bhqk', q_c, k_own).astype(jnp.float32) * scale
                 + bias_table[:, rel_bucket(rel_own)])
        l_own = jnp.where(rel_own >= 0, l_own, NEG)
        k_sel = kb[bi[:, None, None, None], hi[None, :, None, None], idx_c]
        v_sel = vb[bi[:, None, None, None], hi[None, :, None, None], idx_c]
        sel_pos = idx_c[..., None] * MOBA_BLOCK + blk_off
        rel_sel = t[:, None, None] - sel_pos
        l_sel = (jnp.einsum('bhqd,bhqnkd->bhqnk', q_c, k_sel).astype(jnp.float32) * scale
                 + bias_table[hi[None, :, None, None, None], rel_bucket(rel_sel)])
        l_sel = jnp.where(valid_c[..., None], l_sel, NEG)
        logits = jnp.concatenate([l_sel.reshape(B, N_HEADS, QUERY_CHUNK, n_sel * MOBA_BLOCK), l_own], axis=-1)
        probs = jax.nn.softmax(logits, axis=-1)
        p_sel = probs[..., :n_sel * MOBA_BLOCK].reshape(B, N_HEADS, QUERY_CHUNK, n_sel, MOBA_BLOCK).astype(v.dtype)
        p_own = probs[..., n_sel * MOBA_BLOCK:].astype(v.dtype)
        return (jnp.einsum('bhqnk,bhqnkd->bhqd', p_sel, v_sel)
                + jnp.einsum('bhqk,bhkd->bhqd', p_own, v_own))

    out = lax.map(chunk_attn, (chunks(q), chunks(sel_idx), chunks(sel_valid), jnp.arange(NC)))
    out = jnp.moveaxis(out, 0, 2).reshape(B, N_HEADS, S_pad, HEAD_DIM)[:, :, :S]
    return out.transpose(0, 2, 1, 3).reshape(B, S, ATTN_WIDTH)


def conformer_conv(u_pre, conv_w, conv_b, ln_g, ln_b, w_out):
    a, g = jnp.split(u_pre, 2, axis=-1)
    u = a * jax.nn.sigmoid(g)
    u = lax.conv_general_dilated(u, conv_w[:, None, :], window_strides=(1,),
                                 padding=((CONV_KERNEL - 1, 0),),
                                 dimension_numbers=('NWC', 'WIO', 'NWC'),
                                 feature_group_count=CONV_WIDTH) + conv_b
    u = jax.nn.silu(layernorm(u, ln_g, ln_b))
    return u @ w_out


def hier_moe(xn, w_rg, b_rg, w_re, b_re, w_gate, w_up, w_down):
    shp = xn.shape
    xt = xn.reshape(-1, D_MODEL)
    cl = (xt @ w_rg).astype(jnp.float32) + b_rg.astype(jnp.float32)
    cp = jax.nn.softmax(cl, axis=-1)
    _, g_star = lax.top_k(cl, 1)
    p_g = jnp.take_along_axis(cp, g_star, axis=-1)
    fl = ((xt @ w_re).astype(jnp.float32) + b_re.astype(jnp.float32)).reshape(-1, N_GROUPS, EXPERTS_PER_GROUP)
    f_sel = jnp.take_along_axis(fl, g_star[:, :, None], axis=1)[:, 0]
    top_v, top_i = lax.top_k(f_sel, EXPERT_TOPK)
    fw = jax.nn.softmax(top_v, axis=-1)
    fine = jnp.einsum('tk,tke->te', fw, jax.nn.one_hot(top_i, EXPERTS_PER_GROUP, dtype=jnp.float32))
    comb = (jax.nn.one_hot(g_star[:, 0], N_GROUPS, dtype=jnp.float32)[:, :, None]
            * (p_g * fine)[:, None, :]).reshape(-1, N_EXPERTS)
    h = jax.nn.silu(jnp.einsum('td,edf->tef', xt, w_gate)) * jnp.einsum('td,edf->tef', xt, w_up)
    y = jnp.einsum('tef,efd->td', h * comb.astype(h.dtype)[:, :, None], w_down)
    return y.reshape(shp)


def setup_inputs(seed: int = 0) -> dict:
    key = jax.random.key(seed)
    ks = jax.random.split(key, 26)
    f32 = jnp.float32

    def nrm(k, shape, scale):
        return jax.random.normal(k, shape, f32) * scale

    L = DEPTH
    return {
        "x": nrm(ks[0], (BATCH, SEQ, D_MODEL), 1.0),
        "p": nrm(ks[1], (DEPTH, BATCH, SEQ, PLE_DIM), 1.0),
        "g_mix": 1.0 + nrm(ks[2], (L, D_MODEL), 0.1),
        "w_in": nrm(ks[3], (L, D_MODEL, IN_COLS), D_MODEL ** -0.5),
        "rel_bias": nrm(ks[4], (N_BUCKETS, N_HEADS), 0.5),
        "w_attn_br": nrm(ks[5], (L, ATTN_WIDTH, D_MODEL), ATTN_WIDTH ** -0.5),
        "conv_w": nrm(ks[6], (L, CONV_KERNEL, CONV_WIDTH), CONV_KERNEL ** -0.5),
        "conv_b": nrm(ks[7], (L, CONV_WIDTH), 0.02),
        "ln_g": 1.0 + nrm(ks[8], (L, CONV_WIDTH), 0.1),
        "ln_b": nrm(ks[9], (L, CONV_WIDTH), 0.02),
        "w_conv_br": nrm(ks[10], (L, CONV_WIDTH, D_MODEL), CONV_WIDTH ** -0.5),
        "w_o": nrm(ks[11], (L, D_MODEL, D_MODEL), D_MODEL ** -0.5),
        "g_ffn": 1.0 + nrm(ks[12], (L, D_MODEL), 0.1),
        "w_router_g": nrm(ks[13], (L, D_MODEL, N_GROUPS), D_MODEL ** -0.5),
        "b_router_g": nrm(ks[14], (L, N_GROUPS), 0.01),
        "w_router_e": nrm(ks[15], (L, D_MODEL, N_EXPERTS), D_MODEL ** -0.5),
        "b_router_e": nrm(ks[16], (L, N_EXPERTS), 0.01),
        "w_e_gate": nrm(ks[17], (L, N_EXPERTS, D_MODEL, EXPERT_FF), D_MODEL ** -0.5),
        "w_e_up": nrm(ks[18], (L, N_EXPERTS, D_MODEL, EXPERT_FF), D_MODEL ** -0.5),
        "w_e_down": nrm(ks[19], (L, N_EXPERTS, EXPERT_FF, D_MODEL), EXPERT_FF ** -0.5),
        "g_ple": 1.0 + nrm(ks[20], (L, D_MODEL), 0.1),
        "w_ple_gate": nrm(ks[21], (L, D_MODEL, D_MODEL), D_MODEL ** -0.5),
        "w_ple_proj": nrm(ks[22], (L, PLE_DIM, D_MODEL), PLE_DIM ** -0.5),
        "g_final": 1.0 + nrm(ks[23], (D_MODEL,), 0.1),
    }


def reference(x, p, g_mix, w_in, rel_bias, w_attn_br, conv_w, conv_b, ln_g, ln_b, w_conv_br, w_o,
              g_ffn, w_router_g, b_router_g, w_router_e, b_router_e, w_e_gate, w_e_up, w_e_down,
              g_ple, w_ple_gate, w_ple_proj, g_final):
    B, S = x.shape[0], x.shape[1]
    h = x
    splits = [ATTN_WIDTH, 2 * ATTN_WIDTH, 3 * ATTN_WIDTH, 3 * ATTN_WIDTH + 2 * CONV_WIDTH]
    for i in range(DEPTH):
        xn = rmsnorm(h, g_mix[i])
        proj = xn @ w_in[i]
        q, k, v, conv_in, gate_logits = jnp.split(proj, splits, axis=-1)
        q = q.reshape(B, S, N_HEADS, HEAD_DIM)
        k = k.reshape(B, S, N_HEADS, HEAD_DIM)
        v = v.reshape(B, S, N_HEADS, HEAD_DIM)
        y_attn = moba_attention(q, k, v, rel_bias) @ w_attn_br[i]
        y_conv = conformer_conv(conv_in, conv_w[i], conv_b[i], ln_g[i], ln_b[i], w_conv_br[i])
        g_attn, g_conv = jnp.split(jax.nn.sigmoid(gate_logits), 2, axis=-1)
        h = h + (g_attn * y_attn + g_conv * y_conv) @ w_o[i]
        h = h + hier_moe(rmsnorm(h, g_ffn[i]), w_router_g[i], b_router_g[i], w_router_e[i],
                         b_router_e[i], w_e_gate[i], w_e_up[i], w_e_down[i])
        h = h + (p[i] @ w_ple_proj[i]) * jax.nn.sigmoid(rmsnorm(h, g_ple[i]) @ w_ple_gate[i])
    return rmsnorm(h, g_final)
```

```python
import functools
import math

import numpy as np
import jax
import jax.numpy as jnp
from jax import lax
from jax.experimental import pallas as pl
from jax.experimental.pallas import tpu as pltpu

F32 = jnp.float32
BF16 = jnp.bfloat16

N_HEADS = 16
HEAD_DIM = 128
MOBA_BLOCK = 256
MOBA_TOPK = 3
N_BUCKETS = 32
MAX_DISTANCE = 128
CONV_KERNEL = 31
N_GROUPS = 4
EXPERTS_PER_GROUP = 4
N_EXPERTS = N_GROUPS * EXPERTS_PER_GROUP
EPS = 1e-6
NEG = -1e30

V7X_VMEM_LIMIT_BYTES = 56 * 1024 * 1024
LANES = 128

_NT = (((1,), (1,)), ((), ()))


def _cparams(*sem):
    return pltpu.CompilerParams(dimension_semantics=sem, vmem_limit_bytes=V7X_VMEM_LIMIT_BYTES)


def _resident(shape):
    return pl.BlockSpec(shape, lambda *_: (0,) * len(shape), pipeline_mode=pl.Buffered(1))


def _rms(x, g):
    return x * lax.rsqrt(jnp.mean(x * x, axis=-1, keepdims=True) + EPS) * g


def _in_proj_kernel(x_ref, g_ref, w_ref, o_ref, xn_ref):
    @pl.when(pl.program_id(1) == 0)
    def _():
        xn_ref[...] = _rms(x_ref[...], g_ref[...]).astype(BF16)

    o_ref[...] = jnp.dot(xn_ref[...], w_ref[...], preferred_element_type=F32).astype(o_ref.dtype)


def _in_proj(x2, g, w, tm=1024, tn=1024):
    T, D = x2.shape
    N = w.shape[1]
    return pl.pallas_call(
        _in_proj_kernel,
        grid=(T // tm, N // tn),
        in_specs=[pl.BlockSpec((tm, D), lambda i, j: (i, 0)),
                  pl.BlockSpec((1, D), lambda i, j: (0, 0)),
                  pl.BlockSpec((D, tn), lambda i, j: (0, j))],
        out_specs=pl.BlockSpec((tm, tn), lambda i, j: (i, j)),
        out_shape=jax.ShapeDtypeStruct((T, N), BF16),
        scratch_shapes=[pltpu.VMEM((tm, D), BF16)],
        compiler_params=_cparams("parallel", "arbitrary"),
        name="in_proj",
    )(x2, g, w)


def _rel_buckets():
    r = np.arange(MOBA_BLOCK)[:, None]
    c = np.arange(MOBA_BLOCK)[None, :]
    dist = np.stack([r - c, MOBA_BLOCK + r - c]).astype(np.int32)
    n = np.maximum(dist, 0)
    max_exact = N_BUCKETS // 2
    nf = np.maximum(n, 1).astype(np.float32)
    large = max_exact + (np.log(nf / np.float32(max_exact)) / np.float32(math.log(MAX_DISTANCE / max_exact))
                         * np.float32(N_BUCKETS - max_exact)).astype(np.int32)
    large = np.minimum(large, N_BUCKETS - 1)
    return np.where(n < max_exact, n, large).astype(np.int32)


def _moba_kernel(tbl_ref, bkt_ref, q_ref, k_ref, v_ref, o_ref,
                 kmean_ref, bias_ref, sel_ref, m_ref, l_ref, acc_ref, *, n_blocks):
    h = pl.program_id(0)
    b = pl.program_id(1)
    qi = pl.program_id(2)
    BS = MOBA_BLOCK
    scale = HEAD_DIM ** -0.5

    @pl.when((b == 0) & (qi == 0))
    def _build_bias():
        bkt = bkt_ref[...]
        bias = jnp.zeros(bkt.shape, F32)
        for n in range(N_BUCKETS):
            bias = jnp.where(bkt == n, tbl_ref[h, n], bias)
        bias_ref[...] = bias

    @pl.when(qi == 0)
    def _block_means():
        for n in range(n_blocks):
            kb = k_ref[0, n * BS:(n + 1) * BS, :].astype(F32)
            kmean_ref[n:n + 1, :] = jnp.mean(kb, axis=0, keepdims=True)

    q = q_ref[0]

    gate = lax.dot_general(q, kmean_ref[...].astype(BF16), _NT, preferred_element_type=F32)
    blk = lax.broadcasted_iota(jnp.int32, gate.shape, 1)
    blkf = blk.astype(F32)
    g = jnp.where(blk < qi, gate, NEG)
    bits = jnp.zeros((BS, 1), jnp.int32)
    for _ in range(MOBA_TOPK):
        mx = jnp.max(g, axis=1, keepdims=True)
        ix = jnp.min(jnp.where(g == mx, blkf, float(n_blocks)), axis=1, keepdims=True).astype(jnp.int32)
        bits = bits | jnp.where(ix < qi, lax.shift_left(jnp.int32(1), ix), 0)
        g = jnp.where(blk == ix, -jnp.inf, g)
    sel_ref[...] = bits

    def logits(j0):
        kj = k_ref[0, pl.ds(j0, BS), :]
        return lax.dot_general(q_ref[0], kj, _NT, preferred_element_type=F32) * scale

    d0 = pl.multiple_of(qi * BS, BS)
    s = logits(d0) + bias_ref[0]
    row = lax.broadcasted_iota(jnp.int32, (BS, BS), 0)
    col = lax.broadcasted_iota(jnp.int32, (BS, BS), 1)
    s = jnp.where(row >= col, s, NEG)
    m0 = jnp.max(s, axis=1, keepdims=True)
    p = jnp.exp(s - m0)
    m_ref[...] = m0
    l_ref[...] = jnp.sum(p, axis=1, keepdims=True)
    acc_ref[...] = jnp.dot(p.astype(BF16), v_ref[0, pl.ds(d0, BS), :], preferred_element_type=F32)

    def past_block(j, bias):
        j0 = pl.multiple_of(j * BS, BS)
        chosen = (lax.shift_right_logical(sel_ref[...], j) & 1) == 1
        s = jnp.where(chosen, logits(j0) + bias, NEG)
        m_prev = m_ref[...]
        m_new = jnp.maximum(m_prev, jnp.max(s, axis=1, keepdims=True))
        alpha = jnp.exp(m_prev - m_new)
        p = jnp.exp(s - m_new)
        l_ref[...] = alpha * l_ref[...] + jnp.sum(p, axis=1, keepdims=True)
        acc_ref[...] = alpha * acc_ref[...] + jnp.dot(
            p.astype(BF16), v_ref[0, pl.ds(j0, BS), :], preferred_element_type=F32)
        m_ref[...] = m_new

    @pl.when(qi >= 1)
    def _prev_block():
        past_block(qi - 1, bias_ref[1])

    far_bias = tbl_ref[h, N_BUCKETS - 1]

    def far_body(j, carry):
        past_block(j, far_bias)
        return carry

    lax.fori_loop(0, jnp.maximum(qi - 1, 0), far_body, 0)

    o_ref[0] = (acc_ref[...] * (1.0 / l_ref[...])).astype(o_ref.dtype)


def _moba(proj3, rel_bias):
    B, S, _ = proj3.shape
    BS = MOBA_BLOCK
    nb = S // BS
    bkt = jnp.asarray(_rel_buckets())
    tbl = rel_bias.astype(F32).T
    grid_spec = pltpu.PrefetchScalarGridSpec(
        num_scalar_prefetch=1,
        grid=(N_HEADS, B, nb),
        in_specs=[pl.BlockSpec((2, BS, BS), lambda h, b, i, t: (0, 0, 0)),
                  pl.BlockSpec((1, BS, HEAD_DIM), lambda h, b, i, t: (b, i, h)),
                  pl.BlockSpec((1, S, HEAD_DIM), lambda h, b, i, t: (b, 0, N_HEADS + h)),
                  pl.BlockSpec((1, S, HEAD_DIM), lambda h, b, i, t: (b, 0, 2 * N_HEADS + h))],
        out_specs=pl.BlockSpec((1, BS, HEAD_DIM), lambda h, b, i, t: (b, i, h)),
        scratch_shapes=[pltpu.VMEM((nb, HEAD_DIM), F32),
                        pltpu.VMEM((2, BS, BS), F32),
                        pltpu.VMEM((BS, 1), jnp.int32),
                        pltpu.VMEM((BS, 1), F32),
                        pltpu.VMEM((BS, 1), F32),
                        pltpu.VMEM((BS, HEAD_DIM), F32)],
    )
    return pl.pallas_call(
        functools.partial(_moba_kernel, n_blocks=nb),
        grid_spec=grid_spec,
        out_shape=jax.ShapeDtypeStruct((B, S, N_HEADS * HEAD_DIM), BF16),
        compiler_params=_cparams("arbitrary", "arbitrary", "arbitrary"),
        name="moba",
    )(tbl, bkt, proj3, proj3, proj3)


CONV_HALO = 32
CONV_ROWS = 64
CONV_COLS = 256


def _conv_kernel(a_ref, g_ref, ah_ref, gh_ref, w_ref, cb_ref, lng_ref, lnb_ref, o_ref, u_ref, c_ref, *, ts):
    i = pl.program_id(1)
    D = u_ref.shape[1]
    u_ref[CONV_HALO:, :] = a_ref[0].astype(F32) * jax.nn.sigmoid(g_ref[0].astype(F32))
    uh = ah_ref[0].astype(F32) * jax.nn.sigmoid(gh_ref[0].astype(F32))
    u_ref[:CONV_HALO, :] = jnp.where(i > 0, uh, 0.0)

    first = CONV_HALO - (CONV_KERNEL - 1)

    def col_body(c, carry):
        c0 = pl.multiple_of(c * CONV_COLS, CONV_COLS)
        for r in range(ts // CONV_ROWS):
            acc = jnp.zeros((CONV_ROWS, CONV_COLS), F32)
            for k in range(CONV_KERNEL):
                r0 = r * CONV_ROWS + first + k
                acc = acc + w_ref[k:k + 1, pl.ds(c0, CONV_COLS)] * u_ref[r0:r0 + CONV_ROWS, pl.ds(c0, CONV_COLS)]
            c_ref[r * CONV_ROWS:(r + 1) * CONV_ROWS, pl.ds(c0, CONV_COLS)] = acc + cb_ref[:, pl.ds(c0, CONV_COLS)]
        return carry

    lax.fori_loop(0, D // CONV_COLS, col_body, 0)

    y = c_ref[...]
    mu = jnp.mean(y, axis=-1, keepdims=True)
    yc = y - mu
    var = jnp.mean(yc * yc, axis=-1, keepdims=True)
    z = yc * lax.rsqrt(var + EPS) * lng_ref[...] + lnb_ref[...]
    o_ref[0] = (z * jax.nn.sigmoid(z)).astype(o_ref.dtype)


def _conv(proj3, conv_w, conv_b, ln_g, ln_b, a_blk, ts=256):
    B, S, _ = proj3.shape
    D = conv_w.shape[1]
    hpb = ts // CONV_HALO

    def halo(col):
        return pl.BlockSpec((1, CONV_HALO, D), lambda b, i: (b, jnp.maximum(i * hpb - 1, 0), col))

    vec = pl.BlockSpec((1, D), lambda b, i: (0, 0))
    return pl.pallas_call(
        functools.partial(_conv_kernel, ts=ts),
        grid=(B, S // ts),
        in_specs=[pl.BlockSpec((1, ts, D), lambda b, i: (b, i, a_blk)),
                  pl.BlockSpec((1, ts, D), lambda b, i: (b, i, a_blk + 1)),
                  halo(a_blk), halo(a_blk + 1),
                  pl.BlockSpec((CONV_KERNEL, D), lambda b, i: (0, 0)),
                  vec, vec, vec],
        out_specs=pl.BlockSpec((1, ts, D), lambda b, i: (b, i, 0)),
        out_shape=jax.ShapeDtypeStruct((B, S, D), BF16),
        scratch_shapes=[pltpu.VMEM((CONV_HALO + ts, D), F32), pltpu.VMEM((ts, D), F32)],
        compiler_params=_cparams("parallel", "arbitrary"),
        name="conv",
    )(proj3, proj3, proj3, proj3, conv_w, conv_b, ln_g, ln_b)


def _mix_kernel(x_ref, at_ref, cv_ref, ga_ref, gc_ref, wa_ref, wc_ref, wo_ref, o_ref):
    ya = jnp.dot(at_ref[...], wa_ref[...], preferred_element_type=F32)
    yc = jnp.dot(cv_ref[...], wc_ref[...], preferred_element_type=F32)
    mixed = (jax.nn.sigmoid(ga_ref[...].astype(F32)) * ya + jax.nn.sigmoid(gc_ref[...].astype(F32)) * yc)
    o_ref[...] = x_ref[...] + jnp.dot(mixed.astype(BF16), wo_ref[...], preferred_element_type=F32)


def _mix(x2, attn2, conv2, proj2, wa, wc, wo, gate_blk, tm=256):
    T, D = x2.shape
    row = lambda col: pl.BlockSpec((tm, D), lambda i: (i, col))
    return pl.pallas_call(
        _mix_kernel,
        grid=(T // tm,),
        in_specs=[row(0), row(0), row(0), row(gate_blk), row(gate_blk + 1),
                  _resident((D, D)), _resident((D, D)), _resident((D, D))],
        out_specs=row(0),
        out_shape=jax.ShapeDtypeStruct((T, D), F32),
        compiler_params=_cparams("parallel"),
        name="mix",
    )(x2, attn2, conv2, proj2, proj2, wa, wc, wo)


def _route(logits):
    lane = lax.broadcasted_iota(jnp.int32, logits.shape, 1)
    lanef = lane.astype(F32)
    big = float(LANES)
    is_grp = (lane >= N_EXPERTS) & (lane < N_EXPERTS + N_GROUPS)
    cl = jnp.where(is_grp, logits, -jnp.inf)
    mg = jnp.max(cl, axis=1, keepdims=True)
    g_star = jnp.min(jnp.where(cl == mg, lanef, big), axis=1, keepdims=True).astype(jnp.int32) - N_EXPERTS
    p_g = 1.0 / jnp.sum(jnp.where(is_grp, jnp.exp(cl - mg), 0.0), axis=1, keepdims=True)
    in_grp = (lane < N_EXPERTS) & ((lane // EXPERTS_PER_GROUP) == g_star)
    f1 = jnp.where(in_grp, logits, -jnp.inf)
    v1 = jnp.max(f1, axis=1, keepdims=True)
    i1 = jnp.min(jnp.where(f1 == v1, lanef, big), axis=1, keepdims=True).astype(jnp.int32)
    f2 = jnp.where(lane == i1, -jnp.inf, f1)
    v2 = jnp.max(f2, axis=1, keepdims=True)
    i2 = jnp.min(jnp.where(f2 == v2, lanef, big), axis=1, keepdims=True).astype(jnp.int32)
    e2 = jnp.exp(v2 - v1)
    w1 = 1.0 / (1.0 + e2)
    w2 = e2 / (1.0 + e2)
    return jnp.where(lane == i1, p_g * w1, 0.0) + jnp.where(lane == i2, p_g * w2, 0.0)


def _moe_kernel(h_ref, g_ref, wr_ref, br_ref, wg_ref, wu_ref, wd_ref, o_ref, hn_ref, comb_ref, acc_ref):
    e = pl.program_id(1)

    @pl.when(e == 0)
    def _():
        h = h_ref[...]
        hn = _rms(h, g_ref[...])
        hn_ref[...] = hn.astype(BF16)
        logits = jnp.dot(hn, wr_ref[...], precision=lax.Precision.HIGHEST,
                         preferred_element_type=F32) + br_ref[...]
        comb_ref[...] = _route(logits)
        acc_ref[...] = h

    hn = hn_ref[...]
    gate = jnp.dot(hn, wg_ref[0], preferred_element_type=F32)
    up = jnp.dot(hn, wu_ref[0], preferred_element_type=F32)
    lane = lax.broadcasted_iota(jnp.int32, comb_ref.shape, 1)
    c_e = jnp.sum(jnp.where(lane == e, comb_ref[...], 0.0), axis=1, keepdims=True)
    hh = (gate * jax.nn.sigmoid(gate)) * up * c_e
    acc_ref[...] += jnp.dot(hh.astype(BF16), wd_ref[0], preferred_element_type=F32)

    @pl.when(e == pl.num_programs(1) - 1)
    def _():
        o_ref[...] = acc_ref[...]


def _moe(h1, g, wr, br, wg, wu, wd, tm=512):
    T, D = h1.shape
    E, _, FF = wg.shape
    return pl.pallas_call(
        _moe_kernel,
        grid=(T // tm, E),
        in_specs=[pl.BlockSpec((tm, D), lambda i, e: (i, 0)),
                  pl.BlockSpec((1, D), lambda i, e: (0, 0)),
                  pl.BlockSpec((D, LANES), lambda i, e: (0, 0)),
                  pl.BlockSpec((1, LANES), lambda i, e: (0, 0)),
                  pl.BlockSpec((1, D, FF), lambda i, e: (e, 0, 0)),
                  pl.BlockSpec((1, D, FF), lambda i, e: (e, 0, 0)),
                  pl.BlockSpec((1, FF, D), lambda i, e: (e, 0, 0))],
        out_specs=pl.BlockSpec((tm, D), lambda i, e: (i, 0)),
        out_shape=jax.ShapeDtypeStruct((T, D), F32),
        scratch_shapes=[pltpu.VMEM((tm, D), BF16), pltpu.VMEM((tm, LANES), F32), pltpu.VMEM((tm, D), F32)],
        compiler_params=_cparams("parallel", "arbitrary"),
        name="moe",
    )(h1, g, wr, br, wg, wu, wd)


def _ple_kernel(h_ref, p_ref, gp_ref, gf_ref, wg_ref, wp_ref, o_ref):
    h = h_ref[...]
    t = _rms(h, gp_ref[...]).astype(BF16)
    gate = jax.nn.sigmoid(jnp.dot(t, wg_ref[...], preferred_element_type=F32))
    emb = jnp.dot(p_ref[...].astype(BF16), wp_ref[...], preferred_element_type=F32)
    o_ref[...] = _rms(h + emb * gate, gf_ref[...])


def _ple(h2, p2, g_ple, g_final, wg, wp, tm=512):
    T, D = h2.shape
    P = p2.shape[1]
    vec = pl.BlockSpec((1, D), lambda i: (0, 0))
    return pl.pallas_call(
        _ple_kernel,
        grid=(T // tm,),
        in_specs=[pl.BlockSpec((tm, D), lambda i: (i, 0)),
                  pl.BlockSpec((tm, P), lambda i: (i, 0)),
                  vec, vec, _resident((D, D)), _resident((P, D))],
        out_specs=pl.BlockSpec((tm, D), lambda i: (i, 0)),
        out_shape=jax.ShapeDtypeStruct((T, D), F32),
        compiler_params=_cparams("parallel"),
        name="ple",
    )(h2, p2, g_ple, g_final, wg, wp)


def kernel(x, p, g_mix, w_in, rel_bias, w_attn_br, conv_w, conv_b, ln_g, ln_b, w_conv_br, w_o, g_ffn,
           w_router_g, b_router_g, w_router_e, b_router_e, w_e_gate, w_e_up, w_e_down, g_ple, w_ple_gate,
           w_ple_proj, g_final):
    B, S, D = x.shape
    T = B * S
    depth = w_in.shape[0]
    width = N_HEADS * HEAD_DIM
    assert D == width and conv_w.shape[2] == D, "column-block indexing assumes all branch widths equal d_model"
    assert S % MOBA_BLOCK == 0
    assert depth == 1, "the final RMSNorm is fused into the last stage of a single layer"
    a_blk = 3 * width // D
    gate_blk = a_blk + 2

    h = x.reshape(T, D)
    for i in range(depth):
        proj = _in_proj(h, g_mix[i][None], w_in[i].astype(BF16))
        proj3 = proj.reshape(B, S, -1)
        attn = _moba(proj3, rel_bias)
        conv = _conv(proj3, conv_w[i], conv_b[i][None], ln_g[i][None], ln_b[i][None], a_blk)
        h = _mix(h, attn.reshape(T, D), conv.reshape(T, D), proj,
                 w_attn_br[i].astype(BF16), w_conv_br[i].astype(BF16), w_o[i].astype(BF16), gate_blk)
        pad = LANES - N_EXPERTS - N_GROUPS
        wr = jnp.concatenate([w_router_e[i], w_router_g[i], jnp.zeros((D, pad), F32)], axis=1)
        br = jnp.concatenate([b_router_e[i], b_router_g[i], jnp.zeros((pad,), F32)])[None]
        h = _moe(h, g_ffn[i][None], wr, br,
                 w_e_gate[i].astype(BF16), w_e_up[i].astype(BF16), w_e_down[i].astype(BF16))
        out = _ple(h, p[i].reshape(T, -1), g_ple[i][None], g_final[None],
                   w_ple_gate[i].astype(BF16), w_ple_proj[i].astype(BF16))
    return out.reshape(B, S, D)
```

```python
import functools
import math

import numpy as np
import jax
import jax.numpy as jnp
from jax import lax
from jax.experimental import pallas as pl
from jax.experimental.pallas import tpu as pltpu

F32 = jnp.float32
BF16 = jnp.bfloat16

N_HEADS = 16
HEAD_DIM = 128
MOBA_BLOCK = 256
MOBA_TOPK = 3
N_BUCKETS = 32
MAX_DISTANCE = 128
CONV_KERNEL = 31
N_GROUPS = 4
EXPERTS_PER_GROUP = 4
N_EXPERTS = N_GROUPS * EXPERTS_PER_GROUP
EPS = 1e-6
NEG = -1e30

V7X_VMEM_LIMIT_BYTES = 56 * 1024 * 1024
LANES = 128


def _cparams(*sem):
    return pltpu.CompilerParams(dimension_semantics=sem, vmem_limit_bytes=V7X_VMEM_LIMIT_BYTES)


def _resident(shape):
    return pl.BlockSpec(shape, lambda *_: (0,) * len(shape), pipeline_mode=pl.Buffered(1))


def _rms(x, g):
    return x * lax.rsqrt(jnp.mean(x * x, axis=-1, keepdims=True) + EPS) * g


def _in_proj_kernel(x_ref, g_ref, w_ref, o_ref, xn_ref):
    @pl.when(pl.program_id(1) == 0)
    def _():
        xn_ref[...] = _rms(x_ref[...], g_ref[...]).astype(BF16)

    o_ref[...] = jnp.dot(xn_ref[...], w_ref[...], preferred_element_type=F32).astype(o_ref.dtype)


def _in_proj(x2, g, w, tm=1024, tn=1024):
    T, D = x2.shape
    N = w.shape[1]
    return pl.pallas_call(
        _in_proj_kernel,
        grid=(T // tm, N // tn),
        in_specs=[pl.BlockSpec((tm, D), lambda i, j: (i, 0)),
                  pl.BlockSpec((1, D), lambda i, j: (0, 0)),
                  pl.BlockSpec((D, tn), lambda i, j: (0, j))],
        out_specs=pl.BlockSpec((tm, tn), lambda i, j: (i, j)),
        out_shape=jax.ShapeDtypeStruct((T, N), BF16),
        scratch_shapes=[pltpu.VMEM((tm, D), BF16)],
        compiler_params=_cparams("parallel", "arbitrary"),
        name="in_proj",
    )(x2, g, w)


MOBA_HEADS_PER_STEP = 2


def _rel_buckets_t():
    k = np.arange(MOBA_BLOCK)[:, None]
    q = np.arange(MOBA_BLOCK)[None, :]
    dist = np.stack([q - k, MOBA_BLOCK + q - k]).astype(np.int32)
    n = np.maximum(dist, 0)
    max_exact = N_BUCKETS // 2
    nf = np.maximum(n, 1).astype(np.float32)
    large = max_exact + (np.log(nf / np.float32(max_exact)) / np.float32(math.log(MAX_DISTANCE / max_exact))
                         * np.float32(N_BUCKETS - max_exact)).astype(np.int32)
    large = np.minimum(large, N_BUCKETS - 1)
    return np.where(n < max_exact, n, large).astype(np.int32)


def _moba_kernel(tbl_ref, bkt_ref, q_ref, k_ref, v_ref, o_ref,
                 kmean_ref, vt_ref, bias_ref, qt_ref, sel_ref, s_ref, p_ref, m_ref, l_ref, a_ref, acc_ref,
                 *, n_blocks):
    hg = pl.program_id(0)
    b = pl.program_id(1)
    qi = pl.program_id(2)
    BS = MOBA_BLOCK
    HPS = MOBA_HEADS_PER_STEP
    scale = HEAD_DIM ** -0.5
    lanes = [slice(hh * HEAD_DIM, (hh + 1) * HEAD_DIM) for hh in range(HPS)]

    @pl.when((b == 0) & (qi == 0))
    def _build_bias():
        bkt = bkt_ref[...]
        krow = lax.broadcasted_iota(jnp.int32, (BS, BS), 0)
        qcol = lax.broadcasted_iota(jnp.int32, (BS, BS), 1)
        for hh in range(HPS):
            bias = jnp.zeros(bkt.shape, F32)
            for n in range(N_BUCKETS):
                bias = jnp.where(bkt == n, tbl_ref[hg * HPS + hh, n], bias)
            bias_ref[hh, 0] = jnp.where(qcol >= krow, bias[0], NEG)
            bias_ref[hh, 1] = bias[1]
            bias_ref[hh, 2] = jnp.full((BS, BS), tbl_ref[hg * HPS + hh, N_BUCKETS - 1], F32)

    @pl.when(qi == 0)
    def _per_sequence():
        for hh in range(HPS):
            for n in range(n_blocks):
                rows = slice(n * BS, (n + 1) * BS)
                kmean_ref[hh, n:n + 1, :] = jnp.mean(k_ref[0, rows, lanes[hh]].astype(F32), axis=0, keepdims=True)
                vt_ref[hh, n] = v_ref[0, rows, lanes[hh]].astype(F32).T.astype(BF16)

    def qk_stage(hh, t):
        j0 = pl.multiple_of(jnp.maximum(qi - t, 0) * BS, BS)
        s_ref[hh] = jnp.dot(k_ref[0, pl.ds(j0, BS), lanes[hh]], qt_ref[hh], preferred_element_type=F32)

    def softmax_stage(hh, t):
        chosen = (lax.shift_right_logical(sel_ref[hh], qi - t) & 1) == 1
        st = s_ref[hh] * scale + bias_ref[hh, jnp.minimum(t, 2)]
        st = jnp.where(chosen, st, NEG)
        m_prev = m_ref[hh]
        m_new = jnp.maximum(m_prev, jnp.max(st, axis=0, keepdims=True))
        alpha = jnp.exp(m_prev - m_new)
        p = jnp.exp(st - m_new)
        l_ref[hh] = alpha * l_ref[hh] + jnp.sum(p, axis=0, keepdims=True)
        m_ref[hh] = m_new
        a_ref[hh] = alpha
        p_ref[hh] = p.astype(BF16)

    def pv_stage(hh, t):
        j = jnp.clip(qi - t, 0, n_blocks - 1)
        acc_ref[hh] = a_ref[hh] * acc_ref[hh] + jnp.dot(vt_ref[hh, j], p_ref[hh], preferred_element_type=F32)

    for hh in range(HPS):
        qt = q_ref[0, :, lanes[hh]].astype(F32).T.astype(BF16)
        qt_ref[hh] = qt
        gate = jnp.dot(kmean_ref[hh].astype(BF16), qt, preferred_element_type=F32)
        blk = lax.broadcasted_iota(jnp.int32, gate.shape, 0)
        blkf = blk.astype(F32)
        g = jnp.where(blk < qi, gate, NEG)
        bits = jnp.full((1, BS), lax.shift_left(jnp.int32(1), qi), jnp.int32)
        for _ in range(MOBA_TOPK):
            mx = jnp.max(g, axis=0, keepdims=True)
            ix = jnp.min(jnp.where(g == mx, blkf, float(n_blocks)), axis=0, keepdims=True).astype(jnp.int32)
            bits = bits | jnp.where(ix < qi, lax.shift_left(jnp.int32(1), ix), 0)
            g = jnp.where(blk == ix, -jnp.inf, g)
        sel_ref[hh] = bits
        m_ref[hh] = jnp.full((1, BS), NEG, F32)
        l_ref[hh] = jnp.zeros((1, BS), F32)
        a_ref[hh] = jnp.zeros((1, BS), F32)
        acc_ref[hh] = jnp.zeros((HEAD_DIM, BS), F32)
        p_ref[hh] = jnp.zeros((BS, BS), BF16)
        qk_stage(hh, 0)

    def sweep(t, carry):
        for hh in range(HPS):
            pv_stage(hh, t - 1)
        for hh in range(HPS):
            softmax_stage(hh, t)
        for hh in range(HPS):
            qk_stage(hh, t + 1)
        return carry

    lax.fori_loop(0, qi + 1, sweep, 0)

    for hh in range(HPS):
        pv_stage(hh, qi)
        out_t = acc_ref[hh] * (1.0 / l_ref[hh])
        o_ref[0, :, lanes[hh]] = out_t.T.astype(o_ref.dtype)


def _moba(proj3, rel_bias):
    B, S, _ = proj3.shape
    BS = MOBA_BLOCK
    HPS = MOBA_HEADS_PER_STEP
    W = HPS * HEAD_DIM
    nb = S // BS
    n_groups = N_HEADS // HPS
    bkt = jnp.asarray(_rel_buckets_t())
    tbl = rel_bias.astype(F32).T
    grid_spec = pltpu.PrefetchScalarGridSpec(
        num_scalar_prefetch=1,
        grid=(n_groups, B, nb),
        in_specs=[pl.BlockSpec((2, BS, BS), lambda h, b, i, t: (0, 0, 0)),
                  pl.BlockSpec((1, BS, W), lambda h, b, i, t: (b, i, h)),
                  pl.BlockSpec((1, S, W), lambda h, b, i, t: (b, 0, n_groups + h)),
                  pl.BlockSpec((1, S, W), lambda h, b, i, t: (b, 0, 2 * n_groups + h))],
        out_specs=pl.BlockSpec((1, BS, W), lambda h, b, i, t: (b, i, h)),
        scratch_shapes=[pltpu.VMEM((HPS, nb, HEAD_DIM), F32),
                        pltpu.VMEM((HPS, nb, HEAD_DIM, BS), BF16),
                        pltpu.VMEM((HPS, 3, BS, BS), F32),
                        pltpu.VMEM((HPS, HEAD_DIM, BS), BF16),
                        pltpu.VMEM((HPS, 1, BS), jnp.int32),
                        pltpu.VMEM((HPS, BS, BS), F32),
                        pltpu.VMEM((HPS, BS, BS), BF16),
                        pltpu.VMEM((HPS, 1, BS), F32),
                        pltpu.VMEM((HPS, 1, BS), F32),
                        pltpu.VMEM((HPS, 1, BS), F32),
                        pltpu.VMEM((HPS, HEAD_DIM, BS), F32)],
    )
    return pl.pallas_call(
        functools.partial(_moba_kernel, n_blocks=nb),
        grid_spec=grid_spec,
        out_shape=jax.ShapeDtypeStruct((B, S, N_HEADS * HEAD_DIM), BF16),
        compiler_params=_cparams("arbitrary", "arbitrary", "arbitrary"),
        name="moba",
    )(tbl, bkt, proj3, proj3, proj3)


CONV_HALO = 32
CONV_ROWS = 64
CONV_COLS = 256


def _conv_kernel(a_ref, g_ref, ah_ref, gh_ref, w_ref, cb_ref, lng_ref, lnb_ref, o_ref, u_ref, c_ref, *, ts):
    i = pl.program_id(1)
    D = u_ref.shape[1]
    u_ref[CONV_HALO:, :] = a_ref[0].astype(F32) * jax.nn.sigmoid(g_ref[0].astype(F32))
    uh = ah_ref[0].astype(F32) * jax.nn.sigmoid(gh_ref[0].astype(F32))
    u_ref[:CONV_HALO, :] = jnp.where(i > 0, uh, 0.0)

    first = CONV_HALO - (CONV_KERNEL - 1)

    def col_body(c, carry):
        c0 = pl.multiple_of(c * CONV_COLS, CONV_COLS)
        for r in range(ts // CONV_ROWS):
            acc = jnp.zeros((CONV_ROWS, CONV_COLS), F32)
            for k in range(CONV_KERNEL):
                r0 = r * CONV_ROWS + first + k
                acc = acc + w_ref[k:k + 1, pl.ds(c0, CONV_COLS)] * u_ref[r0:r0 + CONV_ROWS, pl.ds(c0, CONV_COLS)]
            c_ref[r * CONV_ROWS:(r + 1) * CONV_ROWS, pl.ds(c0, CONV_COLS)] = acc + cb_ref[:, pl.ds(c0, CONV_COLS)]
        return carry

    lax.fori_loop(0, D // CONV_COLS, col_body, 0)

    y = c_ref[...]
    mu = jnp.mean(y, axis=-1, keepdims=True)
    yc = y - mu
    var = jnp.mean(yc * yc, axis=-1, keepdims=True)
    z = yc * lax.rsqrt(var + EPS) * lng_ref[...] + lnb_ref[...]
    o_ref[0] = (z * jax.nn.sigmoid(z)).astype(o_ref.dtype)


def _conv(proj3, conv_w, conv_b, ln_g, ln_b, a_blk, ts=256):
    B, S, _ = proj3.shape
    D = conv_w.shape[1]
    hpb = ts // CONV_HALO

    def halo(col):
        return pl.BlockSpec((1, CONV_HALO, D), lambda b, i: (b, jnp.maximum(i * hpb - 1, 0), col))

    vec = pl.BlockSpec((1, D), lambda b, i: (0, 0))
    return pl.pallas_call(
        functools.partial(_conv_kernel, ts=ts),
        grid=(B, S // ts),
        in_specs=[pl.BlockSpec((1, ts, D), lambda b, i: (b, i, a_blk)),
                  pl.BlockSpec((1, ts, D), lambda b, i: (b, i, a_blk + 1)),
                  halo(a_blk), halo(a_blk + 1),
                  pl.BlockSpec((CONV_KERNEL, D), lambda b, i: (0, 0)),
                  vec, vec, vec],
        out_specs=pl.BlockSpec((1, ts, D), lambda b, i: (b, i, 0)),
        out_shape=jax.ShapeDtypeStruct((B, S, D), BF16),
        scratch_shapes=[pltpu.VMEM((CONV_HALO + ts, D), F32), pltpu.VMEM((ts, D), F32)],
        compiler_params=_cparams("parallel", "arbitrary"),
        name="conv",
    )(proj3, proj3, proj3, proj3, conv_w, conv_b, ln_g, ln_b)


def _mix_kernel(x_ref, at_ref, cv_ref, ga_ref, gc_ref, wa_ref, wc_ref, wo_ref, o_ref):
    ya = jnp.dot(at_ref[...], wa_ref[...], preferred_element_type=F32)
    yc = jnp.dot(cv_ref[...], wc_ref[...], preferred_element_type=F32)
    mixed = (jax.nn.sigmoid(ga_ref[...].astype(F32)) * ya + jax.nn.sigmoid(gc_ref[...].astype(F32)) * yc)
    o_ref[...] = x_ref[...] + jnp.dot(mixed.astype(BF16), wo_ref[...], preferred_element_type=F32)


def _mix(x2, attn2, conv2, proj2, wa, wc, wo, gate_blk, tm=256):
    T, D = x2.shape
    row = lambda col: pl.BlockSpec((tm, D), lambda i: (i, col))
    return pl.pallas_call(
        _mix_kernel,
        grid=(T // tm,),
        in_specs=[row(0), row(0), row(0), row(gate_blk), row(gate_blk + 1),
                  _resident((D, D)), _resident((D, D)), _resident((D, D))],
        out_specs=row(0),
        out_shape=jax.ShapeDtypeStruct((T, D), F32),
        compiler_params=_cparams("parallel"),
        name="mix",
    )(x2, attn2, conv2, proj2, proj2, wa, wc, wo)


def _route(logits):
    lane = lax.broadcasted_iota(jnp.int32, logits.shape, 1)
    lanef = lane.astype(F32)
    big = float(LANES)
    is_grp = (lane >= N_EXPERTS) & (lane < N_EXPERTS + N_GROUPS)
    cl = jnp.where(is_grp, logits, -jnp.inf)
    mg = jnp.max(cl, axis=1, keepdims=True)
    g_star = jnp.min(jnp.where(cl == mg, lanef, big), axis=1, keepdims=True).astype(jnp.int32) - N_EXPERTS
    p_g = 1.0 / jnp.sum(jnp.where(is_grp, jnp.exp(cl - mg), 0.0), axis=1, keepdims=True)
    in_grp = (lane < N_EXPERTS) & ((lane // EXPERTS_PER_GROUP) == g_star)
    f1 = jnp.where(in_grp, logits, -jnp.inf)
    v1 = jnp.max(f1, axis=1, keepdims=True)
    i1 = jnp.min(jnp.where(f1 == v1, lanef, big), axis=1, keepdims=True).astype(jnp.int32)
    f2 = jnp.where(lane == i1, -jnp.inf, f1)
    v2 = jnp.max(f2, axis=1, keepdims=True)
    i2 = jnp.min(jnp.where(f2 == v2, lanef, big), axis=1, keepdims=True).astype(jnp.int32)
    e2 = jnp.exp(v2 - v1)
    w1 = 1.0 / (1.0 + e2)
    w2 = e2 / (1.0 + e2)
    return jnp.where(lane == i1, p_g * w1, 0.0) + jnp.where(lane == i2, p_g * w2, 0.0)


def _moe_kernel(h_ref, g_ref, wr_ref, br_ref, wg_ref, wu_ref, wd_ref, o_ref, hn_ref, comb_ref, acc_ref):
    e = pl.program_id(1)

    @pl.when(e == 0)
    def _():
        h = h_ref[...]
        hn = _rms(h, g_ref[...])
        hn_ref[...] = hn.astype(BF16)
        logits = jnp.dot(hn, wr_ref[...], precision=lax.Precision.HIGHEST,
                         preferred_element_type=F32) + br_ref[...]
        comb_ref[...] = _route(logits)
        acc_ref[...] = h

    hn = hn_ref[...]
    gate = jnp.dot(hn, wg_ref[0], preferred_element_type=F32)
    up = jnp.dot(hn, wu_ref[0], preferred_element_type=F32)
    lane = lax.broadcasted_iota(jnp.int32, comb_ref.shape, 1)
    c_e = jnp.sum(jnp.where(lane == e, comb_ref[...], 0.0), axis=1, keepdims=True)
    hh = (gate * jax.nn.sigmoid(gate)) * up * c_e
    acc_ref[...] += jnp.dot(hh.astype(BF16), wd_ref[0], preferred_element_type=F32)

    @pl.when(e == pl.num_programs(1) - 1)
    def _():
        o_ref[...] = acc_ref[...]


def _moe(h1, g, wr, br, wg, wu, wd, tm=512):
    T, D = h1.shape
    E, _, FF = wg.shape
    return pl.pallas_call(
        _moe_kernel,
        grid=(T // tm, E),
        in_specs=[pl.BlockSpec((tm, D), lambda i, e: (i, 0)),
                  pl.BlockSpec((1, D), lambda i, e: (0, 0)),
                  pl.BlockSpec((D, LANES), lambda i, e: (0, 0)),
                  pl.BlockSpec((1, LANES), lambda i, e: (0, 0)),
                  pl.BlockSpec((1, D, FF), lambda i, e: (e, 0, 0)),
                  pl.BlockSpec((1, D, FF), lambda i, e: (e, 0, 0)),
                  pl.BlockSpec((1, FF, D), lambda i, e: (e, 0, 0))],
        out_specs=pl.BlockSpec((tm, D), lambda i, e: (i, 0)),
        out_shape=jax.ShapeDtypeStruct((T, D), F32),
        scratch_shapes=[pltpu.VMEM((tm, D), BF16), pltpu.VMEM((tm, LANES), F32), pltpu.VMEM((tm, D), F32)],
        compiler_params=_cparams("parallel", "arbitrary"),
        name="moe",
    )(h1, g, wr, br, wg, wu, wd)


def _ple_kernel(h_ref, p_ref, gp_ref, gf_ref, wg_ref, wp_ref, o_ref):
    h = h_ref[...]
    t = _rms(h, gp_ref[...]).astype(BF16)
    gate = jax.nn.sigmoid(jnp.dot(t, wg_ref[...], preferred_element_type=F32))
    emb = jnp.dot(p_ref[...].astype(BF16), wp_ref[...], preferred_element_type=F32)
    o_ref[...] = _rms(h + emb * gate, gf_ref[...])


def _ple(h2, p2, g_ple, g_final, wg, wp, tm=512):
    T, D = h2.shape
    P = p2.shape[1]
    vec = pl.BlockSpec((1, D), lambda i: (0, 0))
    return pl.pallas_call(
        _ple_kernel,
        grid=(T // tm,),
        in_specs=[pl.BlockSpec((tm, D), lambda i: (i, 0)),
                  pl.BlockSpec((tm, P), lambda i: (i, 0)),
                  vec, vec, _resident((D, D)), _resident((P, D))],
        out_specs=pl.BlockSpec((tm, D), lambda i: (i, 0)),
        out_shape=jax.ShapeDtypeStruct((T, D), F32),
        compiler_params=_cparams("parallel"),
        name="ple",
    )(h2, p2, g_ple, g_final, wg, wp)


def kernel(x, p, g_mix, w_in, rel_bias, w_attn_br, conv_w, conv_b, ln_g, ln_b, w_conv_br, w_o, g_ffn,
           w_router_g, b_router_g, w_router_e, b_router_e, w_e_gate, w_e_up, w_e_down, g_ple, w_ple_gate,
           w_ple_proj, g_final):
    B, S, D = x.shape
    T = B * S
    depth = w_in.shape[0]
    width = N_HEADS * HEAD_DIM
    assert D == width and conv_w.shape[2] == D, "column-block indexing assumes all branch widths equal d_model"
    assert S % MOBA_BLOCK == 0
    assert depth == 1, "the final RMSNorm is fused into the last stage of a single layer"
    a_blk = 3 * width // D
    gate_blk = a_blk + 2

    h = x.reshape(T, D)
    for i in range(depth):
        proj = _in_proj(h, g_mix[i][None], w_in[i].astype(BF16))
        proj3 = proj.reshape(B, S, -1)
        attn = _moba(proj3, rel_bias)
        conv = _conv(proj3, conv_w[i], conv_b[i][None], ln_g[i][None], ln_b[i][None], a_blk)
        h = _mix(h, attn.reshape(T, D), conv.reshape(T, D), proj,
                 w_attn_br[i].astype(BF16), w_conv_br[i].astype(BF16), w_o[i].astype(BF16), gate_blk)
        pad = LANES - N_EXPERTS - N_GROUPS
        wr = jnp.concatenate([w_router_e[i], w_router_g[i], jnp.zeros((D, pad), F32)], axis=1)
        br = jnp.concatenate([b_router_e[i], b_router_g[i], jnp.zeros((pad,), F32)])[None]
        h = _moe(h, g_ffn[i][None], wr, br,
                 w_e_gate[i].astype(BF16), w_e_up[i].astype(BF16), w_e_down[i].astype(BF16))
        out = _ple(h, p[i].reshape(T, -1), g_ple[i][None], g_final[None],
                   w_ple_gate[i].astype(BF16), w_ple_proj[i].astype(BF16))
    return out.reshape(B, S, D)
```

```python
import functools
import math

import numpy as np
import jax
import jax.numpy as jnp
from jax import lax
from jax.experimental import pallas as pl
from jax.experimental.pallas import tpu as pltpu

F32 = jnp.float32
BF16 = jnp.bfloat16

N_HEADS = 16
HEAD_DIM = 128
MOBA_BLOCK = 256
MOBA_TOPK = 3
N_BUCKETS = 32
MAX_DISTANCE = 128
CONV_KERNEL = 31
N_GROUPS = 4
EXPERTS_PER_GROUP = 4
N_EXPERTS = N_GROUPS * EXPERTS_PER_GROUP
EPS = 1e-6
NEG = -1e30
LOG2E = 1.4426950408889634

V7X_VMEM_LIMIT_BYTES = 56 * 1024 * 1024
LANES = 128
SUBLANES = 8


def _cparams(*sem):
    return pltpu.CompilerParams(dimension_semantics=sem, vmem_limit_bytes=V7X_VMEM_LIMIT_BYTES)


def _resident(shape):
    return pl.BlockSpec(shape, lambda *_: (0,) * len(shape), pipeline_mode=pl.Buffered(1))


def _sigmoid(x):
    return 0.5 * jnp.tanh(0.5 * x) + 0.5


def _rms(x, g):
    return x * lax.rsqrt(jnp.mean(x * x, axis=-1, keepdims=True) + EPS) * g


def _in_proj_kernel(x_ref, g_ref, w_ref, o_ref, xn_ref):
    @pl.when(pl.program_id(1) == 0)
    def _():
        xn_ref[...] = _rms(x_ref[...], g_ref[...]).astype(BF16)

    o_ref[...] = jnp.dot(xn_ref[...], w_ref[...], preferred_element_type=F32).astype(o_ref.dtype)


def _in_proj(x2, g, w, tm=1024, tn=1024):
    T, D = x2.shape
    N = w.shape[1]
    return pl.pallas_call(
        _in_proj_kernel,
        grid=(T // tm, N // tn),
        in_specs=[pl.BlockSpec((tm, D), lambda i, j: (i, 0)),
                  pl.BlockSpec((1, D), lambda i, j: (0, 0)),
                  pl.BlockSpec((D, tn), lambda i, j: (0, j))],
        out_specs=pl.BlockSpec((tm, tn), lambda i, j: (i, j)),
        out_shape=jax.ShapeDtypeStruct((T, N), BF16),
        scratch_shapes=[pltpu.VMEM((tm, D), BF16)],
        compiler_params=_cparams("parallel", "arbitrary"),
        name="in_proj",
    )(x2, g, w)


MOBA_HEADS_PER_STEP = 4


def _rel_buckets_t():
    k = np.arange(MOBA_BLOCK)[:, None]
    q = np.arange(MOBA_BLOCK)[None, :]
    dist = np.stack([q - k, MOBA_BLOCK + q - k]).astype(np.int32)
    n = np.maximum(dist, 0)
    max_exact = N_BUCKETS // 2
    nf = np.maximum(n, 1).astype(np.float32)
    large = max_exact + (np.log(nf / np.float32(max_exact)) / np.float32(math.log(MAX_DISTANCE / max_exact))
                         * np.float32(N_BUCKETS - max_exact)).astype(np.int32)
    large = np.minimum(large, N_BUCKETS - 1)
    return np.where(n < max_exact, n, large).astype(np.int32)


def _moba_kernel(tbl_ref, bkt_ref, q_ref, k_ref, v_ref, o_ref,
                 kmean_ref, vt_ref, bias_ref, qt_ref, sel_ref, s_ref, p_ref, m_ref, l_ref, a_ref, acc_ref,
                 *, n_blocks):
    hg = pl.program_id(0)
    b = pl.program_id(1)
    qi = pl.program_id(2)
    BS = MOBA_BLOCK
    HPS = MOBA_HEADS_PER_STEP
    scale = HEAD_DIM ** -0.5
    lanes = [slice(hh * HEAD_DIM, (hh + 1) * HEAD_DIM) for hh in range(HPS)]

    @pl.when((b == 0) & (qi == 0))
    def _build_bias():
        bkt = bkt_ref[...]
        krow = lax.broadcasted_iota(jnp.int32, (BS, BS), 0)
        qcol = lax.broadcasted_iota(jnp.int32, (BS, BS), 1)
        for hh in range(HPS):
            bias = jnp.zeros(bkt.shape, F32)
            for n in range(N_BUCKETS):
                bias = jnp.where(bkt == n, tbl_ref[hg * HPS + hh, n] * LOG2E, bias)
            bias_ref[hh, 0] = jnp.where(qcol >= krow, bias[0], NEG)
            bias_ref[hh, 1] = bias[1]

    @pl.when(qi == 0)
    def _per_sequence():
        for hh in range(HPS):
            for n in range(n_blocks):
                rows = slice(n * BS, (n + 1) * BS)
                kmean_ref[hh, n:n + 1, :] = jnp.mean(k_ref[0, rows, lanes[hh]].astype(F32), axis=0, keepdims=True)
                vt_ref[hh, n] = v_ref[0, rows, lanes[hh]].astype(F32).T.astype(BF16)

    def qk_stage(hh, t):
        j0 = pl.multiple_of(jnp.maximum(qi - t, 0) * BS, BS)
        s_ref[hh] = jnp.dot(k_ref[0, pl.ds(j0, BS), lanes[hh]], qt_ref[hh], preferred_element_type=F32)

    def softmax_stage(hh, t, bias_of):
        for c in range(BS // LANES):
            qs = slice(c * LANES, (c + 1) * LANES)
            chosen = (lax.shift_right_logical(sel_ref[hh, :, qs], qi - t) & 1) == 1
            st = s_ref[hh, :, qs] * (scale * LOG2E) + bias_of(qs)
            m_prev = m_ref[hh, :, qs]
            m_new = jnp.where(chosen, jnp.maximum(m_prev, jnp.max(st, axis=0, keepdims=True)), m_prev)
            alpha = jnp.exp2(m_prev - m_new)
            p = jnp.exp2(st - jnp.where(chosen, m_new, -NEG))
            l_ref[hh, :, qs] = alpha * l_ref[hh, :, qs] + jnp.sum(p, axis=0, keepdims=True)
            m_ref[hh, :, qs] = m_new
            a_ref[hh, :, qs] = alpha
            p_ref[hh, :, qs] = p.astype(BF16)

    def pv_stage(hh, t):
        acc_ref[hh] = a_ref[hh] * acc_ref[hh] + jnp.dot(vt_ref[hh, qi - t], p_ref[hh],
                                                        preferred_element_type=F32)

    for hh in range(HPS):
        qt = q_ref[0, :, lanes[hh]].astype(F32).T.astype(BF16)
        qt_ref[hh] = qt
        gate = jnp.dot(kmean_ref[hh].astype(BF16), qt, preferred_element_type=F32)
        blk = lax.broadcasted_iota(jnp.int32, gate.shape, 0)
        blkf = blk.astype(F32)
        g = jnp.where(blk < qi, gate, NEG)
        bits = jnp.full((1, BS), lax.shift_left(jnp.int32(1), qi), jnp.int32)
        for _ in range(MOBA_TOPK):
            mx = jnp.max(g, axis=0, keepdims=True)
            ix = jnp.min(jnp.where(g == mx, blkf, float(n_blocks)), axis=0, keepdims=True).astype(jnp.int32)
            bits = bits | jnp.where(ix < qi, lax.shift_left(jnp.int32(1), ix), 0)
            g = jnp.where(blk == ix, -jnp.inf, g)
        sel_ref[hh] = bits
        m_ref[hh] = jnp.full((1, BS), NEG, F32)
        l_ref[hh] = jnp.zeros((1, BS), F32)
        acc_ref[hh] = jnp.zeros((HEAD_DIM, BS), F32)
        qk_stage(hh, 0)

    for hh in range(HPS):
        softmax_stage(hh, 0, lambda qs, hh=hh: bias_ref[hh, 0, :, qs])
    for hh in range(HPS):
        qk_stage(hh, 1)

    @pl.when(qi >= 1)
    def _past_blocks():
        for hh in range(HPS):
            pv_stage(hh, 0)
        for hh in range(HPS):
            softmax_stage(hh, 1, lambda qs, hh=hh: bias_ref[hh, 1, :, qs])
        for hh in range(HPS):
            qk_stage(hh, 2)

        def sweep(t, carry):
            for hh in range(HPS):
                pv_stage(hh, t - 1)
            for hh in range(HPS):
                far_bias = tbl_ref[hg * HPS + hh, N_BUCKETS - 1] * LOG2E
                softmax_stage(hh, t, lambda qs, far_bias=far_bias: far_bias)
            for hh in range(HPS):
                qk_stage(hh, t + 1)
            return carry

        lax.fori_loop(2, qi + 1, sweep, 0)

    for hh in range(HPS):
        pv_stage(hh, qi)
        out_t = acc_ref[hh] * (1.0 / l_ref[hh])
        o_ref[0, :, lanes[hh]] = out_t.T.astype(o_ref.dtype)


def _moba(proj3, rel_bias):
    B, S, _ = proj3.shape
    BS = MOBA_BLOCK
    HPS = MOBA_HEADS_PER_STEP
    W = HPS * HEAD_DIM
    nb = S // BS
    n_groups = N_HEADS // HPS
    bkt = jnp.asarray(_rel_buckets_t())
    tbl = rel_bias.astype(F32).T
    grid_spec = pltpu.PrefetchScalarGridSpec(
        num_scalar_prefetch=1,
        grid=(n_groups, B, nb),
        in_specs=[pl.BlockSpec((2, BS, BS), lambda h, b, i, t: (0, 0, 0)),
                  pl.BlockSpec((1, BS, W), lambda h, b, i, t: (b, i, h)),
                  pl.BlockSpec((1, S, W), lambda h, b, i, t: (b, 0, n_groups + h)),
                  pl.BlockSpec((1, S, W), lambda h, b, i, t: (b, 0, 2 * n_groups + h))],
        out_specs=pl.BlockSpec((1, BS, W), lambda h, b, i, t: (b, i, h)),
        scratch_shapes=[pltpu.VMEM((HPS, nb, HEAD_DIM), F32),
                        pltpu.VMEM((HPS, nb, HEAD_DIM, BS), BF16),
                        pltpu.VMEM((HPS, 2, BS, BS), F32),
                        pltpu.VMEM((HPS, HEAD_DIM, BS), BF16),
                        pltpu.VMEM((HPS, 1, BS), jnp.int32),
                        pltpu.VMEM((HPS, BS, BS), F32),
                        pltpu.VMEM((HPS, BS, BS), BF16),
                        pltpu.VMEM((HPS, 1, BS), F32),
                        pltpu.VMEM((HPS, 1, BS), F32),
                        pltpu.VMEM((HPS, 1, BS), F32),
                        pltpu.VMEM((HPS, HEAD_DIM, BS), F32)],
    )
    return pl.pallas_call(
        functools.partial(_moba_kernel, n_blocks=nb),
        grid_spec=grid_spec,
        out_shape=jax.ShapeDtypeStruct((B, S, N_HEADS * HEAD_DIM), BF16),
        compiler_params=_cparams("arbitrary", "arbitrary", "arbitrary"),
        name="moba",
    )(tbl, bkt, proj3, proj3, proj3)


CONV_HALO = 32
CONV_ROWS = 64
CONV_COLS = 256


def _conv_kernel(a_ref, g_ref, ah_ref, gh_ref, w_ref, cb_ref, lng_ref, lnb_ref, o_ref, u_ref, sh_ref, c_ref,
                 *, ts):
    i = pl.program_id(1)
    D = u_ref.shape[1]
    u_ref[CONV_HALO:, :] = a_ref[0].astype(F32) * _sigmoid(g_ref[0].astype(F32))
    uh = ah_ref[0].astype(F32) * _sigmoid(gh_ref[0].astype(F32))
    u_ref[:CONV_HALO, :] = jnp.where(i > 0, uh, 0.0)

    first = CONV_HALO - (CONV_KERNEL - 1)
    sh_rows = sh_ref.shape[1]

    def col_body(c, carry):
        cols = pl.ds(pl.multiple_of(c * CONV_COLS, CONV_COLS), CONV_COLS)
        for s in range(1, SUBLANES):
            sh_ref[s] = u_ref[s:s + sh_rows, cols]
        for r in range(ts // CONV_ROWS):
            acc = jnp.zeros((CONV_ROWS, CONV_COLS), F32)
            for k in range(CONV_KERNEL):
                s = (first + k) % SUBLANES
                r0 = r * CONV_ROWS + first + k - s
                tap = u_ref[r0:r0 + CONV_ROWS, cols] if s == 0 else sh_ref[s, r0:r0 + CONV_ROWS, :]
                acc = acc + jnp.tile(w_ref[k, :, cols], (CONV_ROWS // SUBLANES, 1)) * tap
            c_ref[r * CONV_ROWS:(r + 1) * CONV_ROWS, cols] = acc + cb_ref[:, cols]
        return carry

    lax.fori_loop(0, D // CONV_COLS, col_body, 0)

    y = c_ref[...]
    mu = jnp.mean(y, axis=-1, keepdims=True)
    yc = y - mu
    var = jnp.mean(yc * yc, axis=-1, keepdims=True)
    z = yc * lax.rsqrt(var + EPS) * lng_ref[...] + lnb_ref[...]
    o_ref[0] = (z * _sigmoid(z)).astype(o_ref.dtype)


def _conv(proj3, conv_w, conv_b, ln_g, ln_b, a_blk, ts=256):
    B, S, _ = proj3.shape
    D = conv_w.shape[1]
    hpb = ts // CONV_HALO
    w_rows = jnp.broadcast_to(conv_w[:, None, :], (CONV_KERNEL, SUBLANES, D))

    def halo(col):
        return pl.BlockSpec((1, CONV_HALO, D), lambda b, i: (b, jnp.maximum(i * hpb - 1, 0), col))

    vec = pl.BlockSpec((1, D), lambda b, i: (0, 0))
    return pl.pallas_call(
        functools.partial(_conv_kernel, ts=ts),
        grid=(B, S // ts),
        in_specs=[pl.BlockSpec((1, ts, D), lambda b, i: (b, i, a_blk)),
                  pl.BlockSpec((1, ts, D), lambda b, i: (b, i, a_blk + 1)),
                  halo(a_blk), halo(a_blk + 1),
                  pl.BlockSpec((CONV_KERNEL, SUBLANES, D), lambda b, i: (0, 0, 0)),
                  vec, vec, vec],
        out_specs=pl.BlockSpec((1, ts, D), lambda b, i: (b, i, 0)),
        out_shape=jax.ShapeDtypeStruct((B, S, D), BF16),
        scratch_shapes=[pltpu.VMEM((CONV_HALO + ts, D), F32),
                        pltpu.VMEM((SUBLANES, CONV_HALO + ts - SUBLANES, CONV_COLS), F32),
                        pltpu.VMEM((ts, D), F32)],
        compiler_params=_cparams("parallel", "arbitrary"),
        name="conv",
    )(proj3, proj3, proj3, proj3, w_rows, conv_b, ln_g, ln_b)


def _mix_kernel(x_ref, at_ref, cv_ref, ga_ref, gc_ref, wa_ref, wc_ref, wo_ref, o_ref):
    ya = jnp.dot(at_ref[...], wa_ref[...], preferred_element_type=F32)
    yc = jnp.dot(cv_ref[...], wc_ref[...], preferred_element_type=F32)
    mixed = (jax.nn.sigmoid(ga_ref[...].astype(F32)) * ya + jax.nn.sigmoid(gc_ref[...].astype(F32)) * yc)
    o_ref[...] = x_ref[...] + jnp.dot(mixed.astype(BF16), wo_ref[...], preferred_element_type=F32)


def _mix(x2, attn2, conv2, proj2, wa, wc, wo, gate_blk, tm=256):
    T, D = x2.shape
    row = lambda col: pl.BlockSpec((tm, D), lambda i: (i, col))
    return pl.pallas_call(
        _mix_kernel,
        grid=(T // tm,),
        in_specs=[row(0), row(0), row(0), row(gate_blk), row(gate_blk + 1),
                  _resident((D, D)), _resident((D, D)), _resident((D, D))],
        out_specs=row(0),
        out_shape=jax.ShapeDtypeStruct((T, D), F32),
        compiler_params=_cparams("parallel"),
        name="mix",
    )(x2, attn2, conv2, proj2, proj2, wa, wc, wo)


def _route(logits):
    lane = lax.broadcasted_iota(jnp.int32, logits.shape, 1)
    lanef = lane.astype(F32)
    big = float(LANES)
    is_grp = (lane >= N_EXPERTS) & (lane < N_EXPERTS + N_GROUPS)
    cl = jnp.where(is_grp, logits, -jnp.inf)
    mg = jnp.max(cl, axis=1, keepdims=True)
    g_star = jnp.min(jnp.where(cl == mg, lanef, big), axis=1, keepdims=True).astype(jnp.int32) - N_EXPERTS
    p_g = 1.0 / jnp.sum(jnp.where(is_grp, jnp.exp(cl - mg), 0.0), axis=1, keepdims=True)
    in_grp = (lane < N_EXPERTS) & ((lane // EXPERTS_PER_GROUP) == g_star)
    f1 = jnp.where(in_grp, logits, -jnp.inf)
    v1 = jnp.max(f1, axis=1, keepdims=True)
    i1 = jnp.min(jnp.where(f1 == v1, lanef, big), axis=1, keepdims=True).astype(jnp.int32)
    f2 = jnp.where(lane == i1, -jnp.inf, f1)
    v2 = jnp.max(f2, axis=1, keepdims=True)
    i2 = jnp.min(jnp.where(f2 == v2, lanef, big), axis=1, keepdims=True).astype(jnp.int32)
    e2 = jnp.exp(v2 - v1)
    w1 = 1.0 / (1.0 + e2)
    w2 = e2 / (1.0 + e2)
    return jnp.where(lane == i1, p_g * w1, 0.0) + jnp.where(lane == i2, p_g * w2, 0.0)


def _moe_kernel(h_ref, g_ref, wr_ref, br_ref, wg_ref, wu_ref, wd_ref, o_ref, hn_ref, comb_ref, acc_ref):
    e = pl.program_id(1)

    @pl.when(e == 0)
    def _():
        h = h_ref[...]
        hn = _rms(h, g_ref[...])
        hn_ref[...] = hn.astype(BF16)
        logits = jnp.dot(hn, wr_ref[...], precision=lax.Precision.HIGHEST,
                         preferred_element_type=F32) + br_ref[...]
        comb_ref[...] = _route(logits)
        acc_ref[...] = h

    hn = hn_ref[...]
    gate = jnp.dot(hn, wg_ref[0], preferred_element_type=F32)
    up = jnp.dot(hn, wu_ref[0], preferred_element_type=F32)
    lane = lax.broadcasted_iota(jnp.int32, comb_ref.shape, 1)
    c_e = jnp.sum(jnp.where(lane == e, comb_ref[...], 0.0), axis=1, keepdims=True)
    hh = (gate * jax.nn.sigmoid(gate)) * up * c_e
    acc_ref[...] += jnp.dot(hh.astype(BF16), wd_ref[0], preferred_element_type=F32)

    @pl.when(e == pl.num_programs(1) - 1)
    def _():
        o_ref[...] = acc_ref[...]


def _moe(h1, g, wr, br, wg, wu, wd, tm=512):
    T, D = h1.shape
    E, _, FF = wg.shape
    return pl.pallas_call(
        _moe_kernel,
        grid=(T // tm, E),
        in_specs=[pl.BlockSpec((tm, D), lambda i, e: (i, 0)),
                  pl.BlockSpec((1, D), lambda i, e: (0, 0)),
                  pl.BlockSpec((D, LANES), lambda i, e: (0, 0)),
                  pl.BlockSpec((1, LANES), lambda i, e: (0, 0)),
                  pl.BlockSpec((1, D, FF), lambda i, e: (e, 0, 0)),
                  pl.BlockSpec((1, D, FF), lambda i, e: (e, 0, 0)),
                  pl.BlockSpec((1, FF, D), lambda i, e: (e, 0, 0))],
        out_specs=pl.BlockSpec((tm, D), lambda i, e: (i, 0)),
        out_shape=jax.ShapeDtypeStruct((T, D), F32),
        scratch_shapes=[pltpu.VMEM((tm, D), BF16), pltpu.VMEM((tm, LANES), F32), pltpu.VMEM((tm, D), F32)],
        compiler_params=_cparams("parallel", "arbitrary"),
        name="moe",
    )(h1, g, wr, br, wg, wu, wd)


def _ple_kernel(h_ref, p_ref, gp_ref, gf_ref, wg_ref, wp_ref, o_ref):
    h = h_ref[...]
    t = _rms(h, gp_ref[...]).astype(BF16)
    gate = jax.nn.sigmoid(jnp.dot(t, wg_ref[...], preferred_element_type=F32))
    emb = jnp.dot(p_ref[...].astype(BF16), wp_ref[...], preferred_element_type=F32)
    o_ref[...] = _rms(h + emb * gate, gf_ref[...])


def _ple(h2, p2, g_ple, g_final, wg, wp, tm=512):
    T, D = h2.shape
    P = p2.shape[1]
    vec = pl.BlockSpec((1, D), lambda i: (0, 0))
    return pl.pallas_call(
        _ple_kernel,
        grid=(T // tm,),
        in_specs=[pl.BlockSpec((tm, D), lambda i: (i, 0)),
                  pl.BlockSpec((tm, P), lambda i: (i, 0)),
                  vec, vec, _resident((D, D)), _resident((P, D))],
        out_specs=pl.BlockSpec((tm, D), lambda i: (i, 0)),
        out_shape=jax.ShapeDtypeStruct((T, D), F32),
        compiler_params=_cparams("parallel"),
        name="ple",
    )(h2, p2, g_ple, g_final, wg, wp)


def kernel(x, p, g_mix, w_in, rel_bias, w_attn_br, conv_w, conv_b, ln_g, ln_b, w_conv_br, w_o, g_ffn,
           w_router_g, b_router_g, w_router_e, b_router_e, w_e_gate, w_e_up, w_e_down, g_ple, w_ple_gate,
           w_ple_proj, g_final):
    B, S, D = x.shape
    T = B * S
    depth = w_in.shape[0]
    width = N_HEADS * HEAD_DIM
    assert D == width and conv_w.shape[2] == D, "column-block indexing assumes all branch widths equal d_model"
    assert S % MOBA_BLOCK == 0
    assert depth == 1, "the final RMSNorm is fused into the last stage of a single layer"
    a_blk = 3 * width // D
    gate_blk = a_blk + 2

    h = x.reshape(T, D)
    for i in range(depth):
        proj = _in_proj(h, g_mix[i][None], w_in[i].astype(BF16))
        proj3 = proj.reshape(B, S, -1)
        attn = _moba(proj3, rel_bias)
        conv = _conv(proj3, conv_w[i], conv_b[i][None], ln_g[i][None], ln_b[i][None], a_blk)
        h = _mix(h, attn.reshape(T, D), conv.reshape(T, D), proj,
                 w_attn_br[i].astype(BF16), w_conv_br[i].astype(BF16), w_o[i].astype(BF16), gate_blk)
        pad = LANES - N_EXPERTS - N_GROUPS
        wr = jnp.concatenate([w_router_e[i], w_router_g[i], jnp.zeros((D, pad), F32)], axis=1)
        br = jnp.concatenate([b_router_e[i], b_router_g[i], jnp.zeros((pad,), F32)])[None]
        h = _moe(h, g_ffn[i][None], wr, br,
                 w_e_gate[i].astype(BF16), w_e_up[i].astype(BF16), w_e_down[i].astype(BF16))
        out = _ple(h, p[i].reshape(T, -1), g_ple[i][None], g_final[None],
                   w_ple_gate[i].astype(BF16), w_ple_proj[i].astype(BF16))
    return out.reshape(B, S, D)
```

```python
import functools
import math

import numpy as np
import jax
import jax.numpy as jnp
from jax import lax
from jax.experimental import pallas as pl
from jax.experimental.pallas import tpu as pltpu

F32 = jnp.float32
BF16 = jnp.bfloat16

N_HEADS = 16
HEAD_DIM = 128
MOBA_BLOCK = 256
MOBA_TOPK = 3
N_BUCKETS = 32
MAX_DISTANCE = 128
CONV_KERNEL = 31
N_GROUPS = 4
EXPERTS_PER_GROUP = 4
N_EXPERTS = N_GROUPS * EXPERTS_PER_GROUP
EPS = 1e-6
NEG = -1e30
LOG2E = 1.4426950408889634

V7X_VMEM_LIMIT_BYTES = 56 * 1024 * 1024
LANES = 128
SUBLANES = 8

_NT = (((1,), (1,)), ((), ()))


def _cparams(*sem):
    return pltpu.CompilerParams(dimension_semantics=sem, vmem_limit_bytes=V7X_VMEM_LIMIT_BYTES)


def _resident(shape):
    return pl.BlockSpec(shape, lambda *_: (0,) * len(shape), pipeline_mode=pl.Buffered(1))


def _sigmoid(x):
    return 0.5 * jnp.tanh(0.5 * x) + 0.5


def _rms(x, g):
    return x * lax.rsqrt(jnp.mean(x * x, axis=-1, keepdims=True) + EPS) * g


def _in_proj_kernel(x_ref, g_ref, w_ref, o_ref, xn_ref):
    @pl.when(pl.program_id(1) == 0)
    def _():
        xn_ref[...] = _rms(x_ref[...], g_ref[...]).astype(BF16)

    o_ref[...] = jnp.dot(xn_ref[...], w_ref[...], preferred_element_type=F32).astype(o_ref.dtype)


def _in_proj(x2, g, w, tm=1024, tn=1024):
    T, D = x2.shape
    N = w.shape[1]
    return pl.pallas_call(
        _in_proj_kernel,
        grid=(T // tm, N // tn),
        in_specs=[pl.BlockSpec((tm, D), lambda i, j: (i, 0)),
                  pl.BlockSpec((1, D), lambda i, j: (0, 0)),
                  pl.BlockSpec((D, tn), lambda i, j: (0, j))],
        out_specs=pl.BlockSpec((tm, tn), lambda i, j: (i, j)),
        out_shape=jax.ShapeDtypeStruct((T, N), BF16),
        scratch_shapes=[pltpu.VMEM((tm, D), BF16)],
        compiler_params=_cparams("parallel", "arbitrary"),
        name="in_proj",
    )(x2, g, w)


MOBA_HEADS_PER_STEP = 4


def _rel_buckets_t():
    k = np.arange(MOBA_BLOCK)[:, None]
    q = np.arange(MOBA_BLOCK)[None, :]
    dist = np.stack([q - k, MOBA_BLOCK + q - k]).astype(np.int32)
    n = np.maximum(dist, 0)
    max_exact = N_BUCKETS // 2
    nf = np.maximum(n, 1).astype(np.float32)
    large = max_exact + (np.log(nf / np.float32(max_exact)) / np.float32(math.log(MAX_DISTANCE / max_exact))
                         * np.float32(N_BUCKETS - max_exact)).astype(np.int32)
    large = np.minimum(large, N_BUCKETS - 1)
    return np.where(n < max_exact, n, large).astype(np.int32)


def _moba_kernel(tbl_ref, bkt_ref, q_ref, k_ref, v_ref, o_ref,
                 kmean_ref, vt_ref, bias_ref, qt_ref, sel_ref, s_ref, p_ref, m_ref, l_ref, a_ref, acc_ref,
                 *, n_blocks):
    hg = pl.program_id(0)
    b = pl.program_id(1)
    qi = pl.program_id(2)
    BS = MOBA_BLOCK
    HPS = MOBA_HEADS_PER_STEP
    scale = HEAD_DIM ** -0.5
    lanes = [slice(hh * HEAD_DIM, (hh + 1) * HEAD_DIM) for hh in range(HPS)]

    @pl.when((b == 0) & (qi == 0))
    def _build_bias():
        bkt = bkt_ref[...]
        krow = lax.broadcasted_iota(jnp.int32, (BS, BS), 0)
        qcol = lax.broadcasted_iota(jnp.int32, (BS, BS), 1)
        for hh in range(HPS):
            bias = jnp.zeros(bkt.shape, F32)
            for n in range(N_BUCKETS):
                bias = jnp.where(bkt == n, tbl_ref[hg * HPS + hh, n] * LOG2E, bias)
            bias_ref[hh, 0] = jnp.where(qcol >= krow, bias[0], NEG)
            bias_ref[hh, 1] = bias[1]

    @pl.when(qi == 0)
    def _per_sequence():
        for hh in range(HPS):
            for n in range(n_blocks):
                rows = slice(n * BS, (n + 1) * BS)
                kmean_ref[hh, n:n + 1, :] = jnp.mean(k_ref[0, rows, lanes[hh]].astype(F32), axis=0, keepdims=True)
                vt_ref[hh, n] = v_ref[0, rows, lanes[hh]].astype(F32).T.astype(BF16)

    def qk_stage(hh, t):
        j0 = pl.multiple_of(jnp.maximum(qi - t, 0) * BS, BS)
        s_ref[hh] = jnp.dot(k_ref[0, pl.ds(j0, BS), lanes[hh]], qt_ref[hh], preferred_element_type=F32)

    def softmax_stage(hh, t, bias_of):
        for c in range(BS // LANES):
            qs = slice(c * LANES, (c + 1) * LANES)
            chosen = (lax.shift_right_logical(sel_ref[hh, :, qs], qi - t) & 1) == 1
            st = s_ref[hh, :, qs] * (scale * LOG2E) + bias_of(qs)
            m_prev = m_ref[hh, :, qs]
            m_new = jnp.where(chosen, jnp.maximum(m_prev, jnp.max(st, axis=0, keepdims=True)), m_prev)
            alpha = jnp.exp2(m_prev - m_new)
            p = jnp.exp2(st - jnp.where(chosen, m_new, -NEG))
            l_ref[hh, :, qs] = alpha * l_ref[hh, :, qs] + jnp.sum(p, axis=0, keepdims=True)
            m_ref[hh, :, qs] = m_new
            a_ref[hh, :, qs] = alpha
            p_ref[hh, :, qs] = p.astype(BF16)

    def pv_stage(hh, t):
        acc_ref[hh] = a_ref[hh] * acc_ref[hh] + jnp.dot(vt_ref[hh, qi - t], p_ref[hh],
                                                        preferred_element_type=F32)

    for hh in range(HPS):
        qt = q_ref[0, :, lanes[hh]].astype(F32).T.astype(BF16)
        qt_ref[hh] = qt
        gate = jnp.dot(kmean_ref[hh].astype(BF16), qt, preferred_element_type=F32)
        blk = lax.broadcasted_iota(jnp.int32, gate.shape, 0)
        blkf = blk.astype(F32)
        g = jnp.where(blk < qi, gate, NEG)
        bits = jnp.full((1, BS), lax.shift_left(jnp.int32(1), qi), jnp.int32)
        for _ in range(MOBA_TOPK):
            mx = jnp.max(g, axis=0, keepdims=True)
            ix = jnp.min(jnp.where(g == mx, blkf, float(n_blocks)), axis=0, keepdims=True).astype(jnp.int32)
            bits = bits | jnp.where(ix < qi, lax.shift_left(jnp.int32(1), ix), 0)
            g = jnp.where(blk == ix, -jnp.inf, g)
        sel_ref[hh] = bits
        m_ref[hh] = jnp.full((1, BS), NEG, F32)
        l_ref[hh] = jnp.zeros((1, BS), F32)
        acc_ref[hh] = jnp.zeros((HEAD_DIM, BS), F32)
        qk_stage(hh, 0)

    for hh in range(HPS):
        softmax_stage(hh, 0, lambda qs, hh=hh: bias_ref[hh, 0, :, qs])
    for hh in range(HPS):
        qk_stage(hh, 1)

    @pl.when(qi >= 1)
    def _past_blocks():
        for hh in range(HPS):
            pv_stage(hh, 0)
        for hh in range(HPS):
            softmax_stage(hh, 1, lambda qs, hh=hh: bias_ref[hh, 1, :, qs])
        for hh in range(HPS):
            qk_stage(hh, 2)

        def sweep(t, carry):
            for hh in range(HPS):
                pv_stage(hh, t - 1)
            for hh in range(HPS):
                far_bias = tbl_ref[hg * HPS + hh, N_BUCKETS - 1] * LOG2E
                softmax_stage(hh, t, lambda qs, far_bias=far_bias: far_bias)
            for hh in range(HPS):
                qk_stage(hh, t + 1)
            return carry

        lax.fori_loop(2, qi + 1, sweep, 0)

    for hh in range(HPS):
        pv_stage(hh, qi)
        out_t = acc_ref[hh] * (1.0 / l_ref[hh])
        o_ref[0, :, lanes[hh]] = out_t.T.astype(o_ref.dtype)


def _moba(proj3, rel_bias):
    B, S, _ = proj3.shape
    BS = MOBA_BLOCK
    HPS = MOBA_HEADS_PER_STEP
    W = HPS * HEAD_DIM
    nb = S // BS
    n_groups = N_HEADS // HPS
    bkt = jnp.asarray(_rel_buckets_t())
    tbl = rel_bias.astype(F32).T
    grid_spec = pltpu.PrefetchScalarGridSpec(
        num_scalar_prefetch=1,
        grid=(n_groups, B, nb),
        in_specs=[pl.BlockSpec((2, BS, BS), lambda h, b, i, t: (0, 0, 0)),
                  pl.BlockSpec((1, BS, W), lambda h, b, i, t: (b, i, h)),
                  pl.BlockSpec((1, S, W), lambda h, b, i, t: (b, 0, n_groups + h)),
                  pl.BlockSpec((1, S, W), lambda h, b, i, t: (b, 0, 2 * n_groups + h))],
        out_specs=pl.BlockSpec((1, BS, W), lambda h, b, i, t: (b, i, h)),
        scratch_shapes=[pltpu.VMEM((HPS, nb, HEAD_DIM), F32),
                        pltpu.VMEM((HPS, nb, HEAD_DIM, BS), BF16),
                        pltpu.VMEM((HPS, 2, BS, BS), F32),
                        pltpu.VMEM((HPS, HEAD_DIM, BS), BF16),
                        pltpu.VMEM((HPS, 1, BS), jnp.int32),
                        pltpu.VMEM((HPS, BS, BS), F32),
                        pltpu.VMEM((HPS, BS, BS), BF16),
                        pltpu.VMEM((HPS, 1, BS), F32),
                        pltpu.VMEM((HPS, 1, BS), F32),
                        pltpu.VMEM((HPS, 1, BS), F32),
                        pltpu.VMEM((HPS, HEAD_DIM, BS), F32)],
    )
    return pl.pallas_call(
        functools.partial(_moba_kernel, n_blocks=nb),
        grid_spec=grid_spec,
        out_shape=jax.ShapeDtypeStruct((B, S, N_HEADS * HEAD_DIM), BF16),
        compiler_params=_cparams("arbitrary", "arbitrary", "arbitrary"),
        name="moba",
    )(tbl, bkt, proj3, proj3, proj3)


CONV_HALO = 32
CONV_ROWS = 64
CONV_COLS = 256


def _conv_kernel(a_ref, g_ref, ah_ref, gh_ref, w_ref, cb_ref, lng_ref, lnb_ref, o_ref, u_ref, sh_ref, c_ref,
                 *, ts):
    i = pl.program_id(1)
    D = u_ref.shape[1]
    u_ref[CONV_HALO:, :] = a_ref[0].astype(F32) * _sigmoid(g_ref[0].astype(F32))
    uh = ah_ref[0].astype(F32) * _sigmoid(gh_ref[0].astype(F32))
    u_ref[:CONV_HALO, :] = jnp.where(i > 0, uh, 0.0)

    first = CONV_HALO - (CONV_KERNEL - 1)
    sh_rows = sh_ref.shape[1]

    def col_body(c, carry):
        cols = pl.ds(pl.multiple_of(c * CONV_COLS, CONV_COLS), CONV_COLS)
        for s in range(1, SUBLANES):
            sh_ref[s] = u_ref[s:s + sh_rows, cols]
        for r in range(ts // CONV_ROWS):
            acc = jnp.zeros((CONV_ROWS, CONV_COLS), F32)
            for k in range(CONV_KERNEL):
                s = (first + k) % SUBLANES
                r0 = r * CONV_ROWS + first + k - s
                tap = u_ref[r0:r0 + CONV_ROWS, cols] if s == 0 else sh_ref[s, r0:r0 + CONV_ROWS, :]
                acc = acc + jnp.tile(w_ref[k, :, cols], (CONV_ROWS // SUBLANES, 1)) * tap
            c_ref[r * CONV_ROWS:(r + 1) * CONV_ROWS, cols] = acc + cb_ref[:, cols]
        return carry

    lax.fori_loop(0, D // CONV_COLS, col_body, 0)

    y = c_ref[...]
    mu = jnp.mean(y, axis=-1, keepdims=True)
    yc = y - mu
    var = jnp.mean(yc * yc, axis=-1, keepdims=True)
    z = yc * lax.rsqrt(var + EPS) * lng_ref[...] + lnb_ref[...]
    o_ref[0] = (z * _sigmoid(z)).astype(o_ref.dtype)


def _conv(proj3, conv_w, conv_b, ln_g, ln_b, a_blk, ts=256):
    B, S, _ = proj3.shape
    D = conv_w.shape[1]
    hpb = ts // CONV_HALO
    w_rows = jnp.broadcast_to(conv_w[:, None, :], (CONV_KERNEL, SUBLANES, D))

    def halo(col):
        return pl.BlockSpec((1, CONV_HALO, D), lambda b, i: (b, jnp.maximum(i * hpb - 1, 0), col))

    vec = pl.BlockSpec((1, D), lambda b, i: (0, 0))
    return pl.pallas_call(
        functools.partial(_conv_kernel, ts=ts),
        grid=(B, S // ts),
        in_specs=[pl.BlockSpec((1, ts, D), lambda b, i: (b, i, a_blk)),
                  pl.BlockSpec((1, ts, D), lambda b, i: (b, i, a_blk + 1)),
                  halo(a_blk), halo(a_blk + 1),
                  pl.BlockSpec((CONV_KERNEL, SUBLANES, D), lambda b, i: (0, 0, 0)),
                  vec, vec, vec],
        out_specs=pl.BlockSpec((1, ts, D), lambda b, i: (b, i, 0)),
        out_shape=jax.ShapeDtypeStruct((B, S, D), BF16),
        scratch_shapes=[pltpu.VMEM((CONV_HALO + ts, D), F32),
                        pltpu.VMEM((SUBLANES, CONV_HALO + ts - SUBLANES, CONV_COLS), F32),
                        pltpu.VMEM((ts, D), F32)],
        compiler_params=_cparams("parallel", "arbitrary"),
        name="conv",
    )(proj3, proj3, proj3, proj3, w_rows, conv_b, ln_g, ln_b)


def _mix_kernel(x_ref, at_ref, cv_ref, ga_ref, gc_ref, wa_ref, wc_ref, wo_ref, o_ref):
    ya = jnp.dot(at_ref[...], wa_ref[...], preferred_element_type=F32)
    yc = jnp.dot(cv_ref[...], wc_ref[...], preferred_element_type=F32)
    mixed = (jax.nn.sigmoid(ga_ref[...].astype(F32)) * ya + jax.nn.sigmoid(gc_ref[...].astype(F32)) * yc)
    o_ref[...] = x_ref[...] + jnp.dot(mixed.astype(BF16), wo_ref[...], preferred_element_type=F32)


def _mix(x2, attn2, conv2, proj2, wa, wc, wo, gate_blk, tm=256):
    T, D = x2.shape
    row = lambda col: pl.BlockSpec((tm, D), lambda i: (i, col))
    return pl.pallas_call(
        _mix_kernel,
        grid=(T // tm,),
        in_specs=[row(0), row(0), row(0), row(gate_blk), row(gate_blk + 1),
                  _resident((D, D)), _resident((D, D)), _resident((D, D))],
        out_specs=row(0),
        out_shape=jax.ShapeDtypeStruct((T, D), F32),
        compiler_params=_cparams("parallel"),
        name="mix",
    )(x2, attn2, conv2, proj2, proj2, wa, wc, wo)


MOE_TOK = 512
MOE_CHUNK = 16
MOE_TM = 256
MOE_TOPK = 2
_MOE_WORST_ROWS = MOE_TOPK * MOE_TOK + N_EXPERTS * (MOE_CHUNK - 1)
MOE_ROWS = -(-_MOE_WORST_ROWS // LANES) * LANES
MOE_CPT = MOE_ROWS // MOE_CHUNK
MOE_CPM = MOE_TM // MOE_CHUNK
MOE_EXTRA = LANES


def _route_t(lt):
    row = lax.broadcasted_iota(jnp.int32, lt.shape, 0)
    rowf = row.astype(F32)
    big = float(lt.shape[0])
    is_grp = (row >= N_EXPERTS) & (row < N_EXPERTS + N_GROUPS)
    cl = jnp.where(is_grp, lt, -jnp.inf)
    mg = jnp.max(cl, axis=0, keepdims=True)
    g_star = jnp.min(jnp.where(cl == mg, rowf, big), axis=0, keepdims=True).astype(jnp.int32) - N_EXPERTS
    p_g = 1.0 / jnp.sum(jnp.where(is_grp, jnp.exp(cl - mg), 0.0), axis=0, keepdims=True)
    in_grp = (row < N_EXPERTS) & ((row // EXPERTS_PER_GROUP) == g_star)
    f1 = jnp.where(in_grp, lt, -jnp.inf)
    v1 = jnp.max(f1, axis=0, keepdims=True)
    i1 = jnp.min(jnp.where(f1 == v1, rowf, big), axis=0, keepdims=True).astype(jnp.int32)
    f2 = jnp.where(row == i1, -jnp.inf, f1)
    v2 = jnp.max(f2, axis=0, keepdims=True)
    i2 = jnp.min(jnp.where(f2 == v2, rowf, big), axis=0, keepdims=True).astype(jnp.int32)
    e2 = jnp.exp(v2 - v1)
    return i1, i2, p_g / (1.0 + e2), p_g * e2 / (1.0 + e2)


def _split_bf16(x):
    hi = x.astype(BF16)
    return hi, (x - hi.astype(F32)).astype(BF16)


def _moe_sort_kernel(h_ref, g_ref, wh_ref, wl_ref, br_ref, xs_ref, dest_ref, cnt_ref):
    tok = h_ref.shape[0]
    D = h_ref.shape[1]
    hn = _rms(h_ref[...], g_ref[...])
    hn_hi, hn_lo = _split_bf16(hn)
    nt_dot = lambda a, b: lax.dot_general(a, b, _NT, preferred_element_type=F32)
    lt = (nt_dot(wh_ref[...], hn_hi) + nt_dot(wh_ref[...], hn_lo) + nt_dot(wl_ref[...], hn_hi)) + br_ref[...]
    i1, i2, c1, c2 = _route_t(lt)

    erow = lax.broadcasted_iota(jnp.int32, (N_EXPERTS, tok), 0)
    oh1 = erow == i1
    oh2 = erow == i2
    oh = (oh1 | oh2).astype(F32)
    earlier = (lax.broadcasted_iota(jnp.int32, (tok, tok), 0) < lax.broadcasted_iota(jnp.int32, (tok, tok), 1))
    cum = jnp.dot(oh.astype(BF16), earlier.astype(BF16), preferred_element_type=F32)
    cnt = jnp.sum(oh, axis=1, keepdims=True)
    n_chunks = jnp.floor((cnt + (MOE_CHUNK - 1)) * (1.0 / MOE_CHUNK))
    ncb = jnp.broadcast_to(n_chunks, (N_EXPERTS, LANES))
    erow_l = lax.broadcasted_iota(jnp.int32, (N_EXPERTS, LANES), 0)
    first_chunk = jnp.zeros((N_EXPERTS, LANES), F32)
    for e in range(N_EXPERTS - 1):
        first_chunk = first_chunk + jnp.where(erow_l > e, ncb[e:e + 1, :], 0.0)
    base = jnp.tile(first_chunk, (1, tok // LANES)) * MOE_CHUNK + cum
    d1 = jnp.sum(jnp.where(oh1, base, 0.0), axis=0, keepdims=True).astype(jnp.int32)
    d2 = jnp.sum(jnp.where(oh2, base, 0.0), axis=0, keepdims=True).astype(jnp.int32)

    srow = lax.broadcasted_iota(jnp.int32, (MOE_ROWS, tok), 0)
    p1 = srow == d1
    p2 = srow == d2
    perm1 = p1.astype(BF16)
    perm2 = p2.astype(BF16)
    xs_ref[:, :D] = jnp.dot(perm1 + perm2, hn_hi, preferred_element_type=F32).astype(BF16)
    xrow = lax.broadcasted_iota(jnp.int32, (MOE_EXTRA, tok), 0)

    def weight_rows(c):
        hi = c.astype(BF16).astype(F32)
        return jnp.where(xrow == 0, hi, jnp.where(xrow == 1, c - hi, 0.0)).astype(BF16)

    xs_ref[:, D:] = (nt_dot(perm1, weight_rows(c1)) + nt_dot(perm2, weight_rows(c2))).astype(BF16)

    r8 = lax.broadcasted_iota(jnp.int32, (SUBLANES, tok), 0)
    dest_ref[0] = jnp.where(r8 == 0, d1, jnp.where(r8 == 1, d2, 0))
    cnt_ref[0] = jnp.broadcast_to(cnt, (N_EXPERTS, LANES)).astype(jnp.int32)


MOE_ROUTE_ROWS = 32


def _moe_sort(h1, g, w_router_e, b_router_e, w_router_g, b_router_g):
    T, D = h1.shape
    nt = T // MOE_TOK
    pad = MOE_ROUTE_ROWS - N_EXPERTS - N_GROUPS
    wr = jnp.concatenate([w_router_e.T, w_router_g.T, jnp.zeros((pad, D), F32)], axis=0)
    wr_hi = wr.astype(BF16)
    wr_lo = (wr - wr_hi.astype(F32)).astype(BF16)
    br = jnp.concatenate([b_router_e, b_router_g, jnp.zeros((pad,), F32)])[:, None]
    return pl.pallas_call(
        _moe_sort_kernel,
        grid=(nt,),
        in_specs=[pl.BlockSpec((MOE_TOK, D), lambda i: (i, 0)),
                  pl.BlockSpec((1, D), lambda i: (0, 0)),
                  pl.BlockSpec((MOE_ROUTE_ROWS, D), lambda i: (0, 0)),
                  pl.BlockSpec((MOE_ROUTE_ROWS, D), lambda i: (0, 0)),
                  pl.BlockSpec((MOE_ROUTE_ROWS, 1), lambda i: (0, 0))],
        out_specs=[pl.BlockSpec((MOE_ROWS, D + MOE_EXTRA), lambda i: (i, 0)),
                   pl.BlockSpec((1, SUBLANES, MOE_TOK), lambda i: (i, 0, 0)),
                   pl.BlockSpec((1, N_EXPERTS, LANES), lambda i: (i, 0, 0))],
        out_shape=[jax.ShapeDtypeStruct((nt * MOE_ROWS, D + MOE_EXTRA), BF16),
                   jax.ShapeDtypeStruct((nt, SUBLANES, MOE_TOK), jnp.int32),
                   jax.ShapeDtypeStruct((nt, N_EXPERTS, LANES), jnp.int32)],
        compiler_params=_cparams("parallel"),
        name="moe_sort",
    )(h1, g, wr_hi, wr_lo, br)


def _moe_tables(cnt, n_mm_tiles):
    nt = cnt.shape[0]
    pc = (cnt + (MOE_CHUNK - 1)) // MOE_CHUNK
    run_end = jnp.cumsum(pc, axis=1)
    run_first = run_end - pc
    per_expert = jnp.sum(pc, axis=0)
    mm_tiles = (per_expert + (MOE_CPM - 1)) // MOE_CPM
    mm_end = jnp.cumsum(mm_tiles)
    slot_base = (mm_end - mm_tiles)[None, :] * MOE_CPM + (jnp.cumsum(pc, axis=0) - pc)
    k = jnp.arange(MOE_CPT, dtype=jnp.int32)[None, :, None]
    expert_of = jnp.sum((run_end[:, None, :] <= k).astype(jnp.int32), axis=2)
    valid = expert_of < N_EXPERTS
    e_idx = jnp.minimum(expert_of, N_EXPERTS - 1)
    take = lambda a: jnp.take_along_axis(a, e_idx, axis=1)
    slot = take(slot_base) + (k[:, :, 0] - take(run_first))
    src = (jnp.arange(nt, dtype=jnp.int32)[:, None] * MOE_CPT + k[:, :, 0]).reshape(-1)
    slot = jnp.where(valid, slot, n_mm_tiles * MOE_CPM).reshape(-1)
    gather_tbl = jnp.zeros((n_mm_tiles * MOE_CPM,), jnp.int32).at[slot].set(src, mode="drop")
    return_tbl = jnp.where(valid.reshape(-1), slot, 0).astype(jnp.int32)
    g = jnp.arange(n_mm_tiles, dtype=jnp.int32)
    n_used = mm_end[-1].astype(jnp.int32)
    tile_expert = jnp.sum((mm_end[None, :] <= jnp.minimum(g, n_used - 1)[:, None]).astype(jnp.int32), axis=1)
    return gather_tbl, return_tbl, tile_expert.astype(jnp.int32), n_used.reshape(1)


def _moe_expert_kernel(gt_ref, te_ref, nu_ref, *refs):
    chunk_refs = refs[:MOE_CPM]
    wg_ref, wu_ref, wd_ref, o_ref = refs[MOE_CPM:]
    D = o_ref.shape[1]
    used = pl.program_id(0) < nu_ref[0]

    @pl.when(used)
    def _():
        xs = jnp.concatenate([r[...] for r in chunk_refs], axis=0)
        x = xs[:, :D]
        extra = xs[:, D:].astype(F32)
        c = extra[:, 0:1] + extra[:, 1:2]
        gate = jnp.dot(x, wg_ref[0], preferred_element_type=F32)
        up = jnp.dot(x, wu_ref[0], preferred_element_type=F32)
        hh = (gate * jax.nn.sigmoid(gate)) * up * c
        o_ref[...] = jnp.dot(hh.astype(BF16), wd_ref[0], preferred_element_type=F32).astype(o_ref.dtype)

    @pl.when(jnp.logical_not(used))
    def _():
        o_ref[...] = jnp.zeros(o_ref.shape, o_ref.dtype)


def _moe_experts(xs, gather_tbl, tile_expert, n_used, wg, wu, wd, n_mm_tiles):
    W = xs.shape[1]
    E, D, FF = wg.shape

    def chunk_spec(c):
        return pl.BlockSpec((MOE_CHUNK, W), lambda g, gt, te, nu: (gt[g * MOE_CPM + c], 0))

    grid_spec = pltpu.PrefetchScalarGridSpec(
        num_scalar_prefetch=3,
        grid=(n_mm_tiles,),
        in_specs=[chunk_spec(c) for c in range(MOE_CPM)] + [
            pl.BlockSpec((1, D, FF), lambda g, gt, te, nu: (te[g], 0, 0)),
            pl.BlockSpec((1, D, FF), lambda g, gt, te, nu: (te[g], 0, 0)),
            pl.BlockSpec((1, FF, D), lambda g, gt, te, nu: (te[g], 0, 0))],
        out_specs=pl.BlockSpec((MOE_TM, D), lambda g, gt, te, nu: (g, 0)),
    )
    return pl.pallas_call(
        _moe_expert_kernel,
        grid_spec=grid_spec,
        out_shape=jax.ShapeDtypeStruct((n_mm_tiles * MOE_TM, D), BF16),
        compiler_params=_cparams("arbitrary"),
        name="moe_experts",
    )(gather_tbl, tile_expert, n_used, *([xs] * MOE_CPM), wg, wu, wd)


def _finish_kernel(rt_ref, *refs):
    chunk_refs = refs[:MOE_CPT]
    dest_ref, h_ref, p_ref, gp_ref, gf_ref, wg_ref, wp_ref, o_ref, ys_ref = refs[MOE_CPT:]
    tok = h_ref.shape[0]
    for k, r in enumerate(chunk_refs):
        ys_ref[k * MOE_CHUNK:(k + 1) * MOE_CHUNK, :] = r[...]
    dcol = dest_ref[0].astype(F32).T
    lane = lax.broadcasted_iota(jnp.int32, (tok, MOE_ROWS), 1).astype(F32)
    unsort = ((lane == dcol[:, 0:1]) | (lane == dcol[:, 1:2])).astype(BF16)
    h = h_ref[...] + jnp.dot(unsort, ys_ref[...], preferred_element_type=F32)
    t = _rms(h, gp_ref[...]).astype(BF16)
    gate = jax.nn.sigmoid(jnp.dot(t, wg_ref[...], preferred_element_type=F32))
    emb = jnp.dot(p_ref[...].astype(BF16), wp_ref[...], preferred_element_type=F32)
    o_ref[...] = _rms(h + emb * gate, gf_ref[...])


def _finish(ys, return_tbl, dest, h1, p2, g_ple, g_final, wg, wp):
    T, D = h1.shape
    P = p2.shape[1]
    nt = T // MOE_TOK

    def chunk_spec(k):
        return pl.BlockSpec((MOE_CHUNK, D), lambda i, rt: (rt[i * MOE_CPT + k], 0))

    vec = pl.BlockSpec((1, D), lambda i, rt: (0, 0))
    grid_spec = pltpu.PrefetchScalarGridSpec(
        num_scalar_prefetch=1,
        grid=(nt,),
        in_specs=[chunk_spec(k) for k in range(MOE_CPT)] + [
            pl.BlockSpec((1, SUBLANES, MOE_TOK), lambda i, rt: (i, 0, 0)),
            pl.BlockSpec((MOE_TOK, D), lambda i, rt: (i, 0)),
            pl.BlockSpec((MOE_TOK, P), lambda i, rt: (i, 0)),
            vec, vec,
            pl.BlockSpec((D, D), lambda i, rt: (0, 0), pipeline_mode=pl.Buffered(1)),
            pl.BlockSpec((P, D), lambda i, rt: (0, 0), pipeline_mode=pl.Buffered(1))],
        out_specs=pl.BlockSpec((MOE_TOK, D), lambda i, rt: (i, 0)),
        scratch_shapes=[pltpu.VMEM((MOE_ROWS, D), BF16)],
    )
    return pl.pallas_call(
        _finish_kernel,
        grid_spec=grid_spec,
        out_shape=jax.ShapeDtypeStruct((T, D), F32),
        compiler_params=_cparams("arbitrary"),
        name="finish",
    )(return_tbl, *([ys] * MOE_CPT), dest, h1, p2, g_ple, g_final, wg, wp)


def kernel(x, p, g_mix, w_in, rel_bias, w_attn_br, conv_w, conv_b, ln_g, ln_b, w_conv_br, w_o, g_ffn,
           w_router_g, b_router_g, w_router_e, b_router_e, w_e_gate, w_e_up, w_e_down, g_ple, w_ple_gate,
           w_ple_proj, g_final):
    B, S, D = x.shape
    T = B * S
    depth = w_in.shape[0]
    width = N_HEADS * HEAD_DIM
    assert D == width and conv_w.shape[2] == D, "column-block indexing assumes all branch widths equal d_model"
    assert S % MOBA_BLOCK == 0
    assert depth == 1, "the final RMSNorm is fused into the last stage of a single layer"
    a_blk = 3 * width // D
    gate_blk = a_blk + 2

    h = x.reshape(T, D)
    for i in range(depth):
        proj = _in_proj(h, g_mix[i][None], w_in[i].astype(BF16))
        proj3 = proj.reshape(B, S, -1)
        attn = _moba(proj3, rel_bias)
        conv = _conv(proj3, conv_w[i], conv_b[i][None], ln_g[i][None], ln_b[i][None], a_blk)
        h = _mix(h, attn.reshape(T, D), conv.reshape(T, D), proj,
                 w_attn_br[i].astype(BF16), w_conv_br[i].astype(BF16), w_o[i].astype(BF16), gate_blk)
        xs, dest, cnt = _moe_sort(h, g_ffn[i][None], w_router_e[i], b_router_e[i], w_router_g[i], b_router_g[i])
        n_tok_tiles = T // MOE_TOK
        worst_chunks = (MOE_TOPK * T + n_tok_tiles * N_EXPERTS * (MOE_CHUNK - 1)) // MOE_CHUNK
        n_mm_tiles = -(-worst_chunks // MOE_CPM) + N_EXPERTS
        gather_tbl, return_tbl, tile_expert, n_used = _moe_tables(cnt[:, :, 0], n_mm_tiles)
        ys = _moe_experts(xs, gather_tbl, tile_expert, n_used,
                          w_e_gate[i].astype(BF16), w_e_up[i].astype(BF16), w_e_down[i].astype(BF16), n_mm_tiles)
        out = _finish(ys, return_tbl, dest, h, p[i].reshape(T, -1), g_ple[i][None], g_final[None],
                      w_ple_gate[i].astype(BF16), w_ple_proj[i].astype(BF16))
    return out.reshape(B, S, D)
```

```python
import functools
import math

import numpy as np
import jax
import jax.numpy as jnp
from jax import lax
from jax.experimental import pallas as pl
from jax.experimental.pallas import tpu as pltpu

F32 = jnp.float32
BF16 = jnp.bfloat16

N_HEADS = 16
HEAD_DIM = 128
MOBA_BLOCK = 256
MOBA_TOPK = 3
N_BUCKETS = 32
MAX_DISTANCE = 128
CONV_KERNEL = 31
N_GROUPS = 4
EXPERTS_PER_GROUP = 4
N_EXPERTS = N_GROUPS * EXPERTS_PER_GROUP
EPS = 1e-6
NEG = -1e30
LOG2E = 1.4426950408889634

V7X_VMEM_LIMIT_BYTES = 56 * 1024 * 1024
LANES = 128
SUBLANES = 8

_NT = (((1,), (1,)), ((), ()))


def _cparams(*sem):
    return pltpu.CompilerParams(dimension_semantics=sem, vmem_limit_bytes=V7X_VMEM_LIMIT_BYTES)


def _resident(shape):
    return pl.BlockSpec(shape, lambda *_: (0,) * len(shape), pipeline_mode=pl.Buffered(1))


def _sigmoid(x):
    return 0.5 * jnp.tanh(0.5 * x) + 0.5


def _rms(x, g):
    return x * lax.rsqrt(jnp.mean(x * x, axis=-1, keepdims=True) + EPS) * g


def _in_proj_kernel(x_ref, g_ref, w_ref, o_ref, xn_ref):
    @pl.when(pl.program_id(1) == 0)
    def _():
        xn_ref[...] = _rms(x_ref[...], g_ref[...]).astype(BF16)

    o_ref[...] = jnp.dot(xn_ref[...], w_ref[...], preferred_element_type=F32).astype(o_ref.dtype)


def _in_proj(x2, g, w, tm=1024, tn=1024):
    T, D = x2.shape
    N = w.shape[1]
    return pl.pallas_call(
        _in_proj_kernel,
        grid=(T // tm, N // tn),
        in_specs=[pl.BlockSpec((tm, D), lambda i, j: (i, 0)),
                  pl.BlockSpec((1, D), lambda i, j: (0, 0)),
                  pl.BlockSpec((D, tn), lambda i, j: (0, j))],
        out_specs=pl.BlockSpec((tm, tn), lambda i, j: (i, j)),
        out_shape=jax.ShapeDtypeStruct((T, N), BF16),
        scratch_shapes=[pltpu.VMEM((tm, D), BF16)],
        compiler_params=_cparams("parallel", "arbitrary"),
        name="in_proj",
    )(x2, g, w)


MOBA_PV_ROWS = HEAD_DIM + 16
MOBA_HEADS_PER_STEP = 4


def _rel_buckets_t():
    k = np.arange(MOBA_BLOCK)[:, None]
    q = np.arange(MOBA_BLOCK)[None, :]
    dist = np.stack([q - k, MOBA_BLOCK + q - k]).astype(np.int32)
    n = np.maximum(dist, 0)
    max_exact = N_BUCKETS // 2
    nf = np.maximum(n, 1).astype(np.float32)
    large = max_exact + (np.log(nf / np.float32(max_exact)) / np.float32(math.log(MAX_DISTANCE / max_exact))
                         * np.float32(N_BUCKETS - max_exact)).astype(np.int32)
    large = np.minimum(large, N_BUCKETS - 1)
    return np.where(n < max_exact, n, large).astype(np.int32)


def _moba_kernel(tbl_ref, bkt_ref, q_ref, k_ref, v_ref, o_ref,
                 kmean_ref, vt_ref, bias_ref, qt_ref, sel_ref, s_ref, p_ref, m_ref, a_ref, acc_ref,
                 *, n_blocks):
    hg = pl.program_id(0)
    b = pl.program_id(1)
    qi = pl.program_id(2)
    BS = MOBA_BLOCK
    HPS = MOBA_HEADS_PER_STEP
    scale = HEAD_DIM ** -0.5
    lanes = [slice(hh * HEAD_DIM, (hh + 1) * HEAD_DIM) for hh in range(HPS)]

    @pl.when((b == 0) & (qi == 0))
    def _build_bias():
        krow = lax.broadcasted_iota(jnp.int32, (BS, BS), 0)
        qcol = lax.broadcasted_iota(jnp.int32, (BS, BS), 1)
        for hh in range(HPS):
            bias_ref[hh] = jnp.zeros((2, BS, BS), F32)

            def fill(n, carry):
                bias_ref[hh] = jnp.where(bkt_ref[...] == n, tbl_ref[hg * HPS + hh, n] * LOG2E, bias_ref[hh])
                return carry

            lax.fori_loop(0, N_BUCKETS, fill, 0)
            bias_ref[hh, 0] = jnp.where(qcol >= krow, bias_ref[hh, 0], NEG)

    @pl.when(qi == 0)
    def _per_sequence():
        ones_row = (lax.broadcasted_iota(jnp.int32, (MOBA_PV_ROWS - HEAD_DIM, BS), 0) == 0).astype(BF16)

        def prep(n, carry):
            rows = pl.ds(pl.multiple_of(n * BS, BS), BS)
            for hh in range(HPS):
                kmean_ref[hh, pl.ds(n, 1), :] = jnp.mean(k_ref[0, rows, lanes[hh]].astype(F32), axis=0,
                                                         keepdims=True)
                vt_ref[hh, n, :HEAD_DIM] = v_ref[0, rows, lanes[hh]].astype(F32).T.astype(BF16)
                vt_ref[hh, n, HEAD_DIM:] = ones_row
            return carry

        lax.fori_loop(0, n_blocks, prep, 0)

    def qk_stage(hh, t):
        j0 = pl.multiple_of(jnp.maximum(qi - t, 0) * BS, BS)
        s_ref[hh] = jnp.dot(k_ref[0, pl.ds(j0, BS), lanes[hh]], qt_ref[hh], preferred_element_type=F32)

    def softmax_stage(hh, t, bias_of=None, far_bias=None):
        for c in range(BS // LANES):
            qs = slice(c * LANES, (c + 1) * LANES)
            chosen = (lax.shift_right_logical(sel_ref[hh, :, qs], qi - t) & 1) == 1
            st = s_ref[hh, :, qs] * (scale * LOG2E)
            shift = 0.0 if far_bias is None else far_bias
            if bias_of is not None:
                st = st + bias_of(qs)
            m_prev = m_ref[hh, :, qs]
            m_new = jnp.where(chosen, jnp.maximum(m_prev, jnp.max(st, axis=0, keepdims=True) + shift), m_prev)
            a_ref[hh, :, qs] = jnp.exp2(m_prev - m_new)
            m_ref[hh, :, qs] = m_new
            p_ref[hh, :, qs] = jnp.exp2(st - jnp.where(chosen, m_new - shift, -NEG)).astype(BF16)

    def pv_stage(hh, t):
        acc_ref[hh] = a_ref[hh] * acc_ref[hh] + jnp.dot(vt_ref[hh, qi - t], p_ref[hh],
                                                        preferred_element_type=F32)

    for hh in range(HPS):
        qt = q_ref[0, :, lanes[hh]].astype(F32).T.astype(BF16)
        qt_ref[hh] = qt
        gate = jnp.dot(kmean_ref[hh].astype(BF16), qt, preferred_element_type=F32)
        blk = lax.broadcasted_iota(jnp.int32, gate.shape, 0)
        blkf = blk.astype(F32)
        g = jnp.where(blk < qi, gate, NEG)
        bits = jnp.full((1, BS), lax.shift_left(jnp.int32(1), qi), jnp.int32)
        for _ in range(MOBA_TOPK):
            mx = jnp.max(g, axis=0, keepdims=True)
            ix = jnp.min(jnp.where(g == mx, blkf, float(n_blocks)), axis=0, keepdims=True).astype(jnp.int32)
            bits = bits | jnp.where(ix < qi, lax.shift_left(jnp.int32(1), ix), 0)
            g = jnp.where(blk == ix, -jnp.inf, g)
        sel_ref[hh] = bits
        m_ref[hh] = jnp.full((1, BS), NEG, F32)
        acc_ref[hh] = jnp.zeros((MOBA_PV_ROWS, BS), F32)
        qk_stage(hh, 0)

    for hh in range(HPS):
        softmax_stage(hh, 0, lambda qs, hh=hh: bias_ref[hh, 0, :, qs])
    for hh in range(HPS):
        qk_stage(hh, 1)

    @pl.when(qi >= 1)
    def _past_blocks():
        for hh in range(HPS):
            pv_stage(hh, 0)
        for hh in range(HPS):
            softmax_stage(hh, 1, lambda qs, hh=hh: bias_ref[hh, 1, :, qs])
        for hh in range(HPS):
            qk_stage(hh, 2)

        def sweep(t):
            for hh in range(HPS):
                pv_stage(hh, t - 1)
            for hh in range(HPS):
                softmax_stage(hh, t, far_bias=tbl_ref[hg * HPS + hh, N_BUCKETS - 1] * LOG2E)
            for hh in range(HPS):
                qk_stage(hh, t + 1)

        n_far = qi - 1

        def sweep_pair(i, carry):
            sweep(2 + 2 * i)
            sweep(3 + 2 * i)
            return carry

        lax.fori_loop(0, n_far // 2, sweep_pair, 0)

        @pl.when(n_far % 2 == 1)
        def _odd_block():
            sweep(qi)

    for hh in range(HPS):
        pv_stage(hh, qi)
        denom = acc_ref[hh, HEAD_DIM:HEAD_DIM + 1, :]
        out_t = acc_ref[hh, :HEAD_DIM, :] * (1.0 / denom)
        o_ref[0, :, lanes[hh]] = out_t.T.astype(o_ref.dtype)


def _moba(proj3, rel_bias):
    B, S, _ = proj3.shape
    BS = MOBA_BLOCK
    HPS = MOBA_HEADS_PER_STEP
    W = HPS * HEAD_DIM
    nb = S // BS
    n_groups = N_HEADS // HPS
    bkt = jnp.asarray(_rel_buckets_t())
    tbl = rel_bias.astype(F32).T
    grid_spec = pltpu.PrefetchScalarGridSpec(
        num_scalar_prefetch=1,
        grid=(n_groups, B, nb),
        in_specs=[pl.BlockSpec((2, BS, BS), lambda h, b, i, t: (0, 0, 0)),
                  pl.BlockSpec((1, BS, W), lambda h, b, i, t: (b, i, h)),
                  pl.BlockSpec((1, S, W), lambda h, b, i, t: (b, 0, n_groups + h)),
                  pl.BlockSpec((1, S, W), lambda h, b, i, t: (b, 0, 2 * n_groups + h))],
        out_specs=pl.BlockSpec((1, BS, W), lambda h, b, i, t: (b, i, h)),
        scratch_shapes=[pltpu.VMEM((HPS, nb, HEAD_DIM), F32),
                        pltpu.VMEM((HPS, nb, MOBA_PV_ROWS, BS), BF16),
                        pltpu.VMEM((HPS, 2, BS, BS), F32),
                        pltpu.VMEM((HPS, HEAD_DIM, BS), BF16),
                        pltpu.VMEM((HPS, 1, BS), jnp.int32),
                        pltpu.VMEM((HPS, BS, BS), F32),
                        pltpu.VMEM((HPS, BS, BS), BF16),
                        pltpu.VMEM((HPS, 1, BS), F32),
                        pltpu.VMEM((HPS, 1, BS), F32),
                        pltpu.VMEM((HPS, MOBA_PV_ROWS, BS), F32)],
    )
    return pl.pallas_call(
        functools.partial(_moba_kernel, n_blocks=nb),
        grid_spec=grid_spec,
        out_shape=jax.ShapeDtypeStruct((B, S, N_HEADS * HEAD_DIM), BF16),
        compiler_params=_cparams("arbitrary", "arbitrary", "arbitrary"),
        name="moba",
    )(tbl, bkt, proj3, proj3, proj3)


CONV_HALO = 32
CONV_ROWS = 64
CONV_COLS = 256


def _conv_kernel(a_ref, g_ref, ah_ref, gh_ref, w_ref, cb_ref, lng_ref, lnb_ref, o_ref, u_ref, sh_ref, c_ref,
                 *, ts):
    i = pl.program_id(1)
    D = u_ref.shape[1]
    u_ref[CONV_HALO:, :] = a_ref[0].astype(F32) * _sigmoid(g_ref[0].astype(F32))
    uh = ah_ref[0].astype(F32) * _sigmoid(gh_ref[0].astype(F32))
    u_ref[:CONV_HALO, :] = jnp.where(i > 0, uh, 0.0)

    first = CONV_HALO - (CONV_KERNEL - 1)
    sh_rows = sh_ref.shape[1]

    def col_body(c, carry):
        cols = pl.ds(pl.multiple_of(c * CONV_COLS, CONV_COLS), CONV_COLS)
        for s in range(1, SUBLANES):
            sh_ref[s] = u_ref[s:s + sh_rows, cols]
        for r in range(ts // CONV_ROWS):
            acc = jnp.zeros((CONV_ROWS, CONV_COLS), F32)
            for k in range(CONV_KERNEL):
                s = (first + k) % SUBLANES
                r0 = r * CONV_ROWS + first + k - s
                tap = u_ref[r0:r0 + CONV_ROWS, cols] if s == 0 else sh_ref[s, r0:r0 + CONV_ROWS, :]
                acc = acc + jnp.tile(w_ref[k, :, cols], (CONV_ROWS // SUBLANES, 1)) * tap
            c_ref[r * CONV_ROWS:(r + 1) * CONV_ROWS, cols] = acc + cb_ref[:, cols]
        return carry

    lax.fori_loop(0, D // CONV_COLS, col_body, 0)

    y = c_ref[...]
    mu = jnp.mean(y, axis=-1, keepdims=True)
    yc = y - mu
    var = jnp.mean(yc * yc, axis=-1, keepdims=True)
    z = yc * lax.rsqrt(var + EPS) * lng_ref[...] + lnb_ref[...]
    o_ref[0] = (z * _sigmoid(z)).astype(o_ref.dtype)


def _conv(proj3, conv_w, conv_b, ln_g, ln_b, a_blk, ts=256):
    B, S, _ = proj3.shape
    D = conv_w.shape[1]
    hpb = ts // CONV_HALO
    w_rows = jnp.broadcast_to(conv_w[:, None, :], (CONV_KERNEL, SUBLANES, D))

    def halo(col):
        return pl.BlockSpec((1, CONV_HALO, D), lambda b, i: (b, jnp.maximum(i * hpb - 1, 0), col))

    vec = pl.BlockSpec((1, D), lambda b, i: (0, 0))
    return pl.pallas_call(
        functools.partial(_conv_kernel, ts=ts),
        grid=(B, S // ts),
        in_specs=[pl.BlockSpec((1, ts, D), lambda b, i: (b, i, a_blk)),
                  pl.BlockSpec((1, ts, D), lambda b, i: (b, i, a_blk + 1)),
                  halo(a_blk), halo(a_blk + 1),
                  pl.BlockSpec((CONV_KERNEL, SUBLANES, D), lambda b, i: (0, 0, 0)),
                  vec, vec, vec],
        out_specs=pl.BlockSpec((1, ts, D), lambda b, i: (b, i, 0)),
        out_shape=jax.ShapeDtypeStruct((B, S, D), BF16),
        scratch_shapes=[pltpu.VMEM((CONV_HALO + ts, D), F32),
                        pltpu.VMEM((SUBLANES, CONV_HALO + ts - SUBLANES, CONV_COLS), F32),
                        pltpu.VMEM((ts, D), F32)],
        compiler_params=_cparams("parallel", "arbitrary"),
        name="conv",
    )(proj3, proj3, proj3, proj3, w_rows, conv_b, ln_g, ln_b)


def _mix_kernel(x_ref, at_ref, cv_ref, ga_ref, gc_ref, wa_ref, wc_ref, wo_ref, o_ref):
    ya = jnp.dot(at_ref[...], wa_ref[...], preferred_element_type=F32)
    yc = jnp.dot(cv_ref[...], wc_ref[...], preferred_element_type=F32)
    mixed = (jax.nn.sigmoid(ga_ref[...].astype(F32)) * ya + jax.nn.sigmoid(gc_ref[...].astype(F32)) * yc)
    o_ref[...] = x_ref[...] + jnp.dot(mixed.astype(BF16), wo_ref[...], preferred_element_type=F32)


def _mix(x2, attn2, conv2, proj2, wa, wc, wo, gate_blk, tm=256):
    T, D = x2.shape
    row = lambda col: pl.BlockSpec((tm, D), lambda i: (i, col))
    return pl.pallas_call(
        _mix_kernel,
        grid=(T // tm,),
        in_specs=[row(0), row(0), row(0), row(gate_blk), row(gate_blk + 1),
                  _resident((D, D)), _resident((D, D)), _resident((D, D))],
        out_specs=row(0),
        out_shape=jax.ShapeDtypeStruct((T, D), F32),
        compiler_params=_cparams("parallel"),
        name="mix",
    )(x2, attn2, conv2, proj2, proj2, wa, wc, wo)


MOE_TOK = 512
MOE_CHUNK = 16
MOE_TM = 256
MOE_TOPK = 2
_MOE_WORST_ROWS = MOE_TOPK * MOE_TOK + N_EXPERTS * (MOE_CHUNK - 1)
MOE_ROWS = -(-_MOE_WORST_ROWS // LANES) * LANES
MOE_CPT = MOE_ROWS // MOE_CHUNK
MOE_CPM = MOE_TM // MOE_CHUNK
MOE_EXTRA = LANES


def _route_t(lt):
    row = lax.broadcasted_iota(jnp.int32, lt.shape, 0)
    rowf = row.astype(F32)
    big = float(lt.shape[0])
    is_grp = (row >= N_EXPERTS) & (row < N_EXPERTS + N_GROUPS)
    cl = jnp.where(is_grp, lt, -jnp.inf)
    mg = jnp.max(cl, axis=0, keepdims=True)
    g_star = jnp.min(jnp.where(cl == mg, rowf, big), axis=0, keepdims=True).astype(jnp.int32) - N_EXPERTS
    p_g = 1.0 / jnp.sum(jnp.where(is_grp, jnp.exp(cl - mg), 0.0), axis=0, keepdims=True)
    in_grp = (row < N_EXPERTS) & ((row // EXPERTS_PER_GROUP) == g_star)
    f1 = jnp.where(in_grp, lt, -jnp.inf)
    v1 = jnp.max(f1, axis=0, keepdims=True)
    i1 = jnp.min(jnp.where(f1 == v1, rowf, big), axis=0, keepdims=True).astype(jnp.int32)
    f2 = jnp.where(row == i1, -jnp.inf, f1)
    v2 = jnp.max(f2, axis=0, keepdims=True)
    i2 = jnp.min(jnp.where(f2 == v2, rowf, big), axis=0, keepdims=True).astype(jnp.int32)
    e2 = jnp.exp(v2 - v1)
    return i1, i2, p_g / (1.0 + e2), p_g * e2 / (1.0 + e2)


def _split_bf16(x):
    hi = x.astype(BF16)
    return hi, (x - hi.astype(F32)).astype(BF16)


def _moe_sort_kernel(h_ref, g_ref, wh_ref, wl_ref, br_ref, xs_ref, dest_ref, cnt_ref):
    tok = h_ref.shape[0]
    D = h_ref.shape[1]
    hn = _rms(h_ref[...], g_ref[...])
    hn_hi, hn_lo = _split_bf16(hn)
    nt_dot = lambda a, b: lax.dot_general(a, b, _NT, preferred_element_type=F32)
    lt = (nt_dot(wh_ref[...], hn_hi) + nt_dot(wh_ref[...], hn_lo) + nt_dot(wl_ref[...], hn_hi)) + br_ref[...]
    i1, i2, c1, c2 = _route_t(lt)

    erow = lax.broadcasted_iota(jnp.int32, (N_EXPERTS, tok), 0)
    oh1 = erow == i1
    oh2 = erow == i2
    oh = (oh1 | oh2).astype(F32)
    earlier = (lax.broadcasted_iota(jnp.int32, (tok, tok), 0) < lax.broadcasted_iota(jnp.int32, (tok, tok), 1))
    cum = jnp.dot(oh.astype(BF16), earlier.astype(BF16), preferred_element_type=F32)
    cnt = jnp.sum(oh, axis=1, keepdims=True)
    n_chunks = jnp.floor((cnt + (MOE_CHUNK - 1)) * (1.0 / MOE_CHUNK))
    ncb = jnp.broadcast_to(n_chunks, (N_EXPERTS, LANES))
    erow_l = lax.broadcasted_iota(jnp.int32, (N_EXPERTS, LANES), 0)
    first_chunk = jnp.zeros((N_EXPERTS, LANES), F32)
    for e in range(N_EXPERTS - 1):
        first_chunk = first_chunk + jnp.where(erow_l > e, ncb[e:e + 1, :], 0.0)
    base = jnp.tile(first_chunk, (1, tok // LANES)) * MOE_CHUNK + cum
    d1 = jnp.sum(jnp.where(oh1, base, 0.0), axis=0, keepdims=True).astype(jnp.int32)
    d2 = jnp.sum(jnp.where(oh2, base, 0.0), axis=0, keepdims=True).astype(jnp.int32)

    srow = lax.broadcasted_iota(jnp.int32, (MOE_ROWS, tok), 0)
    p1 = srow == d1
    p2 = srow == d2
    perm1 = p1.astype(BF16)
    perm2 = p2.astype(BF16)
    xs_ref[:, :D] = jnp.dot(perm1 + perm2, hn_hi, preferred_element_type=F32).astype(BF16)
    xrow = lax.broadcasted_iota(jnp.int32, (MOE_EXTRA, tok), 0)

    def weight_rows(c):
        hi = c.astype(BF16).astype(F32)
        return jnp.where(xrow == 0, hi, jnp.where(xrow == 1, c - hi, 0.0)).astype(BF16)

    xs_ref[:, D:] = (nt_dot(perm1, weight_rows(c1)) + nt_dot(perm2, weight_rows(c2))).astype(BF16)

    r8 = lax.broadcasted_iota(jnp.int32, (SUBLANES, tok), 0)
    dest_ref[0] = jnp.where(r8 == 0, d1, jnp.where(r8 == 1, d2, 0))
    cnt_ref[0] = jnp.broadcast_to(cnt, (N_EXPERTS, LANES)).astype(jnp.int32)


MOE_ROUTE_ROWS = 32


def _moe_sort(h1, g, w_router_e, b_router_e, w_router_g, b_router_g):
    T, D = h1.shape
    nt = T // MOE_TOK
    pad = MOE_ROUTE_ROWS - N_EXPERTS - N_GROUPS
    wr = jnp.concatenate([w_router_e.T, w_router_g.T, jnp.zeros((pad, D), F32)], axis=0)
    wr_hi = wr.astype(BF16)
    wr_lo = (wr - wr_hi.astype(F32)).astype(BF16)
    br = jnp.concatenate([b_router_e, b_router_g, jnp.zeros((pad,), F32)])[:, None]
    return pl.pallas_call(
        _moe_sort_kernel,
        grid=(nt,),
        in_specs=[pl.BlockSpec((MOE_TOK, D), lambda i: (i, 0)),
                  pl.BlockSpec((1, D), lambda i: (0, 0)),
                  pl.BlockSpec((MOE_ROUTE_ROWS, D), lambda i: (0, 0)),
                  pl.BlockSpec((MOE_ROUTE_ROWS, D), lambda i: (0, 0)),
                  pl.BlockSpec((MOE_ROUTE_ROWS, 1), lambda i: (0, 0))],
        out_specs=[pl.BlockSpec((MOE_ROWS, D + MOE_EXTRA), lambda i: (i, 0)),
                   pl.BlockSpec((1, SUBLANES, MOE_TOK), lambda i: (i, 0, 0)),
                   pl.BlockSpec((1, N_EXPERTS, LANES), lambda i: (i, 0, 0))],
        out_shape=[jax.ShapeDtypeStruct((nt * MOE_ROWS, D + MOE_EXTRA), BF16),
                   jax.ShapeDtypeStruct((nt, SUBLANES, MOE_TOK), jnp.int32),
                   jax.ShapeDtypeStruct((nt, N_EXPERTS, LANES), jnp.int32)],
        compiler_params=_cparams("parallel"),
        name="moe_sort",
    )(h1, g, wr_hi, wr_lo, br)


def _moe_tables(cnt, n_mm_tiles):
    nt = cnt.shape[0]
    pc = (cnt + (MOE_CHUNK - 1)) // MOE_CHUNK
    run_end = jnp.cumsum(pc, axis=1)
    run_first = run_end - pc
    per_expert = jnp.sum(pc, axis=0)
    mm_tiles = (per_expert + (MOE_CPM - 1)) // MOE_CPM
    mm_end = jnp.cumsum(mm_tiles)
    slot_base = (mm_end - mm_tiles)[None, :] * MOE_CPM + (jnp.cumsum(pc, axis=0) - pc)
    k = jnp.arange(MOE_CPT, dtype=jnp.int32)[None, :, None]
    expert_of = jnp.sum((run_end[:, None, :] <= k).astype(jnp.int32), axis=2)
    valid = expert_of < N_EXPERTS
    pick = expert_of[:, :, None] == jnp.arange(N_EXPERTS, dtype=jnp.int32)
    take = lambda a: jnp.sum(jnp.where(pick, a[:, None, :], 0), axis=2)
    slot = take(slot_base) + (k[:, :, 0] - take(run_first))
    src = (jnp.arange(nt, dtype=jnp.int32)[:, None] * MOE_CPT + k[:, :, 0]).reshape(-1)
    slot = jnp.where(valid, slot, n_mm_tiles * MOE_CPM).reshape(-1)
    gather_tbl = jnp.zeros((n_mm_tiles * MOE_CPM,), jnp.int32).at[slot].set(src, mode="drop")
    return_tbl = jnp.where(valid.reshape(-1), slot, 0).astype(jnp.int32)
    g = jnp.arange(n_mm_tiles, dtype=jnp.int32)
    n_used = mm_end[-1].astype(jnp.int32)
    tile_expert = jnp.sum((mm_end[None, :] <= jnp.minimum(g, n_used - 1)[:, None]).astype(jnp.int32), axis=1)
    return gather_tbl, return_tbl, tile_expert.astype(jnp.int32), n_used.reshape(1)


def _moe_expert_kernel(gt_ref, te_ref, nu_ref, *refs):
    chunk_refs = refs[:MOE_CPM]
    wg_ref, wu_ref, wd_ref, o_ref = refs[MOE_CPM:]
    D = o_ref.shape[1]
    used = pl.program_id(0) < nu_ref[0]

    @pl.when(used)
    def _():
        xs = jnp.concatenate([r[...] for r in chunk_refs], axis=0)
        x = xs[:, :D]
        extra = xs[:, D:].astype(F32)
        c = extra[:, 0:1] + extra[:, 1:2]
        gate = jnp.dot(x, wg_ref[0], preferred_element_type=F32)
        up = jnp.dot(x, wu_ref[0], preferred_element_type=F32)
        hh = (gate * jax.nn.sigmoid(gate)) * up * c
        o_ref[...] = jnp.dot(hh.astype(BF16), wd_ref[0], preferred_element_type=F32).astype(o_ref.dtype)

    @pl.when(jnp.logical_not(used))
    def _():
        o_ref[...] = jnp.zeros(o_ref.shape, o_ref.dtype)


def _moe_experts(xs, gather_tbl, tile_expert, n_used, wg, wu, wd, n_mm_tiles):
    W = xs.shape[1]
    E, D, FF = wg.shape

    def chunk_spec(c):
        return pl.BlockSpec((MOE_CHUNK, W), lambda g, gt, te, nu: (gt[g * MOE_CPM + c], 0))

    grid_spec = pltpu.PrefetchScalarGridSpec(
        num_scalar_prefetch=3,
        grid=(n_mm_tiles,),
        in_specs=[chunk_spec(c) for c in range(MOE_CPM)] + [
            pl.BlockSpec((1, D, FF), lambda g, gt, te, nu: (te[g], 0, 0)),
            pl.BlockSpec((1, D, FF), lambda g, gt, te, nu: (te[g], 0, 0)),
            pl.BlockSpec((1, FF, D), lambda g, gt, te, nu: (te[g], 0, 0))],
        out_specs=pl.BlockSpec((MOE_TM, D), lambda g, gt, te, nu: (g, 0)),
    )
    return pl.pallas_call(
        _moe_expert_kernel,
        grid_spec=grid_spec,
        out_shape=jax.ShapeDtypeStruct((n_mm_tiles * MOE_TM, D), BF16),
        compiler_params=_cparams("arbitrary"),
        name="moe_experts",
    )(gather_tbl, tile_expert, n_used, *([xs] * MOE_CPM), wg, wu, wd)


def _finish_kernel(rt_ref, *refs):
    chunk_refs = refs[:MOE_CPT]
    dest_ref, h_ref, p_ref, gp_ref, gf_ref, wg_ref, wp_ref, o_ref, ys_ref = refs[MOE_CPT:]
    tok = h_ref.shape[0]
    for k, r in enumerate(chunk_refs):
        ys_ref[k * MOE_CHUNK:(k + 1) * MOE_CHUNK, :] = r[...]
    dcol = dest_ref[0].astype(F32).T
    lane = lax.broadcasted_iota(jnp.int32, (tok, MOE_ROWS), 1).astype(F32)
    unsort = ((lane == dcol[:, 0:1]) | (lane == dcol[:, 1:2])).astype(BF16)
    h = h_ref[...] + jnp.dot(unsort, ys_ref[...], preferred_element_type=F32)
    t = _rms(h, gp_ref[...]).astype(BF16)
    gate = jax.nn.sigmoid(jnp.dot(t, wg_ref[...], preferred_element_type=F32))
    emb = jnp.dot(p_ref[...].astype(BF16), wp_ref[...], preferred_element_type=F32)
    o_ref[...] = _rms(h + emb * gate, gf_ref[...])


def _finish(ys, return_tbl, dest, h1, p2, g_ple, g_final, wg, wp):
    T, D = h1.shape
    P = p2.shape[1]
    nt = T // MOE_TOK

    def chunk_spec(k):
        return pl.BlockSpec((MOE_CHUNK, D), lambda i, rt: (rt[i * MOE_CPT + k], 0))

    vec = pl.BlockSpec((1, D), lambda i, rt: (0, 0))
    grid_spec = pltpu.PrefetchScalarGridSpec(
        num_scalar_prefetch=1,
        grid=(nt,),
        in_specs=[chunk_spec(k) for k in range(MOE_CPT)] + [
            pl.BlockSpec((1, SUBLANES, MOE_TOK), lambda i, rt: (i, 0, 0)),
            pl.BlockSpec((MOE_TOK, D), lambda i, rt: (i, 0)),
            pl.BlockSpec((MOE_TOK, P), lambda i, rt: (i, 0)),
            vec, vec,
            pl.BlockSpec((D, D), lambda i, rt: (0, 0), pipeline_mode=pl.Buffered(1)),
            pl.BlockSpec((P, D), lambda i, rt: (0, 0), pipeline_mode=pl.Buffered(1))],
        out_specs=pl.BlockSpec((MOE_TOK, D), lambda i, rt: (i, 0)),
        scratch_shapes=[pltpu.VMEM((MOE_ROWS, D), BF16)],
    )
    return pl.pallas_call(
        _finish_kernel,
        grid_spec=grid_spec,
        out_shape=jax.ShapeDtypeStruct((T, D), F32),
        compiler_params=_cparams("arbitrary"),
        name="finish",
    )(return_tbl, *([ys] * MOE_CPT), dest, h1, p2, g_ple, g_final, wg, wp)


def kernel(x, p, g_mix, w_in, rel_bias, w_attn_br, conv_w, conv_b, ln_g, ln_b, w_conv_br, w_o, g_ffn,
           w_router_g, b_router_g, w_router_e, b_router_e, w_e_gate, w_e_up, w_e_down, g_ple, w_ple_gate,
           w_ple_proj, g_final):
    B, S, D = x.shape
    T = B * S
    depth = w_in.shape[0]
    width = N_HEADS * HEAD_DIM
    assert D == width and conv_w.shape[2] == D, "column-block indexing assumes all branch widths equal d_model"
    assert S % MOBA_BLOCK == 0
    assert depth == 1, "the final RMSNorm is fused into the last stage of a single layer"
    a_blk = 3 * width // D
    gate_blk = a_blk + 2

    h = x.reshape(T, D)
    for i in range(depth):
        proj = _in_proj(h, g_mix[i][None], w_in[i].astype(BF16))
        proj3 = proj.reshape(B, S, -1)
        attn = _moba(proj3, rel_bias)
        conv = _conv(proj3, conv_w[i], conv_b[i][None], ln_g[i][None], ln_b[i][None], a_blk)
        h = _mix(h, attn.reshape(T, D), conv.reshape(T, D), proj,
                 w_attn_br[i].astype(BF16), w_conv_br[i].astype(BF16), w_o[i].astype(BF16), gate_blk)
        xs, dest, cnt = _moe_sort(h, g_ffn[i][None], w_router_e[i], b_router_e[i], w_router_g[i], b_router_g[i])
        n_tok_tiles = T // MOE_TOK
        worst_chunks = (MOE_TOPK * T + n_tok_tiles * N_EXPERTS * (MOE_CHUNK - 1)) // MOE_CHUNK
        n_mm_tiles = -(-worst_chunks // MOE_CPM) + N_EXPERTS
        gather_tbl, return_tbl, tile_expert, n_used = _moe_tables(cnt[:, :, 0], n_mm_tiles)
        ys = _moe_experts(xs, gather_tbl, tile_expert, n_used,
                          w_e_gate[i].astype(BF16), w_e_up[i].astype(BF16), w_e_down[i].astype(BF16), n_mm_tiles)
        out = _finish(ys, return_tbl, dest, h, p[i].reshape(T, -1), g_ple[i][None], g_final[None],
                      w_ple_gate[i].astype(BF16), w_ple_proj[i].astype(BF16))
    return out.reshape(B, S, D)
```

```python
import functools
import math

import numpy as np
import jax
import jax.numpy as jnp
from jax import lax
from jax.experimental import pallas as pl
from jax.experimental.pallas import tpu as pltpu

F32 = jnp.float32
BF16 = jnp.bfloat16

N_HEADS = 16
HEAD_DIM = 128
MOBA_BLOCK = 256
MOBA_TOPK = 3
N_BUCKETS = 32
MAX_DISTANCE = 128
CONV_KERNEL = 31
N_GROUPS = 4
EXPERTS_PER_GROUP = 4
N_EXPERTS = N_GROUPS * EXPERTS_PER_GROUP
EPS = 1e-6
NEG = -1e30
LOG2E = 1.4426950408889634

V7X_VMEM_LIMIT_BYTES = 56 * 1024 * 1024
LANES = 128
SUBLANES = 8

_NT = (((1,), (1,)), ((), ()))


def _cparams(*sem):
    return pltpu.CompilerParams(dimension_semantics=sem, vmem_limit_bytes=V7X_VMEM_LIMIT_BYTES)


def _resident(shape):
    return pl.BlockSpec(shape, lambda *_: (0,) * len(shape), pipeline_mode=pl.Buffered(1))


def _sigmoid(x):
    return 0.5 * jnp.tanh(0.5 * x) + 0.5


def _rms(x, g):
    return x * lax.rsqrt(jnp.mean(x * x, axis=-1, keepdims=True) + EPS) * g


def _in_proj_kernel(x_ref, g_ref, w_ref, o_ref, xn_ref):
    @pl.when(pl.program_id(1) == 0)
    def _():
        xn_ref[...] = _rms(x_ref[...], g_ref[...]).astype(BF16)

    o_ref[...] = jnp.dot(xn_ref[...], w_ref[...], preferred_element_type=F32).astype(o_ref.dtype)


def _in_proj(x2, g, w, tm=1024, tn=1024):
    T, D = x2.shape
    N = w.shape[1]
    return pl.pallas_call(
        _in_proj_kernel,
        grid=(T // tm, N // tn),
        in_specs=[pl.BlockSpec((tm, D), lambda i, j: (i, 0)),
                  pl.BlockSpec((1, D), lambda i, j: (0, 0)),
                  pl.BlockSpec((D, tn), lambda i, j: (0, j))],
        out_specs=pl.BlockSpec((tm, tn), lambda i, j: (i, j)),
        out_shape=jax.ShapeDtypeStruct((T, N), BF16),
        scratch_shapes=[pltpu.VMEM((tm, D), BF16)],
        compiler_params=_cparams("parallel", "arbitrary"),
        name="in_proj",
    )(x2, g, w)


MOBA_PV_ROWS = HEAD_DIM + 16
MOBA_HEADS_PER_STEP = 4


def _rel_buckets_t():
    k = np.arange(MOBA_BLOCK)[:, None]
    q = np.arange(MOBA_BLOCK)[None, :]
    dist = np.stack([q - k, MOBA_BLOCK + q - k]).astype(np.int32)
    n = np.maximum(dist, 0)
    max_exact = N_BUCKETS // 2
    nf = np.maximum(n, 1).astype(np.float32)
    large = max_exact + (np.log(nf / np.float32(max_exact)) / np.float32(math.log(MAX_DISTANCE / max_exact))
                         * np.float32(N_BUCKETS - max_exact)).astype(np.int32)
    large = np.minimum(large, N_BUCKETS - 1)
    return np.where(n < max_exact, n, large).astype(np.int32)


def _moba_kernel(tbl_ref, bkt_ref, q_ref, k_ref, v_ref, o_ref,
                 kmean_ref, vt_ref, bias_ref, qt_ref, sel_ref, s_ref, p_ref, m_ref, a_ref, acc_ref,
                 *, n_blocks):
    hg = pl.program_id(0)
    b = pl.program_id(1)
    qi = pl.program_id(2)
    BS = MOBA_BLOCK
    HPS = MOBA_HEADS_PER_STEP
    scale = HEAD_DIM ** -0.5
    lanes = [slice(hh * HEAD_DIM, (hh + 1) * HEAD_DIM) for hh in range(HPS)]

    @pl.when((b == 0) & (qi == 0))
    def _build_bias():
        krow = lax.broadcasted_iota(jnp.int32, (BS, BS), 0)
        qcol = lax.broadcasted_iota(jnp.int32, (BS, BS), 1)
        for hh in range(HPS):
            bias_ref[hh] = jnp.zeros((2, BS, BS), F32)

            def fill(n, carry):
                bias_ref[hh] = jnp.where(bkt_ref[...] == n, tbl_ref[hg * HPS + hh, n] * LOG2E, bias_ref[hh])
                return carry

            lax.fori_loop(0, N_BUCKETS, fill, 0)
            bias_ref[hh, 0] = jnp.where(qcol >= krow, bias_ref[hh, 0], NEG)

    @pl.when(qi == 0)
    def _per_sequence():
        ones_row = (lax.broadcasted_iota(jnp.int32, (MOBA_PV_ROWS - HEAD_DIM, BS), 0) == 0).astype(BF16)

        def prep(n, carry):
            rows = pl.ds(pl.multiple_of(n * BS, BS), BS)
            for hh in range(HPS):
                kmean_ref[hh, pl.ds(n, 1), :] = jnp.mean(k_ref[0, rows, lanes[hh]].astype(F32), axis=0,
                                                         keepdims=True)
                vt_ref[hh, n, :HEAD_DIM] = v_ref[0, rows, lanes[hh]].astype(F32).T.astype(BF16)
                vt_ref[hh, n, HEAD_DIM:] = ones_row
            return carry

        lax.fori_loop(0, n_blocks, prep, 0)

    def qk_stage(hh, t):
        j0 = pl.multiple_of(jnp.maximum(qi - t, 0) * BS, BS)
        s_ref[hh] = jnp.dot(k_ref[0, pl.ds(j0, BS), lanes[hh]], qt_ref[hh], preferred_element_type=F32)

    def softmax_stage(hh, t, bias_of=None, far_bias=None):
        for c in range(BS // LANES):
            qs = slice(c * LANES, (c + 1) * LANES)
            chosen = (lax.shift_right_logical(sel_ref[hh, :, qs], qi - t) & 1) == 1
            st = s_ref[hh, :, qs] * (scale * LOG2E)
            shift = 0.0 if far_bias is None else far_bias
            if bias_of is not None:
                st = st + bias_of(qs)
            m_prev = m_ref[hh, :, qs]
            m_new = jnp.where(chosen, jnp.maximum(m_prev, jnp.max(st, axis=0, keepdims=True) + shift), m_prev)
            a_ref[hh, :, qs] = jnp.exp2(m_prev - m_new)
            m_ref[hh, :, qs] = m_new
            p_ref[hh, :, qs] = jnp.exp2(st - jnp.where(chosen, m_new - shift, -NEG)).astype(BF16)

    def pv_stage(hh, t):
        acc_ref[hh] = a_ref[hh] * acc_ref[hh] + jnp.dot(vt_ref[hh, qi - t], p_ref[hh],
                                                        preferred_element_type=F32)

    for hh in range(HPS):
        qt = q_ref[0, :, lanes[hh]].astype(F32).T.astype(BF16)
        qt_ref[hh] = qt
        gate = jnp.dot(kmean_ref[hh].astype(BF16), qt, preferred_element_type=F32)
        blk = lax.broadcasted_iota(jnp.int32, gate.shape, 0)
        blkf = blk.astype(F32)
        g = jnp.where(blk < qi, gate, NEG)
        bits = jnp.full((1, BS), lax.shift_left(jnp.int32(1), qi), jnp.int32)
        for _ in range(MOBA_TOPK):
            mx = jnp.max(g, axis=0, keepdims=True)
            ix = jnp.min(jnp.where(g == mx, blkf, float(n_blocks)), axis=0, keepdims=True).astype(jnp.int32)
            bits = bits | jnp.where(ix < qi, lax.shift_left(jnp.int32(1), ix), 0)
            g = jnp.where(blk == ix, -jnp.inf, g)
        sel_ref[hh] = bits
        m_ref[hh] = jnp.full((1, BS), NEG, F32)
        acc_ref[hh] = jnp.zeros((MOBA_PV_ROWS, BS), F32)
        qk_stage(hh, 0)

    for hh in range(HPS):
        softmax_stage(hh, 0, lambda qs, hh=hh: bias_ref[hh, 0, :, qs])
    for hh in range(HPS):
        qk_stage(hh, 1)

    @pl.when(qi >= 1)
    def _past_blocks():
        for hh in range(HPS):
            pv_stage(hh, 0)
        for hh in range(HPS):
            softmax_stage(hh, 1, lambda qs, hh=hh: bias_ref[hh, 1, :, qs])
        for hh in range(HPS):
            qk_stage(hh, 2)

        def sweep(t):
            for hh in range(HPS):
                pv_stage(hh, t - 1)
            for hh in range(HPS):
                softmax_stage(hh, t, far_bias=tbl_ref[hg * HPS + hh, N_BUCKETS - 1] * LOG2E)
            for hh in range(HPS):
                qk_stage(hh, t + 1)

        n_far = qi - 1

        def sweep_pair(i, carry):
            sweep(2 + 2 * i)
            sweep(3 + 2 * i)
            return carry

        lax.fori_loop(0, n_far // 2, sweep_pair, 0)

        @pl.when(n_far % 2 == 1)
        def _odd_block():
            sweep(qi)

    for hh in range(HPS):
        pv_stage(hh, qi)
        denom = acc_ref[hh, HEAD_DIM:HEAD_DIM + 1, :]
        out_t = acc_ref[hh, :HEAD_DIM, :] * (1.0 / denom)
        o_ref[0, :, lanes[hh]] = out_t.T.astype(o_ref.dtype)


def _moba(proj3, rel_bias):
    B, S, _ = proj3.shape
    BS = MOBA_BLOCK
    HPS = MOBA_HEADS_PER_STEP
    W = HPS * HEAD_DIM
    nb = S // BS
    n_groups = N_HEADS // HPS
    bkt = jnp.asarray(_rel_buckets_t())
    tbl = rel_bias.astype(F32).T
    grid_spec = pltpu.PrefetchScalarGridSpec(
        num_scalar_prefetch=1,
        grid=(n_groups, B, nb),
        in_specs=[pl.BlockSpec((2, BS, BS), lambda h, b, i, t: (0, 0, 0)),
                  pl.BlockSpec((1, BS, W), lambda h, b, i, t: (b, i, h)),
                  pl.BlockSpec((1, S, W), lambda h, b, i, t: (b, 0, n_groups + h)),
                  pl.BlockSpec((1, S, W), lambda h, b, i, t: (b, 0, 2 * n_groups + h))],
        out_specs=pl.BlockSpec((1, BS, W), lambda h, b, i, t: (b, i, h)),
        scratch_shapes=[pltpu.VMEM((HPS, nb, HEAD_DIM), F32),
                        pltpu.VMEM((HPS, nb, MOBA_PV_ROWS, BS), BF16),
                        pltpu.VMEM((HPS, 2, BS, BS), F32),
                        pltpu.VMEM((HPS, HEAD_DIM, BS), BF16),
                        pltpu.VMEM((HPS, 1, BS), jnp.int32),
                        pltpu.VMEM((HPS, BS, BS), F32),
                        pltpu.VMEM((HPS, BS, BS), BF16),
                        pltpu.VMEM((HPS, 1, BS), F32),
                        pltpu.VMEM((HPS, 1, BS), F32),
                        pltpu.VMEM((HPS, MOBA_PV_ROWS, BS), F32)],
    )
    return pl.pallas_call(
        functools.partial(_moba_kernel, n_blocks=nb),
        grid_spec=grid_spec,
        out_shape=jax.ShapeDtypeStruct((B, S, N_HEADS * HEAD_DIM), BF16),
        compiler_params=_cparams("arbitrary", "arbitrary", "arbitrary"),
        name="moba",
    )(tbl, bkt, proj3, proj3, proj3)


CONV_HALO = 32
CONV_ROWS = 64
CONV_COLS = 256


def _conv_kernel(a_ref, g_ref, ah_ref, gh_ref, w_ref, cb_ref, lng_ref, lnb_ref, o_ref, u_ref, sh_ref, c_ref,
                 *, ts):
    i = pl.program_id(1)
    D = u_ref.shape[1]
    u_ref[CONV_HALO:, :] = a_ref[0].astype(F32) * _sigmoid(g_ref[0].astype(F32))
    uh = ah_ref[0].astype(F32) * _sigmoid(gh_ref[0].astype(F32))
    u_ref[:CONV_HALO, :] = jnp.where(i > 0, uh, 0.0)

    first = CONV_HALO - (CONV_KERNEL - 1)
    sh_rows = sh_ref.shape[1]

    def col_body(c, carry):
        cols = pl.ds(pl.multiple_of(c * CONV_COLS, CONV_COLS), CONV_COLS)
        for s in range(1, SUBLANES):
            sh_ref[s] = u_ref[s:s + sh_rows, cols]
        for r in range(ts // CONV_ROWS):
            acc = jnp.zeros((CONV_ROWS, CONV_COLS), F32)
            for k in range(CONV_KERNEL):
                s = (first + k) % SUBLANES
                r0 = r * CONV_ROWS + first + k - s
                tap = u_ref[r0:r0 + CONV_ROWS, cols] if s == 0 else sh_ref[s, r0:r0 + CONV_ROWS, :]
                acc = acc + jnp.tile(w_ref[k, :, cols], (CONV_ROWS // SUBLANES, 1)) * tap
            c_ref[r * CONV_ROWS:(r + 1) * CONV_ROWS, cols] = acc + cb_ref[:, cols]
        return carry

    lax.fori_loop(0, D // CONV_COLS, col_body, 0)

    y = c_ref[...]
    mu = jnp.mean(y, axis=-1, keepdims=True)
    yc = y - mu
    var = jnp.mean(yc * yc, axis=-1, keepdims=True)
    z = yc * lax.rsqrt(var + EPS) * lng_ref[...] + lnb_ref[...]
    o_ref[0] = (z * _sigmoid(z)).astype(o_ref.dtype)


def _conv(proj3, conv_w, conv_b, ln_g, ln_b, a_blk, ts=256):
    B, S, _ = proj3.shape
    D = conv_w.shape[1]
    hpb = ts // CONV_HALO
    w_rows = jnp.broadcast_to(conv_w[:, None, :], (CONV_KERNEL, SUBLANES, D))

    def halo(col):
        return pl.BlockSpec((1, CONV_HALO, D), lambda b, i: (b, jnp.maximum(i * hpb - 1, 0), col))

    vec = pl.BlockSpec((1, D), lambda b, i: (0, 0))
    return pl.pallas_call(
        functools.partial(_conv_kernel, ts=ts),
        grid=(B, S // ts),
        in_specs=[pl.BlockSpec((1, ts, D), lambda b, i: (b, i, a_blk)),
                  pl.BlockSpec((1, ts, D), lambda b, i: (b, i, a_blk + 1)),
                  halo(a_blk), halo(a_blk + 1),
                  pl.BlockSpec((CONV_KERNEL, SUBLANES, D), lambda b, i: (0, 0, 0)),
                  vec, vec, vec],
        out_specs=pl.BlockSpec((1, ts, D), lambda b, i: (b, i, 0)),
        out_shape=jax.ShapeDtypeStruct((B, S, D), BF16),
        scratch_shapes=[pltpu.VMEM((CONV_HALO + ts, D), F32),
                        pltpu.VMEM((SUBLANES, CONV_HALO + ts - SUBLANES, CONV_COLS), F32),
                        pltpu.VMEM((ts, D), F32)],
        compiler_params=_cparams("parallel", "arbitrary"),
        name="conv",
    )(proj3, proj3, proj3, proj3, w_rows, conv_b, ln_g, ln_b)


def _mix_kernel(x_ref, at_ref, cv_ref, ga_ref, gc_ref, wa_ref, wc_ref, wo_ref, o_ref):
    ya = jnp.dot(at_ref[...], wa_ref[...], preferred_element_type=F32)
    yc = jnp.dot(cv_ref[...], wc_ref[...], preferred_element_type=F32)
    mixed = (jax.nn.sigmoid(ga_ref[...].astype(F32)) * ya + jax.nn.sigmoid(gc_ref[...].astype(F32)) * yc)
    o_ref[...] = x_ref[...] + jnp.dot(mixed.astype(BF16), wo_ref[...], preferred_element_type=F32)


def _mix(x2, attn2, conv2, proj2, wa, wc, wo, gate_blk, tm=256):
    T, D = x2.shape
    row = lambda col: pl.BlockSpec((tm, D), lambda i: (i, col))
    return pl.pallas_call(
        _mix_kernel,
        grid=(T // tm,),
        in_specs=[row(0), row(0), row(0), row(gate_blk), row(gate_blk + 1),
                  _resident((D, D)), _resident((D, D)), _resident((D, D))],
        out_specs=row(0),
        out_shape=jax.ShapeDtypeStruct((T, D), F32),
        compiler_params=_cparams("parallel"),
        name="mix",
    )(x2, attn2, conv2, proj2, proj2, wa, wc, wo)


MOE_TOK = 512
MOE_CHUNK = 16
MOE_TM = 256
MOE_TOPK = 2
_MOE_WORST_ROWS = MOE_TOPK * MOE_TOK + N_EXPERTS * (MOE_CHUNK - 1)
MOE_ROWS = -(-_MOE_WORST_ROWS // LANES) * LANES
MOE_CPT = MOE_ROWS // MOE_CHUNK
MOE_CPM = MOE_TM // MOE_CHUNK
MOE_EXTRA = LANES


def _route_t(lt):
    row = lax.broadcasted_iota(jnp.int32, lt.shape, 0)
    rowf = row.astype(F32)
    big = float(lt.shape[0])
    is_grp = (row >= N_EXPERTS) & (row < N_EXPERTS + N_GROUPS)
    cl = jnp.where(is_grp, lt, -jnp.inf)
    mg = jnp.max(cl, axis=0, keepdims=True)
    g_star = jnp.min(jnp.where(cl == mg, rowf, big), axis=0, keepdims=True).astype(jnp.int32) - N_EXPERTS
    p_g = 1.0 / jnp.sum(jnp.where(is_grp, jnp.exp(cl - mg), 0.0), axis=0, keepdims=True)
    in_grp = (row < N_EXPERTS) & ((row // EXPERTS_PER_GROUP) == g_star)
    f1 = jnp.where(in_grp, lt, -jnp.inf)
    v1 = jnp.max(f1, axis=0, keepdims=True)
    i1 = jnp.min(jnp.where(f1 == v1, rowf, big), axis=0, keepdims=True).astype(jnp.int32)
    f2 = jnp.where(row == i1, -jnp.inf, f1)
    v2 = jnp.max(f2, axis=0, keepdims=True)
    i2 = jnp.min(jnp.where(f2 == v2, rowf, big), axis=0, keepdims=True).astype(jnp.int32)
    e2 = jnp.exp(v2 - v1)
    return i1, i2, p_g / (1.0 + e2), p_g * e2 / (1.0 + e2)


def _split_bf16(x):
    hi = x.astype(BF16)
    return hi, (x - hi.astype(F32)).astype(BF16)


def _moe_sort_kernel(h_ref, g_ref, wh_ref, wl_ref, br_ref, xs_ref, dest_ref, cnt_ref):
    tok = h_ref.shape[0]
    D = h_ref.shape[1]
    hn = _rms(h_ref[...], g_ref[...])
    hn_hi, hn_lo = _split_bf16(hn)
    nt_dot = lambda a, b: lax.dot_general(a, b, _NT, preferred_element_type=F32)
    w_hi = wh_ref[...]
    both = nt_dot(jnp.concatenate([w_hi, wl_ref[...]], axis=0), hn_hi)
    lt = both[:MOE_ROUTE_ROWS] + both[MOE_ROUTE_ROWS:] + nt_dot(w_hi, hn_lo) + br_ref[...]
    i1, i2, c1, c2 = _route_t(lt)

    erow = lax.broadcasted_iota(jnp.int32, (N_EXPERTS, tok), 0)
    oh1 = erow == i1
    oh2 = erow == i2
    oh = (oh1 | oh2).astype(F32)
    earlier = (lax.broadcasted_iota(jnp.int32, (tok, tok), 0) < lax.broadcasted_iota(jnp.int32, (tok, tok), 1))
    cum = jnp.dot(oh.astype(BF16), earlier.astype(BF16), preferred_element_type=F32)
    cnt = jnp.sum(oh, axis=1, keepdims=True)
    n_chunks = jnp.floor((cnt + (MOE_CHUNK - 1)) * (1.0 / MOE_CHUNK))
    ncb = jnp.broadcast_to(n_chunks, (N_EXPERTS, LANES))
    erow_l = lax.broadcasted_iota(jnp.int32, (N_EXPERTS, LANES), 0)
    first_chunk = jnp.zeros((N_EXPERTS, LANES), F32)
    for e in range(N_EXPERTS - 1):
        first_chunk = first_chunk + jnp.where(erow_l > e, ncb[e:e + 1, :], 0.0)
    base = jnp.tile(first_chunk, (1, tok // LANES)) * MOE_CHUNK + cum
    d1 = jnp.sum(jnp.where(oh1, base, 0.0), axis=0, keepdims=True).astype(jnp.int32)
    d2 = jnp.sum(jnp.where(oh2, base, 0.0), axis=0, keepdims=True).astype(jnp.int32)

    srow = lax.broadcasted_iota(jnp.int32, (MOE_ROWS, tok), 0)
    p1 = srow == d1
    p2 = srow == d2
    perm1 = p1.astype(BF16)
    perm2 = p2.astype(BF16)
    xs_ref[:, :D] = jnp.dot(perm1 + perm2, hn_hi, preferred_element_type=F32).astype(BF16)
    xrow = lax.broadcasted_iota(jnp.int32, (MOE_EXTRA, tok), 0)

    def weight_rows(c):
        hi = c.astype(BF16).astype(F32)
        return jnp.where(xrow == 0, hi, jnp.where(xrow == 1, c - hi, 0.0)).astype(BF16)

    xs_ref[:, D:] = (nt_dot(perm1, weight_rows(c1)) + nt_dot(perm2, weight_rows(c2))).astype(BF16)

    r8 = lax.broadcasted_iota(jnp.int32, (SUBLANES, tok), 0)
    dest_ref[0] = jnp.where(r8 == 0, d1, jnp.where(r8 == 1, d2, 0))
    cnt_ref[0] = jnp.broadcast_to(cnt, (N_EXPERTS, LANES)).astype(jnp.int32)


MOE_ROUTE_ROWS = 32


def _moe_sort(h1, g, w_router_e, b_router_e, w_router_g, b_router_g):
    T, D = h1.shape
    nt = T // MOE_TOK
    pad = MOE_ROUTE_ROWS - N_EXPERTS - N_GROUPS
    wr = jnp.concatenate([w_router_e.T, w_router_g.T, jnp.zeros((pad, D), F32)], axis=0)
    wr_hi = wr.astype(BF16)
    wr_lo = (wr - wr_hi.astype(F32)).astype(BF16)
    br = jnp.concatenate([b_router_e, b_router_g, jnp.zeros((pad,), F32)])[:, None]
    return pl.pallas_call(
        _moe_sort_kernel,
        grid=(nt,),
        in_specs=[pl.BlockSpec((MOE_TOK, D), lambda i: (i, 0)),
                  pl.BlockSpec((1, D), lambda i: (0, 0)),
                  pl.BlockSpec((MOE_ROUTE_ROWS, D), lambda i: (0, 0)),
                  pl.BlockSpec((MOE_ROUTE_ROWS, D), lambda i: (0, 0)),
                  pl.BlockSpec((MOE_ROUTE_ROWS, 1), lambda i: (0, 0))],
        out_specs=[pl.BlockSpec((MOE_ROWS, D + MOE_EXTRA), lambda i: (i, 0)),
                   pl.BlockSpec((1, SUBLANES, MOE_TOK), lambda i: (i, 0, 0)),
                   pl.BlockSpec((1, N_EXPERTS, LANES), lambda i: (i, 0, 0))],
        out_shape=[jax.ShapeDtypeStruct((nt * MOE_ROWS, D + MOE_EXTRA), BF16),
                   jax.ShapeDtypeStruct((nt, SUBLANES, MOE_TOK), jnp.int32),
                   jax.ShapeDtypeStruct((nt, N_EXPERTS, LANES), jnp.int32)],
        compiler_params=_cparams("parallel"),
        name="moe_sort",
    )(h1, g, wr_hi, wr_lo, br)


def _moe_tables(cnt, n_mm_tiles):
    nt = cnt.shape[0]
    pc = (cnt + (MOE_CHUNK - 1)) // MOE_CHUNK
    run_end = jnp.cumsum(pc, axis=1)
    run_first = run_end - pc
    per_expert = jnp.sum(pc, axis=0)
    mm_tiles = (per_expert + (MOE_CPM - 1)) // MOE_CPM
    mm_end = jnp.cumsum(mm_tiles)
    slot_base = (mm_end - mm_tiles)[None, :] * MOE_CPM + (jnp.cumsum(pc, axis=0) - pc)
    k = jnp.arange(MOE_CPT, dtype=jnp.int32)[None, :, None]
    expert_of = jnp.sum((run_end[:, None, :] <= k).astype(jnp.int32), axis=2)
    valid = expert_of < N_EXPERTS
    pick = expert_of[:, :, None] == jnp.arange(N_EXPERTS, dtype=jnp.int32)
    take = lambda a: jnp.sum(jnp.where(pick, a[:, None, :], 0), axis=2)
    slot = take(slot_base) + (k[:, :, 0] - take(run_first))
    src = (jnp.arange(nt, dtype=jnp.int32)[:, None] * MOE_CPT + k[:, :, 0]).reshape(-1)
    slot = jnp.where(valid, slot, n_mm_tiles * MOE_CPM).reshape(-1)
    gather_tbl = jnp.zeros((n_mm_tiles * MOE_CPM,), jnp.int32).at[slot].set(src, mode="drop")
    return_tbl = jnp.where(valid.reshape(-1), slot, 0).astype(jnp.int32)
    g = jnp.arange(n_mm_tiles, dtype=jnp.int32)
    n_used = mm_end[-1].astype(jnp.int32)
    tile_expert = jnp.sum((mm_end[None, :] <= jnp.minimum(g, n_used - 1)[:, None]).astype(jnp.int32), axis=1)
    return gather_tbl, return_tbl, tile_expert.astype(jnp.int32), n_used.reshape(1)


def _moe_expert_kernel(gt_ref, te_ref, nu_ref, *refs):
    chunk_refs = refs[:MOE_CPM]
    wg_ref, wu_ref, wd_ref, o_ref, wg_bf, wu_bf, wd_bf = refs[MOE_CPM:]
    D = o_ref.shape[1]
    g = pl.program_id(0)
    used = g < nu_ref[0]

    @pl.when(used & ((g == 0) | (te_ref[g] != te_ref[jnp.maximum(g - 1, 0)])))
    def _():
        wg_bf[...] = wg_ref[0].astype(BF16)
        wu_bf[...] = wu_ref[0].astype(BF16)
        wd_bf[...] = wd_ref[0].astype(BF16)

    @pl.when(used)
    def _():
        xs = jnp.concatenate([r[...] for r in chunk_refs], axis=0)
        x = xs[:, :D]
        extra = xs[:, D:].astype(F32)
        c = extra[:, 0:1] + extra[:, 1:2]
        gate = jnp.dot(x, wg_bf[...], preferred_element_type=F32)
        up = jnp.dot(x, wu_bf[...], preferred_element_type=F32)
        hh = (gate * jax.nn.sigmoid(gate)) * up * c
        o_ref[...] = jnp.dot(hh.astype(BF16), wd_bf[...], preferred_element_type=F32).astype(o_ref.dtype)

    @pl.when(jnp.logical_not(used))
    def _():
        o_ref[...] = jnp.zeros(o_ref.shape, o_ref.dtype)


def _moe_experts(xs, gather_tbl, tile_expert, n_used, wg, wu, wd, n_mm_tiles):
    W = xs.shape[1]
    E, D, FF = wg.shape

    def chunk_spec(c):
        return pl.BlockSpec((MOE_CHUNK, W), lambda g, gt, te, nu: (gt[g * MOE_CPM + c], 0))

    grid_spec = pltpu.PrefetchScalarGridSpec(
        num_scalar_prefetch=3,
        grid=(n_mm_tiles,),
        in_specs=[chunk_spec(c) for c in range(MOE_CPM)] + [
            pl.BlockSpec((1, D, FF), lambda g, gt, te, nu: (te[g], 0, 0)),
            pl.BlockSpec((1, D, FF), lambda g, gt, te, nu: (te[g], 0, 0)),
            pl.BlockSpec((1, FF, D), lambda g, gt, te, nu: (te[g], 0, 0))],
        out_specs=pl.BlockSpec((MOE_TM, D), lambda g, gt, te, nu: (g, 0)),
        scratch_shapes=[pltpu.VMEM((D, FF), BF16), pltpu.VMEM((D, FF), BF16), pltpu.VMEM((FF, D), BF16)],
    )
    return pl.pallas_call(
        _moe_expert_kernel,
        grid_spec=grid_spec,
        out_shape=jax.ShapeDtypeStruct((n_mm_tiles * MOE_TM, D), BF16),
        compiler_params=_cparams("arbitrary"),
        name="moe_experts",
    )(gather_tbl, tile_expert, n_used, *([xs] * MOE_CPM), wg, wu, wd)


def _finish_kernel(rt_ref, *refs):
    chunk_refs = refs[:MOE_CPT]
    dest_ref, h_ref, p_ref, gp_ref, gf_ref, wg_ref, wp_ref, o_ref, ys_ref = refs[MOE_CPT:]
    tok = h_ref.shape[0]
    for k, r in enumerate(chunk_refs):
        ys_ref[k * MOE_CHUNK:(k + 1) * MOE_CHUNK, :] = r[...]
    dcol = dest_ref[0].astype(F32).T
    lane = lax.broadcasted_iota(jnp.int32, (tok, MOE_ROWS), 1).astype(F32)
    unsort = ((lane == dcol[:, 0:1]) | (lane == dcol[:, 1:2])).astype(BF16)
    h = h_ref[...] + jnp.dot(unsort, ys_ref[...], preferred_element_type=F32)
    t = _rms(h, gp_ref[...]).astype(BF16)
    gate = jax.nn.sigmoid(jnp.dot(t, wg_ref[...], preferred_element_type=F32))
    emb = jnp.dot(p_ref[...].astype(BF16), wp_ref[...], preferred_element_type=F32)
    o_ref[...] = _rms(h + emb * gate, gf_ref[...])


def _finish(ys, return_tbl, dest, h1, p2, g_ple, g_final, wg, wp):
    T, D = h1.shape
    P = p2.shape[1]
    nt = T // MOE_TOK

    def chunk_spec(k):
        return pl.BlockSpec((MOE_CHUNK, D), lambda i, rt: (rt[i * MOE_CPT + k], 0))

    vec = pl.BlockSpec((1, D), lambda i, rt: (0, 0))
    grid_spec = pltpu.PrefetchScalarGridSpec(
        num_scalar_prefetch=1,
        grid=(nt,),
        in_specs=[chunk_spec(k) for k in range(MOE_CPT)] + [
            pl.BlockSpec((1, SUBLANES, MOE_TOK), lambda i, rt: (i, 0, 0)),
            pl.BlockSpec((MOE_TOK, D), lambda i, rt: (i, 0)),
            pl.BlockSpec((MOE_TOK, P), lambda i, rt: (i, 0)),
            vec, vec,
            pl.BlockSpec((D, D), lambda i, rt: (0, 0), pipeline_mode=pl.Buffered(1)),
            pl.BlockSpec((P, D), lambda i, rt: (0, 0), pipeline_mode=pl.Buffered(1))],
        out_specs=pl.BlockSpec((MOE_TOK, D), lambda i, rt: (i, 0)),
        scratch_shapes=[pltpu.VMEM((MOE_ROWS, D), BF16)],
    )
    return pl.pallas_call(
        _finish_kernel,
        grid_spec=grid_spec,
        out_shape=jax.ShapeDtypeStruct((T, D), F32),
        compiler_params=_cparams("arbitrary"),
        name="finish",
    )(return_tbl, *([ys] * MOE_CPT), dest, h1, p2, g_ple, g_final, wg, wp)


def kernel(x, p, g_mix, w_in, rel_bias, w_attn_br, conv_w, conv_b, ln_g, ln_b, w_conv_br, w_o, g_ffn,
           w_router_g, b_router_g, w_router_e, b_router_e, w_e_gate, w_e_up, w_e_down, g_ple, w_ple_gate,
           w_ple_proj, g_final):
    B, S, D = x.shape
    T = B * S
    depth = w_in.shape[0]
    width = N_HEADS * HEAD_DIM
    assert D == width and conv_w.shape[2] == D, "column-block indexing assumes all branch widths equal d_model"
    assert S % MOBA_BLOCK == 0
    assert depth == 1, "the final RMSNorm is fused into the last stage of a single layer"
    a_blk = 3 * width // D
    gate_blk = a_blk + 2

    h = x.reshape(T, D)
    for i in range(depth):
        proj = _in_proj(h, g_mix[i][None], w_in[i].astype(BF16))
        proj3 = proj.reshape(B, S, -1)
        attn = _moba(proj3, rel_bias)
        conv = _conv(proj3, conv_w[i], conv_b[i][None], ln_g[i][None], ln_b[i][None], a_blk)
        h = _mix(h, attn.reshape(T, D), conv.reshape(T, D), proj,
                 w_attn_br[i].astype(BF16), w_conv_br[i].astype(BF16), w_o[i].astype(BF16), gate_blk)
        xs, dest, cnt = _moe_sort(h, g_ffn[i][None], w_router_e[i], b_router_e[i], w_router_g[i], b_router_g[i])
        n_tok_tiles = T // MOE_TOK
        worst_chunks = (MOE_TOPK * T + n_tok_tiles * N_EXPERTS * (MOE_CHUNK - 1)) // MOE_CHUNK
        n_mm_tiles = -(-worst_chunks // MOE_CPM) + N_EXPERTS
        gather_tbl, return_tbl, tile_expert, n_used = _moe_tables(cnt[:, :, 0], n_mm_tiles)
        ys = _moe_experts(xs, gather_tbl, tile_expert, n_used,
                          w_e_gate[i], w_e_up[i], w_e_down[i], n_mm_tiles)
        out = _finish(ys, return_tbl, dest, h, p[i].reshape(T, -1), g_ple[i][None], g_final[None],
                      w_ple_gate[i].astype(BF16), w_ple_proj[i].astype(BF16))
    return out.reshape(B, S, D)
```

```python
import functools
import math

import numpy as np
import jax
import jax.numpy as jnp
from jax import lax
from jax.experimental import pallas as pl
from jax.experimental.pallas import tpu as pltpu

F32 = jnp.float32
BF16 = jnp.bfloat16

N_HEADS = 16
HEAD_DIM = 128
MOBA_BLOCK = 256
MOBA_TOPK = 3
N_BUCKETS = 32
MAX_DISTANCE = 128
CONV_KERNEL = 31
N_GROUPS = 4
EXPERTS_PER_GROUP = 4
N_EXPERTS = N_GROUPS * EXPERTS_PER_GROUP
EPS = 1e-6
NEG = -1e30
LOG2E = 1.4426950408889634

V7X_VMEM_LIMIT_BYTES = 56 * 1024 * 1024
LANES = 128
SUBLANES = 8

_NT = (((1,), (1,)), ((), ()))


def _cparams(*sem):
    return pltpu.CompilerParams(dimension_semantics=sem, vmem_limit_bytes=V7X_VMEM_LIMIT_BYTES)


def _resident(shape):
    return pl.BlockSpec(shape, lambda *_: (0,) * len(shape), pipeline_mode=pl.Buffered(1))


def _sigmoid(x):
    return 0.5 * jnp.tanh(0.5 * x) + 0.5


def _rms(x, g):
    return x * lax.rsqrt(jnp.mean(x * x, axis=-1, keepdims=True) + EPS) * g


def _in_proj_kernel(x_ref, g_ref, w_ref, o_ref, xn_ref):
    @pl.when(pl.program_id(1) == 0)
    def _():
        xn_ref[...] = _rms(x_ref[...], g_ref[...]).astype(BF16)

    o_ref[...] = jnp.dot(xn_ref[...], w_ref[...], preferred_element_type=F32).astype(o_ref.dtype)


def _in_proj(x2, g, w, tm=1024, tn=2048):
    T, D = x2.shape
    N = w.shape[1]
    return pl.pallas_call(
        _in_proj_kernel,
        grid=(T // tm, N // tn),
        in_specs=[pl.BlockSpec((tm, D), lambda i, j: (i, 0)),
                  pl.BlockSpec((1, D), lambda i, j: (0, 0)),
                  pl.BlockSpec((D, tn), lambda i, j: (0, j))],
        out_specs=pl.BlockSpec((tm, tn), lambda i, j: (i, j)),
        out_shape=jax.ShapeDtypeStruct((T, N), BF16),
        scratch_shapes=[pltpu.VMEM((tm, D), BF16)],
        compiler_params=_cparams("parallel", "arbitrary"),
        name="in_proj",
    )(x2, g, w)


MOBA_PV_ROWS = HEAD_DIM + 16
MOBA_HEADS_PER_STEP = 4


def _rel_buckets_t():
    k = np.arange(MOBA_BLOCK)[:, None]
    q = np.arange(MOBA_BLOCK)[None, :]
    dist = np.stack([q - k, MOBA_BLOCK + q - k]).astype(np.int32)
    n = np.maximum(dist, 0)
    max_exact = N_BUCKETS // 2
    nf = np.maximum(n, 1).astype(np.float32)
    large = max_exact + (np.log(nf / np.float32(max_exact)) / np.float32(math.log(MAX_DISTANCE / max_exact))
                         * np.float32(N_BUCKETS - max_exact)).astype(np.int32)
    large = np.minimum(large, N_BUCKETS - 1)
    return np.where(n < max_exact, n, large).astype(np.int32)


def _moba_kernel(tbl_ref, bkt_ref, q_ref, k_ref, v_ref, o_ref,
                 kmean_ref, vt_ref, bias_ref, qt_ref, sel_ref, s_ref, p_ref, m_ref, a_ref, acc_ref,
                 *, n_blocks):
    hg = pl.program_id(0)
    b = pl.program_id(1)
    qi = pl.program_id(2)
    BS = MOBA_BLOCK
    HPS = MOBA_HEADS_PER_STEP
    scale = HEAD_DIM ** -0.5
    lanes = [slice(hh * HEAD_DIM, (hh + 1) * HEAD_DIM) for hh in range(HPS)]

    @pl.when((b == 0) & (qi == 0))
    def _build_bias():
        krow = lax.broadcasted_iota(jnp.int32, (BS, BS), 0)
        qcol = lax.broadcasted_iota(jnp.int32, (BS, BS), 1)
        for hh in range(HPS):
            bias_ref[hh] = jnp.zeros((2, BS, BS), F32)

            def fill(n, carry):
                bias_ref[hh] = jnp.where(bkt_ref[...] == n, tbl_ref[hg * HPS + hh, n] * LOG2E, bias_ref[hh])
                return carry

            lax.fori_loop(0, N_BUCKETS, fill, 0)
            bias_ref[hh, 0] = jnp.where(qcol >= krow, bias_ref[hh, 0], NEG)

    @pl.when(qi == 0)
    def _per_sequence():
        ones_row = (lax.broadcasted_iota(jnp.int32, (MOBA_PV_ROWS - HEAD_DIM, BS), 0) == 0).astype(BF16)

        def prep(n, carry):
            rows = pl.ds(pl.multiple_of(n * BS, BS), BS)
            for hh in range(HPS):
                kmean_ref[hh, pl.ds(n, 1), :] = jnp.mean(k_ref[0, rows, lanes[hh]].astype(F32), axis=0,
                                                         keepdims=True)
                vt_ref[hh, n, :HEAD_DIM] = v_ref[0, rows, lanes[hh]].astype(F32).T.astype(BF16)
                vt_ref[hh, n, HEAD_DIM:] = ones_row
            return carry

        lax.fori_loop(0, n_blocks, prep, 0)

    def qk_stage(hh, t):
        j0 = pl.multiple_of(jnp.maximum(qi - t, 0) * BS, BS)
        s_ref[hh] = jnp.dot(k_ref[0, pl.ds(j0, BS), lanes[hh]], qt_ref[hh], preferred_element_type=F32)

    def softmax_stage(hh, t, bias_of=None, far_bias=None):
        for c in range(BS // LANES):
            qs = slice(c * LANES, (c + 1) * LANES)
            chosen = (lax.shift_right_logical(sel_ref[hh, :, qs], qi - t) & 1) == 1
            st = s_ref[hh, :, qs] * (scale * LOG2E)
            shift = 0.0 if far_bias is None else far_bias
            if bias_of is not None:
                st = st + bias_of(qs)
            m_prev = m_ref[hh, :, qs]
            m_new = jnp.where(chosen, jnp.maximum(m_prev, jnp.max(st, axis=0, keepdims=True) + shift), m_prev)
            a_ref[hh, :, qs] = jnp.exp2(m_prev - m_new)
            m_ref[hh, :, qs] = m_new
            p_ref[hh, :, qs] = jnp.exp2(st - jnp.where(chosen, m_new - shift, -NEG)).astype(BF16)

    def pv_stage(hh, t):
        acc_ref[hh] = a_ref[hh] * acc_ref[hh] + jnp.dot(vt_ref[hh, qi - t], p_ref[hh],
                                                        preferred_element_type=F32)

    for hh in range(HPS):
        qt = q_ref[0, :, lanes[hh]].astype(F32).T.astype(BF16)
        qt_ref[hh] = qt
        gate = jnp.dot(kmean_ref[hh].astype(BF16), qt, preferred_element_type=F32)
        blk = lax.broadcasted_iota(jnp.int32, gate.shape, 0)
        blkf = blk.astype(F32)
        g = jnp.where(blk < qi, gate, NEG)
        bits = jnp.full((1, BS), lax.shift_left(jnp.int32(1), qi), jnp.int32)
        for _ in range(MOBA_TOPK):
            mx = jnp.max(g, axis=0, keepdims=True)
            ix = jnp.min(jnp.where(g == mx, blkf, float(n_blocks)), axis=0, keepdims=True).astype(jnp.int32)
            bits = bits | jnp.where(ix < qi, lax.shift_left(jnp.int32(1), ix), 0)
            g = jnp.where(blk == ix, -jnp.inf, g)
        sel_ref[hh] = bits
        m_ref[hh] = jnp.full((1, BS), NEG, F32)
        acc_ref[hh] = jnp.zeros((MOBA_PV_ROWS, BS), F32)
        qk_stage(hh, 0)

    for hh in range(HPS):
        softmax_stage(hh, 0, lambda qs, hh=hh: bias_ref[hh, 0, :, qs])
    for hh in range(HPS):
        qk_stage(hh, 1)

    @pl.when(qi >= 1)
    def _past_blocks():
        for hh in range(HPS):
            pv_stage(hh, 0)
        for hh in range(HPS):
            softmax_stage(hh, 1, lambda qs, hh=hh: bias_ref[hh, 1, :, qs])
        for hh in range(HPS):
            qk_stage(hh, 2)

        def sweep(t):
            for hh in range(HPS):
                pv_stage(hh, t - 1)
            for hh in range(HPS):
                softmax_stage(hh, t, far_bias=tbl_ref[hg * HPS + hh, N_BUCKETS - 1] * LOG2E)
            for hh in range(HPS):
                qk_stage(hh, t + 1)

        n_far = qi - 1

        def sweep_pair(i, carry):
            sweep(2 + 2 * i)
            sweep(3 + 2 * i)
            return carry

        lax.fori_loop(0, n_far // 2, sweep_pair, 0)

        @pl.when(n_far % 2 == 1)
        def _odd_block():
            sweep(qi)

    for hh in range(HPS):
        pv_stage(hh, qi)
        denom = acc_ref[hh, HEAD_DIM:HEAD_DIM + 1, :]
        out_t = acc_ref[hh, :HEAD_DIM, :] * (1.0 / denom)
        o_ref[0, :, lanes[hh]] = out_t.T.astype(o_ref.dtype)


def _moba(proj3, rel_bias):
    B, S, _ = proj3.shape
    BS = MOBA_BLOCK
    HPS = MOBA_HEADS_PER_STEP
    W = HPS * HEAD_DIM
    nb = S // BS
    n_groups = N_HEADS // HPS
    bkt = jnp.asarray(_rel_buckets_t())
    tbl = rel_bias.astype(F32).T
    grid_spec = pltpu.PrefetchScalarGridSpec(
        num_scalar_prefetch=1,
        grid=(n_groups, B, nb),
        in_specs=[pl.BlockSpec((2, BS, BS), lambda h, b, i, t: (0, 0, 0)),
                  pl.BlockSpec((1, BS, W), lambda h, b, i, t: (b, i, h)),
                  pl.BlockSpec((1, S, W), lambda h, b, i, t: (b, 0, n_groups + h)),
                  pl.BlockSpec((1, S, W), lambda h, b, i, t: (b, 0, 2 * n_groups + h))],
        out_specs=pl.BlockSpec((1, BS, W), lambda h, b, i, t: (b, i, h)),
        scratch_shapes=[pltpu.VMEM((HPS, nb, HEAD_DIM), F32),
                        pltpu.VMEM((HPS, nb, MOBA_PV_ROWS, BS), BF16),
                        pltpu.VMEM((HPS, 2, BS, BS), F32),
                        pltpu.VMEM((HPS, HEAD_DIM, BS), BF16),
                        pltpu.VMEM((HPS, 1, BS), jnp.int32),
                        pltpu.VMEM((HPS, BS, BS), F32),
                        pltpu.VMEM((HPS, BS, BS), BF16),
                        pltpu.VMEM((HPS, 1, BS), F32),
                        pltpu.VMEM((HPS, 1, BS), F32),
                        pltpu.VMEM((HPS, MOBA_PV_ROWS, BS), F32)],
    )
    return pl.pallas_call(
        functools.partial(_moba_kernel, n_blocks=nb),
        grid_spec=grid_spec,
        out_shape=jax.ShapeDtypeStruct((B, S, N_HEADS * HEAD_DIM), BF16),
        compiler_params=_cparams("arbitrary", "arbitrary", "arbitrary"),
        name="moba",
    )(tbl, bkt, proj3, proj3, proj3)


CONV_HALO = 32
CONV_ROWS = 64
CONV_COLS = 256


def _conv_kernel(a_ref, g_ref, ah_ref, gh_ref, w_ref, cb_ref, lng_ref, lnb_ref, o_ref, u_ref, sh_ref, c_ref,
                 *, ts):
    i = pl.program_id(1)
    D = u_ref.shape[1]
    u_ref[CONV_HALO:, :] = a_ref[0].astype(F32) * _sigmoid(g_ref[0].astype(F32))
    uh = ah_ref[0].astype(F32) * _sigmoid(gh_ref[0].astype(F32))
    u_ref[:CONV_HALO, :] = jnp.where(i > 0, uh, 0.0)

    first = CONV_HALO - (CONV_KERNEL - 1)
    sh_rows = sh_ref.shape[1]

    def col_body(c, carry):
        cols = pl.ds(pl.multiple_of(c * CONV_COLS, CONV_COLS), CONV_COLS)
        for s in range(1, SUBLANES):
            sh_ref[s] = u_ref[s:s + sh_rows, cols]
        for r in range(ts // CONV_ROWS):
            acc = jnp.zeros((CONV_ROWS, CONV_COLS), F32)
            for k in range(CONV_KERNEL):
                s = (first + k) % SUBLANES
                r0 = r * CONV_ROWS + first + k - s
                tap = u_ref[r0:r0 + CONV_ROWS, cols] if s == 0 else sh_ref[s, r0:r0 + CONV_ROWS, :]
                acc = acc + jnp.tile(w_ref[k, :, cols], (CONV_ROWS // SUBLANES, 1)) * tap
            c_ref[r * CONV_ROWS:(r + 1) * CONV_ROWS, cols] = acc + cb_ref[:, cols]
        return carry

    lax.fori_loop(0, D // CONV_COLS, col_body, 0)

    y = c_ref[...]
    mu = jnp.mean(y, axis=-1, keepdims=True)
    yc = y - mu
    var = jnp.mean(yc * yc, axis=-1, keepdims=True)
    z = yc * lax.rsqrt(var + EPS) * lng_ref[...] + lnb_ref[...]
    o_ref[0] = (z * _sigmoid(z)).astype(o_ref.dtype)


def _conv(proj3, conv_w, conv_b, ln_g, ln_b, a_blk, ts=512):
    B, S, _ = proj3.shape
    D = conv_w.shape[1]
    hpb = ts // CONV_HALO
    w_rows = jnp.broadcast_to(conv_w[:, None, :], (CONV_KERNEL, SUBLANES, D))

    def halo(col):
        return pl.BlockSpec((1, CONV_HALO, D), lambda b, i: (b, jnp.maximum(i * hpb - 1, 0), col))

    vec = pl.BlockSpec((1, D), lambda b, i: (0, 0))
    return pl.pallas_call(
        functools.partial(_conv_kernel, ts=ts),
        grid=(B, S // ts),
        in_specs=[pl.BlockSpec((1, ts, D), lambda b, i: (b, i, a_blk)),
                  pl.BlockSpec((1, ts, D), lambda b, i: (b, i, a_blk + 1)),
                  halo(a_blk), halo(a_blk + 1),
                  pl.BlockSpec((CONV_KERNEL, SUBLANES, D), lambda b, i: (0, 0, 0)),
                  vec, vec, vec],
        out_specs=pl.BlockSpec((1, ts, D), lambda b, i: (b, i, 0)),
        out_shape=jax.ShapeDtypeStruct((B, S, D), BF16),
        scratch_shapes=[pltpu.VMEM((CONV_HALO + ts, D), F32),
                        pltpu.VMEM((SUBLANES, CONV_HALO + ts - SUBLANES, CONV_COLS), F32),
                        pltpu.VMEM((ts, D), F32)],
        compiler_params=_cparams("parallel", "arbitrary"),
        name="conv",
    )(proj3, proj3, proj3, proj3, w_rows, conv_b, ln_g, ln_b)


def _mix_kernel(x_ref, at_ref, cv_ref, ga_ref, gc_ref, wa_ref, wc_ref, wo_ref, o_ref):
    ya = jnp.dot(at_ref[...], wa_ref[...], preferred_element_type=F32)
    yc = jnp.dot(cv_ref[...], wc_ref[...], preferred_element_type=F32)
    mixed = (jax.nn.sigmoid(ga_ref[...].astype(F32)) * ya + jax.nn.sigmoid(gc_ref[...].astype(F32)) * yc)
    o_ref[...] = x_ref[...] + jnp.dot(mixed.astype(BF16), wo_ref[...], preferred_element_type=F32)


def _mix(x2, attn2, conv2, proj2, wa, wc, wo, gate_blk, tm=256):
    T, D = x2.shape
    row = lambda col: pl.BlockSpec((tm, D), lambda i: (i, col))
    return pl.pallas_call(
        _mix_kernel,
        grid=(T // tm,),
        in_specs=[row(0), row(0), row(0), row(gate_blk), row(gate_blk + 1),
                  _resident((D, D)), _resident((D, D)), _resident((D, D))],
        out_specs=row(0),
        out_shape=jax.ShapeDtypeStruct((T, D), F32),
        compiler_params=_cparams("parallel"),
        name="mix",
    )(x2, attn2, conv2, proj2, proj2, wa, wc, wo)


MOE_TOK = 512
MOE_CHUNK = 16
MOE_TM = 512
MOE_TOPK = 2
_MOE_WORST_ROWS = MOE_TOPK * MOE_TOK + N_EXPERTS * (MOE_CHUNK - 1)
MOE_ROWS = -(-_MOE_WORST_ROWS // LANES) * LANES
MOE_CPT = MOE_ROWS // MOE_CHUNK
MOE_CPM = MOE_TM // MOE_CHUNK
MOE_EXTRA = LANES


def _route_t(lt):
    row = lax.broadcasted_iota(jnp.int32, lt.shape, 0)
    rowf = row.astype(F32)
    big = float(lt.shape[0])
    is_grp = (row >= N_EXPERTS) & (row < N_EXPERTS + N_GROUPS)
    cl = jnp.where(is_grp, lt, -jnp.inf)
    mg = jnp.max(cl, axis=0, keepdims=True)
    g_star = jnp.min(jnp.where(cl == mg, rowf, big), axis=0, keepdims=True).astype(jnp.int32) - N_EXPERTS
    p_g = 1.0 / jnp.sum(jnp.where(is_grp, jnp.exp(cl - mg), 0.0), axis=0, keepdims=True)
    in_grp = (row < N_EXPERTS) & ((row // EXPERTS_PER_GROUP) == g_star)
    f1 = jnp.where(in_grp, lt, -jnp.inf)
    v1 = jnp.max(f1, axis=0, keepdims=True)
    i1 = jnp.min(jnp.where(f1 == v1, rowf, big), axis=0, keepdims=True).astype(jnp.int32)
    f2 = jnp.where(row == i1, -jnp.inf, f1)
    v2 = jnp.max(f2, axis=0, keepdims=True)
    i2 = jnp.min(jnp.where(f2 == v2, rowf, big), axis=0, keepdims=True).astype(jnp.int32)
    e2 = jnp.exp(v2 - v1)
    return i1, i2, p_g / (1.0 + e2), p_g * e2 / (1.0 + e2)


def _split_bf16(x):
    hi = x.astype(BF16)
    return hi, (x - hi.astype(F32)).astype(BF16)


def _moe_sort_kernel(h_ref, g_ref, wh_ref, wl_ref, br_ref, xs_ref, dest_ref, cnt_ref):
    tok = h_ref.shape[0]
    D = h_ref.shape[1]
    hn = _rms(h_ref[...], g_ref[...])
    hn_hi, hn_lo = _split_bf16(hn)
    nt_dot = lambda a, b: lax.dot_general(a, b, _NT, preferred_element_type=F32)
    w_hi = wh_ref[...]
    both = nt_dot(jnp.concatenate([w_hi, wl_ref[...]], axis=0), hn_hi)
    lt = both[:MOE_ROUTE_ROWS] + both[MOE_ROUTE_ROWS:] + nt_dot(w_hi, hn_lo) + br_ref[...]
    i1, i2, c1, c2 = _route_t(lt)

    erow = lax.broadcasted_iota(jnp.int32, (N_EXPERTS, tok), 0)
    oh1 = erow == i1
    oh2 = erow == i2
    oh = (oh1 | oh2).astype(F32)
    earlier = (lax.broadcasted_iota(jnp.int32, (tok, tok), 0) < lax.broadcasted_iota(jnp.int32, (tok, tok), 1))
    cum = jnp.dot(oh.astype(BF16), earlier.astype(BF16), preferred_element_type=F32)
    cnt = jnp.sum(oh, axis=1, keepdims=True)
    n_chunks = jnp.floor((cnt + (MOE_CHUNK - 1)) * (1.0 / MOE_CHUNK))
    ncb = jnp.broadcast_to(n_chunks, (N_EXPERTS, LANES))
    erow_l = lax.broadcasted_iota(jnp.int32, (N_EXPERTS, LANES), 0)
    first_chunk = jnp.zeros((N_EXPERTS, LANES), F32)
    for e in range(N_EXPERTS - 1):
        first_chunk = first_chunk + jnp.where(erow_l > e, ncb[e:e + 1, :], 0.0)
    base = jnp.tile(first_chunk, (1, tok // LANES)) * MOE_CHUNK + cum
    d1 = jnp.sum(jnp.where(oh1, base, 0.0), axis=0, keepdims=True).astype(jnp.int32)
    d2 = jnp.sum(jnp.where(oh2, base, 0.0), axis=0, keepdims=True).astype(jnp.int32)

    srow = lax.broadcasted_iota(jnp.int32, (MOE_ROWS, tok), 0)
    p1 = srow == d1
    p2 = srow == d2
    perm1 = p1.astype(BF16)
    perm2 = p2.astype(BF16)
    xs_ref[:, :D] = jnp.dot(perm1 + perm2, hn_hi, preferred_element_type=F32).astype(BF16)
    xrow = lax.broadcasted_iota(jnp.int32, (MOE_EXTRA, tok), 0)

    def weight_rows(c):
        hi = c.astype(BF16).astype(F32)
        return jnp.where(xrow == 0, hi, jnp.where(xrow == 1, c - hi, 0.0)).astype(BF16)

    xs_ref[:, D:] = (nt_dot(perm1, weight_rows(c1)) + nt_dot(perm2, weight_rows(c2))).astype(BF16)

    r8 = lax.broadcasted_iota(jnp.int32, (SUBLANES, tok), 0)
    dest_ref[0] = jnp.where(r8 == 0, d1, jnp.where(r8 == 1, d2, 0))
    cnt_ref[0] = jnp.broadcast_to(cnt, (N_EXPERTS, LANES)).astype(jnp.int32)


MOE_ROUTE_ROWS = 32


def _moe_sort(h1, g, w_router_e, b_router_e, w_router_g, b_router_g):
    T, D = h1.shape
    nt = T // MOE_TOK
    pad = MOE_ROUTE_ROWS - N_EXPERTS - N_GROUPS
    wr = jnp.concatenate([w_router_e.T, w_router_g.T, jnp.zeros((pad, D), F32)], axis=0)
    wr_hi = wr.astype(BF16)
    wr_lo = (wr - wr_hi.astype(F32)).astype(BF16)
    br = jnp.concatenate([b_router_e, b_router_g, jnp.zeros((pad,), F32)])[:, None]
    return pl.pallas_call(
        _moe_sort_kernel,
        grid=(nt,),
        in_specs=[pl.BlockSpec((MOE_TOK, D), lambda i: (i, 0)),
                  pl.BlockSpec((1, D), lambda i: (0, 0)),
                  pl.BlockSpec((MOE_ROUTE_ROWS, D), lambda i: (0, 0)),
                  pl.BlockSpec((MOE_ROUTE_ROWS, D), lambda i: (0, 0)),
                  pl.BlockSpec((MOE_ROUTE_ROWS, 1), lambda i: (0, 0))],
        out_specs=[pl.BlockSpec((MOE_ROWS, D + MOE_EXTRA), lambda i: (i, 0)),
                   pl.BlockSpec((1, SUBLANES, MOE_TOK), lambda i: (i, 0, 0)),
                   pl.BlockSpec((1, N_EXPERTS, LANES), lambda i: (i, 0, 0))],
        out_shape=[jax.ShapeDtypeStruct((nt * MOE_ROWS, D + MOE_EXTRA), BF16),
                   jax.ShapeDtypeStruct((nt, SUBLANES, MOE_TOK), jnp.int32),
                   jax.ShapeDtypeStruct((nt, N_EXPERTS, LANES), jnp.int32)],
        compiler_params=_cparams("parallel"),
        name="moe_sort",
    )(h1, g, wr_hi, wr_lo, br)


def _moe_tables(cnt, n_mm_tiles):
    nt = cnt.shape[0]
    pc = (cnt + (MOE_CHUNK - 1)) // MOE_CHUNK
    run_end = jnp.cumsum(pc, axis=1)
    run_first = run_end - pc
    per_expert = jnp.sum(pc, axis=0)
    mm_tiles = (per_expert + (MOE_CPM - 1)) // MOE_CPM
    mm_end = jnp.cumsum(mm_tiles)
    slot_base = (mm_end - mm_tiles)[None, :] * MOE_CPM + (jnp.cumsum(pc, axis=0) - pc)
    k = jnp.arange(MOE_CPT, dtype=jnp.int32)[None, :, None]
    expert_of = jnp.sum((run_end[:, None, :] <= k).astype(jnp.int32), axis=2)
    valid = expert_of < N_EXPERTS
    pick = expert_of[:, :, None] == jnp.arange(N_EXPERTS, dtype=jnp.int32)
    take = lambda a: jnp.sum(jnp.where(pick, a[:, None, :], 0), axis=2)
    slot = take(slot_base) + (k[:, :, 0] - take(run_first))
    src = (jnp.arange(nt, dtype=jnp.int32)[:, None] * MOE_CPT + k[:, :, 0]).reshape(-1)
    slot = jnp.where(valid, slot, n_mm_tiles * MOE_CPM).reshape(-1)
    gather_tbl = jnp.zeros((n_mm_tiles * MOE_CPM,), jnp.int32).at[slot].set(src, mode="drop")
    return_tbl = jnp.where(valid.reshape(-1), slot, 0).astype(jnp.int32)
    g = jnp.arange(n_mm_tiles, dtype=jnp.int32)
    n_used = mm_end[-1].astype(jnp.int32)
    tile_expert = jnp.sum((mm_end[None, :] <= jnp.minimum(g, n_used - 1)[:, None]).astype(jnp.int32), axis=1)
    return gather_tbl, return_tbl, tile_expert.astype(jnp.int32), n_used.reshape(1)


def _moe_expert_kernel(gt_ref, te_ref, nu_ref, *refs):
    chunk_refs = refs[:MOE_CPM]
    wg_ref, wu_ref, wd_ref, o_ref, wg_bf, wu_bf, wd_bf = refs[MOE_CPM:]
    D = o_ref.shape[1]
    g = pl.program_id(0)
    used = g < nu_ref[0]

    @pl.when(used & ((g == 0) | (te_ref[g] != te_ref[jnp.maximum(g - 1, 0)])))
    def _():
        wg_bf[...] = wg_ref[0].astype(BF16)
        wu_bf[...] = wu_ref[0].astype(BF16)
        wd_bf[...] = wd_ref[0].astype(BF16)

    @pl.when(used)
    def _():
        xs = jnp.concatenate([r[...] for r in chunk_refs], axis=0)
        x = xs[:, :D]
        extra = xs[:, D:].astype(F32)
        c = extra[:, 0:1] + extra[:, 1:2]
        gate = jnp.dot(x, wg_bf[...], preferred_element_type=F32)
        up = jnp.dot(x, wu_bf[...], preferred_element_type=F32)
        hh = (gate * jax.nn.sigmoid(gate)) * up * c
        o_ref[...] = jnp.dot(hh.astype(BF16), wd_bf[...], preferred_element_type=F32).astype(o_ref.dtype)

    @pl.when(jnp.logical_not(used))
    def _():
        o_ref[...] = jnp.zeros(o_ref.shape, o_ref.dtype)


def _moe_experts(xs, gather_tbl, tile_expert, n_used, wg, wu, wd, n_mm_tiles):
    W = xs.shape[1]
    E, D, FF = wg.shape

    def chunk_spec(c):
        return pl.BlockSpec((MOE_CHUNK, W), lambda g, gt, te, nu: (gt[g * MOE_CPM + c], 0))

    grid_spec = pltpu.PrefetchScalarGridSpec(
        num_scalar_prefetch=3,
        grid=(n_mm_tiles,),
        in_specs=[chunk_spec(c) for c in range(MOE_CPM)] + [
            pl.BlockSpec((1, D, FF), lambda g, gt, te, nu: (te[g], 0, 0)),
            pl.BlockSpec((1, D, FF), lambda g, gt, te, nu: (te[g], 0, 0)),
            pl.BlockSpec((1, FF, D), lambda g, gt, te, nu: (te[g], 0, 0))],
        out_specs=pl.BlockSpec((MOE_TM, D), lambda g, gt, te, nu: (g, 0)),
        scratch_shapes=[pltpu.VMEM((D, FF), BF16), pltpu.VMEM((D, FF), BF16), pltpu.VMEM((FF, D), BF16)],
    )
    return pl.pallas_call(
        _moe_expert_kernel,
        grid_spec=grid_spec,
        out_shape=jax.ShapeDtypeStruct((n_mm_tiles * MOE_TM, D), BF16),
        compiler_params=_cparams("arbitrary"),
        name="moe_experts",
    )(gather_tbl, tile_expert, n_used, *([xs] * MOE_CPM), wg, wu, wd)


def _finish_kernel(rt_ref, *refs):
    chunk_refs = refs[:MOE_CPT]
    dest_ref, h_ref, p_ref, gp_ref, gf_ref, wg_ref, wp_ref, o_ref, ys_ref = refs[MOE_CPT:]
    tok = h_ref.shape[0]
    for k, r in enumerate(chunk_refs):
        ys_ref[k * MOE_CHUNK:(k + 1) * MOE_CHUNK, :] = r[...]
    dcol = dest_ref[0].astype(F32).T
    lane = lax.broadcasted_iota(jnp.int32, (tok, MOE_ROWS), 1).astype(F32)
    unsort = ((lane == dcol[:, 0:1]) | (lane == dcol[:, 1:2])).astype(BF16)
    h = h_ref[...] + jnp.dot(unsort, ys_ref[...], preferred_element_type=F32)
    t = _rms(h, gp_ref[...]).astype(BF16)
    gate = jax.nn.sigmoid(jnp.dot(t, wg_ref[...], preferred_element_type=F32))
    emb = jnp.dot(p_ref[...].astype(BF16), wp_ref[...], preferred_element_type=F32)
    o_ref[...] = _rms(h + emb * gate, gf_ref[...])


def _finish(ys, return_tbl, dest, h1, p2, g_ple, g_final, wg, wp):
    T, D = h1.shape
    P = p2.shape[1]
    nt = T // MOE_TOK

    def chunk_spec(k):
        return pl.BlockSpec((MOE_CHUNK, D), lambda i, rt: (rt[i * MOE_CPT + k], 0))

    vec = pl.BlockSpec((1, D), lambda i, rt: (0, 0))
    grid_spec = pltpu.PrefetchScalarGridSpec(
        num_scalar_prefetch=1,
        grid=(nt,),
        in_specs=[chunk_spec(k) for k in range(MOE_CPT)] + [
            pl.BlockSpec((1, SUBLANES, MOE_TOK), lambda i, rt: (i, 0, 0)),
            pl.BlockSpec((MOE_TOK, D), lambda i, rt: (i, 0)),
            pl.BlockSpec((MOE_TOK, P), lambda i, rt: (i, 0)),
            vec, vec,
            pl.BlockSpec((D, D), lambda i, rt: (0, 0), pipeline_mode=pl.Buffered(1)),
            pl.BlockSpec((P, D), lambda i, rt: (0, 0), pipeline_mode=pl.Buffered(1))],
        out_specs=pl.BlockSpec((MOE_TOK, D), lambda i, rt: (i, 0)),
        scratch_shapes=[pltpu.VMEM((MOE_ROWS, D), BF16)],
    )
    return pl.pallas_call(
        _finish_kernel,
        grid_spec=grid_spec,
        out_shape=jax.ShapeDtypeStruct((T, D), F32),
        compiler_params=_cparams("arbitrary"),
        name="finish",
    )(return_tbl, *([ys] * MOE_CPT), dest, h1, p2, g_ple, g_final, wg, wp)


def kernel(x, p, g_mix, w_in, rel_bias, w_attn_br, conv_w, conv_b, ln_g, ln_b, w_conv_br, w_o, g_ffn,
           w_router_g, b_router_g, w_router_e, b_router_e, w_e_gate, w_e_up, w_e_down, g_ple, w_ple_gate,
           w_ple_proj, g_final):
    B, S, D = x.shape
    T = B * S
    depth = w_in.shape[0]
    width = N_HEADS * HEAD_DIM
    assert D == width and conv_w.shape[2] == D, "column-block indexing assumes all branch widths equal d_model"
    assert S % MOBA_BLOCK == 0
    assert depth == 1, "the final RMSNorm is fused into the last stage of a single layer"
    a_blk = 3 * width // D
    gate_blk = a_blk + 2

    h = x.reshape(T, D)
    for i in range(depth):
        proj = _in_proj(h, g_mix[i][None], w_in[i].astype(BF16))
        proj3 = proj.reshape(B, S, -1)
        attn = _moba(proj3, rel_bias)
        conv = _conv(proj3, conv_w[i], conv_b[i][None], ln_g[i][None], ln_b[i][None], a_blk)
        h = _mix(h, attn.reshape(T, D), conv.reshape(T, D), proj,
                 w_attn_br[i].astype(BF16), w_conv_br[i].astype(BF16), w_o[i].astype(BF16), gate_blk)
        xs, dest, cnt = _moe_sort(h, g_ffn[i][None], w_router_e[i], b_router_e[i], w_router_g[i], b_router_g[i])
        n_tok_tiles = T // MOE_TOK
        worst_chunks = (MOE_TOPK * T + n_tok_tiles * N_EXPERTS * (MOE_CHUNK - 1)) // MOE_CHUNK
        n_mm_tiles = -(-worst_chunks // MOE_CPM) + N_EXPERTS
        gather_tbl, return_tbl, tile_expert, n_used = _moe_tables(cnt[:, :, 0], n_mm_tiles)
        ys = _moe_experts(xs, gather_tbl, tile_expert, n_used,
                          w_e_gate[i], w_e_up[i], w_e_down[i], n_mm_tiles)
        out = _finish(ys, return_tbl, dest, h, p[i].reshape(T, -1), g_ple[i][None], g_final[None],
                      w_ple_gate[i].astype(BF16), w_ple_proj[i].astype(BF16))
    return out.reshape(B, S, D)
```

```python
import functools
import math

import numpy as np
import jax
import jax.numpy as jnp
from jax import lax
from jax.experimental import pallas as pl
from jax.experimental.pallas import tpu as pltpu

F32 = jnp.float32
BF16 = jnp.bfloat16

N_HEADS = 16
HEAD_DIM = 128
MOBA_BLOCK = 256
MOBA_TOPK = 3
N_BUCKETS = 32
MAX_DISTANCE = 128
CONV_KERNEL = 31
N_GROUPS = 4
EXPERTS_PER_GROUP = 4
N_EXPERTS = N_GROUPS * EXPERTS_PER_GROUP
EPS = 1e-6
NEG = -1e30
LOG2E = 1.4426950408889634

V7X_VMEM_LIMIT_BYTES = 56 * 1024 * 1024
LANES = 128
SUBLANES = 8

_NT = (((1,), (1,)), ((), ()))


def _cparams(*sem):
    return pltpu.CompilerParams(dimension_semantics=sem, vmem_limit_bytes=V7X_VMEM_LIMIT_BYTES)


def _resident(shape):
    return pl.BlockSpec(shape, lambda *_: (0,) * len(shape), pipeline_mode=pl.Buffered(1))


def _sigmoid(x):
    return 0.5 * jnp.tanh(0.5 * x) + 0.5


def _rms(x, g):
    return x * lax.rsqrt(jnp.mean(x * x, axis=-1, keepdims=True) + EPS) * g


def _in_proj_kernel(x_ref, g_ref, w_ref, o_ref, xn_ref):
    @pl.when(pl.program_id(1) == 0)
    def _():
        xn_ref[...] = _rms(x_ref[...], g_ref[...]).astype(BF16)

    o_ref[...] = jnp.dot(xn_ref[...], w_ref[...], preferred_element_type=F32).astype(o_ref.dtype)


def _in_proj(x2, g, w, tm=1024, tn=2048):
    T, D = x2.shape
    N = w.shape[1]
    return pl.pallas_call(
        _in_proj_kernel,
        grid=(T // tm, N // tn),
        in_specs=[pl.BlockSpec((tm, D), lambda i, j: (i, 0)),
                  pl.BlockSpec((1, D), lambda i, j: (0, 0)),
                  pl.BlockSpec((D, tn), lambda i, j: (0, j))],
        out_specs=pl.BlockSpec((tm, tn), lambda i, j: (i, j)),
        out_shape=jax.ShapeDtypeStruct((T, N), BF16),
        scratch_shapes=[pltpu.VMEM((tm, D), BF16)],
        compiler_params=_cparams("parallel", "arbitrary"),
        name="in_proj",
    )(x2, g, w)


MOBA_PV_ROWS = HEAD_DIM + 16
MOBA_HEADS_PER_STEP = 4


def _rel_buckets_t():
    k = np.arange(MOBA_BLOCK)[:, None]
    q = np.arange(MOBA_BLOCK)[None, :]
    dist = np.stack([q - k, MOBA_BLOCK + q - k]).astype(np.int32)
    n = np.maximum(dist, 0)
    max_exact = N_BUCKETS // 2
    nf = np.maximum(n, 1).astype(np.float32)
    large = max_exact + (np.log(nf / np.float32(max_exact)) / np.float32(math.log(MAX_DISTANCE / max_exact))
                         * np.float32(N_BUCKETS - max_exact)).astype(np.int32)
    large = np.minimum(large, N_BUCKETS - 1)
    return np.where(n < max_exact, n, large).astype(np.int32)


def _moba_kernel(tbl_ref, bkt_ref, q_ref, k_ref, v_ref, o_ref,
                 kmean_ref, vt_ref, bias_ref, qt_ref, sel_ref, s_ref, p_ref, m_ref, a_ref, acc_ref,
                 *, n_blocks):
    hg = pl.program_id(0)
    b = pl.program_id(1)
    BS = MOBA_BLOCK
    HPS = MOBA_HEADS_PER_STEP
    lanes = [slice(hh * HEAD_DIM, (hh + 1) * HEAD_DIM) for hh in range(HPS)]

    @pl.when(b == 0)
    def _build_bias():
        krow = lax.broadcasted_iota(jnp.int32, (BS, BS), 0)
        qcol = lax.broadcasted_iota(jnp.int32, (BS, BS), 1)
        for hh in range(HPS):
            bias_ref[hh] = jnp.zeros((2, BS, BS), F32)

            def fill(n, carry):
                bias_ref[hh] = jnp.where(bkt_ref[...] == n, tbl_ref[hg * HPS + hh, n] * LOG2E, bias_ref[hh])
                return carry

            lax.fori_loop(0, N_BUCKETS, fill, 0)
            bias_ref[hh, 0] = jnp.where(qcol >= krow, bias_ref[hh, 0], NEG)

    ones_row = (lax.broadcasted_iota(jnp.int32, (MOBA_PV_ROWS - HEAD_DIM, BS), 0) == 0).astype(BF16)

    def prep(n, carry):
        rows = pl.ds(pl.multiple_of(n * BS, BS), BS)
        for hh in range(HPS):
            kmean_ref[hh, pl.ds(n, 1), :] = jnp.mean(k_ref[0, rows, lanes[hh]].astype(F32), axis=0, keepdims=True)
            vt_ref[hh, n, :HEAD_DIM] = v_ref[0, rows, lanes[hh]].astype(F32).T.astype(BF16)
            vt_ref[hh, n, HEAD_DIM:] = ones_row
        return carry

    lax.fori_loop(0, n_blocks, prep, 0)

    def q_block(qi, carry):
        _moba_q_block(qi, hg, tbl_ref, q_ref, k_ref, o_ref, kmean_ref, vt_ref, bias_ref, qt_ref, sel_ref,
                      s_ref, p_ref, m_ref, a_ref, acc_ref, n_blocks=n_blocks)
        return carry

    lax.fori_loop(0, n_blocks, q_block, 0)


def _moba_q_block(qi, hg, tbl_ref, q_ref, k_ref, o_ref, kmean_ref, vt_ref, bias_ref, qt_ref, sel_ref,
                  s_ref, p_ref, m_ref, a_ref, acc_ref, *, n_blocks):
    BS = MOBA_BLOCK
    HPS = MOBA_HEADS_PER_STEP
    scale = HEAD_DIM ** -0.5
    lanes = [slice(hh * HEAD_DIM, (hh + 1) * HEAD_DIM) for hh in range(HPS)]
    q_rows = pl.ds(pl.multiple_of(qi * BS, BS), BS)

    def qk_stage(hh, t):
        j0 = pl.multiple_of(jnp.maximum(qi - t, 0) * BS, BS)
        s_ref[hh] = jnp.dot(k_ref[0, pl.ds(j0, BS), lanes[hh]], qt_ref[hh], preferred_element_type=F32)

    def softmax_stage(hh, t, bias_of=None, far_bias=None):
        for c in range(BS // LANES):
            qs = slice(c * LANES, (c + 1) * LANES)
            chosen = (lax.shift_right_logical(sel_ref[hh, :, qs], qi - t) & 1) == 1
            st = s_ref[hh, :, qs] * (scale * LOG2E)
            shift = 0.0 if far_bias is None else far_bias
            if bias_of is not None:
                st = st + bias_of(qs)
            m_prev = m_ref[hh, :, qs]
            m_new = jnp.where(chosen, jnp.maximum(m_prev, jnp.max(st, axis=0, keepdims=True) + shift), m_prev)
            a_ref[hh, :, qs] = jnp.exp2(m_prev - m_new)
            m_ref[hh, :, qs] = m_new
            p_ref[hh, :, qs] = jnp.exp2(st - jnp.where(chosen, m_new - shift, -NEG)).astype(BF16)

    def pv_stage(hh, t):
        acc_ref[hh] = a_ref[hh] * acc_ref[hh] + jnp.dot(vt_ref[hh, qi - t], p_ref[hh],
                                                        preferred_element_type=F32)

    for hh in range(HPS):
        qt = q_ref[0, q_rows, lanes[hh]].astype(F32).T.astype(BF16)
        qt_ref[hh] = qt
        gate = jnp.dot(kmean_ref[hh].astype(BF16), qt, preferred_element_type=F32)
        blk = lax.broadcasted_iota(jnp.int32, gate.shape, 0)
        blkf = blk.astype(F32)
        g = jnp.where(blk < qi, gate, NEG)
        bits = jnp.full((1, BS), lax.shift_left(jnp.int32(1), qi), jnp.int32)
        for _ in range(MOBA_TOPK):
            mx = jnp.max(g, axis=0, keepdims=True)
            ix = jnp.min(jnp.where(g == mx, blkf, float(n_blocks)), axis=0, keepdims=True).astype(jnp.int32)
            bits = bits | jnp.where(ix < qi, lax.shift_left(jnp.int32(1), ix), 0)
            g = jnp.where(blk == ix, -jnp.inf, g)
        sel_ref[hh] = bits
        m_ref[hh] = jnp.full((1, BS), NEG, F32)
        acc_ref[hh] = jnp.zeros((MOBA_PV_ROWS, BS), F32)
        qk_stage(hh, 0)

    for hh in range(HPS):
        softmax_stage(hh, 0, lambda qs, hh=hh: bias_ref[hh, 0, :, qs])
    for hh in range(HPS):
        qk_stage(hh, 1)

    @pl.when(qi >= 1)
    def _past_blocks():
        for hh in range(HPS):
            pv_stage(hh, 0)
        for hh in range(HPS):
            softmax_stage(hh, 1, lambda qs, hh=hh: bias_ref[hh, 1, :, qs])
        for hh in range(HPS):
            qk_stage(hh, 2)

        def sweep(t):
            for hh in range(HPS):
                pv_stage(hh, t - 1)
            for hh in range(HPS):
                softmax_stage(hh, t, far_bias=tbl_ref[hg * HPS + hh, N_BUCKETS - 1] * LOG2E)
            for hh in range(HPS):
                qk_stage(hh, t + 1)

        n_far = qi - 1

        def sweep_pair(i, carry):
            sweep(2 + 2 * i)
            sweep(3 + 2 * i)
            return carry

        lax.fori_loop(0, n_far // 2, sweep_pair, 0)

        @pl.when(n_far % 2 == 1)
        def _odd_block():
            sweep(qi)

    for hh in range(HPS):
        pv_stage(hh, qi)
        denom = acc_ref[hh, HEAD_DIM:HEAD_DIM + 1, :]
        out_t = acc_ref[hh, :HEAD_DIM, :] * (1.0 / denom)
        o_ref[0, q_rows, lanes[hh]] = out_t.T.astype(o_ref.dtype)


def _moba(proj3, rel_bias):
    B, S, _ = proj3.shape
    BS = MOBA_BLOCK
    HPS = MOBA_HEADS_PER_STEP
    W = HPS * HEAD_DIM
    nb = S // BS
    n_groups = N_HEADS // HPS
    bkt = jnp.asarray(_rel_buckets_t())
    tbl = rel_bias.astype(F32).T
    grid_spec = pltpu.PrefetchScalarGridSpec(
        num_scalar_prefetch=1,
        grid=(n_groups, B),
        in_specs=[pl.BlockSpec((2, BS, BS), lambda h, b, t: (0, 0, 0)),
                  pl.BlockSpec((1, S, W), lambda h, b, t: (b, 0, h)),
                  pl.BlockSpec((1, S, W), lambda h, b, t: (b, 0, n_groups + h)),
                  pl.BlockSpec((1, S, W), lambda h, b, t: (b, 0, 2 * n_groups + h))],
        out_specs=pl.BlockSpec((1, S, W), lambda h, b, t: (b, 0, h)),
        scratch_shapes=[pltpu.VMEM((HPS, nb, HEAD_DIM), F32),
                        pltpu.VMEM((HPS, nb, MOBA_PV_ROWS, BS), BF16),
                        pltpu.VMEM((HPS, 2, BS, BS), F32),
                        pltpu.VMEM((HPS, HEAD_DIM, BS), BF16),
                        pltpu.VMEM((HPS, 1, BS), jnp.int32),
                        pltpu.VMEM((HPS, BS, BS), F32),
                        pltpu.VMEM((HPS, BS, BS), BF16),
                        pltpu.VMEM((HPS, 1, BS), F32),
                        pltpu.VMEM((HPS, 1, BS), F32),
                        pltpu.VMEM((HPS, MOBA_PV_ROWS, BS), F32)],
    )
    return pl.pallas_call(
        functools.partial(_moba_kernel, n_blocks=nb),
        grid_spec=grid_spec,
        out_shape=jax.ShapeDtypeStruct((B, S, N_HEADS * HEAD_DIM), BF16),
        compiler_params=_cparams("arbitrary", "arbitrary"),
        name="moba",
    )(tbl, bkt, proj3, proj3, proj3)


CONV_HALO = 32
CONV_ROWS = 64
CONV_COLS = 256


def _conv_kernel(a_ref, g_ref, ah_ref, gh_ref, w_ref, cb_ref, lng_ref, lnb_ref, o_ref, u_ref, sh_ref, c_ref,
                 *, ts):
    i = pl.program_id(1)
    D = u_ref.shape[1]
    u_ref[CONV_HALO:, :] = a_ref[0].astype(F32) * _sigmoid(g_ref[0].astype(F32))
    uh = ah_ref[0].astype(F32) * _sigmoid(gh_ref[0].astype(F32))
    u_ref[:CONV_HALO, :] = jnp.where(i > 0, uh, 0.0)

    first = CONV_HALO - (CONV_KERNEL - 1)
    sh_rows = sh_ref.shape[1]

    def col_body(c, carry):
        cols = pl.ds(pl.multiple_of(c * CONV_COLS, CONV_COLS), CONV_COLS)
        for s in range(1, SUBLANES):
            sh_ref[s] = u_ref[s:s + sh_rows, cols]
        for r in range(ts // CONV_ROWS):
            acc = jnp.zeros((CONV_ROWS, CONV_COLS), F32)
            for k in range(CONV_KERNEL):
                s = (first + k) % SUBLANES
                r0 = r * CONV_ROWS + first + k - s
                tap = u_ref[r0:r0 + CONV_ROWS, cols] if s == 0 else sh_ref[s, r0:r0 + CONV_ROWS, :]
                acc = acc + jnp.tile(w_ref[k, :, cols], (CONV_ROWS // SUBLANES, 1)) * tap
            c_ref[r * CONV_ROWS:(r + 1) * CONV_ROWS, cols] = acc + cb_ref[:, cols]
        return carry

    lax.fori_loop(0, D // CONV_COLS, col_body, 0)

    y = c_ref[...]
    mu = jnp.mean(y, axis=-1, keepdims=True)
    yc = y - mu
    var = jnp.mean(yc * yc, axis=-1, keepdims=True)
    z = yc * lax.rsqrt(var + EPS) * lng_ref[...] + lnb_ref[...]
    o_ref[0] = (z * _sigmoid(z)).astype(o_ref.dtype)


def _conv(proj3, conv_w, conv_b, ln_g, ln_b, a_blk, ts=512):
    B, S, _ = proj3.shape
    D = conv_w.shape[1]
    hpb = ts // CONV_HALO
    w_rows = jnp.broadcast_to(conv_w[:, None, :], (CONV_KERNEL, SUBLANES, D))

    def halo(col):
        return pl.BlockSpec((1, CONV_HALO, D), lambda b, i: (b, jnp.maximum(i * hpb - 1, 0), col))

    vec = pl.BlockSpec((1, D), lambda b, i: (0, 0))
    return pl.pallas_call(
        functools.partial(_conv_kernel, ts=ts),
        grid=(B, S // ts),
        in_specs=[pl.BlockSpec((1, ts, D), lambda b, i: (b, i, a_blk)),
                  pl.BlockSpec((1, ts, D), lambda b, i: (b, i, a_blk + 1)),
                  halo(a_blk), halo(a_blk + 1),
                  pl.BlockSpec((CONV_KERNEL, SUBLANES, D), lambda b, i: (0, 0, 0)),
                  vec, vec, vec],
        out_specs=pl.BlockSpec((1, ts, D), lambda b, i: (b, i, 0)),
        out_shape=jax.ShapeDtypeStruct((B, S, D), BF16),
        scratch_shapes=[pltpu.VMEM((CONV_HALO + ts, D), F32),
                        pltpu.VMEM((SUBLANES, CONV_HALO + ts - SUBLANES, CONV_COLS), F32),
                        pltpu.VMEM((ts, D), F32)],
        compiler_params=_cparams("parallel", "arbitrary"),
        name="conv",
    )(proj3, proj3, proj3, proj3, w_rows, conv_b, ln_g, ln_b)


def _mix_kernel(x_ref, at_ref, cv_ref, ga_ref, gc_ref, wa_ref, wc_ref, wo_ref, o_ref):
    ya = jnp.dot(at_ref[...], wa_ref[...], preferred_element_type=F32)
    yc = jnp.dot(cv_ref[...], wc_ref[...], preferred_element_type=F32)
    mixed = (jax.nn.sigmoid(ga_ref[...].astype(F32)) * ya + jax.nn.sigmoid(gc_ref[...].astype(F32)) * yc)
    o_ref[...] = x_ref[...] + jnp.dot(mixed.astype(BF16), wo_ref[...], preferred_element_type=F32)


def _mix(x2, attn2, conv2, proj2, wa, wc, wo, gate_blk, tm=256):
    T, D = x2.shape
    row = lambda col: pl.BlockSpec((tm, D), lambda i: (i, col))
    return pl.pallas_call(
        _mix_kernel,
        grid=(T // tm,),
        in_specs=[row(0), row(0), row(0), row(gate_blk), row(gate_blk + 1),
                  _resident((D, D)), _resident((D, D)), _resident((D, D))],
        out_specs=row(0),
        out_shape=jax.ShapeDtypeStruct((T, D), F32),
        compiler_params=_cparams("parallel"),
        name="mix",
    )(x2, attn2, conv2, proj2, proj2, wa, wc, wo)


MOE_TOK = 512
MOE_CHUNK = 16
MOE_TM = 512
MOE_TOPK = 2
_MOE_WORST_ROWS = MOE_TOPK * MOE_TOK + N_EXPERTS * (MOE_CHUNK - 1)
MOE_ROWS = -(-_MOE_WORST_ROWS // LANES) * LANES
MOE_CPT = MOE_ROWS // MOE_CHUNK
MOE_CPM = MOE_TM // MOE_CHUNK
MOE_EXTRA = LANES


def _route_t(lt):
    row = lax.broadcasted_iota(jnp.int32, lt.shape, 0)
    rowf = row.astype(F32)
    big = float(lt.shape[0])
    is_grp = (row >= N_EXPERTS) & (row < N_EXPERTS + N_GROUPS)
    cl = jnp.where(is_grp, lt, -jnp.inf)
    mg = jnp.max(cl, axis=0, keepdims=True)
    g_star = jnp.min(jnp.where(cl == mg, rowf, big), axis=0, keepdims=True).astype(jnp.int32) - N_EXPERTS
    p_g = 1.0 / jnp.sum(jnp.where(is_grp, jnp.exp(cl - mg), 0.0), axis=0, keepdims=True)
    in_grp = (row < N_EXPERTS) & ((row // EXPERTS_PER_GROUP) == g_star)
    f1 = jnp.where(in_grp, lt, -jnp.inf)
    v1 = jnp.max(f1, axis=0, keepdims=True)
    i1 = jnp.min(jnp.where(f1 == v1, rowf, big), axis=0, keepdims=True).astype(jnp.int32)
    f2 = jnp.where(row == i1, -jnp.inf, f1)
    v2 = jnp.max(f2, axis=0, keepdims=True)
    i2 = jnp.min(jnp.where(f2 == v2, rowf, big), axis=0, keepdims=True).astype(jnp.int32)
    e2 = jnp.exp(v2 - v1)
    return i1, i2, p_g / (1.0 + e2), p_g * e2 / (1.0 + e2)


def _split_bf16(x):
    hi = x.astype(BF16)
    return hi, (x - hi.astype(F32)).astype(BF16)


def _moe_sort_kernel(h_ref, g_ref, wh_ref, wl_ref, br_ref, xs_ref, dest_ref, cnt_ref):
    tok = h_ref.shape[0]
    D = h_ref.shape[1]
    hn = _rms(h_ref[...], g_ref[...])
    hn_hi, hn_lo = _split_bf16(hn)
    nt_dot = lambda a, b: lax.dot_general(a, b, _NT, preferred_element_type=F32)
    w_hi = wh_ref[...]
    both = nt_dot(jnp.concatenate([w_hi, wl_ref[...]], axis=0), hn_hi)
    lt = both[:MOE_ROUTE_ROWS] + both[MOE_ROUTE_ROWS:] + nt_dot(w_hi, hn_lo) + br_ref[...]
    i1, i2, c1, c2 = _route_t(lt)

    erow = lax.broadcasted_iota(jnp.int32, (N_EXPERTS, tok), 0)
    oh1 = erow == i1
    oh2 = erow == i2
    oh = (oh1 | oh2).astype(F32)
    earlier = (lax.broadcasted_iota(jnp.int32, (tok, tok), 0) < lax.broadcasted_iota(jnp.int32, (tok, tok), 1))
    cum = jnp.dot(oh.astype(BF16), earlier.astype(BF16), preferred_element_type=F32)
    cnt = jnp.sum(oh, axis=1, keepdims=True)
    n_chunks = jnp.floor((cnt + (MOE_CHUNK - 1)) * (1.0 / MOE_CHUNK))
    ncb = jnp.broadcast_to(n_chunks, (N_EXPERTS, LANES))
    erow_l = lax.broadcasted_iota(jnp.int32, (N_EXPERTS, LANES), 0)
    first_chunk = jnp.zeros((N_EXPERTS, LANES), F32)
    for e in range(N_EXPERTS - 1):
        first_chunk = first_chunk + jnp.where(erow_l > e, ncb[e:e + 1, :], 0.0)
    base = jnp.tile(first_chunk, (1, tok // LANES)) * MOE_CHUNK + cum
    d1 = jnp.sum(jnp.where(oh1, base, 0.0), axis=0, keepdims=True).astype(jnp.int32)
    d2 = jnp.sum(jnp.where(oh2, base, 0.0), axis=0, keepdims=True).astype(jnp.int32)

    srow = lax.broadcasted_iota(jnp.int32, (MOE_ROWS, tok), 0)
    p1 = srow == d1
    p2 = srow == d2
    perm1 = p1.astype(BF16)
    perm2 = p2.astype(BF16)
    xs_ref[:, :D] = jnp.dot(perm1 + perm2, hn_hi, preferred_element_type=F32).astype(BF16)
    xrow = lax.broadcasted_iota(jnp.int32, (MOE_EXTRA, tok), 0)

    def weight_rows(c):
        hi = c.astype(BF16).astype(F32)
        return jnp.where(xrow == 0, hi, jnp.where(xrow == 1, c - hi, 0.0)).astype(BF16)

    xs_ref[:, D:] = (nt_dot(perm1, weight_rows(c1)) + nt_dot(perm2, weight_rows(c2))).astype(BF16)

    r8 = lax.broadcasted_iota(jnp.int32, (SUBLANES, tok), 0)
    dest_ref[0] = jnp.where(r8 == 0, d1, jnp.where(r8 == 1, d2, 0))
    cnt_ref[0] = jnp.broadcast_to(cnt, (N_EXPERTS, LANES)).astype(jnp.int32)


MOE_ROUTE_ROWS = 32


def _moe_sort(h1, g, w_router_e, b_router_e, w_router_g, b_router_g):
    T, D = h1.shape
    nt = T // MOE_TOK
    pad = MOE_ROUTE_ROWS - N_EXPERTS - N_GROUPS
    wr = jnp.concatenate([w_router_e.T, w_router_g.T, jnp.zeros((pad, D), F32)], axis=0)
    wr_hi = wr.astype(BF16)
    wr_lo = (wr - wr_hi.astype(F32)).astype(BF16)
    br = jnp.concatenate([b_router_e, b_router_g, jnp.zeros((pad,), F32)])[:, None]
    return pl.pallas_call(
        _moe_sort_kernel,
        grid=(nt,),
        in_specs=[pl.BlockSpec((MOE_TOK, D), lambda i: (i, 0)),
                  pl.BlockSpec((1, D), lambda i: (0, 0)),
                  pl.BlockSpec((MOE_ROUTE_ROWS, D), lambda i: (0, 0)),
                  pl.BlockSpec((MOE_ROUTE_ROWS, D), lambda i: (0, 0)),
                  pl.BlockSpec((MOE_ROUTE_ROWS, 1), lambda i: (0, 0))],
        out_specs=[pl.BlockSpec((MOE_ROWS, D + MOE_EXTRA), lambda i: (i, 0)),
                   pl.BlockSpec((1, SUBLANES, MOE_TOK), lambda i: (i, 0, 0)),
                   pl.BlockSpec((1, N_EXPERTS, LANES), lambda i: (i, 0, 0))],
        out_shape=[jax.ShapeDtypeStruct((nt * MOE_ROWS, D + MOE_EXTRA), BF16),
                   jax.ShapeDtypeStruct((nt, SUBLANES, MOE_TOK), jnp.int32),
                   jax.ShapeDtypeStruct((nt, N_EXPERTS, LANES), jnp.int32)],
        compiler_params=_cparams("parallel"),
        name="moe_sort",
    )(h1, g, wr_hi, wr_lo, br)


def _moe_tables(cnt, n_mm_tiles):
    nt = cnt.shape[0]
    pc = (cnt + (MOE_CHUNK - 1)) // MOE_CHUNK
    run_end = jnp.cumsum(pc, axis=1)
    run_first = run_end - pc
    per_expert = jnp.sum(pc, axis=0)
    mm_tiles = (per_expert + (MOE_CPM - 1)) // MOE_CPM
    mm_end = jnp.cumsum(mm_tiles)
    slot_base = (mm_end - mm_tiles)[None, :] * MOE_CPM + (jnp.cumsum(pc, axis=0) - pc)
    k = jnp.arange(MOE_CPT, dtype=jnp.int32)[None, :, None]
    expert_of = jnp.sum((run_end[:, None, :] <= k).astype(jnp.int32), axis=2)
    valid = expert_of < N_EXPERTS
    pick = expert_of[:, :, None] == jnp.arange(N_EXPERTS, dtype=jnp.int32)
    take = lambda a: jnp.sum(jnp.where(pick, a[:, None, :], 0), axis=2)
    slot = take(slot_base) + (k[:, :, 0] - take(run_first))
    src = (jnp.arange(nt, dtype=jnp.int32)[:, None] * MOE_CPT + k[:, :, 0]).reshape(-1)
    slot = jnp.where(valid, slot, n_mm_tiles * MOE_CPM).reshape(-1)
    gather_tbl = jnp.zeros((n_mm_tiles * MOE_CPM,), jnp.int32).at[slot].set(src, mode="drop")
    return_tbl = jnp.where(valid.reshape(-1), slot, 0).astype(jnp.int32)
    g = jnp.arange(n_mm_tiles, dtype=jnp.int32)
    n_used = mm_end[-1].astype(jnp.int32)
    tile_expert = jnp.sum((mm_end[None, :] <= jnp.minimum(g, n_used - 1)[:, None]).astype(jnp.int32), axis=1)
    return gather_tbl, return_tbl, tile_expert.astype(jnp.int32), n_used.reshape(1)


def _moe_expert_kernel(gt_ref, te_ref, nu_ref, *refs):
    chunk_refs = refs[:MOE_CPM]
    wg_ref, wu_ref, wd_ref, o_ref, wg_bf, wu_bf, wd_bf = refs[MOE_CPM:]
    D = o_ref.shape[1]
    g = pl.program_id(0)
    used = g < nu_ref[0]

    @pl.when(used & ((g == 0) | (te_ref[g] != te_ref[jnp.maximum(g - 1, 0)])))
    def _():
        wg_bf[...] = wg_ref[0].astype(BF16)
        wu_bf[...] = wu_ref[0].astype(BF16)
        wd_bf[...] = wd_ref[0].astype(BF16)

    @pl.when(used)
    def _():
        xs = jnp.concatenate([r[...] for r in chunk_refs], axis=0)
        x = xs[:, :D]
        extra = xs[:, D:].astype(F32)
        c = extra[:, 0:1] + extra[:, 1:2]
        gate = jnp.dot(x, wg_bf[...], preferred_element_type=F32)
        up = jnp.dot(x, wu_bf[...], preferred_element_type=F32)
        hh = (gate * jax.nn.sigmoid(gate)) * up * c
        o_ref[...] = jnp.dot(hh.astype(BF16), wd_bf[...], preferred_element_type=F32).astype(o_ref.dtype)

    @pl.when(jnp.logical_not(used))
    def _():
        o_ref[...] = jnp.zeros(o_ref.shape, o_ref.dtype)


def _moe_experts(xs, gather_tbl, tile_expert, n_used, wg, wu, wd, n_mm_tiles):
    W = xs.shape[1]
    E, D, FF = wg.shape

    def chunk_spec(c):
        return pl.BlockSpec((MOE_CHUNK, W), lambda g, gt, te, nu: (gt[g * MOE_CPM + c], 0))

    grid_spec = pltpu.PrefetchScalarGridSpec(
        num_scalar_prefetch=3,
        grid=(n_mm_tiles,),
        in_specs=[chunk_spec(c) for c in range(MOE_CPM)] + [
            pl.BlockSpec((1, D, FF), lambda g, gt, te, nu: (te[g], 0, 0)),
            pl.BlockSpec((1, D, FF), lambda g, gt, te, nu: (te[g], 0, 0)),
            pl.BlockSpec((1, FF, D), lambda g, gt, te, nu: (te[g], 0, 0))],
        out_specs=pl.BlockSpec((MOE_TM, D), lambda g, gt, te, nu: (g, 0)),
        scratch_shapes=[pltpu.VMEM((D, FF), BF16), pltpu.VMEM((D, FF), BF16), pltpu.VMEM((FF, D), BF16)],
    )
    return pl.pallas_call(
        _moe_expert_kernel,
        grid_spec=grid_spec,
        out_shape=jax.ShapeDtypeStruct((n_mm_tiles * MOE_TM, D), BF16),
        compiler_params=_cparams("arbitrary"),
        name="moe_experts",
    )(gather_tbl, tile_expert, n_used, *([xs] * MOE_CPM), wg, wu, wd)


def _finish_kernel(rt_ref, *refs):
    chunk_refs = refs[:MOE_CPT]
    dest_ref, h_ref, p_ref, gp_ref, gf_ref, wg_ref, wp_ref, o_ref, ys_ref = refs[MOE_CPT:]
    tok = h_ref.shape[0]
    for k, r in enumerate(chunk_refs):
        ys_ref[k * MOE_CHUNK:(k + 1) * MOE_CHUNK, :] = r[...]
    dcol = dest_ref[0].astype(F32).T
    lane = lax.broadcasted_iota(jnp.int32, (tok, MOE_ROWS), 1).astype(F32)
    unsort = ((lane == dcol[:, 0:1]) | (lane == dcol[:, 1:2])).astype(BF16)
    h = h_ref[...] + jnp.dot(unsort, ys_ref[...], preferred_element_type=F32)
    t = _rms(h, gp_ref[...]).astype(BF16)
    gate = jax.nn.sigmoid(jnp.dot(t, wg_ref[...], preferred_element_type=F32))
    emb = jnp.dot(p_ref[...].astype(BF16), wp_ref[...], preferred_element_type=F32)
    o_ref[...] = _rms(h + emb * gate, gf_ref[...])


def _finish(ys, return_tbl, dest, h1, p2, g_ple, g_final, wg, wp):
    T, D = h1.shape
    P = p2.shape[1]
    nt = T // MOE_TOK

    def chunk_spec(k):
        return pl.BlockSpec((MOE_CHUNK, D), lambda i, rt: (rt[i * MOE_CPT + k], 0))

    vec = pl.BlockSpec((1, D), lambda i, rt: (0, 0))
    grid_spec = pltpu.PrefetchScalarGridSpec(
        num_scalar_prefetch=1,
        grid=(nt,),
        in_specs=[chunk_spec(k) for k in range(MOE_CPT)] + [
            pl.BlockSpec((1, SUBLANES, MOE_TOK), lambda i, rt: (i, 0, 0)),
            pl.BlockSpec((MOE_TOK, D), lambda i, rt: (i, 0)),
            pl.BlockSpec((MOE_TOK, P), lambda i, rt: (i, 0)),
            vec, vec,
            pl.BlockSpec((D, D), lambda i, rt: (0, 0), pipeline_mode=pl.Buffered(1)),
            pl.BlockSpec((P, D), lambda i, rt: (0, 0), pipeline_mode=pl.Buffered(1))],
        out_specs=pl.BlockSpec((MOE_TOK, D), lambda i, rt: (i, 0)),
        scratch_shapes=[pltpu.VMEM((MOE_ROWS, D), BF16)],
    )
    return pl.pallas_call(
        _finish_kernel,
        grid_spec=grid_spec,
        out_shape=jax.ShapeDtypeStruct((T, D), F32),
        compiler_params=_cparams("arbitrary"),
        name="finish",
    )(return_tbl, *([ys] * MOE_CPT), dest, h1, p2, g_ple, g_final, wg, wp)


def kernel(x, p, g_mix, w_in, rel_bias, w_attn_br, conv_w, conv_b, ln_g, ln_b, w_conv_br, w_o, g_ffn,
           w_router_g, b_router_g, w_router_e, b_router_e, w_e_gate, w_e_up, w_e_down, g_ple, w_ple_gate,
           w_ple_proj, g_final):
    B, S, D = x.shape
    T = B * S
    depth = w_in.shape[0]
    width = N_HEADS * HEAD_DIM
    assert D == width and conv_w.shape[2] == D, "column-block indexing assumes all branch widths equal d_model"
    assert S % MOBA_BLOCK == 0
    assert depth == 1, "the final RMSNorm is fused into the last stage of a single layer"
    a_blk = 3 * width // D
    gate_blk = a_blk + 2

    h = x.reshape(T, D)
    for i in range(depth):
        proj = _in_proj(h, g_mix[i][None], w_in[i].astype(BF16))
        proj3 = proj.reshape(B, S, -1)
        attn = _moba(proj3, rel_bias)
        conv = _conv(proj3, conv_w[i], conv_b[i][None], ln_g[i][None], ln_b[i][None], a_blk)
        h = _mix(h, attn.reshape(T, D), conv.reshape(T, D), proj,
                 w_attn_br[i].astype(BF16), w_conv_br[i].astype(BF16), w_o[i].astype(BF16), gate_blk)
        xs, dest, cnt = _moe_sort(h, g_ffn[i][None], w_router_e[i], b_router_e[i], w_router_g[i], b_router_g[i])
        n_tok_tiles = T // MOE_TOK
        worst_chunks = (MOE_TOPK * T + n_tok_tiles * N_EXPERTS * (MOE_CHUNK - 1)) // MOE_CHUNK
        n_mm_tiles = -(-worst_chunks // MOE_CPM) + N_EXPERTS
        gather_tbl, return_tbl, tile_expert, n_used = _moe_tables(cnt[:, :, 0], n_mm_tiles)
        ys = _moe_experts(xs, gather_tbl, tile_expert, n_used,
                          w_e_gate[i], w_e_up[i], w_e_down[i], n_mm_tiles)
        out = _finish(ys, return_tbl, dest, h, p[i].reshape(T, -1), g_ple[i][None], g_final[None],
                      w_ple_gate[i].astype(BF16), w_ple_proj[i].astype(BF16))
    return out.reshape(B, S, D)
```

```python
import functools
import math

import numpy as np
import jax
import jax.numpy as jnp
from jax import lax
from jax.experimental import pallas as pl
from jax.experimental.pallas import tpu as pltpu

F32 = jnp.float32
BF16 = jnp.bfloat16

N_HEADS = 16
HEAD_DIM = 128
MOBA_BLOCK = 256
MOBA_TOPK = 3
N_BUCKETS = 32
MAX_DISTANCE = 128
CONV_KERNEL = 31
N_GROUPS = 4
EXPERTS_PER_GROUP = 4
N_EXPERTS = N_GROUPS * EXPERTS_PER_GROUP
EPS = 1e-6
NEG = -1e30
LOG2E = 1.4426950408889634

V7X_VMEM_LIMIT_BYTES = 56 * 1024 * 1024
LANES = 128
SUBLANES = 8

_NT = (((1,), (1,)), ((), ()))


def _cparams(*sem):
    return pltpu.CompilerParams(dimension_semantics=sem, vmem_limit_bytes=V7X_VMEM_LIMIT_BYTES)


def _resident(shape):
    return pl.BlockSpec(shape, lambda *_: (0,) * len(shape), pipeline_mode=pl.Buffered(1))


def _sigmoid(x):
    return 0.5 * jnp.tanh(0.5 * x) + 0.5


def _rms(x, g):
    return x * lax.rsqrt(jnp.mean(x * x, axis=-1, keepdims=True) + EPS) * g


def _in_proj_kernel(x_ref, g_ref, w_ref, o_ref, xn_ref):
    @pl.when(pl.program_id(1) == 0)
    def _():
        xn_ref[...] = _rms(x_ref[...], g_ref[...]).astype(BF16)

    o_ref[...] = jnp.dot(xn_ref[...], w_ref[...], preferred_element_type=F32).astype(o_ref.dtype)


def _in_proj(x2, g, w, tm=1024, tn=2048):
    T, D = x2.shape
    N = w.shape[1]
    return pl.pallas_call(
        _in_proj_kernel,
        grid=(T // tm, N // tn),
        in_specs=[pl.BlockSpec((tm, D), lambda i, j: (i, 0)),
                  pl.BlockSpec((1, D), lambda i, j: (0, 0)),
                  pl.BlockSpec((D, tn), lambda i, j: (0, j))],
        out_specs=pl.BlockSpec((tm, tn), lambda i, j: (i, j)),
        out_shape=jax.ShapeDtypeStruct((T, N), BF16),
        scratch_shapes=[pltpu.VMEM((tm, D), BF16)],
        compiler_params=_cparams("parallel", "arbitrary"),
        name="in_proj",
    )(x2, g, w)


MOBA_PV_ROWS = HEAD_DIM + 16
MOBA_Q_LANES = MOBA_BLOCK
MOBA_HEADS_PER_STEP = 4


def _rel_buckets_t():
    k = np.arange(MOBA_BLOCK)[:, None]
    q = np.arange(MOBA_BLOCK)[None, :]
    dist = np.stack([q - k, MOBA_BLOCK + q - k]).astype(np.int32)
    n = np.maximum(dist, 0)
    max_exact = N_BUCKETS // 2
    nf = np.maximum(n, 1).astype(np.float32)
    large = max_exact + (np.log(nf / np.float32(max_exact)) / np.float32(math.log(MAX_DISTANCE / max_exact))
                         * np.float32(N_BUCKETS - max_exact)).astype(np.int32)
    large = np.minimum(large, N_BUCKETS - 1)
    return np.where(n < max_exact, n, large).astype(np.int32)


def _moba_kernel(tbl_ref, bkt_ref, q_ref, k_ref, v_ref, o_ref,
                 kmean_ref, vt_ref, bias_ref, qt_ref, sel_ref, s_ref, p_ref, m_ref, a_ref, acc_ref,
                 *, n_blocks):
    hg = pl.program_id(0)
    b = pl.program_id(1)
    BS = MOBA_BLOCK
    HPS = MOBA_HEADS_PER_STEP
    lanes = [slice(hh * HEAD_DIM, (hh + 1) * HEAD_DIM) for hh in range(HPS)]

    @pl.when(b == 0)
    def _build_bias():
        krow = lax.broadcasted_iota(jnp.int32, (BS, BS), 0)
        qcol = lax.broadcasted_iota(jnp.int32, (BS, BS), 1)
        for hh in range(HPS):
            bias_ref[hh] = jnp.zeros((2, BS, BS), F32)

            def fill(n, carry):
                bias_ref[hh] = jnp.where(bkt_ref[...] == n, tbl_ref[hg * HPS + hh, n] * LOG2E, bias_ref[hh])
                return carry

            lax.fori_loop(0, N_BUCKETS, fill, 0)
            bias_ref[hh, 0] = jnp.where(qcol >= krow, bias_ref[hh, 0], NEG)

    ones_row = (lax.broadcasted_iota(jnp.int32, (MOBA_PV_ROWS - HEAD_DIM, BS), 0) == 0).astype(BF16)

    def prep(n, carry):
        rows = pl.ds(pl.multiple_of(n * BS, BS), BS)
        for hh in range(HPS):
            kmean_ref[hh, pl.ds(n, 1), :] = jnp.mean(k_ref[0, rows, lanes[hh]].astype(F32), axis=0, keepdims=True)
            vt_ref[hh, n, :HEAD_DIM] = v_ref[0, rows, lanes[hh]].astype(F32).T.astype(BF16)
            vt_ref[hh, n, HEAD_DIM:] = ones_row
        return carry

    lax.fori_loop(0, n_blocks, prep, 0)

    def q_block(qi, carry):
        _moba_q_block(qi, hg, tbl_ref, q_ref, k_ref, o_ref, kmean_ref, vt_ref, bias_ref, qt_ref, sel_ref,
                      s_ref, p_ref, m_ref, a_ref, acc_ref, n_blocks=n_blocks)
        return carry

    lax.fori_loop(0, n_blocks, q_block, 0)


def _moba_q_block(qi, hg, tbl_ref, q_ref, k_ref, o_ref, kmean_ref, vt_ref, bias_ref, qt_ref, sel_ref,
                  s_ref, p_ref, m_ref, a_ref, acc_ref, *, n_blocks):
    BS = MOBA_BLOCK
    HPS = MOBA_HEADS_PER_STEP
    scale = HEAD_DIM ** -0.5
    lanes = [slice(hh * HEAD_DIM, (hh + 1) * HEAD_DIM) for hh in range(HPS)]
    q_rows = pl.ds(pl.multiple_of(qi * BS, BS), BS)

    def qk_stage(hh, t):
        j0 = pl.multiple_of(jnp.maximum(qi - t, 0) * BS, BS)
        s_ref[hh] = jnp.dot(k_ref[0, pl.ds(j0, BS), lanes[hh]], qt_ref[hh], preferred_element_type=F32)

    def softmax_stage(hh, t, bias_of=None, far_bias=None):
        for c in range(BS // MOBA_Q_LANES):
            qs = slice(c * MOBA_Q_LANES, (c + 1) * MOBA_Q_LANES)
            chosen = (lax.shift_right_logical(sel_ref[hh, :, qs], qi - t) & 1) == 1
            c2 = scale * LOG2E
            if bias_of is not None:
                st = s_ref[hh, :, qs] * c2 + bias_of(qs)
                shift = 0.0
                top = jnp.max(st, axis=0, keepdims=True)
            else:
                st = None
                shift = far_bias
                top = jnp.max(s_ref[hh, :, qs], axis=0, keepdims=True) * c2 + shift
            m_prev = m_ref[hh, :, qs]
            m_new = jnp.where(chosen, jnp.maximum(m_prev, top), m_prev)
            a_ref[hh, :, qs] = jnp.exp2(m_prev - m_new)
            m_ref[hh, :, qs] = m_new
            cut = jnp.where(chosen, m_new - shift, -NEG)
            if st is None:
                st = s_ref[hh, :, qs] * c2
            p_ref[hh, :, qs] = jnp.exp2(st - cut).astype(BF16)

    def pv_stage(hh, t):
        acc_ref[hh] = a_ref[hh] * acc_ref[hh] + jnp.dot(vt_ref[hh, qi - t], p_ref[hh],
                                                        preferred_element_type=F32)

    for hh in range(HPS):
        qt = q_ref[0, q_rows, lanes[hh]].astype(F32).T.astype(BF16)
        qt_ref[hh] = qt
        gate = jnp.dot(kmean_ref[hh].astype(BF16), qt, preferred_element_type=F32)
        blk = lax.broadcasted_iota(jnp.int32, gate.shape, 0)
        blkf = blk.astype(F32)
        g = jnp.where(blk < qi, gate, NEG)
        bits = jnp.full((1, BS), lax.shift_left(jnp.int32(1), qi), jnp.int32)
        for _ in range(MOBA_TOPK):
            mx = jnp.max(g, axis=0, keepdims=True)
            ix = jnp.min(jnp.where(g == mx, blkf, float(n_blocks)), axis=0, keepdims=True).astype(jnp.int32)
            bits = bits | jnp.where(ix < qi, lax.shift_left(jnp.int32(1), ix), 0)
            g = jnp.where(blk == ix, -jnp.inf, g)
        sel_ref[hh] = bits
        m_ref[hh] = jnp.full((1, BS), NEG, F32)
        acc_ref[hh] = jnp.zeros((MOBA_PV_ROWS, BS), F32)
        qk_stage(hh, 0)

    for hh in range(HPS):
        softmax_stage(hh, 0, lambda qs, hh=hh: bias_ref[hh, 0, :, qs])
    for hh in range(HPS):
        qk_stage(hh, 1)

    @pl.when(qi >= 1)
    def _past_blocks():
        for hh in range(HPS):
            pv_stage(hh, 0)
        for hh in range(HPS):
            softmax_stage(hh, 1, lambda qs, hh=hh: bias_ref[hh, 1, :, qs])
        for hh in range(HPS):
            qk_stage(hh, 2)

        def sweep(t):
            for hh in range(HPS):
                pv_stage(hh, t - 1)
            for hh in range(HPS):
                softmax_stage(hh, t, far_bias=tbl_ref[hg * HPS + hh, N_BUCKETS - 1] * LOG2E)
            for hh in range(HPS):
                qk_stage(hh, t + 1)

        n_far = qi - 1

        def sweep_pair(i, carry):
            sweep(2 + 2 * i)
            sweep(3 + 2 * i)
            return carry

        lax.fori_loop(0, n_far // 2, sweep_pair, 0)

        @pl.when(n_far % 2 == 1)
        def _odd_block():
            sweep(qi)

    for hh in range(HPS):
        pv_stage(hh, qi)
        denom = acc_ref[hh, HEAD_DIM:HEAD_DIM + 1, :]
        out_t = acc_ref[hh, :HEAD_DIM, :] * (1.0 / denom)
        o_ref[0, q_rows, lanes[hh]] = out_t.T.astype(o_ref.dtype)


def _moba(proj3, rel_bias):
    B, S, _ = proj3.shape
    BS = MOBA_BLOCK
    HPS = MOBA_HEADS_PER_STEP
    W = HPS * HEAD_DIM
    nb = S // BS
    n_groups = N_HEADS // HPS
    bkt = jnp.asarray(_rel_buckets_t())
    tbl = rel_bias.astype(F32).T
    grid_spec = pltpu.PrefetchScalarGridSpec(
        num_scalar_prefetch=1,
        grid=(n_groups, B),
        in_specs=[pl.BlockSpec((2, BS, BS), lambda h, b, t: (0, 0, 0)),
                  pl.BlockSpec((1, S, W), lambda h, b, t: (b, 0, h)),
                  pl.BlockSpec((1, S, W), lambda h, b, t: (b, 0, n_groups + h)),
                  pl.BlockSpec((1, S, W), lambda h, b, t: (b, 0, 2 * n_groups + h))],
        out_specs=pl.BlockSpec((1, S, W), lambda h, b, t: (b, 0, h)),
        scratch_shapes=[pltpu.VMEM((HPS, nb, HEAD_DIM), F32),
                        pltpu.VMEM((HPS, nb, MOBA_PV_ROWS, BS), BF16),
                        pltpu.VMEM((HPS, 2, BS, BS), F32),
                        pltpu.VMEM((HPS, HEAD_DIM, BS), BF16),
                        pltpu.VMEM((HPS, 1, BS), jnp.int32),
                        pltpu.VMEM((HPS, BS, BS), F32),
                        pltpu.VMEM((HPS, BS, BS), BF16),
                        pltpu.VMEM((HPS, 1, BS), F32),
                        pltpu.VMEM((HPS, 1, BS), F32),
                        pltpu.VMEM((HPS, MOBA_PV_ROWS, BS), F32)],
    )
    return pl.pallas_call(
        functools.partial(_moba_kernel, n_blocks=nb),
        grid_spec=grid_spec,
        out_shape=jax.ShapeDtypeStruct((B, S, N_HEADS * HEAD_DIM), BF16),
        compiler_params=_cparams("arbitrary", "arbitrary"),
        name="moba",
    )(tbl, bkt, proj3, proj3, proj3)


CONV_HALO = 32
CONV_ROWS = 64
CONV_COLS = 256


def _conv_kernel(a_ref, g_ref, ah_ref, gh_ref, w_ref, cb_ref, lng_ref, lnb_ref, o_ref, u_ref, sh_ref, c_ref,
                 *, ts):
    i = pl.program_id(1)
    D = u_ref.shape[1]
    u_ref[CONV_HALO:, :] = a_ref[0].astype(F32) * _sigmoid(g_ref[0].astype(F32))
    uh = ah_ref[0].astype(F32) * _sigmoid(gh_ref[0].astype(F32))
    u_ref[:CONV_HALO, :] = jnp.where(i > 0, uh, 0.0)

    first = CONV_HALO - (CONV_KERNEL - 1)
    sh_rows = sh_ref.shape[1]

    def col_body(c, carry):
        cols = pl.ds(pl.multiple_of(c * CONV_COLS, CONV_COLS), CONV_COLS)
        for s in range(1, SUBLANES):
            sh_ref[s] = u_ref[s:s + sh_rows, cols]
        for r in range(ts // CONV_ROWS):
            acc = jnp.zeros((CONV_ROWS, CONV_COLS), F32)
            for k in range(CONV_KERNEL):
                s = (first + k) % SUBLANES
                r0 = r * CONV_ROWS + first + k - s
                tap = u_ref[r0:r0 + CONV_ROWS, cols] if s == 0 else sh_ref[s, r0:r0 + CONV_ROWS, :]
                acc = acc + jnp.tile(w_ref[k, :, cols], (CONV_ROWS // SUBLANES, 1)) * tap
            c_ref[r * CONV_ROWS:(r + 1) * CONV_ROWS, cols] = acc + cb_ref[:, cols]
        return carry

    lax.fori_loop(0, D // CONV_COLS, col_body, 0)

    y = c_ref[...]
    mu = jnp.mean(y, axis=-1, keepdims=True)
    yc = y - mu
    var = jnp.mean(yc * yc, axis=-1, keepdims=True)
    z = yc * lax.rsqrt(var + EPS) * lng_ref[...] + lnb_ref[...]
    o_ref[0] = (z * _sigmoid(z)).astype(o_ref.dtype)


def _conv(proj3, conv_w, conv_b, ln_g, ln_b, a_blk, ts=512):
    B, S, _ = proj3.shape
    D = conv_w.shape[1]
    hpb = ts // CONV_HALO
    w_rows = jnp.broadcast_to(conv_w[:, None, :], (CONV_KERNEL, SUBLANES, D))

    def halo(col):
        return pl.BlockSpec((1, CONV_HALO, D), lambda b, i: (b, jnp.maximum(i * hpb - 1, 0), col))

    vec = pl.BlockSpec((1, D), lambda b, i: (0, 0))
    return pl.pallas_call(
        functools.partial(_conv_kernel, ts=ts),
        grid=(B, S // ts),
        in_specs=[pl.BlockSpec((1, ts, D), lambda b, i: (b, i, a_blk)),
                  pl.BlockSpec((1, ts, D), lambda b, i: (b, i, a_blk + 1)),
                  halo(a_blk), halo(a_blk + 1),
                  pl.BlockSpec((CONV_KERNEL, SUBLANES, D), lambda b, i: (0, 0, 0)),
                  vec, vec, vec],
        out_specs=pl.BlockSpec((1, ts, D), lambda b, i: (b, i, 0)),
        out_shape=jax.ShapeDtypeStruct((B, S, D), BF16),
        scratch_shapes=[pltpu.VMEM((CONV_HALO + ts, D), F32),
                        pltpu.VMEM((SUBLANES, CONV_HALO + ts - SUBLANES, CONV_COLS), F32),
                        pltpu.VMEM((ts, D), F32)],
        compiler_params=_cparams("parallel", "arbitrary"),
        name="conv",
    )(proj3, proj3, proj3, proj3, w_rows, conv_b, ln_g, ln_b)


def _mix_kernel(x_ref, at_ref, cv_ref, ga_ref, gc_ref, wa_ref, wc_ref, wo_ref, o_ref):
    ya = jnp.dot(at_ref[...], wa_ref[...], preferred_element_type=F32)
    yc = jnp.dot(cv_ref[...], wc_ref[...], preferred_element_type=F32)
    mixed = (jax.nn.sigmoid(ga_ref[...].astype(F32)) * ya + jax.nn.sigmoid(gc_ref[...].astype(F32)) * yc)
    o_ref[...] = x_ref[...] + jnp.dot(mixed.astype(BF16), wo_ref[...], preferred_element_type=F32)


def _mix(x2, attn2, conv2, proj2, wa, wc, wo, gate_blk, tm=256):
    T, D = x2.shape
    row = lambda col: pl.BlockSpec((tm, D), lambda i: (i, col))
    return pl.pallas_call(
        _mix_kernel,
        grid=(T // tm,),
        in_specs=[row(0), row(0), row(0), row(gate_blk), row(gate_blk + 1),
                  _resident((D, D)), _resident((D, D)), _resident((D, D))],
        out_specs=row(0),
        out_shape=jax.ShapeDtypeStruct((T, D), F32),
        compiler_params=_cparams("parallel"),
        name="mix",
    )(x2, attn2, conv2, proj2, proj2, wa, wc, wo)


MOE_TOK = 512
MOE_CHUNK = 16
MOE_TM = 512
MOE_TOPK = 2
_MOE_WORST_ROWS = MOE_TOPK * MOE_TOK + N_EXPERTS * (MOE_CHUNK - 1)
MOE_ROWS = -(-_MOE_WORST_ROWS // LANES) * LANES
MOE_CPT = MOE_ROWS // MOE_CHUNK
MOE_CPM = MOE_TM // MOE_CHUNK
MOE_EXTRA = LANES


def _route_t(lt):
    row = lax.broadcasted_iota(jnp.int32, lt.shape, 0)
    rowf = row.astype(F32)
    big = float(lt.shape[0])
    is_grp = (row >= N_EXPERTS) & (row < N_EXPERTS + N_GROUPS)
    cl = jnp.where(is_grp, lt, -jnp.inf)
    mg = jnp.max(cl, axis=0, keepdims=True)
    g_star = jnp.min(jnp.where(cl == mg, rowf, big), axis=0, keepdims=True).astype(jnp.int32) - N_EXPERTS
    p_g = 1.0 / jnp.sum(jnp.where(is_grp, jnp.exp(cl - mg), 0.0), axis=0, keepdims=True)
    in_grp = (row < N_EXPERTS) & ((row // EXPERTS_PER_GROUP) == g_star)
    f1 = jnp.where(in_grp, lt, -jnp.inf)
    v1 = jnp.max(f1, axis=0, keepdims=True)
    i1 = jnp.min(jnp.where(f1 == v1, rowf, big), axis=0, keepdims=True).astype(jnp.int32)
    f2 = jnp.where(row == i1, -jnp.inf, f1)
    v2 = jnp.max(f2, axis=0, keepdims=True)
    i2 = jnp.min(jnp.where(f2 == v2, rowf, big), axis=0, keepdims=True).astype(jnp.int32)
    e2 = jnp.exp(v2 - v1)
    return i1, i2, p_g / (1.0 + e2), p_g * e2 / (1.0 + e2)


def _split_bf16(x):
    hi = x.astype(BF16)
    return hi, (x - hi.astype(F32)).astype(BF16)


def _moe_sort_kernel(h_ref, g_ref, wh_ref, wl_ref, br_ref, xs_ref, dest_ref, cnt_ref):
    tok = h_ref.shape[0]
    D = h_ref.shape[1]
    hn = _rms(h_ref[...], g_ref[...])
    hn_hi, hn_lo = _split_bf16(hn)
    nt_dot = lambda a, b: lax.dot_general(a, b, _NT, preferred_element_type=F32)
    w_hi = wh_ref[...]
    both = nt_dot(jnp.concatenate([w_hi, wl_ref[...]], axis=0), hn_hi)
    lt = both[:MOE_ROUTE_ROWS] + both[MOE_ROUTE_ROWS:] + nt_dot(w_hi, hn_lo) + br_ref[...]
    i1, i2, c1, c2 = _route_t(lt)

    erow = lax.broadcasted_iota(jnp.int32, (N_EXPERTS, tok), 0)
    oh1 = erow == i1
    oh2 = erow == i2
    oh = (oh1 | oh2).astype(F32)
    earlier = (lax.broadcasted_iota(jnp.int32, (tok, tok), 0) < lax.broadcasted_iota(jnp.int32, (tok, tok), 1))
    cum = jnp.dot(oh.astype(BF16), earlier.astype(BF16), preferred_element_type=F32)
    cnt = jnp.sum(oh, axis=1, keepdims=True)
    n_chunks = jnp.floor((cnt + (MOE_CHUNK - 1)) * (1.0 / MOE_CHUNK))
    ncb = jnp.broadcast_to(n_chunks, (N_EXPERTS, LANES))
    erow_l = lax.broadcasted_iota(jnp.int32, (N_EXPERTS, LANES), 0)
    first_chunk = jnp.zeros((N_EXPERTS, LANES), F32)
    for e in range(N_EXPERTS - 1):
        first_chunk = first_chunk + jnp.where(erow_l > e, ncb[e:e + 1, :], 0.0)
    base = jnp.tile(first_chunk, (1, tok // LANES)) * MOE_CHUNK + cum
    d1 = jnp.sum(jnp.where(oh1, base, 0.0), axis=0, keepdims=True).astype(jnp.int32)
    d2 = jnp.sum(jnp.where(oh2, base, 0.0), axis=0, keepdims=True).astype(jnp.int32)

    srow = lax.broadcasted_iota(jnp.int32, (MOE_ROWS, tok), 0)
    p1 = srow == d1
    p2 = srow == d2
    perm1 = p1.astype(BF16)
    perm2 = p2.astype(BF16)
    xs_ref[:, :D] = jnp.dot(perm1 + perm2, hn_hi, preferred_element_type=F32).astype(BF16)
    xrow = lax.broadcasted_iota(jnp.int32, (MOE_EXTRA, tok), 0)

    def weight_rows(c):
        hi = c.astype(BF16).astype(F32)
        return jnp.where(xrow == 0, hi, jnp.where(xrow == 1, c - hi, 0.0)).astype(BF16)

    xs_ref[:, D:] = (nt_dot(perm1, weight_rows(c1)) + nt_dot(perm2, weight_rows(c2))).astype(BF16)

    r8 = lax.broadcasted_iota(jnp.int32, (SUBLANES, tok), 0)
    dest_ref[0] = jnp.where(r8 == 0, d1, jnp.where(r8 == 1, d2, 0))
    cnt_ref[0] = jnp.broadcast_to(cnt, (N_EXPERTS, LANES)).astype(jnp.int32)


MOE_ROUTE_ROWS = 32


def _moe_sort(h1, g, w_router_e, b_router_e, w_router_g, b_router_g):
    T, D = h1.shape
    nt = T // MOE_TOK
    pad = MOE_ROUTE_ROWS - N_EXPERTS - N_GROUPS
    wr = jnp.concatenate([w_router_e.T, w_router_g.T, jnp.zeros((pad, D), F32)], axis=0)
    wr_hi = wr.astype(BF16)
    wr_lo = (wr - wr_hi.astype(F32)).astype(BF16)
    br = jnp.concatenate([b_router_e, b_router_g, jnp.zeros((pad,), F32)])[:, None]
    return pl.pallas_call(
        _moe_sort_kernel,
        grid=(nt,),
        in_specs=[pl.BlockSpec((MOE_TOK, D), lambda i: (i, 0)),
                  pl.BlockSpec((1, D), lambda i: (0, 0)),
                  pl.BlockSpec((MOE_ROUTE_ROWS, D), lambda i: (0, 0)),
                  pl.BlockSpec((MOE_ROUTE_ROWS, D), lambda i: (0, 0)),
                  pl.BlockSpec((MOE_ROUTE_ROWS, 1), lambda i: (0, 0))],
        out_specs=[pl.BlockSpec((MOE_ROWS, D + MOE_EXTRA), lambda i: (i, 0)),
                   pl.BlockSpec((1, SUBLANES, MOE_TOK), lambda i: (i, 0, 0)),
                   pl.BlockSpec((1, N_EXPERTS, LANES), lambda i: (i, 0, 0))],
        out_shape=[jax.ShapeDtypeStruct((nt * MOE_ROWS, D + MOE_EXTRA), BF16),
                   jax.ShapeDtypeStruct((nt, SUBLANES, MOE_TOK), jnp.int32),
                   jax.ShapeDtypeStruct((nt, N_EXPERTS, LANES), jnp.int32)],
        compiler_params=_cparams("parallel"),
        name="moe_sort",
    )(h1, g, wr_hi, wr_lo, br)


def _moe_tables(cnt, n_mm_tiles):
    nt = cnt.shape[0]
    pc = (cnt + (MOE_CHUNK - 1)) // MOE_CHUNK
    run_end = jnp.cumsum(pc, axis=1)
    run_first = run_end - pc
    per_expert = jnp.sum(pc, axis=0)
    mm_tiles = (per_expert + (MOE_CPM - 1)) // MOE_CPM
    mm_end = jnp.cumsum(mm_tiles)
    slot_base = (mm_end - mm_tiles)[None, :] * MOE_CPM + (jnp.cumsum(pc, axis=0) - pc)
    k = jnp.arange(MOE_CPT, dtype=jnp.int32)[None, :, None]
    expert_of = jnp.sum((run_end[:, None, :] <= k).astype(jnp.int32), axis=2)
    valid = expert_of < N_EXPERTS
    pick = expert_of[:, :, None] == jnp.arange(N_EXPERTS, dtype=jnp.int32)
    take = lambda a: jnp.sum(jnp.where(pick, a[:, None, :], 0), axis=2)
    slot = take(slot_base) + (k[:, :, 0] - take(run_first))
    src = (jnp.arange(nt, dtype=jnp.int32)[:, None] * MOE_CPT + k[:, :, 0]).reshape(-1)
    slot = jnp.where(valid, slot, n_mm_tiles * MOE_CPM).reshape(-1)
    gather_tbl = jnp.zeros((n_mm_tiles * MOE_CPM,), jnp.int32).at[slot].set(src, mode="drop")
    return_tbl = jnp.where(valid.reshape(-1), slot, 0).astype(jnp.int32)
    g = jnp.arange(n_mm_tiles, dtype=jnp.int32)
    n_used = mm_end[-1].astype(jnp.int32)
    tile_expert = jnp.sum((mm_end[None, :] <= jnp.minimum(g, n_used - 1)[:, None]).astype(jnp.int32), axis=1)
    return gather_tbl, return_tbl, tile_expert.astype(jnp.int32), n_used.reshape(1)


def _moe_expert_kernel(gt_ref, te_ref, nu_ref, *refs):
    chunk_refs = refs[:MOE_CPM]
    wg_ref, wu_ref, wd_ref, o_ref, wg_bf, wu_bf, wd_bf = refs[MOE_CPM:]
    D = o_ref.shape[1]
    g = pl.program_id(0)
    used = g < nu_ref[0]

    @pl.when(used & ((g == 0) | (te_ref[g] != te_ref[jnp.maximum(g - 1, 0)])))
    def _():
        wg_bf[...] = wg_ref[0].astype(BF16)
        wu_bf[...] = wu_ref[0].astype(BF16)
        wd_bf[...] = wd_ref[0].astype(BF16)

    @pl.when(used)
    def _():
        xs = jnp.concatenate([r[...] for r in chunk_refs], axis=0)
        x = xs[:, :D]
        extra = xs[:, D:].astype(F32)
        c = extra[:, 0:1] + extra[:, 1:2]
        gate = jnp.dot(x, wg_bf[...], preferred_element_type=F32)
        up = jnp.dot(x, wu_bf[...], preferred_element_type=F32)
        hh = (gate * jax.nn.sigmoid(gate)) * up * c
        o_ref[...] = jnp.dot(hh.astype(BF16), wd_bf[...], preferred_element_type=F32).astype(o_ref.dtype)

    @pl.when(jnp.logical_not(used))
    def _():
        o_ref[...] = jnp.zeros(o_ref.shape, o_ref.dtype)


def _moe_experts(xs, gather_tbl, tile_expert, n_used, wg, wu, wd, n_mm_tiles):
    W = xs.shape[1]
    E, D, FF = wg.shape

    def chunk_spec(c):
        return pl.BlockSpec((MOE_CHUNK, W), lambda g, gt, te, nu: (gt[g * MOE_CPM + c], 0))

    grid_spec = pltpu.PrefetchScalarGridSpec(
        num_scalar_prefetch=3,
        grid=(n_mm_tiles,),
        in_specs=[chunk_spec(c) for c in range(MOE_CPM)] + [
            pl.BlockSpec((1, D, FF), lambda g, gt, te, nu: (te[g], 0, 0)),
            pl.BlockSpec((1, D, FF), lambda g, gt, te, nu: (te[g], 0, 0)),
            pl.BlockSpec((1, FF, D), lambda g, gt, te, nu: (te[g], 0, 0))],
        out_specs=pl.BlockSpec((MOE_TM, D), lambda g, gt, te, nu: (g, 0)),
        scratch_shapes=[pltpu.VMEM((D, FF), BF16), pltpu.VMEM((D, FF), BF16), pltpu.VMEM((FF, D), BF16)],
    )
    return pl.pallas_call(
        _moe_expert_kernel,
        grid_spec=grid_spec,
        out_shape=jax.ShapeDtypeStruct((n_mm_tiles * MOE_TM, D), BF16),
        compiler_params=_cparams("arbitrary"),
        name="moe_experts",
    )(gather_tbl, tile_expert, n_used, *([xs] * MOE_CPM), wg, wu, wd)


def _finish_kernel(rt_ref, *refs):
    chunk_refs = refs[:MOE_CPT]
    dest_ref, h_ref, p_ref, gp_ref, gf_ref, wg_ref, wp_ref, o_ref, ys_ref = refs[MOE_CPT:]
    tok = h_ref.shape[0]
    for k, r in enumerate(chunk_refs):
        ys_ref[k * MOE_CHUNK:(k + 1) * MOE_CHUNK, :] = r[...]
    dcol = dest_ref[0].astype(F32).T
    lane = lax.broadcasted_iota(jnp.int32, (tok, MOE_ROWS), 1).astype(F32)
    unsort = ((lane == dcol[:, 0:1]) | (lane == dcol[:, 1:2])).astype(BF16)
    h = h_ref[...] + jnp.dot(unsort, ys_ref[...], preferred_element_type=F32)
    t = _rms(h, gp_ref[...]).astype(BF16)
    gate = jax.nn.sigmoid(jnp.dot(t, wg_ref[...], preferred_element_type=F32))
    emb = jnp.dot(p_ref[...].astype(BF16), wp_ref[...], preferred_element_type=F32)
    o_ref[...] = _rms(h + emb * gate, gf_ref[...])


def _finish(ys, return_tbl, dest, h1, p2, g_ple, g_final, wg, wp):
    T, D = h1.shape
    P = p2.shape[1]
    nt = T // MOE_TOK

    def chunk_spec(k):
        return pl.BlockSpec((MOE_CHUNK, D), lambda i, rt: (rt[i * MOE_CPT + k], 0))

    vec = pl.BlockSpec((1, D), lambda i, rt: (0, 0))
    grid_spec = pltpu.PrefetchScalarGridSpec(
        num_scalar_prefetch=1,
        grid=(nt,),
        in_specs=[chunk_spec(k) for k in range(MOE_CPT)] + [
            pl.BlockSpec((1, SUBLANES, MOE_TOK), lambda i, rt: (i, 0, 0)),
            pl.BlockSpec((MOE_TOK, D), lambda i, rt: (i, 0)),
            pl.BlockSpec((MOE_TOK, P), lambda i, rt: (i, 0)),
            vec, vec,
            pl.BlockSpec((D, D), lambda i, rt: (0, 0), pipeline_mode=pl.Buffered(1)),
            pl.BlockSpec((P, D), lambda i, rt: (0, 0), pipeline_mode=pl.Buffered(1))],
        out_specs=pl.BlockSpec((MOE_TOK, D), lambda i, rt: (i, 0)),
        scratch_shapes=[pltpu.VMEM((MOE_ROWS, D), BF16)],
    )
    return pl.pallas_call(
        _finish_kernel,
        grid_spec=grid_spec,
        out_shape=jax.ShapeDtypeStruct((T, D), F32),
        compiler_params=_cparams("arbitrary"),
        name="finish",
    )(return_tbl, *([ys] * MOE_CPT), dest, h1, p2, g_ple, g_final, wg, wp)


def kernel(x, p, g_mix, w_in, rel_bias, w_attn_br, conv_w, conv_b, ln_g, ln_b, w_conv_br, w_o, g_ffn,
           w_router_g, b_router_g, w_router_e, b_router_e, w_e_gate, w_e_up, w_e_down, g_ple, w_ple_gate,
           w_ple_proj, g_final):
    B, S, D = x.shape
    T = B * S
    depth = w_in.shape[0]
    width = N_HEADS * HEAD_DIM
    assert D == width and conv_w.shape[2] == D, "column-block indexing assumes all branch widths equal d_model"
    assert S % MOBA_BLOCK == 0
    assert depth == 1, "the final RMSNorm is fused into the last stage of a single layer"
    a_blk = 3 * width // D
    gate_blk = a_blk + 2

    h = x.reshape(T, D)
    for i in range(depth):
        proj = _in_proj(h, g_mix[i][None], w_in[i].astype(BF16))
        proj3 = proj.reshape(B, S, -1)
        attn = _moba(proj3, rel_bias)
        conv = _conv(proj3, conv_w[i], conv_b[i][None], ln_g[i][None], ln_b[i][None], a_blk)
        h = _mix(h, attn.reshape(T, D), conv.reshape(T, D), proj,
                 w_attn_br[i].astype(BF16), w_conv_br[i].astype(BF16), w_o[i].astype(BF16), gate_blk)
        xs, dest, cnt = _moe_sort(h, g_ffn[i][None], w_router_e[i], b_router_e[i], w_router_g[i], b_router_g[i])
        n_tok_tiles = T // MOE_TOK
        worst_chunks = (MOE_TOPK * T + n_tok_tiles * N_EXPERTS * (MOE_CHUNK - 1)) // MOE_CHUNK
        n_mm_tiles = -(-worst_chunks // MOE_CPM) + N_EXPERTS
        gather_tbl, return_tbl, tile_expert, n_used = _moe_tables(cnt[:, :, 0], n_mm_tiles)
        ys = _moe_experts(xs, gather_tbl, tile_expert, n_used,
                          w_e_gate[i], w_e_up[i], w_e_down[i], n_mm_tiles)
        out = _finish(ys, return_tbl, dest, h, p[i].reshape(T, -1), g_ple[i][None], g_final[None],
                      w_ple_gate[i].astype(BF16), w_ple_proj[i].astype(BF16))
    return out.reshape(B, S, D)
```

```python
import functools
import math

import numpy as np
import jax
import jax.numpy as jnp
from jax import lax
from jax.experimental import pallas as pl
from jax.experimental.pallas import tpu as pltpu

F32 = jnp.float32
BF16 = jnp.bfloat16

N_HEADS = 16
HEAD_DIM = 128
MOBA_BLOCK = 256
MOBA_TOPK = 3
N_BUCKETS = 32
MAX_DISTANCE = 128
CONV_KERNEL = 31
N_GROUPS = 4
EXPERTS_PER_GROUP = 4
N_EXPERTS = N_GROUPS * EXPERTS_PER_GROUP
EPS = 1e-6
NEG = -1e30
LOG2E = 1.4426950408889634

V7X_VMEM_LIMIT_BYTES = 56 * 1024 * 1024
LANES = 128
SUBLANES = 8

_NT = (((1,), (1,)), ((), ()))


def _cparams(*sem):
    return pltpu.CompilerParams(dimension_semantics=sem, vmem_limit_bytes=V7X_VMEM_LIMIT_BYTES)


def _resident(shape):
    return pl.BlockSpec(shape, lambda *_: (0,) * len(shape), pipeline_mode=pl.Buffered(1))


def _sigmoid(x):
    return 0.5 * jnp.tanh(0.5 * x) + 0.5


def _rms(x, g):
    return x * lax.rsqrt(jnp.mean(x * x, axis=-1, keepdims=True) + EPS) * g


def _in_proj_kernel(x_ref, g_ref, w_ref, o_ref, xn_ref):
    @pl.when(pl.program_id(1) == 0)
    def _():
        xn_ref[...] = _rms(x_ref[...], g_ref[...]).astype(BF16)

    o_ref[...] = jnp.dot(xn_ref[...], w_ref[...], preferred_element_type=F32).astype(o_ref.dtype)


def _in_proj(x2, g, w, tm=1024, tn=2048):
    T, D = x2.shape
    N = w.shape[1]
    return pl.pallas_call(
        _in_proj_kernel,
        grid=(T // tm, N // tn),
        in_specs=[pl.BlockSpec((tm, D), lambda i, j: (i, 0)),
                  pl.BlockSpec((1, D), lambda i, j: (0, 0)),
                  pl.BlockSpec((D, tn), lambda i, j: (0, j))],
        out_specs=pl.BlockSpec((tm, tn), lambda i, j: (i, j)),
        out_shape=jax.ShapeDtypeStruct((T, N), BF16),
        scratch_shapes=[pltpu.VMEM((tm, D), BF16)],
        compiler_params=_cparams("parallel", "arbitrary"),
        name="in_proj",
    )(x2, g, w)


MOBA_PV_ROWS = HEAD_DIM + 16
MOBA_Q_LANES = MOBA_BLOCK
MOBA_HEADS_PER_STEP = 4


def _rel_buckets_t():
    k = np.arange(MOBA_BLOCK)[:, None]
    q = np.arange(MOBA_BLOCK)[None, :]
    dist = np.stack([q - k, MOBA_BLOCK + q - k]).astype(np.int32)
    n = np.maximum(dist, 0)
    max_exact = N_BUCKETS // 2
    nf = np.maximum(n, 1).astype(np.float32)
    large = max_exact + (np.log(nf / np.float32(max_exact)) / np.float32(math.log(MAX_DISTANCE / max_exact))
                         * np.float32(N_BUCKETS - max_exact)).astype(np.int32)
    large = np.minimum(large, N_BUCKETS - 1)
    return np.where(n < max_exact, n, large).astype(np.int32)


def _moba_kernel(tbl_ref, bkt_ref, q_ref, k_ref, v_ref, o_ref,
                 kmean_ref, vt_ref, bias_ref, qt_ref, sel_ref, s_ref, p_ref, m_ref, a_ref, acc_ref,
                 *, n_blocks):
    hg = pl.program_id(0)
    b = pl.program_id(1)
    BS = MOBA_BLOCK
    HPS = MOBA_HEADS_PER_STEP
    lanes = [slice(hh * HEAD_DIM, (hh + 1) * HEAD_DIM) for hh in range(HPS)]

    @pl.when(b == 0)
    def _build_bias():
        krow = lax.broadcasted_iota(jnp.int32, (BS, BS), 0)
        qcol = lax.broadcasted_iota(jnp.int32, (BS, BS), 1)
        for hh in range(HPS):
            bias_ref[hh] = jnp.zeros((2, BS, BS), F32)

            def fill(n, carry):
                bias_ref[hh] = jnp.where(bkt_ref[...] == n, tbl_ref[hg * HPS + hh, n] * LOG2E, bias_ref[hh])
                return carry

            lax.fori_loop(0, N_BUCKETS, fill, 0)
            bias_ref[hh, 0] = jnp.where(qcol >= krow, bias_ref[hh, 0], NEG)

    ones_row = (lax.broadcasted_iota(jnp.int32, (MOBA_PV_ROWS - HEAD_DIM, BS), 0) == 0).astype(BF16)

    def prep(n, carry):
        rows = pl.ds(pl.multiple_of(n * BS, BS), BS)
        for hh in range(HPS):
            kmean_ref[hh, pl.ds(n, 1), :] = jnp.mean(k_ref[0, rows, lanes[hh]].astype(F32), axis=0, keepdims=True)
            vt_ref[hh, n, :HEAD_DIM] = v_ref[0, rows, lanes[hh]].astype(F32).T.astype(BF16)
            vt_ref[hh, n, HEAD_DIM:] = ones_row
        return carry

    lax.fori_loop(0, n_blocks, prep, 0)

    def q_block(qi, carry):
        _moba_q_block(qi, hg, tbl_ref, q_ref, k_ref, o_ref, kmean_ref, vt_ref, bias_ref, qt_ref, sel_ref,
                      s_ref, p_ref, m_ref, a_ref, acc_ref, n_blocks=n_blocks)
        return carry

    lax.fori_loop(0, n_blocks, q_block, 0)


def _moba_q_block(qi, hg, tbl_ref, q_ref, k_ref, o_ref, kmean_ref, vt_ref, bias_ref, qt_ref, sel_ref,
                  s_ref, p_ref, m_ref, a_ref, acc_ref, *, n_blocks):
    BS = MOBA_BLOCK
    HPS = MOBA_HEADS_PER_STEP
    scale = HEAD_DIM ** -0.5
    lanes = [slice(hh * HEAD_DIM, (hh + 1) * HEAD_DIM) for hh in range(HPS)]
    q_rows = pl.ds(pl.multiple_of(qi * BS, BS), BS)

    def qk_stage(hh, t):
        j0 = pl.multiple_of(jnp.maximum(qi - t, 0) * BS, BS)
        s_ref[hh] = jnp.dot(k_ref[0, pl.ds(j0, BS), lanes[hh]], qt_ref[hh], preferred_element_type=F32)

    def softmax_stage(hh, t, bias_of=None, far_bias=None):
        for c in range(BS // MOBA_Q_LANES):
            qs = slice(c * MOBA_Q_LANES, (c + 1) * MOBA_Q_LANES)
            chosen = (lax.shift_right_logical(sel_ref[hh, :, qs], qi - t) & 1) == 1
            c2 = scale * LOG2E
            if bias_of is not None:
                st = s_ref[hh, :, qs] * c2 + bias_of(qs)
                shift = 0.0
                top = jnp.max(st, axis=0, keepdims=True)
            else:
                st = None
                shift = far_bias
                top = jnp.max(s_ref[hh, :, qs], axis=0, keepdims=True) * c2 + shift
            m_prev = m_ref[hh, :, qs]
            m_new = jnp.where(chosen, jnp.maximum(m_prev, top), m_prev)
            a_ref[hh, :, qs] = jnp.exp2(m_prev - m_new)
            m_ref[hh, :, qs] = m_new
            cut = jnp.where(chosen, m_new - shift, -NEG)
            if st is None:
                st = s_ref[hh, :, qs] * c2
            p_ref[hh, :, qs] = jnp.exp2(st - cut).astype(BF16)

    def pv_stage(hh, t):
        acc_ref[hh] = a_ref[hh] * acc_ref[hh] + jnp.dot(vt_ref[hh, qi - t], p_ref[hh],
                                                        preferred_element_type=F32)

    for hh in range(HPS):
        qt = q_ref[0, q_rows, lanes[hh]].astype(F32).T.astype(BF16)
        qt_ref[hh] = qt
        gate = jnp.dot(kmean_ref[hh].astype(BF16), qt, preferred_element_type=F32)
        blk = lax.broadcasted_iota(jnp.int32, gate.shape, 0)
        blkf = blk.astype(F32)
        g = jnp.where(blk < qi, gate, NEG)
        bits = jnp.full((1, BS), lax.shift_left(jnp.int32(1), qi), jnp.int32)
        for _ in range(MOBA_TOPK):
            mx = jnp.max(g, axis=0, keepdims=True)
            ix = jnp.min(jnp.where(g == mx, blkf, float(n_blocks)), axis=0, keepdims=True).astype(jnp.int32)
            bits = bits | jnp.where(ix < qi, lax.shift_left(jnp.int32(1), ix), 0)
            g = jnp.where(blk == ix, -jnp.inf, g)
        sel_ref[hh] = bits
        m_ref[hh] = jnp.full((1, BS), NEG, F32)
        acc_ref[hh] = jnp.zeros((MOBA_PV_ROWS, BS), F32)
        qk_stage(hh, 0)

    for hh in range(HPS):
        softmax_stage(hh, 0, lambda qs, hh=hh: bias_ref[hh, 0, :, qs])
    for hh in range(HPS):
        qk_stage(hh, 1)

    @pl.when(qi >= 1)
    def _past_blocks():
        for hh in range(HPS):
            pv_stage(hh, 0)
        for hh in range(HPS):
            softmax_stage(hh, 1, lambda qs, hh=hh: bias_ref[hh, 1, :, qs])
        for hh in range(HPS):
            qk_stage(hh, 2)

        def sweep(t):
            for hh in range(HPS):
                pv_stage(hh, t - 1)
            for hh in range(HPS):
                softmax_stage(hh, t, far_bias=tbl_ref[hg * HPS + hh, N_BUCKETS - 1] * LOG2E)
            for hh in range(HPS):
                qk_stage(hh, t + 1)

        n_far = qi - 1

        def sweep_pair(i, carry):
            sweep(2 + 2 * i)
            sweep(3 + 2 * i)
            return carry

        lax.fori_loop(0, n_far // 2, sweep_pair, 0)

        @pl.when(n_far % 2 == 1)
        def _odd_block():
            sweep(qi)

    for hh in range(HPS):
        pv_stage(hh, qi)
        denom = acc_ref[hh, HEAD_DIM:HEAD_DIM + 1, :]
        out_t = acc_ref[hh, :HEAD_DIM, :] * (1.0 / denom)
        o_ref[0, q_rows, lanes[hh]] = out_t.T.astype(o_ref.dtype)


def _moba(proj3, rel_bias):
    B, S, _ = proj3.shape
    BS = MOBA_BLOCK
    HPS = MOBA_HEADS_PER_STEP
    W = HPS * HEAD_DIM
    nb = S // BS
    n_groups = N_HEADS // HPS
    bkt = jnp.asarray(_rel_buckets_t())
    tbl = rel_bias.astype(F32).T
    grid_spec = pltpu.PrefetchScalarGridSpec(
        num_scalar_prefetch=1,
        grid=(n_groups, B),
        in_specs=[pl.BlockSpec((2, BS, BS), lambda h, b, t: (0, 0, 0)),
                  pl.BlockSpec((1, S, W), lambda h, b, t: (b, 0, h)),
                  pl.BlockSpec((1, S, W), lambda h, b, t: (b, 0, n_groups + h)),
                  pl.BlockSpec((1, S, W), lambda h, b, t: (b, 0, 2 * n_groups + h))],
        out_specs=pl.BlockSpec((1, S, W), lambda h, b, t: (b, 0, h)),
        scratch_shapes=[pltpu.VMEM((HPS, nb, HEAD_DIM), F32),
                        pltpu.VMEM((HPS, nb, MOBA_PV_ROWS, BS), BF16),
                        pltpu.VMEM((HPS, 2, BS, BS), F32),
                        pltpu.VMEM((HPS, HEAD_DIM, BS), BF16),
                        pltpu.VMEM((HPS, 1, BS), jnp.int32),
                        pltpu.VMEM((HPS, BS, BS), F32),
                        pltpu.VMEM((HPS, BS, BS), BF16),
                        pltpu.VMEM((HPS, 1, BS), F32),
                        pltpu.VMEM((HPS, 1, BS), F32),
                        pltpu.VMEM((HPS, MOBA_PV_ROWS, BS), F32)],
    )
    return pl.pallas_call(
        functools.partial(_moba_kernel, n_blocks=nb),
        grid_spec=grid_spec,
        out_shape=jax.ShapeDtypeStruct((B, S, N_HEADS * HEAD_DIM), BF16),
        compiler_params=_cparams("arbitrary", "arbitrary"),
        name="moba",
    )(tbl, bkt, proj3, proj3, proj3)


CONV_HALO = 32
CONV_ROWS = 64
CONV_COLS = 256


def _conv_kernel(a_ref, g_ref, ah_ref, gh_ref, w_ref, cb_ref, lng_ref, lnb_ref, o_ref, u_ref, sh_ref, c_ref,
                 *, ts):
    i = pl.program_id(1)
    D = u_ref.shape[1]
    u_ref[CONV_HALO:, :] = a_ref[0].astype(F32) * _sigmoid(g_ref[0].astype(F32))
    uh = ah_ref[0].astype(F32) * _sigmoid(gh_ref[0].astype(F32))
    u_ref[:CONV_HALO, :] = jnp.where(i > 0, uh, 0.0)

    first = CONV_HALO - (CONV_KERNEL - 1)
    sh_rows = sh_ref.shape[1]

    def col_body(c, carry):
        cols = pl.ds(pl.multiple_of(c * CONV_COLS, CONV_COLS), CONV_COLS)
        for s in range(1, SUBLANES):
            sh_ref[s] = u_ref[s:s + sh_rows, cols]
        for r in range(ts // CONV_ROWS):
            acc = jnp.zeros((CONV_ROWS, CONV_COLS), F32)
            for k in range(CONV_KERNEL):
                s = (first + k) % SUBLANES
                r0 = r * CONV_ROWS + first + k - s
                tap = u_ref[r0:r0 + CONV_ROWS, cols] if s == 0 else sh_ref[s, r0:r0 + CONV_ROWS, :]
                acc = acc + jnp.tile(w_ref[k, :, cols], (CONV_ROWS // SUBLANES, 1)) * tap
            c_ref[r * CONV_ROWS:(r + 1) * CONV_ROWS, cols] = acc + cb_ref[:, cols]
        return carry

    lax.fori_loop(0, D // CONV_COLS, col_body, 0)

    y = c_ref[...]
    mu = jnp.mean(y, axis=-1, keepdims=True)
    yc = y - mu
    var = jnp.mean(yc * yc, axis=-1, keepdims=True)
    z = yc * lax.rsqrt(var + EPS) * lng_ref[...] + lnb_ref[...]
    o_ref[0] = (z * _sigmoid(z)).astype(o_ref.dtype)


def _conv(proj3, conv_w, conv_b, ln_g, ln_b, a_blk, ts=512):
    B, S, _ = proj3.shape
    D = conv_w.shape[1]
    hpb = ts // CONV_HALO
    w_rows = jnp.broadcast_to(conv_w[:, None, :], (CONV_KERNEL, SUBLANES, D))

    def halo(col):
        return pl.BlockSpec((1, CONV_HALO, D), lambda b, i: (b, jnp.maximum(i * hpb - 1, 0), col))

    vec = pl.BlockSpec((1, D), lambda b, i: (0, 0))
    return pl.pallas_call(
        functools.partial(_conv_kernel, ts=ts),
        grid=(B, S // ts),
        in_specs=[pl.BlockSpec((1, ts, D), lambda b, i: (b, i, a_blk)),
                  pl.BlockSpec((1, ts, D), lambda b, i: (b, i, a_blk + 1)),
                  halo(a_blk), halo(a_blk + 1),
                  pl.BlockSpec((CONV_KERNEL, SUBLANES, D), lambda b, i: (0, 0, 0)),
                  vec, vec, vec],
        out_specs=pl.BlockSpec((1, ts, D), lambda b, i: (b, i, 0)),
        out_shape=jax.ShapeDtypeStruct((B, S, D), BF16),
        scratch_shapes=[pltpu.VMEM((CONV_HALO + ts, D), F32),
                        pltpu.VMEM((SUBLANES, CONV_HALO + ts - SUBLANES, CONV_COLS), F32),
                        pltpu.VMEM((ts, D), F32)],
        compiler_params=_cparams("parallel", "arbitrary"),
        name="conv",
    )(proj3, proj3, proj3, proj3, w_rows, conv_b, ln_g, ln_b)


def _mix_kernel(x_ref, at_ref, cv_ref, ga_ref, gc_ref, wa_ref, wc_ref, wo_ref, o_ref):
    ya = jnp.dot(at_ref[...], wa_ref[...], preferred_element_type=F32)
    yc = jnp.dot(cv_ref[...], wc_ref[...], preferred_element_type=F32)
    mixed = (jax.nn.sigmoid(ga_ref[...].astype(F32)) * ya + jax.nn.sigmoid(gc_ref[...].astype(F32)) * yc)
    o_ref[...] = x_ref[...] + jnp.dot(mixed.astype(BF16), wo_ref[...], preferred_element_type=F32)


def _mix(x2, attn2, conv2, proj2, wa, wc, wo, gate_blk, tm=256):
    T, D = x2.shape
    row = lambda col: pl.BlockSpec((tm, D), lambda i: (i, col))
    return pl.pallas_call(
        _mix_kernel,
        grid=(T // tm,),
        in_specs=[row(0), row(0), row(0), row(gate_blk), row(gate_blk + 1),
                  _resident((D, D)), _resident((D, D)), _resident((D, D))],
        out_specs=row(0),
        out_shape=jax.ShapeDtypeStruct((T, D), F32),
        compiler_params=_cparams("parallel"),
        name="mix",
    )(x2, attn2, conv2, proj2, proj2, wa, wc, wo)


MOE_TOK = 512
MOE_CHUNK = 16
MOE_TM = 512
MOE_TOPK = 2
_MOE_WORST_ROWS = MOE_TOPK * MOE_TOK + N_EXPERTS * (MOE_CHUNK - 1)
MOE_ROWS = -(-_MOE_WORST_ROWS // LANES) * LANES
MOE_CPT = MOE_ROWS // MOE_CHUNK
MOE_CPM = MOE_TM // MOE_CHUNK
MOE_EXTRA = LANES


def _route_t(lt):
    row = lax.broadcasted_iota(jnp.int32, lt.shape, 0)
    rowf = row.astype(F32)
    big = float(lt.shape[0])
    is_grp = (row >= N_EXPERTS) & (row < N_EXPERTS + N_GROUPS)
    cl = jnp.where(is_grp, lt, -jnp.inf)
    mg = jnp.max(cl, axis=0, keepdims=True)
    g_star = jnp.min(jnp.where(cl == mg, rowf, big), axis=0, keepdims=True).astype(jnp.int32) - N_EXPERTS
    p_g = 1.0 / jnp.sum(jnp.where(is_grp, jnp.exp(cl - mg), 0.0), axis=0, keepdims=True)
    in_grp = (row < N_EXPERTS) & ((row // EXPERTS_PER_GROUP) == g_star)
    f1 = jnp.where(in_grp, lt, -jnp.inf)
    v1 = jnp.max(f1, axis=0, keepdims=True)
    i1 = jnp.min(jnp.where(f1 == v1, rowf, big), axis=0, keepdims=True).astype(jnp.int32)
    f2 = jnp.where(row == i1, -jnp.inf, f1)
    v2 = jnp.max(f2, axis=0, keepdims=True)
    i2 = jnp.min(jnp.where(f2 == v2, rowf, big), axis=0, keepdims=True).astype(jnp.int32)
    e2 = jnp.exp(v2 - v1)
    return i1, i2, p_g / (1.0 + e2), p_g * e2 / (1.0 + e2)


def _split_bf16(x):
    hi = x.astype(BF16)
    return hi, (x - hi.astype(F32)).astype(BF16)


def _moe_sort_kernel(h_ref, g_ref, wh_ref, wl_ref, br_ref, xs_ref, dest_ref, cnt_ref):
    tok = h_ref.shape[0]
    D = h_ref.shape[1]
    hn = _rms(h_ref[...], g_ref[...])
    hn_hi, hn_lo = _split_bf16(hn)
    nt_dot = lambda a, b: lax.dot_general(a, b, _NT, preferred_element_type=F32)
    w_hi = wh_ref[...]
    both = nt_dot(jnp.concatenate([w_hi, wl_ref[...]], axis=0), hn_hi)
    lt = both[:MOE_ROUTE_ROWS] + both[MOE_ROUTE_ROWS:] + nt_dot(w_hi, hn_lo) + br_ref[...]
    i1, i2, c1, c2 = _route_t(lt)

    erow = lax.broadcasted_iota(jnp.int32, (N_EXPERTS, tok), 0)
    oh1 = erow == i1
    oh2 = erow == i2
    oh = (oh1 | oh2).astype(F32)
    earlier = (lax.broadcasted_iota(jnp.int32, (tok, tok), 0) < lax.broadcasted_iota(jnp.int32, (tok, tok), 1))
    cum = jnp.dot(oh.astype(BF16), earlier.astype(BF16), preferred_element_type=F32)
    cnt = jnp.sum(oh, axis=1, keepdims=True)
    n_chunks = jnp.floor((cnt + (MOE_CHUNK - 1)) * (1.0 / MOE_CHUNK))
    ncb = jnp.broadcast_to(n_chunks, (N_EXPERTS, LANES))
    erow_l = lax.broadcasted_iota(jnp.int32, (N_EXPERTS, LANES), 0)
    first_chunk = jnp.zeros((N_EXPERTS, LANES), F32)
    for e in range(N_EXPERTS - 1):
        first_chunk = first_chunk + jnp.where(erow_l > e, ncb[e:e + 1, :], 0.0)
    base = jnp.tile(first_chunk, (1, tok // LANES)) * MOE_CHUNK + cum
    d1 = jnp.sum(jnp.where(oh1, base, 0.0), axis=0, keepdims=True).astype(jnp.int32)
    d2 = jnp.sum(jnp.where(oh2, base, 0.0), axis=0, keepdims=True).astype(jnp.int32)

    srow = lax.broadcasted_iota(jnp.int32, (MOE_ROWS, tok), 0)
    p1 = srow == d1
    p2 = srow == d2
    perm1 = p1.astype(BF16)
    perm2 = p2.astype(BF16)
    xs_ref[:, :D] = jnp.dot(perm1 + perm2, hn_hi, preferred_element_type=F32).astype(BF16)
    xrow = lax.broadcasted_iota(jnp.int32, (MOE_EXTRA, tok), 0)

    def weight_rows(c):
        hi = c.astype(BF16).astype(F32)
        return jnp.where(xrow == 0, hi, jnp.where(xrow == 1, c - hi, 0.0)).astype(BF16)

    xs_ref[:, D:] = (nt_dot(perm1, weight_rows(c1)) + nt_dot(perm2, weight_rows(c2))).astype(BF16)

    r8 = lax.broadcasted_iota(jnp.int32, (SUBLANES, tok), 0)
    dest_ref[0] = jnp.where(r8 == 0, d1, jnp.where(r8 == 1, d2, 0))
    cnt_ref[0] = jnp.broadcast_to(cnt, (N_EXPERTS, LANES)).astype(jnp.int32)


MOE_ROUTE_ROWS = 32


def _moe_sort(h1, g, w_router_e, b_router_e, w_router_g, b_router_g):
    T, D = h1.shape
    nt = T // MOE_TOK
    pad = MOE_ROUTE_ROWS - N_EXPERTS - N_GROUPS
    wr = jnp.concatenate([w_router_e.T, w_router_g.T, jnp.zeros((pad, D), F32)], axis=0)
    wr_hi = wr.astype(BF16)
    wr_lo = (wr - wr_hi.astype(F32)).astype(BF16)
    br = jnp.concatenate([b_router_e, b_router_g, jnp.zeros((pad,), F32)])[:, None]
    return pl.pallas_call(
        _moe_sort_kernel,
        grid=(nt,),
        in_specs=[pl.BlockSpec((MOE_TOK, D), lambda i: (i, 0)),
                  pl.BlockSpec((1, D), lambda i: (0, 0)),
                  pl.BlockSpec((MOE_ROUTE_ROWS, D), lambda i: (0, 0)),
                  pl.BlockSpec((MOE_ROUTE_ROWS, D), lambda i: (0, 0)),
                  pl.BlockSpec((MOE_ROUTE_ROWS, 1), lambda i: (0, 0))],
        out_specs=[pl.BlockSpec((MOE_ROWS, D + MOE_EXTRA), lambda i: (i, 0)),
                   pl.BlockSpec((1, SUBLANES, MOE_TOK), lambda i: (i, 0, 0)),
                   pl.BlockSpec((1, N_EXPERTS, LANES), lambda i: (i, 0, 0))],
        out_shape=[jax.ShapeDtypeStruct((nt * MOE_ROWS, D + MOE_EXTRA), BF16),
                   jax.ShapeDtypeStruct((nt, SUBLANES, MOE_TOK), jnp.int32),
                   jax.ShapeDtypeStruct((nt, N_EXPERTS, LANES), jnp.int32)],
        compiler_params=_cparams("parallel"),
        name="moe_sort",
    )(h1, g, wr_hi, wr_lo, br)


def _moe_tables(cnt, n_mm_tiles):
    nt = cnt.shape[0]
    pc = (cnt + (MOE_CHUNK - 1)) // MOE_CHUNK
    run_end = jnp.cumsum(pc, axis=1)
    run_first = run_end - pc
    per_expert = jnp.sum(pc, axis=0)
    mm_tiles = (per_expert + (MOE_CPM - 1)) // MOE_CPM
    mm_end = jnp.cumsum(mm_tiles)
    slot_base = (mm_end - mm_tiles)[None, :] * MOE_CPM + (jnp.cumsum(pc, axis=0) - pc)
    k = jnp.arange(MOE_CPT, dtype=jnp.int32)[None, :, None]
    expert_of = jnp.sum((run_end[:, None, :] <= k).astype(jnp.int32), axis=2)
    valid = expert_of < N_EXPERTS
    pick = expert_of[:, :, None] == jnp.arange(N_EXPERTS, dtype=jnp.int32)
    take = lambda a: jnp.sum(jnp.where(pick, a[:, None, :], 0), axis=2)
    slot = take(slot_base) + (k[:, :, 0] - take(run_first))
    src = (jnp.arange(nt, dtype=jnp.int32)[:, None] * MOE_CPT + k[:, :, 0]).reshape(-1)
    slot = jnp.where(valid, slot, n_mm_tiles * MOE_CPM).reshape(-1)
    gather_tbl = jnp.zeros((n_mm_tiles * MOE_CPM,), jnp.int32).at[slot].set(src, mode="drop")
    return_tbl = jnp.where(valid.reshape(-1), slot, 0).astype(jnp.int32)
    g = jnp.arange(n_mm_tiles, dtype=jnp.int32)
    n_used = mm_end[-1].astype(jnp.int32)
    tile_expert = jnp.sum((mm_end[None, :] <= jnp.minimum(g, n_used - 1)[:, None]).astype(jnp.int32), axis=1)
    return gather_tbl, return_tbl, tile_expert.astype(jnp.int32), n_used.reshape(1)


def _chunk_gather(tbl_ref, src_hbm, buf, sem, chunks_per_tile):
    def chunk_copy(tile, c, slot):
        src = pl.multiple_of(tbl_ref[tile * chunks_per_tile + c] * MOE_CHUNK, MOE_CHUNK)
        dst = pl.multiple_of(c * MOE_CHUNK, MOE_CHUNK)
        return pltpu.make_async_copy(src_hbm.at[pl.ds(src, MOE_CHUNK), :],
                                     buf.at[slot, pl.ds(dst, MOE_CHUNK), :], sem.at[slot])

    def for_each_chunk(tile, slot, act):
        def body(c, carry):
            act(chunk_copy(tile, c, slot))
            return carry
        lax.fori_loop(0, chunks_per_tile, body, 0)

    def start(tile, slot):
        for_each_chunk(tile, slot, lambda copy: copy.start())

    def wait(tile, slot):
        for_each_chunk(tile, slot, lambda copy: copy.wait())

    return start, wait


def _moe_expert_kernel(gt_ref, te_ref, nu_ref, xs_hbm, wg_ref, wu_ref, wd_ref, o_ref,
                       wg_bf, wu_bf, wd_bf, x_buf, x_sem):
    D = o_ref.shape[1]
    g = pl.program_id(0)
    n_used = nu_ref[0]
    used = g < n_used

    start_gather, wait_gather = _chunk_gather(gt_ref, xs_hbm, x_buf, x_sem, MOE_CPM)

    @pl.when((g == 0) & (n_used > 0))
    def _():
        start_gather(0, 0)

    @pl.when(g + 1 < n_used)
    def _():
        start_gather(g + 1, (g + 1) % 2)

    @pl.when(used & ((g == 0) | (te_ref[g] != te_ref[jnp.maximum(g - 1, 0)])))
    def _():
        wg_bf[...] = wg_ref[0].astype(BF16)
        wu_bf[...] = wu_ref[0].astype(BF16)
        wd_bf[...] = wd_ref[0].astype(BF16)

    @pl.when(used)
    def _():
        slot = g % 2
        wait_gather(g, slot)
        xs = x_buf[slot]
        x = xs[:, :D]
        extra = xs[:, D:].astype(F32)
        c = extra[:, 0:1] + extra[:, 1:2]
        gate = jnp.dot(x, wg_bf[...], preferred_element_type=F32)
        up = jnp.dot(x, wu_bf[...], preferred_element_type=F32)
        hh = (gate * jax.nn.sigmoid(gate)) * up * c
        o_ref[...] = jnp.dot(hh.astype(BF16), wd_bf[...], preferred_element_type=F32).astype(o_ref.dtype)

    @pl.when(jnp.logical_not(used))
    def _():
        o_ref[...] = jnp.zeros(o_ref.shape, o_ref.dtype)


def _moe_experts(xs, gather_tbl, tile_expert, n_used, wg, wu, wd, n_mm_tiles):
    W = xs.shape[1]
    E, D, FF = wg.shape

    grid_spec = pltpu.PrefetchScalarGridSpec(
        num_scalar_prefetch=3,
        grid=(n_mm_tiles,),
        in_specs=[pl.BlockSpec(memory_space=pl.ANY),
                  pl.BlockSpec((1, D, FF), lambda g, gt, te, nu: (te[g], 0, 0)),
                  pl.BlockSpec((1, D, FF), lambda g, gt, te, nu: (te[g], 0, 0)),
                  pl.BlockSpec((1, FF, D), lambda g, gt, te, nu: (te[g], 0, 0))],
        out_specs=pl.BlockSpec((MOE_TM, D), lambda g, gt, te, nu: (g, 0)),
        scratch_shapes=[pltpu.VMEM((D, FF), BF16), pltpu.VMEM((D, FF), BF16), pltpu.VMEM((FF, D), BF16),
                        pltpu.VMEM((2, MOE_TM, W), BF16), pltpu.SemaphoreType.DMA((2,))],
    )
    return pl.pallas_call(
        _moe_expert_kernel,
        grid_spec=grid_spec,
        out_shape=jax.ShapeDtypeStruct((n_mm_tiles * MOE_TM, D), BF16),
        compiler_params=_cparams("arbitrary"),
        name="moe_experts",
    )(gather_tbl, tile_expert, n_used, xs, wg, wu, wd)


def _finish_kernel(rt_ref, ys_hbm, dest_ref, h_ref, p_ref, gp_ref, gf_ref, wg_ref, wp_ref, o_ref, y_buf, y_sem):
    tok = h_ref.shape[0]
    i = pl.program_id(0)
    slot = i % 2
    start_gather, wait_gather = _chunk_gather(rt_ref, ys_hbm, y_buf, y_sem, MOE_CPT)

    @pl.when(i == 0)
    def _():
        start_gather(0, 0)

    @pl.when(i + 1 < pl.num_programs(0))
    def _():
        start_gather(i + 1, (i + 1) % 2)

    wait_gather(i, slot)
    dcol = dest_ref[0].astype(F32).T
    lane = lax.broadcasted_iota(jnp.int32, (tok, MOE_ROWS), 1).astype(F32)
    unsort = ((lane == dcol[:, 0:1]) | (lane == dcol[:, 1:2])).astype(BF16)
    h = h_ref[...] + jnp.dot(unsort, y_buf[slot], preferred_element_type=F32)
    t = _rms(h, gp_ref[...]).astype(BF16)
    gate = jax.nn.sigmoid(jnp.dot(t, wg_ref[...], preferred_element_type=F32))
    emb = jnp.dot(p_ref[...].astype(BF16), wp_ref[...], preferred_element_type=F32)
    o_ref[...] = _rms(h + emb * gate, gf_ref[...])


def _finish(ys, return_tbl, dest, h1, p2, g_ple, g_final, wg, wp):
    T, D = h1.shape
    P = p2.shape[1]
    nt = T // MOE_TOK

    vec = pl.BlockSpec((1, D), lambda i, rt: (0, 0))
    grid_spec = pltpu.PrefetchScalarGridSpec(
        num_scalar_prefetch=1,
        grid=(nt,),
        in_specs=[
            pl.BlockSpec(memory_space=pl.ANY),
            pl.BlockSpec((1, SUBLANES, MOE_TOK), lambda i, rt: (i, 0, 0)),
            pl.BlockSpec((MOE_TOK, D), lambda i, rt: (i, 0)),
            pl.BlockSpec((MOE_TOK, P), lambda i, rt: (i, 0)),
            vec, vec,
            pl.BlockSpec((D, D), lambda i, rt: (0, 0), pipeline_mode=pl.Buffered(1)),
            pl.BlockSpec((P, D), lambda i, rt: (0, 0), pipeline_mode=pl.Buffered(1))],
        out_specs=pl.BlockSpec((MOE_TOK, D), lambda i, rt: (i, 0)),
        scratch_shapes=[pltpu.VMEM((2, MOE_ROWS, D), BF16), pltpu.SemaphoreType.DMA((2,))],
    )
    return pl.pallas_call(
        _finish_kernel,
        grid_spec=grid_spec,
        out_shape=jax.ShapeDtypeStruct((T, D), F32),
        compiler_params=_cparams("arbitrary"),
        name="finish",
    )(return_tbl, ys, dest, h1, p2, g_ple, g_final, wg, wp)


def kernel(x, p, g_mix, w_in, rel_bias, w_attn_br, conv_w, conv_b, ln_g, ln_b, w_conv_br, w_o, g_ffn,
           w_router_g, b_router_g, w_router_e, b_router_e, w_e_gate, w_e_up, w_e_down, g_ple, w_ple_gate,
           w_ple_proj, g_final):
    B, S, D = x.shape
    T = B * S
    depth = w_in.shape[0]
    width = N_HEADS * HEAD_DIM
    assert D == width and conv_w.shape[2] == D, "column-block indexing assumes all branch widths equal d_model"
    assert S % MOBA_BLOCK == 0
    assert depth == 1, "the final RMSNorm is fused into the last stage of a single layer"
    a_blk = 3 * width // D
    gate_blk = a_blk + 2

    h = x.reshape(T, D)
    for i in range(depth):
        proj = _in_proj(h, g_mix[i][None], w_in[i].astype(BF16))
        proj3 = proj.reshape(B, S, -1)
        attn = _moba(proj3, rel_bias)
        conv = _conv(proj3, conv_w[i], conv_b[i][None], ln_g[i][None], ln_b[i][None], a_blk)
        h = _mix(h, attn.reshape(T, D), conv.reshape(T, D), proj,
                 w_attn_br[i].astype(BF16), w_conv_br[i].astype(BF16), w_o[i].astype(BF16), gate_blk)
        xs, dest, cnt = _moe_sort(h, g_ffn[i][None], w_router_e[i], b_router_e[i], w_router_g[i], b_router_g[i])
        n_tok_tiles = T // MOE_TOK
        worst_chunks = (MOE_TOPK * T + n_tok_tiles * N_EXPERTS * (MOE_CHUNK - 1)) // MOE_CHUNK
        n_mm_tiles = -(-worst_chunks // MOE_CPM) + N_EXPERTS
        gather_tbl, return_tbl, tile_expert, n_used = _moe_tables(cnt[:, :, 0], n_mm_tiles)
        ys = _moe_experts(xs, gather_tbl, tile_expert, n_used,
                          w_e_gate[i], w_e_up[i], w_e_down[i], n_mm_tiles)
        out = _finish(ys, return_tbl, dest, h, p[i].reshape(T, -1), g_ple[i][None], g_final[None],
                      w_ple_gate[i].astype(BF16), w_ple_proj[i].astype(BF16))
    return out.reshape(B, S, D)
```

```python
import functools
import math

import numpy as np
import jax
import jax.numpy as jnp
from jax import lax
from jax.experimental import pallas as pl
from jax.experimental.pallas import tpu as pltpu

F32 = jnp.float32
BF16 = jnp.bfloat16

N_HEADS = 16
HEAD_DIM = 128
MOBA_BLOCK = 256
MOBA_TOPK = 3
N_BUCKETS = 32
MAX_DISTANCE = 128
CONV_KERNEL = 31
N_GROUPS = 4
EXPERTS_PER_GROUP = 4
N_EXPERTS = N_GROUPS * EXPERTS_PER_GROUP
EPS = 1e-6
NEG = -1e30
LOG2E = 1.4426950408889634

V7X_VMEM_LIMIT_BYTES = 56 * 1024 * 1024
LANES = 128
SUBLANES = 8

_NT = (((1,), (1,)), ((), ()))


def _cparams(*sem):
    return pltpu.CompilerParams(dimension_semantics=sem, vmem_limit_bytes=V7X_VMEM_LIMIT_BYTES)


def _resident(shape):
    return pl.BlockSpec(shape, lambda *_: (0,) * len(shape), pipeline_mode=pl.Buffered(1))


def _sigmoid(x):
    return 0.5 * jnp.tanh(0.5 * x) + 0.5


def _rms(x, g):
    return x * lax.rsqrt(jnp.mean(x * x, axis=-1, keepdims=True) + EPS) * g


def _in_proj_kernel(x_ref, g_ref, w_ref, o_ref, xn_ref):
    @pl.when(pl.program_id(1) == 0)
    def _():
        xn_ref[...] = _rms(x_ref[...], g_ref[...]).astype(BF16)

    o_ref[...] = jnp.dot(xn_ref[...], w_ref[...], preferred_element_type=F32).astype(o_ref.dtype)


def _in_proj(x2, g, w, tm=1024, tn=2048):
    T, D = x2.shape
    N = w.shape[1]
    return pl.pallas_call(
        _in_proj_kernel,
        grid=(T // tm, N // tn),
        in_specs=[pl.BlockSpec((tm, D), lambda i, j: (i, 0)),
                  pl.BlockSpec((1, D), lambda i, j: (0, 0)),
                  pl.BlockSpec((D, tn), lambda i, j: (0, j))],
        out_specs=pl.BlockSpec((tm, tn), lambda i, j: (i, j)),
        out_shape=jax.ShapeDtypeStruct((T, N), BF16),
        scratch_shapes=[pltpu.VMEM((tm, D), BF16)],
        compiler_params=_cparams("parallel", "arbitrary"),
        name="in_proj",
    )(x2, g, w)


MOBA_PV_ROWS = HEAD_DIM + 16
MOBA_Q_LANES = MOBA_BLOCK
MOBA_HEADS_PER_STEP = 4


def _rel_buckets_t():
    k = np.arange(MOBA_BLOCK)[:, None]
    q = np.arange(MOBA_BLOCK)[None, :]
    dist = np.stack([q - k, MOBA_BLOCK + q - k]).astype(np.int32)
    n = np.maximum(dist, 0)
    max_exact = N_BUCKETS // 2
    nf = np.maximum(n, 1).astype(np.float32)
    large = max_exact + (np.log(nf / np.float32(max_exact)) / np.float32(math.log(MAX_DISTANCE / max_exact))
                         * np.float32(N_BUCKETS - max_exact)).astype(np.int32)
    large = np.minimum(large, N_BUCKETS - 1)
    return np.where(n < max_exact, n, large).astype(np.int32)


def _moba_kernel(tbl_ref, bkt_ref, q_ref, k_ref, v_ref, o_ref,
                 kmean_ref, vt_ref, bias_ref, qt_ref, sel_ref, s_ref, p_ref, m_ref, a_ref, acc_ref,
                 *, n_blocks):
    hg = pl.program_id(0)
    b = pl.program_id(1)
    BS = MOBA_BLOCK
    HPS = MOBA_HEADS_PER_STEP
    lanes = [slice(hh * HEAD_DIM, (hh + 1) * HEAD_DIM) for hh in range(HPS)]

    @pl.when(b == 0)
    def _build_bias():
        krow = lax.broadcasted_iota(jnp.int32, (BS, BS), 0)
        qcol = lax.broadcasted_iota(jnp.int32, (BS, BS), 1)
        for hh in range(HPS):
            bias_ref[hh] = jnp.zeros((2, BS, BS), F32)

            def fill(n, carry):
                bias_ref[hh] = jnp.where(bkt_ref[...] == n, tbl_ref[hg * HPS + hh, n] * LOG2E, bias_ref[hh])
                return carry

            lax.fori_loop(0, N_BUCKETS, fill, 0)
            bias_ref[hh, 0] = jnp.where(qcol >= krow, bias_ref[hh, 0], NEG)

    ones_row = (lax.broadcasted_iota(jnp.int32, (MOBA_PV_ROWS - HEAD_DIM, BS), 0) == 0).astype(BF16)

    def prep(n, carry):
        rows = pl.ds(pl.multiple_of(n * BS, BS), BS)
        for hh in range(HPS):
            kmean_ref[hh, pl.ds(n, 1), :] = jnp.mean(k_ref[0, rows, lanes[hh]].astype(F32), axis=0, keepdims=True)
            vt_ref[hh, n, :HEAD_DIM] = v_ref[0, rows, lanes[hh]].astype(F32).T.astype(BF16)
            vt_ref[hh, n, HEAD_DIM:] = ones_row
        return carry

    lax.fori_loop(0, n_blocks, prep, 0)

    q_block = functools.partial(_moba_q_block, hg=hg, tbl_ref=tbl_ref, q_ref=q_ref, k_ref=k_ref, o_ref=o_ref,
                                kmean_ref=kmean_ref, vt_ref=vt_ref, bias_ref=bias_ref, qt_ref=qt_ref,
                                sel_ref=sel_ref, s_ref=s_ref, p_ref=p_ref, m_ref=m_ref, a_ref=a_ref,
                                acc_ref=acc_ref, n_blocks=n_blocks)

    q_block(jnp.int32(0), finish_previous=False)

    def next_block(qi, carry):
        q_block(qi, finish_previous=True)
        return carry

    lax.fori_loop(1, n_blocks, next_block, 0)
    _moba_close(jnp.int32(n_blocks - 1), vt_ref=vt_ref, p_ref=p_ref, a_ref=a_ref, acc_ref=acc_ref, o_ref=o_ref)


def _moba_close(qi, *, vt_ref, p_ref, a_ref, acc_ref, o_ref):
    lanes = [slice(hh * HEAD_DIM, (hh + 1) * HEAD_DIM) for hh in range(MOBA_HEADS_PER_STEP)]
    q_rows = pl.ds(pl.multiple_of(qi * MOBA_BLOCK, MOBA_BLOCK), MOBA_BLOCK)
    for hh in range(MOBA_HEADS_PER_STEP):
        acc = a_ref[hh] * acc_ref[hh] + jnp.dot(vt_ref[hh, 0], p_ref[hh], preferred_element_type=F32)
        denom = acc[HEAD_DIM:HEAD_DIM + 1, :]
        out_t = acc[:HEAD_DIM, :] * (1.0 / denom)
        o_ref[0, q_rows, lanes[hh]] = out_t.T.astype(o_ref.dtype)


def _moba_q_block(qi, *, finish_previous, hg, tbl_ref, q_ref, k_ref, o_ref, kmean_ref, vt_ref, bias_ref, qt_ref,
                  sel_ref, s_ref, p_ref, m_ref, a_ref, acc_ref, n_blocks):
    BS = MOBA_BLOCK
    HPS = MOBA_HEADS_PER_STEP
    scale = HEAD_DIM ** -0.5
    lanes = [slice(hh * HEAD_DIM, (hh + 1) * HEAD_DIM) for hh in range(HPS)]
    q_rows = pl.ds(pl.multiple_of(qi * BS, BS), BS)

    def qk_stage(hh, t):
        j0 = pl.multiple_of(jnp.maximum(qi - t, 0) * BS, BS)
        s_ref[hh] = jnp.dot(k_ref[0, pl.ds(j0, BS), lanes[hh]], qt_ref[hh], preferred_element_type=F32)

    def softmax_stage(hh, t, bias_of=None, far_bias=None):
        for c in range(BS // MOBA_Q_LANES):
            qs = slice(c * MOBA_Q_LANES, (c + 1) * MOBA_Q_LANES)
            chosen = (lax.shift_right_logical(sel_ref[hh, :, qs], qi - t) & 1) == 1
            c2 = scale * LOG2E
            if bias_of is not None:
                st = s_ref[hh, :, qs] * c2 + bias_of(qs)
                shift = 0.0
                top = jnp.max(st, axis=0, keepdims=True)
            else:
                st = None
                shift = far_bias
                top = jnp.max(s_ref[hh, :, qs], axis=0, keepdims=True) * c2 + shift
            m_prev = m_ref[hh, :, qs]
            m_new = jnp.where(chosen, jnp.maximum(m_prev, top), m_prev)
            a_ref[hh, :, qs] = jnp.exp2(m_prev - m_new)
            m_ref[hh, :, qs] = m_new
            cut = jnp.where(chosen, m_new - shift, -NEG)
            if st is None:
                st = s_ref[hh, :, qs] * c2
            p_ref[hh, :, qs] = jnp.exp2(st - cut).astype(BF16)

    def pv_stage(hh, t):
        acc_ref[hh] = a_ref[hh] * acc_ref[hh] + jnp.dot(vt_ref[hh, qi - t], p_ref[hh],
                                                        preferred_element_type=F32)

    if finish_previous:
        _moba_close(qi - 1, vt_ref=vt_ref, p_ref=p_ref, a_ref=a_ref, acc_ref=acc_ref, o_ref=o_ref)

    for hh in range(HPS):
        qt = q_ref[0, q_rows, lanes[hh]].astype(F32).T.astype(BF16)
        qt_ref[hh] = qt
        gate = jnp.dot(kmean_ref[hh].astype(BF16), qt, preferred_element_type=F32)
        blk = lax.broadcasted_iota(jnp.int32, gate.shape, 0)
        blkf = blk.astype(F32)
        g = jnp.where(blk < qi, gate, NEG)
        bits = jnp.full((1, BS), lax.shift_left(jnp.int32(1), qi), jnp.int32)
        for _ in range(MOBA_TOPK):
            mx = jnp.max(g, axis=0, keepdims=True)
            ix = jnp.min(jnp.where(g == mx, blkf, float(n_blocks)), axis=0, keepdims=True).astype(jnp.int32)
            bits = bits | jnp.where(ix < qi, lax.shift_left(jnp.int32(1), ix), 0)
            g = jnp.where(blk == ix, -jnp.inf, g)
        sel_ref[hh] = bits
        m_ref[hh] = jnp.full((1, BS), NEG, F32)
        acc_ref[hh] = jnp.zeros((MOBA_PV_ROWS, BS), F32)
        qk_stage(hh, 0)

    for hh in range(HPS):
        softmax_stage(hh, 0, lambda qs, hh=hh: bias_ref[hh, 0, :, qs])
    for hh in range(HPS):
        qk_stage(hh, 1)

    @pl.when(qi >= 1)
    def _past_blocks():
        for hh in range(HPS):
            pv_stage(hh, 0)
        for hh in range(HPS):
            softmax_stage(hh, 1, lambda qs, hh=hh: bias_ref[hh, 1, :, qs])
        for hh in range(HPS):
            qk_stage(hh, 2)

        def sweep(t):
            for hh in range(HPS):
                pv_stage(hh, t - 1)
            for hh in range(HPS):
                softmax_stage(hh, t, far_bias=tbl_ref[hg * HPS + hh, N_BUCKETS - 1] * LOG2E)
            for hh in range(HPS):
                qk_stage(hh, t + 1)

        n_far = qi - 1

        def sweep_pair(i, carry):
            sweep(2 + 2 * i)
            sweep(3 + 2 * i)
            return carry

        lax.fori_loop(0, n_far // 2, sweep_pair, 0)

        @pl.when(n_far % 2 == 1)
        def _odd_block():
            sweep(qi)


def _moba(proj3, rel_bias):
    B, S, _ = proj3.shape
    BS = MOBA_BLOCK
    HPS = MOBA_HEADS_PER_STEP
    W = HPS * HEAD_DIM
    nb = S // BS
    n_groups = N_HEADS // HPS
    bkt = jnp.asarray(_rel_buckets_t())
    tbl = rel_bias.astype(F32).T
    grid_spec = pltpu.PrefetchScalarGridSpec(
        num_scalar_prefetch=1,
        grid=(n_groups, B),
        in_specs=[pl.BlockSpec((2, BS, BS), lambda h, b, t: (0, 0, 0)),
                  pl.BlockSpec((1, S, W), lambda h, b, t: (b, 0, h)),
                  pl.BlockSpec((1, S, W), lambda h, b, t: (b, 0, n_groups + h)),
                  pl.BlockSpec((1, S, W), lambda h, b, t: (b, 0, 2 * n_groups + h))],
        out_specs=pl.BlockSpec((1, S, W), lambda h, b, t: (b, 0, h)),
        scratch_shapes=[pltpu.VMEM((HPS, nb, HEAD_DIM), F32),
                        pltpu.VMEM((HPS, nb, MOBA_PV_ROWS, BS), BF16),
                        pltpu.VMEM((HPS, 2, BS, BS), F32),
                        pltpu.VMEM((HPS, HEAD_DIM, BS), BF16),
                        pltpu.VMEM((HPS, 1, BS), jnp.int32),
                        pltpu.VMEM((HPS, BS, BS), F32),
                        pltpu.VMEM((HPS, BS, BS), BF16),
                        pltpu.VMEM((HPS, 1, BS), F32),
                        pltpu.VMEM((HPS, 1, BS), F32),
                        pltpu.VMEM((HPS, MOBA_PV_ROWS, BS), F32)],
    )
    return pl.pallas_call(
        functools.partial(_moba_kernel, n_blocks=nb),
        grid_spec=grid_spec,
        out_shape=jax.ShapeDtypeStruct((B, S, N_HEADS * HEAD_DIM), BF16),
        compiler_params=_cparams("arbitrary", "arbitrary"),
        name="moba",
    )(tbl, bkt, proj3, proj3, proj3)


CONV_HALO = 32
CONV_ROWS = 64
CONV_COLS = 256


def _conv_kernel(a_ref, g_ref, ah_ref, gh_ref, w_ref, cb_ref, lng_ref, lnb_ref, o_ref, u_ref, sh_ref, c_ref,
                 *, ts):
    i = pl.program_id(1)
    D = u_ref.shape[1]
    u_ref[CONV_HALO:, :] = a_ref[0].astype(F32) * _sigmoid(g_ref[0].astype(F32))
    uh = ah_ref[0].astype(F32) * _sigmoid(gh_ref[0].astype(F32))
    u_ref[:CONV_HALO, :] = jnp.where(i > 0, uh, 0.0)

    first = CONV_HALO - (CONV_KERNEL - 1)
    sh_rows = sh_ref.shape[1]

    def col_body(c, carry):
        cols = pl.ds(pl.multiple_of(c * CONV_COLS, CONV_COLS), CONV_COLS)
        for s in range(1, SUBLANES):
            sh_ref[s] = u_ref[s:s + sh_rows, cols]
        for r in range(ts // CONV_ROWS):
            acc = jnp.zeros((CONV_ROWS, CONV_COLS), F32)
            for k in range(CONV_KERNEL):
                s = (first + k) % SUBLANES
                r0 = r * CONV_ROWS + first + k - s
                tap = u_ref[r0:r0 + CONV_ROWS, cols] if s == 0 else sh_ref[s, r0:r0 + CONV_ROWS, :]
                acc = acc + jnp.tile(w_ref[k, :, cols], (CONV_ROWS // SUBLANES, 1)) * tap
            c_ref[r * CONV_ROWS:(r + 1) * CONV_ROWS, cols] = acc + cb_ref[:, cols]
        return carry

    lax.fori_loop(0, D // CONV_COLS, col_body, 0)

    y = c_ref[...]
    mu = jnp.mean(y, axis=-1, keepdims=True)
    yc = y - mu
    var = jnp.mean(yc * yc, axis=-1, keepdims=True)
    z = yc * lax.rsqrt(var + EPS) * lng_ref[...] + lnb_ref[...]
    o_ref[0] = (z * _sigmoid(z)).astype(o_ref.dtype)


def _conv(proj3, conv_w, conv_b, ln_g, ln_b, a_blk, ts=512):
    B, S, _ = proj3.shape
    D = conv_w.shape[1]
    hpb = ts // CONV_HALO
    w_rows = jnp.broadcast_to(conv_w[:, None, :], (CONV_KERNEL, SUBLANES, D))

    def halo(col):
        return pl.BlockSpec((1, CONV_HALO, D), lambda b, i: (b, jnp.maximum(i * hpb - 1, 0), col))

    vec = pl.BlockSpec((1, D), lambda b, i: (0, 0))
    return pl.pallas_call(
        functools.partial(_conv_kernel, ts=ts),
        grid=(B, S // ts),
        in_specs=[pl.BlockSpec((1, ts, D), lambda b, i: (b, i, a_blk)),
                  pl.BlockSpec((1, ts, D), lambda b, i: (b, i, a_blk + 1)),
                  halo(a_blk), halo(a_blk + 1),
                  pl.BlockSpec((CONV_KERNEL, SUBLANES, D), lambda b, i: (0, 0, 0)),
                  vec, vec, vec],
        out_specs=pl.BlockSpec((1, ts, D), lambda b, i: (b, i, 0)),
        out_shape=jax.ShapeDtypeStruct((B, S, D), BF16),
        scratch_shapes=[pltpu.VMEM((CONV_HALO + ts, D), F32),
                        pltpu.VMEM((SUBLANES, CONV_HALO + ts - SUBLANES, CONV_COLS), F32),
                        pltpu.VMEM((ts, D), F32)],
        compiler_params=_cparams("parallel", "arbitrary"),
        name="conv",
    )(proj3, proj3, proj3, proj3, w_rows, conv_b, ln_g, ln_b)


def _mix_kernel(x_ref, at_ref, cv_ref, ga_ref, gc_ref, wa_ref, wc_ref, wo_ref, o_ref):
    ya = jnp.dot(at_ref[...], wa_ref[...], preferred_element_type=F32)
    yc = jnp.dot(cv_ref[...], wc_ref[...], preferred_element_type=F32)
    mixed = (jax.nn.sigmoid(ga_ref[...].astype(F32)) * ya + jax.nn.sigmoid(gc_ref[...].astype(F32)) * yc)
    o_ref[...] = x_ref[...] + jnp.dot(mixed.astype(BF16), wo_ref[...], preferred_element_type=F32)


def _mix(x2, attn2, conv2, proj2, wa, wc, wo, gate_blk, tm=256):
    T, D = x2.shape
    row = lambda col: pl.BlockSpec((tm, D), lambda i: (i, col))
    return pl.pallas_call(
        _mix_kernel,
        grid=(T // tm,),
        in_specs=[row(0), row(0), row(0), row(gate_blk), row(gate_blk + 1),
                  _resident((D, D)), _resident((D, D)), _resident((D, D))],
        out_specs=row(0),
        out_shape=jax.ShapeDtypeStruct((T, D), F32),
        compiler_params=_cparams("parallel"),
        name="mix",
    )(x2, attn2, conv2, proj2, proj2, wa, wc, wo)


MOE_TOK = 512
MOE_CHUNK = 16
MOE_TM = 512
MOE_TOPK = 2
_MOE_WORST_ROWS = MOE_TOPK * MOE_TOK + N_EXPERTS * (MOE_CHUNK - 1)
MOE_ROWS = -(-_MOE_WORST_ROWS // LANES) * LANES
MOE_CPT = MOE_ROWS // MOE_CHUNK
MOE_CPM = MOE_TM // MOE_CHUNK
MOE_EXTRA = LANES


def _route_t(lt):
    row = lax.broadcasted_iota(jnp.int32, lt.shape, 0)
    rowf = row.astype(F32)
    big = float(lt.shape[0])
    is_grp = (row >= N_EXPERTS) & (row < N_EXPERTS + N_GROUPS)
    cl = jnp.where(is_grp, lt, -jnp.inf)
    mg = jnp.max(cl, axis=0, keepdims=True)
    g_star = jnp.min(jnp.where(cl == mg, rowf, big), axis=0, keepdims=True).astype(jnp.int32) - N_EXPERTS
    p_g = 1.0 / jnp.sum(jnp.where(is_grp, jnp.exp(cl - mg), 0.0), axis=0, keepdims=True)
    in_grp = (row < N_EXPERTS) & ((row // EXPERTS_PER_GROUP) == g_star)
    f1 = jnp.where(in_grp, lt, -jnp.inf)
    v1 = jnp.max(f1, axis=0, keepdims=True)
    i1 = jnp.min(jnp.where(f1 == v1, rowf, big), axis=0, keepdims=True).astype(jnp.int32)
    f2 = jnp.where(row == i1, -jnp.inf, f1)
    v2 = jnp.max(f2, axis=0, keepdims=True)
    i2 = jnp.min(jnp.where(f2 == v2, rowf, big), axis=0, keepdims=True).astype(jnp.int32)
    e2 = jnp.exp(v2 - v1)
    return i1, i2, p_g / (1.0 + e2), p_g * e2 / (1.0 + e2)


def _split_bf16(x):
    hi = x.astype(BF16)
    return hi, (x - hi.astype(F32)).astype(BF16)


def _moe_sort_kernel(h_ref, g_ref, wh_ref, wl_ref, br_ref, xs_ref, dest_ref, cnt_ref):
    tok = h_ref.shape[0]
    D = h_ref.shape[1]
    hn = _rms(h_ref[...], g_ref[...])
    hn_hi, hn_lo = _split_bf16(hn)
    nt_dot = lambda a, b: lax.dot_general(a, b, _NT, preferred_element_type=F32)
    w_hi = wh_ref[...]
    both = nt_dot(jnp.concatenate([w_hi, wl_ref[...]], axis=0), hn_hi)
    lt = both[:MOE_ROUTE_ROWS] + both[MOE_ROUTE_ROWS:] + nt_dot(w_hi, hn_lo) + br_ref[...]
    i1, i2, c1, c2 = _route_t(lt)

    erow = lax.broadcasted_iota(jnp.int32, (N_EXPERTS, tok), 0)
    oh1 = erow == i1
    oh2 = erow == i2
    oh = (oh1 | oh2).astype(F32)
    earlier = (lax.broadcasted_iota(jnp.int32, (tok, tok), 0) < lax.broadcasted_iota(jnp.int32, (tok, tok), 1))
    cum = jnp.dot(oh.astype(BF16), earlier.astype(BF16), preferred_element_type=F32)
    cnt = jnp.sum(oh, axis=1, keepdims=True)
    n_chunks = jnp.floor((cnt + (MOE_CHUNK - 1)) * (1.0 / MOE_CHUNK))
    ncb = jnp.broadcast_to(n_chunks, (N_EXPERTS, LANES))
    erow_l = lax.broadcasted_iota(jnp.int32, (N_EXPERTS, LANES), 0)
    first_chunk = jnp.zeros((N_EXPERTS, LANES), F32)
    for e in range(N_EXPERTS - 1):
        first_chunk = first_chunk + jnp.where(erow_l > e, ncb[e:e + 1, :], 0.0)
    base = jnp.tile(first_chunk, (1, tok // LANES)) * MOE_CHUNK + cum
    d1 = jnp.sum(jnp.where(oh1, base, 0.0), axis=0, keepdims=True).astype(jnp.int32)
    d2 = jnp.sum(jnp.where(oh2, base, 0.0), axis=0, keepdims=True).astype(jnp.int32)

    srow = lax.broadcasted_iota(jnp.int32, (MOE_ROWS, tok), 0)
    p1 = srow == d1
    p2 = srow == d2
    perm1 = p1.astype(BF16)
    perm2 = p2.astype(BF16)
    xs_ref[:, :D] = jnp.dot(perm1 + perm2, hn_hi, preferred_element_type=F32).astype(BF16)
    xrow = lax.broadcasted_iota(jnp.int32, (MOE_EXTRA, tok), 0)

    def weight_rows(c):
        hi = c.astype(BF16).astype(F32)
        return jnp.where(xrow == 0, hi, jnp.where(xrow == 1, c - hi, 0.0)).astype(BF16)

    xs_ref[:, D:] = (nt_dot(perm1, weight_rows(c1)) + nt_dot(perm2, weight_rows(c2))).astype(BF16)

    r8 = lax.broadcasted_iota(jnp.int32, (SUBLANES, tok), 0)
    dest_ref[0] = jnp.where(r8 == 0, d1, jnp.where(r8 == 1, d2, 0))
    cnt_ref[0] = jnp.broadcast_to(cnt, (N_EXPERTS, LANES)).astype(jnp.int32)


MOE_ROUTE_ROWS = 32


def _moe_sort(h1, g, w_router_e, b_router_e, w_router_g, b_router_g):
    T, D = h1.shape
    nt = T // MOE_TOK
    pad = MOE_ROUTE_ROWS - N_EXPERTS - N_GROUPS
    wr = jnp.concatenate([w_router_e.T, w_router_g.T, jnp.zeros((pad, D), F32)], axis=0)
    wr_hi = wr.astype(BF16)
    wr_lo = (wr - wr_hi.astype(F32)).astype(BF16)
    br = jnp.concatenate([b_router_e, b_router_g, jnp.zeros((pad,), F32)])[:, None]
    return pl.pallas_call(
        _moe_sort_kernel,
        grid=(nt,),
        in_specs=[pl.BlockSpec((MOE_TOK, D), lambda i: (i, 0)),
                  pl.BlockSpec((1, D), lambda i: (0, 0)),
                  pl.BlockSpec((MOE_ROUTE_ROWS, D), lambda i: (0, 0)),
                  pl.BlockSpec((MOE_ROUTE_ROWS, D), lambda i: (0, 0)),
                  pl.BlockSpec((MOE_ROUTE_ROWS, 1), lambda i: (0, 0))],
        out_specs=[pl.BlockSpec((MOE_ROWS, D + MOE_EXTRA), lambda i: (i, 0)),
                   pl.BlockSpec((1, SUBLANES, MOE_TOK), lambda i: (i, 0, 0)),
                   pl.BlockSpec((1, N_EXPERTS, LANES), lambda i: (i, 0, 0))],
        out_shape=[jax.ShapeDtypeStruct((nt * MOE_ROWS, D + MOE_EXTRA), BF16),
                   jax.ShapeDtypeStruct((nt, SUBLANES, MOE_TOK), jnp.int32),
                   jax.ShapeDtypeStruct((nt, N_EXPERTS, LANES), jnp.int32)],
        compiler_params=_cparams("parallel"),
        name="moe_sort",
    )(h1, g, wr_hi, wr_lo, br)


def _moe_tables(cnt, n_mm_tiles):
    nt = cnt.shape[0]
    pc = (cnt + (MOE_CHUNK - 1)) // MOE_CHUNK
    run_end = jnp.cumsum(pc, axis=1)
    run_first = run_end - pc
    per_expert = jnp.sum(pc, axis=0)
    mm_tiles = (per_expert + (MOE_CPM - 1)) // MOE_CPM
    mm_end = jnp.cumsum(mm_tiles)
    slot_base = (mm_end - mm_tiles)[None, :] * MOE_CPM + (jnp.cumsum(pc, axis=0) - pc)
    k = jnp.arange(MOE_CPT, dtype=jnp.int32)[None, :, None]
    expert_of = jnp.sum((run_end[:, None, :] <= k).astype(jnp.int32), axis=2)
    valid = expert_of < N_EXPERTS
    pick = expert_of[:, :, None] == jnp.arange(N_EXPERTS, dtype=jnp.int32)
    take = lambda a: jnp.sum(jnp.where(pick, a[:, None, :], 0), axis=2)
    slot = take(slot_base) + (k[:, :, 0] - take(run_first))
    src = (jnp.arange(nt, dtype=jnp.int32)[:, None] * MOE_CPT + k[:, :, 0]).reshape(-1)
    slot = jnp.where(valid, slot, n_mm_tiles * MOE_CPM).reshape(-1)
    gather_tbl = jnp.zeros((n_mm_tiles * MOE_CPM,), jnp.int32).at[slot].set(src, mode="drop")
    return_tbl = jnp.where(valid.reshape(-1), slot, 0).astype(jnp.int32)
    g = jnp.arange(n_mm_tiles, dtype=jnp.int32)
    n_used = mm_end[-1].astype(jnp.int32)
    tile_expert = jnp.sum((mm_end[None, :] <= jnp.minimum(g, n_used - 1)[:, None]).astype(jnp.int32), axis=1)
    return gather_tbl, return_tbl, tile_expert.astype(jnp.int32), n_used.reshape(1)


def _chunk_gather(tbl_ref, src_hbm, buf, sem, chunks_per_tile):
    def chunk_copy(tile, c, slot):
        src = pl.multiple_of(tbl_ref[tile * chunks_per_tile + c] * MOE_CHUNK, MOE_CHUNK)
        dst = pl.multiple_of(c * MOE_CHUNK, MOE_CHUNK)
        return pltpu.make_async_copy(src_hbm.at[pl.ds(src, MOE_CHUNK), :],
                                     buf.at[slot, pl.ds(dst, MOE_CHUNK), :], sem.at[slot])

    def for_each_chunk(tile, slot, act):
        def body(c, carry):
            act(chunk_copy(tile, c, slot))
            return carry
        lax.fori_loop(0, chunks_per_tile, body, 0)

    def start(tile, slot):
        for_each_chunk(tile, slot, lambda copy: copy.start())

    def wait(tile, slot):
        for_each_chunk(tile, slot, lambda copy: copy.wait())

    return start, wait


def _moe_expert_kernel(gt_ref, te_ref, nu_ref, xs_hbm, wg_ref, wu_ref, wd_ref, o_ref,
                       wg_bf, wu_bf, wd_bf, x_buf, x_sem):
    D = o_ref.shape[1]
    g = pl.program_id(0)
    n_used = nu_ref[0]
    used = g < n_used

    start_gather, wait_gather = _chunk_gather(gt_ref, xs_hbm, x_buf, x_sem, MOE_CPM)

    @pl.when((g == 0) & (n_used > 0))
    def _():
        start_gather(0, 0)

    @pl.when(g + 1 < n_used)
    def _():
        start_gather(g + 1, (g + 1) % 2)

    @pl.when(used & ((g == 0) | (te_ref[g] != te_ref[jnp.maximum(g - 1, 0)])))
    def _():
        wg_bf[...] = wg_ref[0].astype(BF16)
        wu_bf[...] = wu_ref[0].astype(BF16)
        wd_bf[...] = wd_ref[0].astype(BF16)

    @pl.when(used)
    def _():
        slot = g % 2
        wait_gather(g, slot)
        xs = x_buf[slot]
        x = xs[:, :D]
        extra = xs[:, D:].astype(F32)
        c = extra[:, 0:1] + extra[:, 1:2]
        gate = jnp.dot(x, wg_bf[...], preferred_element_type=F32)
        up = jnp.dot(x, wu_bf[...], preferred_element_type=F32)
        hh = (gate * jax.nn.sigmoid(gate)) * up * c
        o_ref[...] = jnp.dot(hh.astype(BF16), wd_bf[...], preferred_element_type=F32).astype(o_ref.dtype)

    @pl.when(jnp.logical_not(used))
    def _():
        o_ref[...] = jnp.zeros(o_ref.shape, o_ref.dtype)


def _moe_experts(xs, gather_tbl, tile_expert, n_used, wg, wu, wd, n_mm_tiles):
    W = xs.shape[1]
    E, D, FF = wg.shape

    grid_spec = pltpu.PrefetchScalarGridSpec(
        num_scalar_prefetch=3,
        grid=(n_mm_tiles,),
        in_specs=[pl.BlockSpec(memory_space=pl.ANY),
                  pl.BlockSpec((1, D, FF), lambda g, gt, te, nu: (te[g], 0, 0)),
                  pl.BlockSpec((1, D, FF), lambda g, gt, te, nu: (te[g], 0, 0)),
                  pl.BlockSpec((1, FF, D), lambda g, gt, te, nu: (te[g], 0, 0))],
        out_specs=pl.BlockSpec((MOE_TM, D), lambda g, gt, te, nu: (g, 0)),
        scratch_shapes=[pltpu.VMEM((D, FF), BF16), pltpu.VMEM((D, FF), BF16), pltpu.VMEM((FF, D), BF16),
                        pltpu.VMEM((2, MOE_TM, W), BF16), pltpu.SemaphoreType.DMA((2,))],
    )
    return pl.pallas_call(
        _moe_expert_kernel,
        grid_spec=grid_spec,
        out_shape=jax.ShapeDtypeStruct((n_mm_tiles * MOE_TM, D), BF16),
        compiler_params=_cparams("arbitrary"),
        name="moe_experts",
    )(gather_tbl, tile_expert, n_used, xs, wg, wu, wd)


def _finish_kernel(rt_ref, ys_hbm, dest_ref, h_ref, p_ref, gp_ref, gf_ref, wg_ref, wp_ref, o_ref, y_buf, y_sem):
    tok = h_ref.shape[0]
    i = pl.program_id(0)
    slot = i % 2
    start_gather, wait_gather = _chunk_gather(rt_ref, ys_hbm, y_buf, y_sem, MOE_CPT)

    @pl.when(i == 0)
    def _():
        start_gather(0, 0)

    @pl.when(i + 1 < pl.num_programs(0))
    def _():
        start_gather(i + 1, (i + 1) % 2)

    wait_gather(i, slot)
    dcol = dest_ref[0].astype(F32).T
    lane = lax.broadcasted_iota(jnp.int32, (tok, MOE_ROWS), 1).astype(F32)
    unsort = ((lane == dcol[:, 0:1]) | (lane == dcol[:, 1:2])).astype(BF16)
    h = h_ref[...] + jnp.dot(unsort, y_buf[slot], preferred_element_type=F32)
    t = _rms(h, gp_ref[...]).astype(BF16)
    gate = jax.nn.sigmoid(jnp.dot(t, wg_ref[...], preferred_element_type=F32))
    emb = jnp.dot(p_ref[...].astype(BF16), wp_ref[...], preferred_element_type=F32)
    o_ref[...] = _rms(h + emb * gate, gf_ref[...])


def _finish(ys, return_tbl, dest, h1, p2, g_ple, g_final, wg, wp):
    T, D = h1.shape
    P = p2.shape[1]
    nt = T // MOE_TOK

    vec = pl.BlockSpec((1, D), lambda i, rt: (0, 0))
    grid_spec = pltpu.PrefetchScalarGridSpec(
        num_scalar_prefetch=1,
        grid=(nt,),
        in_specs=[
            pl.BlockSpec(memory_space=pl.ANY),
            pl.BlockSpec((1, SUBLANES, MOE_TOK), lambda i, rt: (i, 0, 0)),
            pl.BlockSpec((MOE_TOK, D), lambda i, rt: (i, 0)),
            pl.BlockSpec((MOE_TOK, P), lambda i, rt: (i, 0)),
            vec, vec,
            pl.BlockSpec((D, D), lambda i, rt: (0, 0), pipeline_mode=pl.Buffered(1)),
            pl.BlockSpec((P, D), lambda i, rt: (0, 0), pipeline_mode=pl.Buffered(1))],
        out_specs=pl.BlockSpec((MOE_TOK, D), lambda i, rt: (i, 0)),
        scratch_shapes=[pltpu.VMEM((2, MOE_ROWS, D), BF16), pltpu.SemaphoreType.DMA((2,))],
    )
    return pl.pallas_call(
        _finish_kernel,
        grid_spec=grid_spec,
        out_shape=jax.ShapeDtypeStruct((T, D), F32),
        compiler_params=_cparams("arbitrary"),
        name="finish",
    )(return_tbl, ys, dest, h1, p2, g_ple, g_final, wg, wp)


def kernel(x, p, g_mix, w_in, rel_bias, w_attn_br, conv_w, conv_b, ln_g, ln_b, w_conv_br, w_o, g_ffn,
           w_router_g, b_router_g, w_router_e, b_router_e, w_e_gate, w_e_up, w_e_down, g_ple, w_ple_gate,
           w_ple_proj, g_final):
    B, S, D = x.shape
    T = B * S
    depth = w_in.shape[0]
    width = N_HEADS * HEAD_DIM
    assert D == width and conv_w.shape[2] == D, "column-block indexing assumes all branch widths equal d_model"
    assert S % MOBA_BLOCK == 0
    assert depth == 1, "the final RMSNorm is fused into the last stage of a single layer"
    a_blk = 3 * width // D
    gate_blk = a_blk + 2

    h = x.reshape(T, D)
    for i in range(depth):
        proj = _in_proj(h, g_mix[i][None], w_in[i].astype(BF16))
        proj3 = proj.reshape(B, S, -1)
        attn = _moba(proj3, rel_bias)
        conv = _conv(proj3, conv_w[i], conv_b[i][None], ln_g[i][None], ln_b[i][None], a_blk)
        h = _mix(h, attn.reshape(T, D), conv.reshape(T, D), proj,
                 w_attn_br[i].astype(BF16), w_conv_br[i].astype(BF16), w_o[i].astype(BF16), gate_blk)
        xs, dest, cnt = _moe_sort(h, g_ffn[i][None], w_router_e[i], b_router_e[i], w_router_g[i], b_router_g[i])
        n_tok_tiles = T // MOE_TOK
        worst_chunks = (MOE_TOPK * T + n_tok_tiles * N_EXPERTS * (MOE_CHUNK - 1)) // MOE_CHUNK
        n_mm_tiles = -(-worst_chunks // MOE_CPM) + N_EXPERTS
        gather_tbl, return_tbl, tile_expert, n_used = _moe_tables(cnt[:, :, 0], n_mm_tiles)
        ys = _moe_experts(xs, gather_tbl, tile_expert, n_used,
                          w_e_gate[i], w_e_up[i], w_e_down[i], n_mm_tiles)
        out = _finish(ys, return_tbl, dest, h, p[i].reshape(T, -1), g_ple[i][None], g_final[None],
                      w_ple_gate[i].astype(BF16), w_ple_proj[i].astype(BF16))
    return out.reshape(B, S, D)
```

```python
import functools
import math

import numpy as np
import jax
import jax.numpy as jnp
from jax import lax
from jax.experimental import pallas as pl
from jax.experimental.pallas import tpu as pltpu

F32 = jnp.float32
BF16 = jnp.bfloat16

N_HEADS = 16
HEAD_DIM = 128
MOBA_BLOCK = 256
MOBA_TOPK = 3
N_BUCKETS = 32
MAX_DISTANCE = 128
CONV_KERNEL = 31
N_GROUPS = 4
EXPERTS_PER_GROUP = 4
N_EXPERTS = N_GROUPS * EXPERTS_PER_GROUP
EPS = 1e-6
NEG = -1e30
LOG2E = 1.4426950408889634

V7X_VMEM_LIMIT_BYTES = 56 * 1024 * 1024
LANES = 128
SUBLANES = 8

_NT = (((1,), (1,)), ((), ()))


def _cparams(*sem):
    return pltpu.CompilerParams(dimension_semantics=sem, vmem_limit_bytes=V7X_VMEM_LIMIT_BYTES)


def _resident(shape):
    return pl.BlockSpec(shape, lambda *_: (0,) * len(shape), pipeline_mode=pl.Buffered(1))


def _sigmoid(x):
    return 0.5 * jnp.tanh(0.5 * x) + 0.5


def _rms(x, g):
    return x * lax.rsqrt(jnp.mean(x * x, axis=-1, keepdims=True) + EPS) * g


def _in_proj_kernel(x_ref, g_ref, w_ref, o_ref, xn_ref):
    @pl.when(pl.program_id(1) == 0)
    def _():
        xn_ref[...] = _rms(x_ref[...], g_ref[...]).astype(BF16)

    o_ref[...] = jnp.dot(xn_ref[...], w_ref[...], preferred_element_type=F32).astype(o_ref.dtype)


def _in_proj(x2, g, w, tm=1024, tn=2048):
    T, D = x2.shape
    N = w.shape[1]
    return pl.pallas_call(
        _in_proj_kernel,
        grid=(T // tm, N // tn),
        in_specs=[pl.BlockSpec((tm, D), lambda i, j: (i, 0)),
                  pl.BlockSpec((1, D), lambda i, j: (0, 0)),
                  pl.BlockSpec((D, tn), lambda i, j: (0, j))],
        out_specs=pl.BlockSpec((tm, tn), lambda i, j: (i, j)),
        out_shape=jax.ShapeDtypeStruct((T, N), BF16),
        scratch_shapes=[pltpu.VMEM((tm, D), BF16)],
        compiler_params=_cparams("parallel", "arbitrary"),
        name="in_proj",
    )(x2, g, w)


MOBA_PV_ROWS = HEAD_DIM + 16
MOBA_Q_LANES = MOBA_BLOCK
MOBA_HEADS_PER_STEP = 4


def _rel_buckets_t():
    k = np.arange(MOBA_BLOCK)[:, None]
    q = np.arange(MOBA_BLOCK)[None, :]
    dist = np.stack([q - k, MOBA_BLOCK + q - k]).astype(np.int32)
    n = np.maximum(dist, 0)
    max_exact = N_BUCKETS // 2
    nf = np.maximum(n, 1).astype(np.float32)
    large = max_exact + (np.log(nf / np.float32(max_exact)) / np.float32(math.log(MAX_DISTANCE / max_exact))
                         * np.float32(N_BUCKETS - max_exact)).astype(np.int32)
    large = np.minimum(large, N_BUCKETS - 1)
    return np.where(n < max_exact, n, large).astype(np.int32)


def _moba_kernel(tbl_ref, bkt_ref, q_ref, k_ref, v_ref, o_ref,
                 kmean_ref, vt_ref, bias_ref, qt_ref, sel_ref, s_ref, p_ref, m_ref, a_ref, acc_ref,
                 *, n_blocks):
    hg = pl.program_id(0)
    b = pl.program_id(1)
    BS = MOBA_BLOCK
    HPS = MOBA_HEADS_PER_STEP
    lanes = [slice(hh * HEAD_DIM, (hh + 1) * HEAD_DIM) for hh in range(HPS)]

    @pl.when(b == 0)
    def _build_bias():
        krow = lax.broadcasted_iota(jnp.int32, (BS, BS), 0)
        qcol = lax.broadcasted_iota(jnp.int32, (BS, BS), 1)
        for hh in range(HPS):
            bias_ref[hh] = jnp.zeros((2, BS, BS), F32)

            def fill(n, carry):
                bias_ref[hh] = jnp.where(bkt_ref[...] == n, tbl_ref[hg * HPS + hh, n] * LOG2E, bias_ref[hh])
                return carry

            lax.fori_loop(0, N_BUCKETS, fill, 0)
            bias_ref[hh, 0] = jnp.where(qcol >= krow, bias_ref[hh, 0], NEG)

    ones_row = (lax.broadcasted_iota(jnp.int32, (MOBA_PV_ROWS - HEAD_DIM, BS), 0) == 0).astype(BF16)

    def prep(n, carry):
        rows = pl.ds(pl.multiple_of(n * BS, BS), BS)
        for hh in range(HPS):
            kmean_ref[hh, pl.ds(n, 1), :] = jnp.mean(k_ref[0, rows, lanes[hh]].astype(F32), axis=0, keepdims=True)
            vt_ref[hh, n, :HEAD_DIM] = v_ref[0, rows, lanes[hh]].astype(F32).T.astype(BF16)
            vt_ref[hh, n, HEAD_DIM:] = ones_row
        return carry

    lax.fori_loop(0, n_blocks, prep, 0)

    q_block = functools.partial(_moba_q_block, hg=hg, tbl_ref=tbl_ref, q_ref=q_ref, k_ref=k_ref, o_ref=o_ref,
                                kmean_ref=kmean_ref, vt_ref=vt_ref, bias_ref=bias_ref, qt_ref=qt_ref,
                                sel_ref=sel_ref, s_ref=s_ref, p_ref=p_ref, m_ref=m_ref, a_ref=a_ref,
                                acc_ref=acc_ref, n_blocks=n_blocks)

    q_block(jnp.int32(0), finish_previous=False)

    def next_block(qi, carry):
        q_block(qi, finish_previous=True)
        return carry

    lax.fori_loop(1, n_blocks, next_block, 0)
    _moba_close(jnp.int32(n_blocks - 1), vt_ref=vt_ref, p_ref=p_ref, a_ref=a_ref, acc_ref=acc_ref, o_ref=o_ref)


def _moba_close(qi, *, vt_ref, p_ref, a_ref, acc_ref, o_ref):
    lanes = [slice(hh * HEAD_DIM, (hh + 1) * HEAD_DIM) for hh in range(MOBA_HEADS_PER_STEP)]
    q_rows = pl.ds(pl.multiple_of(qi * MOBA_BLOCK, MOBA_BLOCK), MOBA_BLOCK)
    for hh in range(MOBA_HEADS_PER_STEP):
        acc = a_ref[hh] * acc_ref[hh] + jnp.dot(vt_ref[hh, 0], p_ref[hh], preferred_element_type=F32)
        denom = acc[HEAD_DIM:HEAD_DIM + 1, :]
        out_t = acc[:HEAD_DIM, :] * (1.0 / denom)
        o_ref[0, q_rows, lanes[hh]] = out_t.T.astype(o_ref.dtype)


def _moba_q_block(qi, *, finish_previous, hg, tbl_ref, q_ref, k_ref, o_ref, kmean_ref, vt_ref, bias_ref, qt_ref,
                  sel_ref, s_ref, p_ref, m_ref, a_ref, acc_ref, n_blocks):
    BS = MOBA_BLOCK
    HPS = MOBA_HEADS_PER_STEP
    scale = HEAD_DIM ** -0.5
    lanes = [slice(hh * HEAD_DIM, (hh + 1) * HEAD_DIM) for hh in range(HPS)]
    q_rows = pl.ds(pl.multiple_of(qi * BS, BS), BS)

    def qk_stage(hh, t):
        j0 = pl.multiple_of(jnp.maximum(qi - t, 0) * BS, BS)
        s_ref[hh] = jnp.dot(k_ref[0, pl.ds(j0, BS), lanes[hh]], qt_ref[hh], preferred_element_type=F32)

    def softmax_stage(hh, t, bias_of=None, far_bias=None):
        for c in range(BS // MOBA_Q_LANES):
            qs = slice(c * MOBA_Q_LANES, (c + 1) * MOBA_Q_LANES)
            chosen = (lax.shift_right_logical(sel_ref[hh, :, qs], qi - t) & 1) == 1
            c2 = scale * LOG2E
            if bias_of is not None:
                st = s_ref[hh, :, qs] * c2 + bias_of(qs)
                shift = 0.0
                top = jnp.max(st, axis=0, keepdims=True)
            else:
                st = None
                shift = far_bias
                top = jnp.max(s_ref[hh, :, qs], axis=0, keepdims=True) * c2 + shift
            m_prev = m_ref[hh, :, qs]
            m_new = jnp.where(chosen, jnp.maximum(m_prev, top), m_prev)
            a_ref[hh, :, qs] = jnp.exp2(m_prev - m_new)
            m_ref[hh, :, qs] = m_new
            cut = jnp.where(chosen, m_new - shift, -NEG)
            if st is None:
                st = s_ref[hh, :, qs] * c2
            p_ref[hh, :, qs] = jnp.exp2(st - cut).astype(BF16)

    def pv_stage(hh, t):
        acc_ref[hh] = a_ref[hh] * acc_ref[hh] + jnp.dot(vt_ref[hh, qi - t], p_ref[hh],
                                                        preferred_element_type=F32)

    if finish_previous:
        _moba_close(qi - 1, vt_ref=vt_ref, p_ref=p_ref, a_ref=a_ref, acc_ref=acc_ref, o_ref=o_ref)

    for hh in range(HPS):
        qt = q_ref[0, q_rows, lanes[hh]].astype(F32).T.astype(BF16)
        qt_ref[hh] = qt
        gate = jnp.dot(kmean_ref[hh].astype(BF16), qt, preferred_element_type=F32)
        blk = lax.broadcasted_iota(jnp.int32, gate.shape, 0)
        blkf = blk.astype(F32)
        g = jnp.where(blk < qi, gate, NEG)
        bits = jnp.full((1, BS), lax.shift_left(jnp.int32(1), qi), jnp.int32)
        for _ in range(MOBA_TOPK):
            mx = jnp.max(g, axis=0, keepdims=True)
            ix = jnp.min(jnp.where(g == mx, blkf, float(n_blocks)), axis=0, keepdims=True).astype(jnp.int32)
            bits = bits | jnp.where(ix < qi, lax.shift_left(jnp.int32(1), ix), 0)
            g = jnp.where(blk == ix, -jnp.inf, g)
        sel_ref[hh] = bits
        m_ref[hh] = jnp.full((1, BS), NEG, F32)
        acc_ref[hh] = jnp.zeros((MOBA_PV_ROWS, BS), F32)
        qk_stage(hh, 0)

    for hh in range(HPS):
        softmax_stage(hh, 0, lambda qs, hh=hh: bias_ref[hh, 0, :, qs])
    if not finish_previous:
        return

    for hh in range(HPS):
        qk_stage(hh, 1)
    for hh in range(HPS):
        pv_stage(hh, 0)
    for hh in range(HPS):
        softmax_stage(hh, 1, lambda qs, hh=hh: bias_ref[hh, 1, :, qs])
    for hh in range(HPS):
        qk_stage(hh, 2)

    def sweep(t):
        for hh in range(HPS):
            pv_stage(hh, t - 1)
        for hh in range(HPS):
            softmax_stage(hh, t, far_bias=tbl_ref[hg * HPS + hh, N_BUCKETS - 1] * LOG2E)
        for hh in range(HPS):
            qk_stage(hh, t + 1)

    n_far = qi - 1

    def sweep_pair(i, carry):
        sweep(2 + 2 * i)
        sweep(3 + 2 * i)
        return carry

    lax.fori_loop(0, n_far // 2, sweep_pair, 0)

    @pl.when(n_far % 2 == 1)
    def _odd_block():
        sweep(qi)


def _moba(proj3, rel_bias):
    B, S, _ = proj3.shape
    BS = MOBA_BLOCK
    HPS = MOBA_HEADS_PER_STEP
    W = HPS * HEAD_DIM
    nb = S // BS
    n_groups = N_HEADS // HPS
    bkt = jnp.asarray(_rel_buckets_t())
    tbl = rel_bias.astype(F32).T
    grid_spec = pltpu.PrefetchScalarGridSpec(
        num_scalar_prefetch=1,
        grid=(n_groups, B),
        in_specs=[pl.BlockSpec((2, BS, BS), lambda h, b, t: (0, 0, 0)),
                  pl.BlockSpec((1, S, W), lambda h, b, t: (b, 0, h)),
                  pl.BlockSpec((1, S, W), lambda h, b, t: (b, 0, n_groups + h)),
                  pl.BlockSpec((1, S, W), lambda h, b, t: (b, 0, 2 * n_groups + h))],
        out_specs=pl.BlockSpec((1, S, W), lambda h, b, t: (b, 0, h)),
        scratch_shapes=[pltpu.VMEM((HPS, nb, HEAD_DIM), F32),
                        pltpu.VMEM((HPS, nb, MOBA_PV_ROWS, BS), BF16),
                        pltpu.VMEM((HPS, 2, BS, BS), F32),
                        pltpu.VMEM((HPS, HEAD_DIM, BS), BF16),
                        pltpu.VMEM((HPS, 1, BS), jnp.int32),
                        pltpu.VMEM((HPS, BS, BS), F32),
                        pltpu.VMEM((HPS, BS, BS), BF16),
                        pltpu.VMEM((HPS, 1, BS), F32),
                        pltpu.VMEM((HPS, 1, BS), F32),
                        pltpu.VMEM((HPS, MOBA_PV_ROWS, BS), F32)],
    )
    return pl.pallas_call(
        functools.partial(_moba_kernel, n_blocks=nb),
        grid_spec=grid_spec,
        out_shape=jax.ShapeDtypeStruct((B, S, N_HEADS * HEAD_DIM), BF16),
        compiler_params=_cparams("arbitrary", "arbitrary"),
        name="moba",
    )(tbl, bkt, proj3, proj3, proj3)


CONV_HALO = 32
CONV_ROWS = 64
CONV_COLS = 256


def _conv_kernel(a_ref, g_ref, ah_ref, gh_ref, w_ref, cb_ref, lng_ref, lnb_ref, o_ref, u_ref, sh_ref, c_ref,
                 *, ts):
    i = pl.program_id(1)
    D = u_ref.shape[1]
    u_ref[CONV_HALO:, :] = a_ref[0].astype(F32) * _sigmoid(g_ref[0].astype(F32))
    uh = ah_ref[0].astype(F32) * _sigmoid(gh_ref[0].astype(F32))
    u_ref[:CONV_HALO, :] = jnp.where(i > 0, uh, 0.0)

    first = CONV_HALO - (CONV_KERNEL - 1)
    sh_rows = sh_ref.shape[1]

    def col_body(c, carry):
        cols = pl.ds(pl.multiple_of(c * CONV_COLS, CONV_COLS), CONV_COLS)
        for s in range(1, SUBLANES):
            sh_ref[s] = u_ref[s:s + sh_rows, cols]
        for r in range(ts // CONV_ROWS):
            acc = jnp.zeros((CONV_ROWS, CONV_COLS), F32)
            for k in range(CONV_KERNEL):
                s = (first + k) % SUBLANES
                r0 = r * CONV_ROWS + first + k - s
                tap = u_ref[r0:r0 + CONV_ROWS, cols] if s == 0 else sh_ref[s, r0:r0 + CONV_ROWS, :]
                acc = acc + jnp.tile(w_ref[k, :, cols], (CONV_ROWS // SUBLANES, 1)) * tap
            c_ref[r * CONV_ROWS:(r + 1) * CONV_ROWS, cols] = acc + cb_ref[:, cols]
        return carry

    lax.fori_loop(0, D // CONV_COLS, col_body, 0)

    y = c_ref[...]
    mu = jnp.mean(y, axis=-1, keepdims=True)
    yc = y - mu
    var = jnp.mean(yc * yc, axis=-1, keepdims=True)
    z = yc * lax.rsqrt(var + EPS) * lng_ref[...] + lnb_ref[...]
    o_ref[0] = (z * _sigmoid(z)).astype(o_ref.dtype)


def _conv(proj3, conv_w, conv_b, ln_g, ln_b, a_blk, ts=512):
    B, S, _ = proj3.shape
    D = conv_w.shape[1]
    hpb = ts // CONV_HALO
    w_rows = jnp.broadcast_to(conv_w[:, None, :], (CONV_KERNEL, SUBLANES, D))

    def halo(col):
        return pl.BlockSpec((1, CONV_HALO, D), lambda b, i: (b, jnp.maximum(i * hpb - 1, 0), col))

    vec = pl.BlockSpec((1, D), lambda b, i: (0, 0))
    return pl.pallas_call(
        functools.partial(_conv_kernel, ts=ts),
        grid=(B, S // ts),
        in_specs=[pl.BlockSpec((1, ts, D), lambda b, i: (b, i, a_blk)),
                  pl.BlockSpec((1, ts, D), lambda b, i: (b, i, a_blk + 1)),
                  halo(a_blk), halo(a_blk + 1),
                  pl.BlockSpec((CONV_KERNEL, SUBLANES, D), lambda b, i: (0, 0, 0)),
                  vec, vec, vec],
        out_specs=pl.BlockSpec((1, ts, D), lambda b, i: (b, i, 0)),
        out_shape=jax.ShapeDtypeStruct((B, S, D), BF16),
        scratch_shapes=[pltpu.VMEM((CONV_HALO + ts, D), F32),
                        pltpu.VMEM((SUBLANES, CONV_HALO + ts - SUBLANES, CONV_COLS), F32),
                        pltpu.VMEM((ts, D), F32)],
        compiler_params=_cparams("parallel", "arbitrary"),
        name="conv",
    )(proj3, proj3, proj3, proj3, w_rows, conv_b, ln_g, ln_b)


def _mix_kernel(x_ref, at_ref, cv_ref, ga_ref, gc_ref, wa_ref, wc_ref, wo_ref, o_ref):
    ya = jnp.dot(at_ref[...], wa_ref[...], preferred_element_type=F32)
    yc = jnp.dot(cv_ref[...], wc_ref[...], preferred_element_type=F32)
    mixed = (jax.nn.sigmoid(ga_ref[...].astype(F32)) * ya + jax.nn.sigmoid(gc_ref[...].astype(F32)) * yc)
    o_ref[...] = x_ref[...] + jnp.dot(mixed.astype(BF16), wo_ref[...], preferred_element_type=F32)


def _mix(x2, attn2, conv2, proj2, wa, wc, wo, gate_blk, tm=256):
    T, D = x2.shape
    row = lambda col: pl.BlockSpec((tm, D), lambda i: (i, col))
    return pl.pallas_call(
        _mix_kernel,
        grid=(T // tm,),
        in_specs=[row(0), row(0), row(0), row(gate_blk), row(gate_blk + 1),
                  _resident((D, D)), _resident((D, D)), _resident((D, D))],
        out_specs=row(0),
        out_shape=jax.ShapeDtypeStruct((T, D), F32),
        compiler_params=_cparams("parallel"),
        name="mix",
    )(x2, attn2, conv2, proj2, proj2, wa, wc, wo)


MOE_TOK = 512
MOE_CHUNK = 16
MOE_TM = 512
MOE_TOPK = 2
_MOE_WORST_ROWS = MOE_TOPK * MOE_TOK + N_EXPERTS * (MOE_CHUNK - 1)
MOE_ROWS = -(-_MOE_WORST_ROWS // LANES) * LANES
MOE_CPT = MOE_ROWS // MOE_CHUNK
MOE_CPM = MOE_TM // MOE_CHUNK
MOE_EXTRA = LANES


def _route_t(lt):
    row = lax.broadcasted_iota(jnp.int32, lt.shape, 0)
    rowf = row.astype(F32)
    big = float(lt.shape[0])
    is_grp = (row >= N_EXPERTS) & (row < N_EXPERTS + N_GROUPS)
    cl = jnp.where(is_grp, lt, -jnp.inf)
    mg = jnp.max(cl, axis=0, keepdims=True)
    g_star = jnp.min(jnp.where(cl == mg, rowf, big), axis=0, keepdims=True).astype(jnp.int32) - N_EXPERTS
    p_g = 1.0 / jnp.sum(jnp.where(is_grp, jnp.exp(cl - mg), 0.0), axis=0, keepdims=True)
    in_grp = (row < N_EXPERTS) & ((row // EXPERTS_PER_GROUP) == g_star)
    f1 = jnp.where(in_grp, lt, -jnp.inf)
    v1 = jnp.max(f1, axis=0, keepdims=True)
    i1 = jnp.min(jnp.where(f1 == v1, rowf, big), axis=0, keepdims=True).astype(jnp.int32)
    f2 = jnp.where(row == i1, -jnp.inf, f1)
    v2 = jnp.max(f2, axis=0, keepdims=True)
    i2 = jnp.min(jnp.where(f2 == v2, rowf, big), axis=0, keepdims=True).astype(jnp.int32)
    e2 = jnp.exp(v2 - v1)
    return i1, i2, p_g / (1.0 + e2), p_g * e2 / (1.0 + e2)


def _split_bf16(x):
    hi = x.astype(BF16)
    return hi, (x - hi.astype(F32)).astype(BF16)


def _moe_sort_kernel(h_ref, g_ref, wh_ref, wl_ref, br_ref, xs_ref, dest_ref, cnt_ref):
    tok = h_ref.shape[0]
    D = h_ref.shape[1]
    hn = _rms(h_ref[...], g_ref[...])
    hn_hi, hn_lo = _split_bf16(hn)
    nt_dot = lambda a, b: lax.dot_general(a, b, _NT, preferred_element_type=F32)
    w_hi = wh_ref[...]
    both = nt_dot(jnp.concatenate([w_hi, wl_ref[...]], axis=0), hn_hi)
    lt = both[:MOE_ROUTE_ROWS] + both[MOE_ROUTE_ROWS:] + nt_dot(w_hi, hn_lo) + br_ref[...]
    i1, i2, c1, c2 = _route_t(lt)

    erow = lax.broadcasted_iota(jnp.int32, (N_EXPERTS, tok), 0)
    oh1 = erow == i1
    oh2 = erow == i2
    oh = (oh1 | oh2).astype(F32)
    earlier = (lax.broadcasted_iota(jnp.int32, (tok, tok), 0) < lax.broadcasted_iota(jnp.int32, (tok, tok), 1))
    cum = jnp.dot(oh.astype(BF16), earlier.astype(BF16), preferred_element_type=F32)
    cnt = jnp.sum(oh, axis=1, keepdims=True)
    n_chunks = jnp.floor((cnt + (MOE_CHUNK - 1)) * (1.0 / MOE_CHUNK))
    ncb = jnp.broadcast_to(n_chunks, (N_EXPERTS, LANES))
    erow_l = lax.broadcasted_iota(jnp.int32, (N_EXPERTS, LANES), 0)
    first_chunk = jnp.zeros((N_EXPERTS, LANES), F32)
    for e in range(N_EXPERTS - 1):
        first_chunk = first_chunk + jnp.where(erow_l > e, ncb[e:e + 1, :], 0.0)
    base = jnp.tile(first_chunk, (1, tok // LANES)) * MOE_CHUNK + cum
    d1 = jnp.sum(jnp.where(oh1, base, 0.0), axis=0, keepdims=True).astype(jnp.int32)
    d2 = jnp.sum(jnp.where(oh2, base, 0.0), axis=0, keepdims=True).astype(jnp.int32)

    srow = lax.broadcasted_iota(jnp.int32, (MOE_ROWS, tok), 0)
    p1 = srow == d1
    p2 = srow == d2
    perm1 = p1.astype(BF16)
    perm2 = p2.astype(BF16)
    xs_ref[:, :D] = jnp.dot(perm1 + perm2, hn_hi, preferred_element_type=F32).astype(BF16)
    xrow = lax.broadcasted_iota(jnp.int32, (MOE_EXTRA, tok), 0)

    def weight_rows(c):
        hi = c.astype(BF16).astype(F32)
        return jnp.where(xrow == 0, hi, jnp.where(xrow == 1, c - hi, 0.0)).astype(BF16)

    xs_ref[:, D:] = (nt_dot(perm1, weight_rows(c1)) + nt_dot(perm2, weight_rows(c2))).astype(BF16)

    r8 = lax.broadcasted_iota(jnp.int32, (SUBLANES, tok), 0)
    dest_ref[0] = jnp.where(r8 == 0, d1, jnp.where(r8 == 1, d2, 0))
    cnt_ref[0] = jnp.broadcast_to(cnt, (N_EXPERTS, LANES)).astype(jnp.int32)


MOE_ROUTE_ROWS = 32


def _moe_sort(h1, g, w_router_e, b_router_e, w_router_g, b_router_g):
    T, D = h1.shape
    nt = T // MOE_TOK
    pad = MOE_ROUTE_ROWS - N_EXPERTS - N_GROUPS
    wr = jnp.concatenate([w_router_e.T, w_router_g.T, jnp.zeros((pad, D), F32)], axis=0)
    wr_hi = wr.astype(BF16)
    wr_lo = (wr - wr_hi.astype(F32)).astype(BF16)
    br = jnp.concatenate([b_router_e, b_router_g, jnp.zeros((pad,), F32)])[:, None]
    return pl.pallas_call(
        _moe_sort_kernel,
        grid=(nt,),
        in_specs=[pl.BlockSpec((MOE_TOK, D), lambda i: (i, 0)),
                  pl.BlockSpec((1, D), lambda i: (0, 0)),
                  pl.BlockSpec((MOE_ROUTE_ROWS, D), lambda i: (0, 0)),
                  pl.BlockSpec((MOE_ROUTE_ROWS, D), lambda i: (0, 0)),
                  pl.BlockSpec((MOE_ROUTE_ROWS, 1), lambda i: (0, 0))],
        out_specs=[pl.BlockSpec((MOE_ROWS, D + MOE_EXTRA), lambda i: (i, 0)),
                   pl.BlockSpec((1, SUBLANES, MOE_TOK), lambda i: (i, 0, 0)),
                   pl.BlockSpec((1, N_EXPERTS, LANES), lambda i: (i, 0, 0))],
        out_shape=[jax.ShapeDtypeStruct((nt * MOE_ROWS, D + MOE_EXTRA), BF16),
                   jax.ShapeDtypeStruct((nt, SUBLANES, MOE_TOK), jnp.int32),
                   jax.ShapeDtypeStruct((nt, N_EXPERTS, LANES), jnp.int32)],
        compiler_params=_cparams("parallel"),
        name="moe_sort",
    )(h1, g, wr_hi, wr_lo, br)


def _moe_tables(cnt, n_mm_tiles):
    nt = cnt.shape[0]
    pc = (cnt + (MOE_CHUNK - 1)) // MOE_CHUNK
    run_end = jnp.cumsum(pc, axis=1)
    run_first = run_end - pc
    per_expert = jnp.sum(pc, axis=0)
    mm_tiles = (per_expert + (MOE_CPM - 1)) // MOE_CPM
    mm_end = jnp.cumsum(mm_tiles)
    slot_base = (mm_end - mm_tiles)[None, :] * MOE_CPM + (jnp.cumsum(pc, axis=0) - pc)
    k = jnp.arange(MOE_CPT, dtype=jnp.int32)[None, :, None]
    expert_of = jnp.sum((run_end[:, None, :] <= k).astype(jnp.int32), axis=2)
    valid = expert_of < N_EXPERTS
    pick = expert_of[:, :, None] == jnp.arange(N_EXPERTS, dtype=jnp.int32)
    take = lambda a: jnp.sum(jnp.where(pick, a[:, None, :], 0), axis=2)
    slot = take(slot_base) + (k[:, :, 0] - take(run_first))
    src = (jnp.arange(nt, dtype=jnp.int32)[:, None] * MOE_CPT + k[:, :, 0]).reshape(-1)
    slot = jnp.where(valid, slot, n_mm_tiles * MOE_CPM).reshape(-1)
    gather_tbl = jnp.zeros((n_mm_tiles * MOE_CPM,), jnp.int32).at[slot].set(src, mode="drop")
    return_tbl = jnp.where(valid.reshape(-1), slot, 0).astype(jnp.int32)
    g = jnp.arange(n_mm_tiles, dtype=jnp.int32)
    n_used = mm_end[-1].astype(jnp.int32)
    tile_expert = jnp.sum((mm_end[None, :] <= jnp.minimum(g, n_used - 1)[:, None]).astype(jnp.int32), axis=1)
    return gather_tbl, return_tbl, tile_expert.astype(jnp.int32), n_used.reshape(1)


def _chunk_gather(tbl_ref, src_hbm, buf, sem, chunks_per_tile):
    def chunk_copy(tile, c, slot):
        src = pl.multiple_of(tbl_ref[tile * chunks_per_tile + c] * MOE_CHUNK, MOE_CHUNK)
        dst = pl.multiple_of(c * MOE_CHUNK, MOE_CHUNK)
        return pltpu.make_async_copy(src_hbm.at[pl.ds(src, MOE_CHUNK), :],
                                     buf.at[slot, pl.ds(dst, MOE_CHUNK), :], sem.at[slot])

    def for_each_chunk(tile, slot, act):
        def body(c, carry):
            act(chunk_copy(tile, c, slot))
            return carry
        lax.fori_loop(0, chunks_per_tile, body, 0)

    def start(tile, slot):
        for_each_chunk(tile, slot, lambda copy: copy.start())

    def wait(tile, slot):
        for_each_chunk(tile, slot, lambda copy: copy.wait())

    return start, wait


def _moe_expert_kernel(gt_ref, te_ref, nu_ref, xs_hbm, wg_ref, wu_ref, wd_ref, o_ref,
                       wg_bf, wu_bf, wd_bf, x_buf, x_sem):
    D = o_ref.shape[1]
    g = pl.program_id(0)
    n_used = nu_ref[0]
    used = g < n_used

    start_gather, wait_gather = _chunk_gather(gt_ref, xs_hbm, x_buf, x_sem, MOE_CPM)

    @pl.when((g == 0) & (n_used > 0))
    def _():
        start_gather(0, 0)

    @pl.when(g + 1 < n_used)
    def _():
        start_gather(g + 1, (g + 1) % 2)

    @pl.when(used & ((g == 0) | (te_ref[g] != te_ref[jnp.maximum(g - 1, 0)])))
    def _():
        wg_bf[...] = wg_ref[0].astype(BF16)
        wu_bf[...] = wu_ref[0].astype(BF16)
        wd_bf[...] = wd_ref[0].astype(BF16)

    @pl.when(used)
    def _():
        slot = g % 2
        wait_gather(g, slot)
        xs = x_buf[slot]
        x = xs[:, :D]
        extra = xs[:, D:].astype(F32)
        c = extra[:, 0:1] + extra[:, 1:2]
        gate = jnp.dot(x, wg_bf[...], preferred_element_type=F32)
        up = jnp.dot(x, wu_bf[...], preferred_element_type=F32)
        hh = (gate * jax.nn.sigmoid(gate)) * up * c
        o_ref[...] = jnp.dot(hh.astype(BF16), wd_bf[...], preferred_element_type=F32).astype(o_ref.dtype)

    @pl.when(jnp.logical_not(used))
    def _():
        o_ref[...] = jnp.zeros(o_ref.shape, o_ref.dtype)


def _moe_experts(xs, gather_tbl, tile_expert, n_used, wg, wu, wd, n_mm_tiles):
    W = xs.shape[1]
    E, D, FF = wg.shape

    grid_spec = pltpu.PrefetchScalarGridSpec(
        num_scalar_prefetch=3,
        grid=(n_mm_tiles,),
        in_specs=[pl.BlockSpec(memory_space=pl.ANY),
                  pl.BlockSpec((1, D, FF), lambda g, gt, te, nu: (te[g], 0, 0)),
                  pl.BlockSpec((1, D, FF), lambda g, gt, te, nu: (te[g], 0, 0)),
                  pl.BlockSpec((1, FF, D), lambda g, gt, te, nu: (te[g], 0, 0))],
        out_specs=pl.BlockSpec((MOE_TM, D), lambda g, gt, te, nu: (g, 0)),
        scratch_shapes=[pltpu.VMEM((D, FF), BF16), pltpu.VMEM((D, FF), BF16), pltpu.VMEM((FF, D), BF16),
                        pltpu.VMEM((2, MOE_TM, W), BF16), pltpu.SemaphoreType.DMA((2,))],
    )
    return pl.pallas_call(
        _moe_expert_kernel,
        grid_spec=grid_spec,
        out_shape=jax.ShapeDtypeStruct((n_mm_tiles * MOE_TM, D), BF16),
        compiler_params=_cparams("arbitrary"),
        name="moe_experts",
    )(gather_tbl, tile_expert, n_used, xs, wg, wu, wd)


def _finish_kernel(rt_ref, ys_hbm, dest_ref, h_ref, p_ref, gp_ref, gf_ref, wg_ref, wp_ref, o_ref, y_buf, y_sem):
    tok = h_ref.shape[0]
    i = pl.program_id(0)
    slot = i % 2
    start_gather, wait_gather = _chunk_gather(rt_ref, ys_hbm, y_buf, y_sem, MOE_CPT)

    @pl.when(i == 0)
    def _():
        start_gather(0, 0)

    @pl.when(i + 1 < pl.num_programs(0))
    def _():
        start_gather(i + 1, (i + 1) % 2)

    wait_gather(i, slot)
    dcol = dest_ref[0].astype(F32).T
    lane = lax.broadcasted_iota(jnp.int32, (tok, MOE_ROWS), 1).astype(F32)
    unsort = ((lane == dcol[:, 0:1]) | (lane == dcol[:, 1:2])).astype(BF16)
    h = h_ref[...] + jnp.dot(unsort, y_buf[slot], preferred_element_type=F32)
    t = _rms(h, gp_ref[...]).astype(BF16)
    gate = jax.nn.sigmoid(jnp.dot(t, wg_ref[...], preferred_element_type=F32))
    emb = jnp.dot(p_ref[...].astype(BF16), wp_ref[...], preferred_element_type=F32)
    o_ref[...] = _rms(h + emb * gate, gf_ref[...])


def _finish(ys, return_tbl, dest, h1, p2, g_ple, g_final, wg, wp):
    T, D = h1.shape
    P = p2.shape[1]
    nt = T // MOE_TOK

    vec = pl.BlockSpec((1, D), lambda i, rt: (0, 0))
    grid_spec = pltpu.PrefetchScalarGridSpec(
        num_scalar_prefetch=1,
        grid=(nt,),
        in_specs=[
            pl.BlockSpec(memory_space=pl.ANY),
            pl.BlockSpec((1, SUBLANES, MOE_TOK), lambda i, rt: (i, 0, 0)),
            pl.BlockSpec((MOE_TOK, D), lambda i, rt: (i, 0)),
            pl.BlockSpec((MOE_TOK, P), lambda i, rt: (i, 0)),
            vec, vec,
            pl.BlockSpec((D, D), lambda i, rt: (0, 0), pipeline_mode=pl.Buffered(1)),
            pl.BlockSpec((P, D), lambda i, rt: (0, 0), pipeline_mode=pl.Buffered(1))],
        out_specs=pl.BlockSpec((MOE_TOK, D), lambda i, rt: (i, 0)),
        scratch_shapes=[pltpu.VMEM((2, MOE_ROWS, D), BF16), pltpu.SemaphoreType.DMA((2,))],
    )
    return pl.pallas_call(
        _finish_kernel,
        grid_spec=grid_spec,
        out_shape=jax.ShapeDtypeStruct((T, D), F32),
        compiler_params=_cparams("arbitrary"),
        name="finish",
    )(return_tbl, ys, dest, h1, p2, g_ple, g_final, wg, wp)


def kernel(x, p, g_mix, w_in, rel_bias, w_attn_br, conv_w, conv_b, ln_g, ln_b, w_conv_br, w_o, g_ffn,
           w_router_g, b_router_g, w_router_e, b_router_e, w_e_gate, w_e_up, w_e_down, g_ple, w_ple_gate,
           w_ple_proj, g_final):
    B, S, D = x.shape
    T = B * S
    depth = w_in.shape[0]
    width = N_HEADS * HEAD_DIM
    assert D == width and conv_w.shape[2] == D, "column-block indexing assumes all branch widths equal d_model"
    assert S % MOBA_BLOCK == 0
    assert depth == 1, "the final RMSNorm is fused into the last stage of a single layer"
    a_blk = 3 * width // D
    gate_blk = a_blk + 2

    h = x.reshape(T, D)
    for i in range(depth):
        proj = _in_proj(h, g_mix[i][None], w_in[i].astype(BF16))
        proj3 = proj.reshape(B, S, -1)
        attn = _moba(proj3, rel_bias)
        conv = _conv(proj3, conv_w[i], conv_b[i][None], ln_g[i][None], ln_b[i][None], a_blk)
        h = _mix(h, attn.reshape(T, D), conv.reshape(T, D), proj,
                 w_attn_br[i].astype(BF16), w_conv_br[i].astype(BF16), w_o[i].astype(BF16), gate_blk)
        xs, dest, cnt = _moe_sort(h, g_ffn[i][None], w_router_e[i], b_router_e[i], w_router_g[i], b_router_g[i])
        n_tok_tiles = T // MOE_TOK
        worst_chunks = (MOE_TOPK * T + n_tok_tiles * N_EXPERTS * (MOE_CHUNK - 1)) // MOE_CHUNK
        n_mm_tiles = -(-worst_chunks // MOE_CPM) + N_EXPERTS
        gather_tbl, return_tbl, tile_expert, n_used = _moe_tables(cnt[:, :, 0], n_mm_tiles)
        ys = _moe_experts(xs, gather_tbl, tile_expert, n_used,
                          w_e_gate[i], w_e_up[i], w_e_down[i], n_mm_tiles)
        out = _finish(ys, return_tbl, dest, h, p[i].reshape(T, -1), g_ple[i][None], g_final[None],
                      w_ple_gate[i].astype(BF16), w_ple_proj[i].astype(BF16))
    return out.reshape(B, S, D)
```

```python
import functools
import math

import numpy as np
import jax
import jax.numpy as jnp
from jax import lax
from jax.experimental import pallas as pl
from jax.experimental.pallas import tpu as pltpu

F32 = jnp.float32
BF16 = jnp.bfloat16

N_HEADS = 16
HEAD_DIM = 128
MOBA_BLOCK = 256
MOBA_TOPK = 3
N_BUCKETS = 32
MAX_DISTANCE = 128
CONV_KERNEL = 31
N_GROUPS = 4
EXPERTS_PER_GROUP = 4
N_EXPERTS = N_GROUPS * EXPERTS_PER_GROUP
EPS = 1e-6
NEG = -1e30
LOG2E = 1.4426950408889634

V7X_VMEM_LIMIT_BYTES = 56 * 1024 * 1024
LANES = 128
SUBLANES = 8

_NT = (((1,), (1,)), ((), ()))


def _cparams(*sem):
    return pltpu.CompilerParams(dimension_semantics=sem, vmem_limit_bytes=V7X_VMEM_LIMIT_BYTES)


def _resident(shape):
    return pl.BlockSpec(shape, lambda *_: (0,) * len(shape), pipeline_mode=pl.Buffered(1))


def _sigmoid(x):
    return 0.5 * jnp.tanh(0.5 * x) + 0.5


def _rms(x, g):
    return x * lax.rsqrt(jnp.mean(x * x, axis=-1, keepdims=True) + EPS) * g


def _in_proj_kernel(x_ref, g_ref, w_ref, o_ref, xn_ref):
    @pl.when(pl.program_id(1) == 0)
    def _():
        xn_ref[...] = _rms(x_ref[...], g_ref[...]).astype(BF16)

    o_ref[...] = jnp.dot(xn_ref[...], w_ref[...], preferred_element_type=F32).astype(o_ref.dtype)


def _in_proj(x2, g, w, tm=1024, tn=2048):
    T, D = x2.shape
    N = w.shape[1]
    return pl.pallas_call(
        _in_proj_kernel,
        grid=(T // tm, N // tn),
        in_specs=[pl.BlockSpec((tm, D), lambda i, j: (i, 0)),
                  pl.BlockSpec((1, D), lambda i, j: (0, 0)),
                  pl.BlockSpec((D, tn), lambda i, j: (0, j))],
        out_specs=pl.BlockSpec((tm, tn), lambda i, j: (i, j)),
        out_shape=jax.ShapeDtypeStruct((T, N), BF16),
        scratch_shapes=[pltpu.VMEM((tm, D), BF16)],
        compiler_params=_cparams("parallel", "arbitrary"),
        name="in_proj",
    )(x2, g, w)


MOBA_PV_ROWS = HEAD_DIM + 16
MOBA_Q_LANES = MOBA_BLOCK
MOBA_HEADS_PER_STEP = 4


def _rel_buckets_t():
    k = np.arange(MOBA_BLOCK)[:, None]
    q = np.arange(MOBA_BLOCK)[None, :]
    dist = np.stack([q - k, MOBA_BLOCK + q - k]).astype(np.int32)
    n = np.maximum(dist, 0)
    max_exact = N_BUCKETS // 2
    nf = np.maximum(n, 1).astype(np.float32)
    large = max_exact + (np.log(nf / np.float32(max_exact)) / np.float32(math.log(MAX_DISTANCE / max_exact))
                         * np.float32(N_BUCKETS - max_exact)).astype(np.int32)
    large = np.minimum(large, N_BUCKETS - 1)
    return np.where(n < max_exact, n, large).astype(np.int32)


def _moba_kernel(tbl_ref, bkt_ref, q_ref, k_ref, v_ref, o_ref,
                 kmean_ref, vt_ref, bias_ref, qt_ref, sel_ref, s_ref, p_ref, m_ref, a_ref, acc_ref,
                 *, n_blocks):
    hg = pl.program_id(0)
    b = pl.program_id(1)
    BS = MOBA_BLOCK
    HPS = MOBA_HEADS_PER_STEP
    lanes = [slice(hh * HEAD_DIM, (hh + 1) * HEAD_DIM) for hh in range(HPS)]

    @pl.when(b == 0)
    def _build_bias():
        krow = lax.broadcasted_iota(jnp.int32, (BS, BS), 0)
        qcol = lax.broadcasted_iota(jnp.int32, (BS, BS), 1)
        for hh in range(HPS):
            bias_ref[hh] = jnp.zeros((2, BS, BS), F32)

            def fill(n, carry):
                bias_ref[hh] = jnp.where(bkt_ref[...] == n, tbl_ref[hg * HPS + hh, n] * LOG2E, bias_ref[hh])
                return carry

            lax.fori_loop(0, N_BUCKETS, fill, 0)
            bias_ref[hh, 0] = jnp.where(qcol >= krow, bias_ref[hh, 0], NEG)

    ones_row = (lax.broadcasted_iota(jnp.int32, (MOBA_PV_ROWS - HEAD_DIM, BS), 0) == 0).astype(BF16)

    def prep(n, carry):
        rows = pl.ds(pl.multiple_of(n * BS, BS), BS)
        for hh in range(HPS):
            kmean_ref[hh, pl.ds(n, 1), :] = jnp.mean(k_ref[0, rows, lanes[hh]].astype(F32), axis=0, keepdims=True)
            vt_ref[hh, n, :HEAD_DIM] = v_ref[0, rows, lanes[hh]].astype(F32).T.astype(BF16)
            vt_ref[hh, n, HEAD_DIM:] = ones_row
        return carry

    lax.fori_loop(0, n_blocks, prep, 0)

    q_block = functools.partial(_moba_q_block, hg=hg, tbl_ref=tbl_ref, q_ref=q_ref, k_ref=k_ref, o_ref=o_ref,
                                kmean_ref=kmean_ref, vt_ref=vt_ref, bias_ref=bias_ref, qt_ref=qt_ref,
                                sel_ref=sel_ref, s_ref=s_ref, p_ref=p_ref, m_ref=m_ref, a_ref=a_ref,
                                acc_ref=acc_ref, n_blocks=n_blocks)

    q_block(jnp.int32(0), finish_previous=False)

    def next_block(qi, carry):
        q_block(qi, finish_previous=True)
        return carry

    lax.fori_loop(1, n_blocks, next_block, 0)
    _moba_close(jnp.int32(n_blocks - 1), vt_ref=vt_ref, p_ref=p_ref, a_ref=a_ref, acc_ref=acc_ref, o_ref=o_ref)


def _moba_close(qi, *, vt_ref, p_ref, a_ref, acc_ref, o_ref):
    lanes = [slice(hh * HEAD_DIM, (hh + 1) * HEAD_DIM) for hh in range(MOBA_HEADS_PER_STEP)]
    q_rows = pl.ds(pl.multiple_of(qi * MOBA_BLOCK, MOBA_BLOCK), MOBA_BLOCK)
    for hh in range(MOBA_HEADS_PER_STEP):
        acc = a_ref[hh] * acc_ref[hh] + jnp.dot(vt_ref[hh, 0], p_ref[hh], preferred_element_type=F32)
        denom = acc[HEAD_DIM:HEAD_DIM + 1, :]
        out_t = acc[:HEAD_DIM, :] * (1.0 / denom)
        o_ref[0, q_rows, lanes[hh]] = out_t.T.astype(o_ref.dtype)


def _moba_q_block(qi, *, finish_previous, hg, tbl_ref, q_ref, k_ref, o_ref, kmean_ref, vt_ref, bias_ref, qt_ref,
                  sel_ref, s_ref, p_ref, m_ref, a_ref, acc_ref, n_blocks):
    BS = MOBA_BLOCK
    HPS = MOBA_HEADS_PER_STEP
    scale = HEAD_DIM ** -0.5
    lanes = [slice(hh * HEAD_DIM, (hh + 1) * HEAD_DIM) for hh in range(HPS)]
    q_rows = pl.ds(pl.multiple_of(qi * BS, BS), BS)

    def qk_stage(hh, t):
        j0 = pl.multiple_of(jnp.maximum(qi - t, 0) * BS, BS)
        s_ref[hh] = jnp.dot(k_ref[0, pl.ds(j0, BS), lanes[hh]], qt_ref[hh], preferred_element_type=F32)

    def softmax_stage(hh, t, bias_of=None, far_bias=None):
        for c in range(BS // MOBA_Q_LANES):
            qs = slice(c * MOBA_Q_LANES, (c + 1) * MOBA_Q_LANES)
            picks = jnp.where(t <= qi, sel_ref[hh, :, qs], 0)
            chosen = (lax.shift_right_logical(picks, jnp.maximum(qi - t, 0)) & 1) == 1
            c2 = scale * LOG2E
            if bias_of is not None:
                st = s_ref[hh, :, qs] * c2 + bias_of(qs)
                shift = 0.0
                top = jnp.max(st, axis=0, keepdims=True)
            else:
                st = None
                shift = far_bias
                top = jnp.max(s_ref[hh, :, qs], axis=0, keepdims=True) * c2 + shift
            m_prev = m_ref[hh, :, qs]
            m_new = jnp.where(chosen, jnp.maximum(m_prev, top), m_prev)
            a_ref[hh, :, qs] = jnp.exp2(m_prev - m_new)
            m_ref[hh, :, qs] = m_new
            cut = jnp.where(chosen, m_new - shift, -NEG)
            if st is None:
                st = s_ref[hh, :, qs] * c2
            p_ref[hh, :, qs] = jnp.exp2(st - cut).astype(BF16)

    def pv_stage(hh, t):
        acc_ref[hh] = a_ref[hh] * acc_ref[hh] + jnp.dot(vt_ref[hh, qi - t], p_ref[hh],
                                                        preferred_element_type=F32)

    if finish_previous:
        _moba_close(qi - 1, vt_ref=vt_ref, p_ref=p_ref, a_ref=a_ref, acc_ref=acc_ref, o_ref=o_ref)

    for hh in range(HPS):
        qt = q_ref[0, q_rows, lanes[hh]].astype(F32).T.astype(BF16)
        qt_ref[hh] = qt
        gate = jnp.dot(kmean_ref[hh].astype(BF16), qt, preferred_element_type=F32)
        blk = lax.broadcasted_iota(jnp.int32, gate.shape, 0)
        blkf = blk.astype(F32)
        g = jnp.where(blk < qi, gate, NEG)
        bits = jnp.full((1, BS), lax.shift_left(jnp.int32(1), qi), jnp.int32)
        for _ in range(MOBA_TOPK):
            mx = jnp.max(g, axis=0, keepdims=True)
            ix = jnp.min(jnp.where(g == mx, blkf, float(n_blocks)), axis=0, keepdims=True).astype(jnp.int32)
            bits = bits | jnp.where(ix < qi, lax.shift_left(jnp.int32(1), ix), 0)
            g = jnp.where(blk == ix, -jnp.inf, g)
        sel_ref[hh] = bits
        m_ref[hh] = jnp.full((1, BS), NEG, F32)
        acc_ref[hh] = jnp.zeros((MOBA_PV_ROWS, BS), F32)
        qk_stage(hh, 0)

    for hh in range(HPS):
        softmax_stage(hh, 0, lambda qs, hh=hh: bias_ref[hh, 0, :, qs])
    if not finish_previous:
        return

    for hh in range(HPS):
        qk_stage(hh, 1)
    for hh in range(HPS):
        pv_stage(hh, 0)
    for hh in range(HPS):
        softmax_stage(hh, 1, lambda qs, hh=hh: bias_ref[hh, 1, :, qs])
    for hh in range(HPS):
        qk_stage(hh, 2)

    def sweep(t):
        for hh in range(HPS):
            pv_stage(hh, t - 1)
        for hh in range(HPS):
            softmax_stage(hh, t, far_bias=tbl_ref[hg * HPS + hh, N_BUCKETS - 1] * LOG2E)
        for hh in range(HPS):
            qk_stage(hh, t + 1)

    def sweep_pair(i, carry):
        sweep(2 + 2 * i)
        sweep(3 + 2 * i)
        return carry

    lax.fori_loop(0, qi // 2, sweep_pair, 0)


def _moba(proj3, rel_bias):
    B, S, _ = proj3.shape
    BS = MOBA_BLOCK
    HPS = MOBA_HEADS_PER_STEP
    W = HPS * HEAD_DIM
    nb = S // BS
    n_groups = N_HEADS // HPS
    bkt = jnp.asarray(_rel_buckets_t())
    tbl = rel_bias.astype(F32).T
    grid_spec = pltpu.PrefetchScalarGridSpec(
        num_scalar_prefetch=1,
        grid=(n_groups, B),
        in_specs=[pl.BlockSpec((2, BS, BS), lambda h, b, t: (0, 0, 0)),
                  pl.BlockSpec((1, S, W), lambda h, b, t: (b, 0, h)),
                  pl.BlockSpec((1, S, W), lambda h, b, t: (b, 0, n_groups + h)),
                  pl.BlockSpec((1, S, W), lambda h, b, t: (b, 0, 2 * n_groups + h))],
        out_specs=pl.BlockSpec((1, S, W), lambda h, b, t: (b, 0, h)),
        scratch_shapes=[pltpu.VMEM((HPS, nb, HEAD_DIM), F32),
                        pltpu.VMEM((HPS, nb, MOBA_PV_ROWS, BS), BF16),
                        pltpu.VMEM((HPS, 2, BS, BS), F32),
                        pltpu.VMEM((HPS, HEAD_DIM, BS), BF16),
                        pltpu.VMEM((HPS, 1, BS), jnp.int32),
                        pltpu.VMEM((HPS, BS, BS), F32),
                        pltpu.VMEM((HPS, BS, BS), BF16),
                        pltpu.VMEM((HPS, 1, BS), F32),
                        pltpu.VMEM((HPS, 1, BS), F32),
                        pltpu.VMEM((HPS, MOBA_PV_ROWS, BS), F32)],
    )
    return pl.pallas_call(
        functools.partial(_moba_kernel, n_blocks=nb),
        grid_spec=grid_spec,
        out_shape=jax.ShapeDtypeStruct((B, S, N_HEADS * HEAD_DIM), BF16),
        compiler_params=_cparams("arbitrary", "arbitrary"),
        name="moba",
    )(tbl, bkt, proj3, proj3, proj3)


CONV_HALO = 32
CONV_ROWS = 64
CONV_COLS = 256


def _conv_kernel(a_ref, g_ref, ah_ref, gh_ref, w_ref, cb_ref, lng_ref, lnb_ref, o_ref, u_ref, sh_ref, c_ref,
                 *, ts):
    i = pl.program_id(1)
    D = u_ref.shape[1]
    u_ref[CONV_HALO:, :] = a_ref[0].astype(F32) * _sigmoid(g_ref[0].astype(F32))
    uh = ah_ref[0].astype(F32) * _sigmoid(gh_ref[0].astype(F32))
    u_ref[:CONV_HALO, :] = jnp.where(i > 0, uh, 0.0)

    first = CONV_HALO - (CONV_KERNEL - 1)
    sh_rows = sh_ref.shape[1]

    def col_body(c, carry):
        cols = pl.ds(pl.multiple_of(c * CONV_COLS, CONV_COLS), CONV_COLS)
        for s in range(1, SUBLANES):
            sh_ref[s] = u_ref[s:s + sh_rows, cols]
        for r in range(ts // CONV_ROWS):
            acc = jnp.zeros((CONV_ROWS, CONV_COLS), F32)
            for k in range(CONV_KERNEL):
                s = (first + k) % SUBLANES
                r0 = r * CONV_ROWS + first + k - s
                tap = u_ref[r0:r0 + CONV_ROWS, cols] if s == 0 else sh_ref[s, r0:r0 + CONV_ROWS, :]
                acc = acc + jnp.tile(w_ref[k, :, cols], (CONV_ROWS // SUBLANES, 1)) * tap
            c_ref[r * CONV_ROWS:(r + 1) * CONV_ROWS, cols] = acc + cb_ref[:, cols]
        return carry

    lax.fori_loop(0, D // CONV_COLS, col_body, 0)

    y = c_ref[...]
    mu = jnp.mean(y, axis=-1, keepdims=True)
    yc = y - mu
    var = jnp.mean(yc * yc, axis=-1, keepdims=True)
    z = yc * lax.rsqrt(var + EPS) * lng_ref[...] + lnb_ref[...]
    o_ref[0] = (z * _sigmoid(z)).astype(o_ref.dtype)


def _conv(proj3, conv_w, conv_b, ln_g, ln_b, a_blk, ts=512):
    B, S, _ = proj3.shape
    D = conv_w.shape[1]
    hpb = ts // CONV_HALO
    w_rows = jnp.broadcast_to(conv_w[:, None, :], (CONV_KERNEL, SUBLANES, D))

    def halo(col):
        return pl.BlockSpec((1, CONV_HALO, D), lambda b, i: (b, jnp.maximum(i * hpb - 1, 0), col))

    vec = pl.BlockSpec((1, D), lambda b, i: (0, 0))
    return pl.pallas_call(
        functools.partial(_conv_kernel, ts=ts),
        grid=(B, S // ts),
        in_specs=[pl.BlockSpec((1, ts, D), lambda b, i: (b, i, a_blk)),
                  pl.BlockSpec((1, ts, D), lambda b, i: (b, i, a_blk + 1)),
                  halo(a_blk), halo(a_blk + 1),
                  pl.BlockSpec((CONV_KERNEL, SUBLANES, D), lambda b, i: (0, 0, 0)),
                  vec, vec, vec],
        out_specs=pl.BlockSpec((1, ts, D), lambda b, i: (b, i, 0)),
        out_shape=jax.ShapeDtypeStruct((B, S, D), BF16),
        scratch_shapes=[pltpu.VMEM((CONV_HALO + ts, D), F32),
                        pltpu.VMEM((SUBLANES, CONV_HALO + ts - SUBLANES, CONV_COLS), F32),
                        pltpu.VMEM((ts, D), F32)],
        compiler_params=_cparams("parallel", "arbitrary"),
        name="conv",
    )(proj3, proj3, proj3, proj3, w_rows, conv_b, ln_g, ln_b)


def _mix_kernel(x_ref, at_ref, cv_ref, ga_ref, gc_ref, wa_ref, wc_ref, wo_ref, o_ref):
    ya = jnp.dot(at_ref[...], wa_ref[...], preferred_element_type=F32)
    yc = jnp.dot(cv_ref[...], wc_ref[...], preferred_element_type=F32)
    mixed = (jax.nn.sigmoid(ga_ref[...].astype(F32)) * ya + jax.nn.sigmoid(gc_ref[...].astype(F32)) * yc)
    o_ref[...] = x_ref[...] + jnp.dot(mixed.astype(BF16), wo_ref[...], preferred_element_type=F32)


def _mix(x2, attn2, conv2, proj2, wa, wc, wo, gate_blk, tm=256):
    T, D = x2.shape
    row = lambda col: pl.BlockSpec((tm, D), lambda i: (i, col))
    return pl.pallas_call(
        _mix_kernel,
        grid=(T // tm,),
        in_specs=[row(0), row(0), row(0), row(gate_blk), row(gate_blk + 1),
                  _resident((D, D)), _resident((D, D)), _resident((D, D))],
        out_specs=row(0),
        out_shape=jax.ShapeDtypeStruct((T, D), F32),
        compiler_params=_cparams("parallel"),
        name="mix",
    )(x2, attn2, conv2, proj2, proj2, wa, wc, wo)


MOE_TOK = 512
MOE_CHUNK = 16
MOE_TM = 512
MOE_TOPK = 2
_MOE_WORST_ROWS = MOE_TOPK * MOE_TOK + N_EXPERTS * (MOE_CHUNK - 1)
MOE_ROWS = -(-_MOE_WORST_ROWS // LANES) * LANES
MOE_CPT = MOE_ROWS // MOE_CHUNK
MOE_CPM = MOE_TM // MOE_CHUNK
MOE_EXTRA = LANES


def _route_t(lt):
    row = lax.broadcasted_iota(jnp.int32, lt.shape, 0)
    rowf = row.astype(F32)
    big = float(lt.shape[0])
    is_grp = (row >= N_EXPERTS) & (row < N_EXPERTS + N_GROUPS)
    cl = jnp.where(is_grp, lt, -jnp.inf)
    mg = jnp.max(cl, axis=0, keepdims=True)
    g_star = jnp.min(jnp.where(cl == mg, rowf, big), axis=0, keepdims=True).astype(jnp.int32) - N_EXPERTS
    p_g = 1.0 / jnp.sum(jnp.where(is_grp, jnp.exp(cl - mg), 0.0), axis=0, keepdims=True)
    in_grp = (row < N_EXPERTS) & ((row // EXPERTS_PER_GROUP) == g_star)
    f1 = jnp.where(in_grp, lt, -jnp.inf)
    v1 = jnp.max(f1, axis=0, keepdims=True)
    i1 = jnp.min(jnp.where(f1 == v1, rowf, big), axis=0, keepdims=True).astype(jnp.int32)
    f2 = jnp.where(row == i1, -jnp.inf, f1)
    v2 = jnp.max(f2, axis=0, keepdims=True)
    i2 = jnp.min(jnp.where(f2 == v2, rowf, big), axis=0, keepdims=True).astype(jnp.int32)
    e2 = jnp.exp(v2 - v1)
    return i1, i2, p_g / (1.0 + e2), p_g * e2 / (1.0 + e2)


def _split_bf16(x):
    hi = x.astype(BF16)
    return hi, (x - hi.astype(F32)).astype(BF16)


def _moe_sort_kernel(h_ref, g_ref, wh_ref, wl_ref, br_ref, xs_ref, dest_ref, cnt_ref):
    tok = h_ref.shape[0]
    D = h_ref.shape[1]
    hn = _rms(h_ref[...], g_ref[...])
    hn_hi, hn_lo = _split_bf16(hn)
    nt_dot = lambda a, b: lax.dot_general(a, b, _NT, preferred_element_type=F32)
    w_hi = wh_ref[...]
    both = nt_dot(jnp.concatenate([w_hi, wl_ref[...]], axis=0), hn_hi)
    lt = both[:MOE_ROUTE_ROWS] + both[MOE_ROUTE_ROWS:] + nt_dot(w_hi, hn_lo) + br_ref[...]
    i1, i2, c1, c2 = _route_t(lt)

    erow = lax.broadcasted_iota(jnp.int32, (N_EXPERTS, tok), 0)
    oh1 = erow == i1
    oh2 = erow == i2
    oh = (oh1 | oh2).astype(F32)
    earlier = (lax.broadcasted_iota(jnp.int32, (tok, tok), 0) < lax.broadcasted_iota(jnp.int32, (tok, tok), 1))
    cum = jnp.dot(oh.astype(BF16), earlier.astype(BF16), preferred_element_type=F32)
    cnt = jnp.sum(oh, axis=1, keepdims=True)
    n_chunks = jnp.floor((cnt + (MOE_CHUNK - 1)) * (1.0 / MOE_CHUNK))
    ncb = jnp.broadcast_to(n_chunks, (N_EXPERTS, LANES))
    erow_l = lax.broadcasted_iota(jnp.int32, (N_EXPERTS, LANES), 0)
    first_chunk = jnp.zeros((N_EXPERTS, LANES), F32)
    for e in range(N_EXPERTS - 1):
        first_chunk = first_chunk + jnp.where(erow_l > e, ncb[e:e + 1, :], 0.0)
    base = jnp.tile(first_chunk, (1, tok // LANES)) * MOE_CHUNK + cum
    d1 = jnp.sum(jnp.where(oh1, base, 0.0), axis=0, keepdims=True).astype(jnp.int32)
    d2 = jnp.sum(jnp.where(oh2, base, 0.0), axis=0, keepdims=True).astype(jnp.int32)

    srow = lax.broadcasted_iota(jnp.int32, (MOE_ROWS, tok), 0)
    p1 = srow == d1
    p2 = srow == d2
    perm1 = p1.astype(BF16)
    perm2 = p2.astype(BF16)
    xs_ref[:, :D] = jnp.dot(perm1 + perm2, hn_hi, preferred_element_type=F32).astype(BF16)
    xrow = lax.broadcasted_iota(jnp.int32, (MOE_EXTRA, tok), 0)

    def weight_rows(c):
        hi = c.astype(BF16).astype(F32)
        return jnp.where(xrow == 0, hi, jnp.where(xrow == 1, c - hi, 0.0)).astype(BF16)

    xs_ref[:, D:] = (nt_dot(perm1, weight_rows(c1)) + nt_dot(perm2, weight_rows(c2))).astype(BF16)

    r8 = lax.broadcasted_iota(jnp.int32, (SUBLANES, tok), 0)
    dest_ref[0] = jnp.where(r8 == 0, d1, jnp.where(r8 == 1, d2, 0))
    cnt_ref[0] = jnp.broadcast_to(cnt, (N_EXPERTS, LANES)).astype(jnp.int32)


MOE_ROUTE_ROWS = 32


def _moe_sort(h1, g, w_router_e, b_router_e, w_router_g, b_router_g):
    T, D = h1.shape
    nt = T // MOE_TOK
    pad = MOE_ROUTE_ROWS - N_EXPERTS - N_GROUPS
    wr = jnp.concatenate([w_router_e.T, w_router_g.T, jnp.zeros((pad, D), F32)], axis=0)
    wr_hi = wr.astype(BF16)
    wr_lo = (wr - wr_hi.astype(F32)).astype(BF16)
    br = jnp.concatenate([b_router_e, b_router_g, jnp.zeros((pad,), F32)])[:, None]
    return pl.pallas_call(
        _moe_sort_kernel,
        grid=(nt,),
        in_specs=[pl.BlockSpec((MOE_TOK, D), lambda i: (i, 0)),
                  pl.BlockSpec((1, D), lambda i: (0, 0)),
                  pl.BlockSpec((MOE_ROUTE_ROWS, D), lambda i: (0, 0)),
                  pl.BlockSpec((MOE_ROUTE_ROWS, D), lambda i: (0, 0)),
                  pl.BlockSpec((MOE_ROUTE_ROWS, 1), lambda i: (0, 0))],
        out_specs=[pl.BlockSpec((MOE_ROWS, D + MOE_EXTRA), lambda i: (i, 0)),
                   pl.BlockSpec((1, SUBLANES, MOE_TOK), lambda i: (i, 0, 0)),
                   pl.BlockSpec((1, N_EXPERTS, LANES), lambda i: (i, 0, 0))],
        out_shape=[jax.ShapeDtypeStruct((nt * MOE_ROWS, D + MOE_EXTRA), BF16),
                   jax.ShapeDtypeStruct((nt, SUBLANES, MOE_TOK), jnp.int32),
                   jax.ShapeDtypeStruct((nt, N_EXPERTS, LANES), jnp.int32)],
        compiler_params=_cparams("parallel"),
        name="moe_sort",
    )(h1, g, wr_hi, wr_lo, br)


def _moe_tables(cnt, n_mm_tiles):
    nt = cnt.shape[0]
    pc = (cnt + (MOE_CHUNK - 1)) // MOE_CHUNK
    run_end = jnp.cumsum(pc, axis=1)
    run_first = run_end - pc
    per_expert = jnp.sum(pc, axis=0)
    mm_tiles = (per_expert + (MOE_CPM - 1)) // MOE_CPM
    mm_end = jnp.cumsum(mm_tiles)
    slot_base = (mm_end - mm_tiles)[None, :] * MOE_CPM + (jnp.cumsum(pc, axis=0) - pc)
    k = jnp.arange(MOE_CPT, dtype=jnp.int32)[None, :, None]
    expert_of = jnp.sum((run_end[:, None, :] <= k).astype(jnp.int32), axis=2)
    valid = expert_of < N_EXPERTS
    pick = expert_of[:, :, None] == jnp.arange(N_EXPERTS, dtype=jnp.int32)
    take = lambda a: jnp.sum(jnp.where(pick, a[:, None, :], 0), axis=2)
    slot = take(slot_base) + (k[:, :, 0] - take(run_first))
    src = (jnp.arange(nt, dtype=jnp.int32)[:, None] * MOE_CPT + k[:, :, 0]).reshape(-1)
    slot = jnp.where(valid, slot, n_mm_tiles * MOE_CPM).reshape(-1)
    gather_tbl = jnp.zeros((n_mm_tiles * MOE_CPM,), jnp.int32).at[slot].set(src, mode="drop")
    return_tbl = jnp.where(valid.reshape(-1), slot, 0).astype(jnp.int32)
    g = jnp.arange(n_mm_tiles, dtype=jnp.int32)
    n_used = mm_end[-1].astype(jnp.int32)
    tile_expert = jnp.sum((mm_end[None, :] <= jnp.minimum(g, n_used - 1)[:, None]).astype(jnp.int32), axis=1)
    return gather_tbl, return_tbl, tile_expert.astype(jnp.int32), n_used.reshape(1)


def _chunk_gather(tbl_ref, src_hbm, buf, sem, chunks_per_tile):
    def chunk_copy(tile, c, slot):
        src = pl.multiple_of(tbl_ref[tile * chunks_per_tile + c] * MOE_CHUNK, MOE_CHUNK)
        dst = pl.multiple_of(c * MOE_CHUNK, MOE_CHUNK)
        return pltpu.make_async_copy(src_hbm.at[pl.ds(src, MOE_CHUNK), :],
                                     buf.at[slot, pl.ds(dst, MOE_CHUNK), :], sem.at[slot])

    def for_each_chunk(tile, slot, act):
        def body(c, carry):
            act(chunk_copy(tile, c, slot))
            return carry
        lax.fori_loop(0, chunks_per_tile, body, 0)

    def start(tile, slot):
        for_each_chunk(tile, slot, lambda copy: copy.start())

    def wait(tile, slot):
        for_each_chunk(tile, slot, lambda copy: copy.wait())

    return start, wait


def _moe_expert_kernel(gt_ref, te_ref, nu_ref, xs_hbm, wg_ref, wu_ref, wd_ref, o_ref,
                       wg_bf, wu_bf, wd_bf, x_buf, x_sem):
    D = o_ref.shape[1]
    g = pl.program_id(0)
    n_used = nu_ref[0]
    used = g < n_used

    start_gather, wait_gather = _chunk_gather(gt_ref, xs_hbm, x_buf, x_sem, MOE_CPM)

    @pl.when((g == 0) & (n_used > 0))
    def _():
        start_gather(0, 0)

    @pl.when(g + 1 < n_used)
    def _():
        start_gather(g + 1, (g + 1) % 2)

    @pl.when(used & ((g == 0) | (te_ref[g] != te_ref[jnp.maximum(g - 1, 0)])))
    def _():
        wg_bf[...] = wg_ref[0].astype(BF16)
        wu_bf[...] = wu_ref[0].astype(BF16)
        wd_bf[...] = wd_ref[0].astype(BF16)

    @pl.when(used)
    def _():
        slot = g % 2
        wait_gather(g, slot)
        xs = x_buf[slot]
        x = xs[:, :D]
        extra = xs[:, D:].astype(F32)
        c = extra[:, 0:1] + extra[:, 1:2]
        gate = jnp.dot(x, wg_bf[...], preferred_element_type=F32)
        up = jnp.dot(x, wu_bf[...], preferred_element_type=F32)
        hh = (gate * jax.nn.sigmoid(gate)) * up * c
        o_ref[...] = jnp.dot(hh.astype(BF16), wd_bf[...], preferred_element_type=F32).astype(o_ref.dtype)

    @pl.when(jnp.logical_not(used))
    def _():
        o_ref[...] = jnp.zeros(o_ref.shape, o_ref.dtype)


def _moe_experts(xs, gather_tbl, tile_expert, n_used, wg, wu, wd, n_mm_tiles):
    W = xs.shape[1]
    E, D, FF = wg.shape

    grid_spec = pltpu.PrefetchScalarGridSpec(
        num_scalar_prefetch=3,
        grid=(n_mm_tiles,),
        in_specs=[pl.BlockSpec(memory_space=pl.ANY),
                  pl.BlockSpec((1, D, FF), lambda g, gt, te, nu: (te[g], 0, 0)),
                  pl.BlockSpec((1, D, FF), lambda g, gt, te, nu: (te[g], 0, 0)),
                  pl.BlockSpec((1, FF, D), lambda g, gt, te, nu: (te[g], 0, 0))],
        out_specs=pl.BlockSpec((MOE_TM, D), lambda g, gt, te, nu: (g, 0)),
        scratch_shapes=[pltpu.VMEM((D, FF), BF16), pltpu.VMEM((D, FF), BF16), pltpu.VMEM((FF, D), BF16),
                        pltpu.VMEM((2, MOE_TM, W), BF16), pltpu.SemaphoreType.DMA((2,))],
    )
    return pl.pallas_call(
        _moe_expert_kernel,
        grid_spec=grid_spec,
        out_shape=jax.ShapeDtypeStruct((n_mm_tiles * MOE_TM, D), BF16),
        compiler_params=_cparams("arbitrary"),
        name="moe_experts",
    )(gather_tbl, tile_expert, n_used, xs, wg, wu, wd)


def _finish_kernel(rt_ref, ys_hbm, dest_ref, h_ref, p_ref, gp_ref, gf_ref, wg_ref, wp_ref, o_ref, y_buf, y_sem):
    tok = h_ref.shape[0]
    i = pl.program_id(0)
    slot = i % 2
    start_gather, wait_gather = _chunk_gather(rt_ref, ys_hbm, y_buf, y_sem, MOE_CPT)

    @pl.when(i == 0)
    def _():
        start_gather(0, 0)

    @pl.when(i + 1 < pl.num_programs(0))
    def _():
        start_gather(i + 1, (i + 1) % 2)

    wait_gather(i, slot)
    dcol = dest_ref[0].astype(F32).T
    lane = lax.broadcasted_iota(jnp.int32, (tok, MOE_ROWS), 1).astype(F32)
    unsort = ((lane == dcol[:, 0:1]) | (lane == dcol[:, 1:2])).astype(BF16)
    h = h_ref[...] + jnp.dot(unsort, y_buf[slot], preferred_element_type=F32)
    t = _rms(h, gp_ref[...]).astype(BF16)
    gate = jax.nn.sigmoid(jnp.dot(t, wg_ref[...], preferred_element_type=F32))
    emb = jnp.dot(p_ref[...].astype(BF16), wp_ref[...], preferred_element_type=F32)
    o_ref[...] = _rms(h + emb * gate, gf_ref[...])


def _finish(ys, return_tbl, dest, h1, p2, g_ple, g_final, wg, wp):
    T, D = h1.shape
    P = p2.shape[1]
    nt = T // MOE_TOK

    vec = pl.BlockSpec((1, D), lambda i, rt: (0, 0))
    grid_spec = pltpu.PrefetchScalarGridSpec(
        num_scalar_prefetch=1,
        grid=(nt,),
        in_specs=[
            pl.BlockSpec(memory_space=pl.ANY),
            pl.BlockSpec((1, SUBLANES, MOE_TOK), lambda i, rt: (i, 0, 0)),
            pl.BlockSpec((MOE_TOK, D), lambda i, rt: (i, 0)),
            pl.BlockSpec((MOE_TOK, P), lambda i, rt: (i, 0)),
            vec, vec,
            pl.BlockSpec((D, D), lambda i, rt: (0, 0), pipeline_mode=pl.Buffered(1)),
            pl.BlockSpec((P, D), lambda i, rt: (0, 0), pipeline_mode=pl.Buffered(1))],
        out_specs=pl.BlockSpec((MOE_TOK, D), lambda i, rt: (i, 0)),
        scratch_shapes=[pltpu.VMEM((2, MOE_ROWS, D), BF16), pltpu.SemaphoreType.DMA((2,))],
    )
    return pl.pallas_call(
        _finish_kernel,
        grid_spec=grid_spec,
        out_shape=jax.ShapeDtypeStruct((T, D), F32),
        compiler_params=_cparams("arbitrary"),
        name="finish",
    )(return_tbl, ys, dest, h1, p2, g_ple, g_final, wg, wp)


def kernel(x, p, g_mix, w_in, rel_bias, w_attn_br, conv_w, conv_b, ln_g, ln_b, w_conv_br, w_o, g_ffn,
           w_router_g, b_router_g, w_router_e, b_router_e, w_e_gate, w_e_up, w_e_down, g_ple, w_ple_gate,
           w_ple_proj, g_final):
    B, S, D = x.shape
    T = B * S
    depth = w_in.shape[0]
    width = N_HEADS * HEAD_DIM
    assert D == width and conv_w.shape[2] == D, "column-block indexing assumes all branch widths equal d_model"
    assert S % MOBA_BLOCK == 0
    assert depth == 1, "the final RMSNorm is fused into the last stage of a single layer"
    a_blk = 3 * width // D
    gate_blk = a_blk + 2

    h = x.reshape(T, D)
    for i in range(depth):
        proj = _in_proj(h, g_mix[i][None], w_in[i].astype(BF16))
        proj3 = proj.reshape(B, S, -1)
        attn = _moba(proj3, rel_bias)
        conv = _conv(proj3, conv_w[i], conv_b[i][None], ln_g[i][None], ln_b[i][None], a_blk)
        h = _mix(h, attn.reshape(T, D), conv.reshape(T, D), proj,
                 w_attn_br[i].astype(BF16), w_conv_br[i].astype(BF16), w_o[i].astype(BF16), gate_blk)
        xs, dest, cnt = _moe_sort(h, g_ffn[i][None], w_router_e[i], b_router_e[i], w_router_g[i], b_router_g[i])
        n_tok_tiles = T // MOE_TOK
        worst_chunks = (MOE_TOPK * T + n_tok_tiles * N_EXPERTS * (MOE_CHUNK - 1)) // MOE_CHUNK
        n_mm_tiles = -(-worst_chunks // MOE_CPM) + N_EXPERTS
        gather_tbl, return_tbl, tile_expert, n_used = _moe_tables(cnt[:, :, 0], n_mm_tiles)
        ys = _moe_experts(xs, gather_tbl, tile_expert, n_used,
                          w_e_gate[i], w_e_up[i], w_e_down[i], n_mm_tiles)
        out = _finish(ys, return_tbl, dest, h, p[i].reshape(T, -1), g_ple[i][None], g_final[None],
                      w_ple_gate[i].astype(BF16), w_ple_proj[i].astype(BF16))
    return out.reshape(B, S, D)
```

```python
import functools
import math

import numpy as np
import jax
import jax.numpy as jnp
from jax import lax
from jax.experimental import pallas as pl
from jax.experimental.pallas import tpu as pltpu

F32 = jnp.float32
BF16 = jnp.bfloat16

N_HEADS = 16
HEAD_DIM = 128
MOBA_BLOCK = 256
MOBA_TOPK = 3
N_BUCKETS = 32
MAX_DISTANCE = 128
CONV_KERNEL = 31
N_GROUPS = 4
EXPERTS_PER_GROUP = 4
N_EXPERTS = N_GROUPS * EXPERTS_PER_GROUP
EPS = 1e-6
NEG = -1e30
LOG2E = 1.4426950408889634

V7X_VMEM_LIMIT_BYTES = 56 * 1024 * 1024
LANES = 128
SUBLANES = 8

_NT = (((1,), (1,)), ((), ()))


def _cparams(*sem):
    return pltpu.CompilerParams(dimension_semantics=sem, vmem_limit_bytes=V7X_VMEM_LIMIT_BYTES)


def _resident(shape):
    return pl.BlockSpec(shape, lambda *_: (0,) * len(shape), pipeline_mode=pl.Buffered(1))


def _sigmoid(x):
    return 0.5 * jnp.tanh(0.5 * x) + 0.5


def _rms(x, g):
    return x * lax.rsqrt(jnp.mean(x * x, axis=-1, keepdims=True) + EPS) * g


def _in_proj_kernel(x_ref, g_ref, w_ref, o_ref, xn_ref):
    @pl.when(pl.program_id(1) == 0)
    def _():
        xn_ref[...] = _rms(x_ref[...], g_ref[...]).astype(BF16)

    o_ref[...] = jnp.dot(xn_ref[...], w_ref[...], preferred_element_type=F32).astype(o_ref.dtype)


def _in_proj_head_kernel(x_ref, g_ref, w_ref, o_ref, wbf_ref, xn_ref):
    @pl.when(pl.program_id(0) == 0)
    def _():
        xn_ref[...] = _rms(x_ref[...], g_ref[...]).astype(BF16)

    w = w_ref[...].astype(BF16)
    wbf_ref[...] = w
    o_ref[...] = jnp.dot(xn_ref[...], w, preferred_element_type=F32).astype(o_ref.dtype)


def _in_proj_tail_kernel(x_ref, g_ref, w_ref, head_ref, o_ref, xn_ref):
    first_tile = pl.program_id(0) == 0

    @pl.when(first_tile)
    def _():
        o_ref[...] = head_ref[...]

    @pl.when(jnp.logical_not(first_tile))
    def _():
        _in_proj_kernel(x_ref, g_ref, w_ref, o_ref, xn_ref)


def _in_proj(x2, g, w_f32, tm=1024, tn=2048, tn_head=1024):
    T, D = x2.shape
    N = w_f32.shape[1]
    vec = pl.BlockSpec((1, D), lambda *_: (0, 0))
    head, w = pl.pallas_call(
        _in_proj_head_kernel,
        grid=(N // tn_head,),
        in_specs=[pl.BlockSpec((tm, D), lambda j: (0, 0), pipeline_mode=pl.Buffered(1)), vec,
                  pl.BlockSpec((D, tn_head), lambda j: (0, j))],
        out_specs=[pl.BlockSpec((tm, tn_head), lambda j: (0, j)),
                   pl.BlockSpec((D, tn_head), lambda j: (0, j))],
        out_shape=[jax.ShapeDtypeStruct((tm, N), BF16), jax.ShapeDtypeStruct((D, N), BF16)],
        scratch_shapes=[pltpu.VMEM((tm, D), BF16)],
        compiler_params=_cparams("arbitrary"),
        name="in_proj_head",
    )(x2, g, w_f32)
    return pl.pallas_call(
        _in_proj_tail_kernel,
        grid=(T // tm, N // tn),
        in_specs=[pl.BlockSpec((tm, D), lambda i, j: (i, 0)), vec,
                  pl.BlockSpec((D, tn), lambda i, j: (0, j)),
                  pl.BlockSpec((tm, tn), lambda i, j: (0, jnp.where(i == 0, j, 0)))],
        out_specs=pl.BlockSpec((tm, tn), lambda i, j: (i, j)),
        out_shape=jax.ShapeDtypeStruct((T, N), BF16),
        scratch_shapes=[pltpu.VMEM((tm, D), BF16)],
        compiler_params=_cparams("arbitrary", "arbitrary"),
        name="in_proj",
    )(x2, g, w, head)


MOBA_PV_ROWS = HEAD_DIM + 16
MOBA_Q_LANES = MOBA_BLOCK
MOBA_HEADS_PER_STEP = 4


def _rel_buckets_t():
    k = np.arange(MOBA_BLOCK)[:, None]
    q = np.arange(MOBA_BLOCK)[None, :]
    dist = np.stack([q - k, MOBA_BLOCK + q - k]).astype(np.int32)
    n = np.maximum(dist, 0)
    max_exact = N_BUCKETS // 2
    nf = np.maximum(n, 1).astype(np.float32)
    large = max_exact + (np.log(nf / np.float32(max_exact)) / np.float32(math.log(MAX_DISTANCE / max_exact))
                         * np.float32(N_BUCKETS - max_exact)).astype(np.int32)
    large = np.minimum(large, N_BUCKETS - 1)
    return np.where(n < max_exact, n, large).astype(np.int32)


def _moba_kernel(tbl_ref, bkt_ref, q_ref, k_ref, v_ref, o_ref,
                 kmean_ref, vt_ref, bias_ref, qt_ref, sel_ref, s_ref, p_ref, m_ref, a_ref, acc_ref,
                 *, n_blocks):
    hg = pl.program_id(0)
    b = pl.program_id(1)
    BS = MOBA_BLOCK
    HPS = MOBA_HEADS_PER_STEP
    lanes = [slice(hh * HEAD_DIM, (hh + 1) * HEAD_DIM) for hh in range(HPS)]

    @pl.when(b == 0)
    def _build_bias():
        krow = lax.broadcasted_iota(jnp.int32, (BS, BS), 0)
        qcol = lax.broadcasted_iota(jnp.int32, (BS, BS), 1)
        for hh in range(HPS):
            bias_ref[hh] = jnp.zeros((2, BS, BS), F32)

            def fill(n, carry):
                bias_ref[hh] = jnp.where(bkt_ref[...] == n, tbl_ref[hg * HPS + hh, n] * LOG2E, bias_ref[hh])
                return carry

            lax.fori_loop(0, N_BUCKETS, fill, 0)
            bias_ref[hh, 0] = jnp.where(qcol >= krow, bias_ref[hh, 0], NEG)

    ones_row = (lax.broadcasted_iota(jnp.int32, (MOBA_PV_ROWS - HEAD_DIM, BS), 0) == 0).astype(BF16)

    def prep(n, carry):
        rows = pl.ds(pl.multiple_of(n * BS, BS), BS)
        for hh in range(HPS):
            kmean_ref[hh, pl.ds(n, 1), :] = jnp.mean(k_ref[0, rows, lanes[hh]].astype(F32), axis=0, keepdims=True)
            vt_ref[hh, n, :HEAD_DIM] = v_ref[0, rows, lanes[hh]].astype(F32).T.astype(BF16)
            vt_ref[hh, n, HEAD_DIM:] = ones_row
        return carry

    lax.fori_loop(0, n_blocks, prep, 0)

    q_block = functools.partial(_moba_q_block, hg=hg, tbl_ref=tbl_ref, q_ref=q_ref, k_ref=k_ref, o_ref=o_ref,
                                kmean_ref=kmean_ref, vt_ref=vt_ref, bias_ref=bias_ref, qt_ref=qt_ref,
                                sel_ref=sel_ref, s_ref=s_ref, p_ref=p_ref, m_ref=m_ref, a_ref=a_ref,
                                acc_ref=acc_ref, n_blocks=n_blocks)

    q_block(jnp.int32(0), finish_previous=False)

    def next_block(qi, carry):
        q_block(qi, finish_previous=True)
        return carry

    lax.fori_loop(1, n_blocks, next_block, 0)
    _moba_close(jnp.int32(n_blocks - 1), vt_ref=vt_ref, p_ref=p_ref, a_ref=a_ref, acc_ref=acc_ref, o_ref=o_ref)


def _moba_close(qi, *, vt_ref, p_ref, a_ref, acc_ref, o_ref):
    lanes = [slice(hh * HEAD_DIM, (hh + 1) * HEAD_DIM) for hh in range(MOBA_HEADS_PER_STEP)]
    q_rows = pl.ds(pl.multiple_of(qi * MOBA_BLOCK, MOBA_BLOCK), MOBA_BLOCK)
    for hh in range(MOBA_HEADS_PER_STEP):
        acc = a_ref[hh] * acc_ref[hh] + jnp.dot(vt_ref[hh, 0], p_ref[hh], preferred_element_type=F32)
        denom = acc[HEAD_DIM:HEAD_DIM + 1, :]
        out_t = acc[:HEAD_DIM, :] * (1.0 / denom)
        o_ref[0, q_rows, lanes[hh]] = out_t.T.astype(o_ref.dtype)


def _moba_q_block(qi, *, finish_previous, hg, tbl_ref, q_ref, k_ref, o_ref, kmean_ref, vt_ref, bias_ref, qt_ref,
                  sel_ref, s_ref, p_ref, m_ref, a_ref, acc_ref, n_blocks):
    BS = MOBA_BLOCK
    HPS = MOBA_HEADS_PER_STEP
    scale = HEAD_DIM ** -0.5
    lanes = [slice(hh * HEAD_DIM, (hh + 1) * HEAD_DIM) for hh in range(HPS)]
    q_rows = pl.ds(pl.multiple_of(qi * BS, BS), BS)

    def qk_stage(hh, t):
        j0 = pl.multiple_of(jnp.maximum(qi - t, 0) * BS, BS)
        s_ref[hh] = jnp.dot(k_ref[0, pl.ds(j0, BS), lanes[hh]], qt_ref[hh], preferred_element_type=F32)

    def softmax_stage(hh, t, bias_of=None, far_bias=None):
        for c in range(BS // MOBA_Q_LANES):
            qs = slice(c * MOBA_Q_LANES, (c + 1) * MOBA_Q_LANES)
            chosen = (lax.shift_right_logical(sel_ref[hh, :, qs], qi - t) & 1) == 1
            c2 = scale * LOG2E
            if bias_of is not None:
                st = s_ref[hh, :, qs] * c2 + bias_of(qs)
                shift = 0.0
                top = jnp.max(st, axis=0, keepdims=True)
            else:
                st = None
                shift = far_bias
                top = jnp.max(s_ref[hh, :, qs], axis=0, keepdims=True) * c2 + shift
            m_prev = m_ref[hh, :, qs]
            m_new = jnp.where(chosen, jnp.maximum(m_prev, top), m_prev)
            a_ref[hh, :, qs] = jnp.exp2(m_prev - m_new)
            m_ref[hh, :, qs] = m_new
            cut = jnp.where(chosen, m_new - shift, -NEG)
            if st is None:
                st = s_ref[hh, :, qs] * c2
            p_ref[hh, :, qs] = jnp.exp2(st - cut).astype(BF16)

    def pv_stage(hh, t):
        acc_ref[hh] = a_ref[hh] * acc_ref[hh] + jnp.dot(vt_ref[hh, qi - t], p_ref[hh],
                                                        preferred_element_type=F32)

    if finish_previous:
        _moba_close(qi - 1, vt_ref=vt_ref, p_ref=p_ref, a_ref=a_ref, acc_ref=acc_ref, o_ref=o_ref)

    for hh in range(HPS):
        qt = q_ref[0, q_rows, lanes[hh]].astype(F32).T.astype(BF16)
        qt_ref[hh] = qt
        gate = jnp.dot(kmean_ref[hh].astype(BF16), qt, preferred_element_type=F32)
        blk = lax.broadcasted_iota(jnp.int32, gate.shape, 0)
        blkf = blk.astype(F32)
        g = jnp.where(blk < qi, gate, NEG)
        bits = jnp.full((1, BS), lax.shift_left(jnp.int32(1), qi), jnp.int32)
        for _ in range(MOBA_TOPK):
            mx = jnp.max(g, axis=0, keepdims=True)
            ix = jnp.min(jnp.where(g == mx, blkf, float(n_blocks)), axis=0, keepdims=True).astype(jnp.int32)
            bits = bits | jnp.where(ix < qi, lax.shift_left(jnp.int32(1), ix), 0)
            g = jnp.where(blk == ix, -jnp.inf, g)
        sel_ref[hh] = bits
        m_ref[hh] = jnp.full((1, BS), NEG, F32)
        acc_ref[hh] = jnp.zeros((MOBA_PV_ROWS, BS), F32)
        qk_stage(hh, 0)

    for hh in range(HPS):
        softmax_stage(hh, 0, lambda qs, hh=hh: bias_ref[hh, 0, :, qs])
    if not finish_previous:
        return

    for hh in range(HPS):
        qk_stage(hh, 1)
    for hh in range(HPS):
        pv_stage(hh, 0)
    for hh in range(HPS):
        softmax_stage(hh, 1, lambda qs, hh=hh: bias_ref[hh, 1, :, qs])
    for hh in range(HPS):
        qk_stage(hh, 2)

    def sweep(t):
        for hh in range(HPS):
            pv_stage(hh, t - 1)
        for hh in range(HPS):
            softmax_stage(hh, t, far_bias=tbl_ref[hg * HPS + hh, N_BUCKETS - 1] * LOG2E)
        for hh in range(HPS):
            qk_stage(hh, t + 1)

    n_far = qi - 1

    def sweep_pair(i, carry):
        sweep(2 + 2 * i)
        sweep(3 + 2 * i)
        return carry

    lax.fori_loop(0, n_far // 2, sweep_pair, 0)

    @pl.when(n_far % 2 == 1)
    def _odd_block():
        sweep(qi)


def _moba(proj3, rel_bias):
    B, S, _ = proj3.shape
    BS = MOBA_BLOCK
    HPS = MOBA_HEADS_PER_STEP
    W = HPS * HEAD_DIM
    nb = S // BS
    n_groups = N_HEADS // HPS
    bkt = jnp.asarray(_rel_buckets_t())
    tbl = rel_bias.astype(F32).T
    grid_spec = pltpu.PrefetchScalarGridSpec(
        num_scalar_prefetch=1,
        grid=(n_groups, B),
        in_specs=[pl.BlockSpec((2, BS, BS), lambda h, b, t: (0, 0, 0)),
                  pl.BlockSpec((1, S, W), lambda h, b, t: (b, 0, h)),
                  pl.BlockSpec((1, S, W), lambda h, b, t: (b, 0, n_groups + h)),
                  pl.BlockSpec((1, S, W), lambda h, b, t: (b, 0, 2 * n_groups + h))],
        out_specs=pl.BlockSpec((1, S, W), lambda h, b, t: (b, 0, h)),
        scratch_shapes=[pltpu.VMEM((HPS, nb, HEAD_DIM), F32),
                        pltpu.VMEM((HPS, nb, MOBA_PV_ROWS, BS), BF16),
                        pltpu.VMEM((HPS, 2, BS, BS), F32),
                        pltpu.VMEM((HPS, HEAD_DIM, BS), BF16),
                        pltpu.VMEM((HPS, 1, BS), jnp.int32),
                        pltpu.VMEM((HPS, BS, BS), F32),
                        pltpu.VMEM((HPS, BS, BS), BF16),
                        pltpu.VMEM((HPS, 1, BS), F32),
                        pltpu.VMEM((HPS, 1, BS), F32),
                        pltpu.VMEM((HPS, MOBA_PV_ROWS, BS), F32)],
    )
    return pl.pallas_call(
        functools.partial(_moba_kernel, n_blocks=nb),
        grid_spec=grid_spec,
        out_shape=jax.ShapeDtypeStruct((B, S, N_HEADS * HEAD_DIM), BF16),
        compiler_params=_cparams("arbitrary", "arbitrary"),
        name="moba",
    )(tbl, bkt, proj3, proj3, proj3)


CONV_HALO = 32
CONV_ROWS = 64
CONV_COLS = 256


def _conv_kernel(a_ref, g_ref, ah_ref, gh_ref, w_ref, cb_ref, lng_ref, lnb_ref, o_ref, u_ref, sh_ref, c_ref,
                 *, ts):
    i = pl.program_id(1)
    D = u_ref.shape[1]
    u_ref[CONV_HALO:, :] = a_ref[0].astype(F32) * _sigmoid(g_ref[0].astype(F32))
    uh = ah_ref[0].astype(F32) * _sigmoid(gh_ref[0].astype(F32))
    u_ref[:CONV_HALO, :] = jnp.where(i > 0, uh, 0.0)

    first = CONV_HALO - (CONV_KERNEL - 1)
    sh_rows = sh_ref.shape[1]

    def col_body(c, carry):
        cols = pl.ds(pl.multiple_of(c * CONV_COLS, CONV_COLS), CONV_COLS)
        for s in range(1, SUBLANES):
            sh_ref[s] = u_ref[s:s + sh_rows, cols]
        for r in range(ts // CONV_ROWS):
            acc = jnp.zeros((CONV_ROWS, CONV_COLS), F32)
            for k in range(CONV_KERNEL):
                s = (first + k) % SUBLANES
                r0 = r * CONV_ROWS + first + k - s
                tap = u_ref[r0:r0 + CONV_ROWS, cols] if s == 0 else sh_ref[s, r0:r0 + CONV_ROWS, :]
                acc = acc + jnp.tile(w_ref[k, :, cols], (CONV_ROWS // SUBLANES, 1)) * tap
            c_ref[r * CONV_ROWS:(r + 1) * CONV_ROWS, cols] = acc + cb_ref[:, cols]
        return carry

    lax.fori_loop(0, D // CONV_COLS, col_body, 0)

    y = c_ref[...]
    mu = jnp.mean(y, axis=-1, keepdims=True)
    yc = y - mu
    var = jnp.mean(yc * yc, axis=-1, keepdims=True)
    z = yc * lax.rsqrt(var + EPS) * lng_ref[...] + lnb_ref[...]
    o_ref[0] = (z * _sigmoid(z)).astype(o_ref.dtype)


def _conv(proj3, conv_w, conv_b, ln_g, ln_b, a_blk, ts=512):
    B, S, _ = proj3.shape
    D = conv_w.shape[1]
    hpb = ts // CONV_HALO
    w_rows = jnp.broadcast_to(conv_w[:, None, :], (CONV_KERNEL, SUBLANES, D))

    def halo(col):
        return pl.BlockSpec((1, CONV_HALO, D), lambda b, i: (b, jnp.maximum(i * hpb - 1, 0), col))

    vec = pl.BlockSpec((1, D), lambda b, i: (0, 0))
    return pl.pallas_call(
        functools.partial(_conv_kernel, ts=ts),
        grid=(B, S // ts),
        in_specs=[pl.BlockSpec((1, ts, D), lambda b, i: (b, i, a_blk)),
                  pl.BlockSpec((1, ts, D), lambda b, i: (b, i, a_blk + 1)),
                  halo(a_blk), halo(a_blk + 1),
                  pl.BlockSpec((CONV_KERNEL, SUBLANES, D), lambda b, i: (0, 0, 0)),
                  vec, vec, vec],
        out_specs=pl.BlockSpec((1, ts, D), lambda b, i: (b, i, 0)),
        out_shape=jax.ShapeDtypeStruct((B, S, D), BF16),
        scratch_shapes=[pltpu.VMEM((CONV_HALO + ts, D), F32),
                        pltpu.VMEM((SUBLANES, CONV_HALO + ts - SUBLANES, CONV_COLS), F32),
                        pltpu.VMEM((ts, D), F32)],
        compiler_params=_cparams("parallel", "arbitrary"),
        name="conv",
    )(proj3, proj3, proj3, proj3, w_rows, conv_b, ln_g, ln_b)


def _mix_kernel(x_ref, at_ref, cv_ref, ga_ref, gc_ref, wa_ref, wc_ref, wo_ref, o_ref):
    ya = jnp.dot(at_ref[...], wa_ref[...], preferred_element_type=F32)
    yc = jnp.dot(cv_ref[...], wc_ref[...], preferred_element_type=F32)
    mixed = (jax.nn.sigmoid(ga_ref[...].astype(F32)) * ya + jax.nn.sigmoid(gc_ref[...].astype(F32)) * yc)
    o_ref[...] = x_ref[...] + jnp.dot(mixed.astype(BF16), wo_ref[...], preferred_element_type=F32)


def _mix(x2, attn2, conv2, proj2, wa, wc, wo, gate_blk, tm=256):
    T, D = x2.shape
    row = lambda col: pl.BlockSpec((tm, D), lambda i: (i, col))
    return pl.pallas_call(
        _mix_kernel,
        grid=(T // tm,),
        in_specs=[row(0), row(0), row(0), row(gate_blk), row(gate_blk + 1),
                  _resident((D, D)), _resident((D, D)), _resident((D, D))],
        out_specs=row(0),
        out_shape=jax.ShapeDtypeStruct((T, D), F32),
        compiler_params=_cparams("parallel"),
        name="mix",
    )(x2, attn2, conv2, proj2, proj2, wa, wc, wo)


MOE_TOK = 512
MOE_CHUNK = 16
MOE_TM = 512
MOE_TOPK = 2
_MOE_WORST_ROWS = MOE_TOPK * MOE_TOK + N_EXPERTS * (MOE_CHUNK - 1)
MOE_ROWS = -(-_MOE_WORST_ROWS // LANES) * LANES
MOE_CPT = MOE_ROWS // MOE_CHUNK
MOE_CPM = MOE_TM // MOE_CHUNK
MOE_EXTRA = LANES


def _route_t(lt):
    row = lax.broadcasted_iota(jnp.int32, lt.shape, 0)
    rowf = row.astype(F32)
    big = float(lt.shape[0])
    is_grp = (row >= N_EXPERTS) & (row < N_EXPERTS + N_GROUPS)
    cl = jnp.where(is_grp, lt, -jnp.inf)
    mg = jnp.max(cl, axis=0, keepdims=True)
    g_star = jnp.min(jnp.where(cl == mg, rowf, big), axis=0, keepdims=True).astype(jnp.int32) - N_EXPERTS
    p_g = 1.0 / jnp.sum(jnp.where(is_grp, jnp.exp(cl - mg), 0.0), axis=0, keepdims=True)
    in_grp = (row < N_EXPERTS) & ((row // EXPERTS_PER_GROUP) == g_star)
    f1 = jnp.where(in_grp, lt, -jnp.inf)
    v1 = jnp.max(f1, axis=0, keepdims=True)
    i1 = jnp.min(jnp.where(f1 == v1, rowf, big), axis=0, keepdims=True).astype(jnp.int32)
    f2 = jnp.where(row == i1, -jnp.inf, f1)
    v2 = jnp.max(f2, axis=0, keepdims=True)
    i2 = jnp.min(jnp.where(f2 == v2, rowf, big), axis=0, keepdims=True).astype(jnp.int32)
    e2 = jnp.exp(v2 - v1)
    return i1, i2, p_g / (1.0 + e2), p_g * e2 / (1.0 + e2)


def _split_bf16(x):
    hi = x.astype(BF16)
    return hi, (x - hi.astype(F32)).astype(BF16)


def _moe_sort_kernel(h_ref, g_ref, wh_ref, wl_ref, br_ref, xs_ref, dest_ref, cnt_ref):
    tok = h_ref.shape[0]
    D = h_ref.shape[1]
    hn = _rms(h_ref[...], g_ref[...])
    hn_hi, hn_lo = _split_bf16(hn)
    nt_dot = lambda a, b: lax.dot_general(a, b, _NT, preferred_element_type=F32)
    w_hi = wh_ref[...]
    both = nt_dot(jnp.concatenate([w_hi, wl_ref[...]], axis=0), hn_hi)
    lt = both[:MOE_ROUTE_ROWS] + both[MOE_ROUTE_ROWS:] + nt_dot(w_hi, hn_lo) + br_ref[...]
    i1, i2, c1, c2 = _route_t(lt)

    erow = lax.broadcasted_iota(jnp.int32, (N_EXPERTS, tok), 0)
    oh1 = erow == i1
    oh2 = erow == i2
    oh = (oh1 | oh2).astype(F32)
    earlier = (lax.broadcasted_iota(jnp.int32, (tok, tok), 0) < lax.broadcasted_iota(jnp.int32, (tok, tok), 1))
    cum = jnp.dot(oh.astype(BF16), earlier.astype(BF16), preferred_element_type=F32)
    cnt = jnp.sum(oh, axis=1, keepdims=True)
    n_chunks = jnp.floor((cnt + (MOE_CHUNK - 1)) * (1.0 / MOE_CHUNK))
    ncb = jnp.broadcast_to(n_chunks, (N_EXPERTS, LANES))
    erow_l = lax.broadcasted_iota(jnp.int32, (N_EXPERTS, LANES), 0)
    first_chunk = jnp.zeros((N_EXPERTS, LANES), F32)
    for e in range(N_EXPERTS - 1):
        first_chunk = first_chunk + jnp.where(erow_l > e, ncb[e:e + 1, :], 0.0)
    base = jnp.tile(first_chunk, (1, tok // LANES)) * MOE_CHUNK + cum
    d1 = jnp.sum(jnp.where(oh1, base, 0.0), axis=0, keepdims=True).astype(jnp.int32)
    d2 = jnp.sum(jnp.where(oh2, base, 0.0), axis=0, keepdims=True).astype(jnp.int32)

    srow = lax.broadcasted_iota(jnp.int32, (MOE_ROWS, tok), 0)
    p1 = srow == d1
    p2 = srow == d2
    perm1 = p1.astype(BF16)
    perm2 = p2.astype(BF16)
    xs_ref[:, :D] = jnp.dot(perm1 + perm2, hn_hi, preferred_element_type=F32).astype(BF16)
    xrow = lax.broadcasted_iota(jnp.int32, (MOE_EXTRA, tok), 0)

    def weight_rows(c):
        hi = c.astype(BF16).astype(F32)
        return jnp.where(xrow == 0, hi, jnp.where(xrow == 1, c - hi, 0.0)).astype(BF16)

    xs_ref[:, D:] = (nt_dot(perm1, weight_rows(c1)) + nt_dot(perm2, weight_rows(c2))).astype(BF16)

    r8 = lax.broadcasted_iota(jnp.int32, (SUBLANES, tok), 0)
    dest_ref[0] = jnp.where(r8 == 0, d1, jnp.where(r8 == 1, d2, 0))
    cnt_ref[0] = jnp.broadcast_to(cnt, (N_EXPERTS, LANES)).astype(jnp.int32)


MOE_ROUTE_ROWS = 32


def _moe_sort(h1, g, w_router_e, b_router_e, w_router_g, b_router_g):
    T, D = h1.shape
    nt = T // MOE_TOK
    pad = MOE_ROUTE_ROWS - N_EXPERTS - N_GROUPS
    wr = jnp.concatenate([w_router_e.T, w_router_g.T, jnp.zeros((pad, D), F32)], axis=0)
    wr_hi = wr.astype(BF16)
    wr_lo = (wr - wr_hi.astype(F32)).astype(BF16)
    br = jnp.concatenate([b_router_e, b_router_g, jnp.zeros((pad,), F32)])[:, None]
    return pl.pallas_call(
        _moe_sort_kernel,
        grid=(nt,),
        in_specs=[pl.BlockSpec((MOE_TOK, D), lambda i: (i, 0)),
                  pl.BlockSpec((1, D), lambda i: (0, 0)),
                  pl.BlockSpec((MOE_ROUTE_ROWS, D), lambda i: (0, 0)),
                  pl.BlockSpec((MOE_ROUTE_ROWS, D), lambda i: (0, 0)),
                  pl.BlockSpec((MOE_ROUTE_ROWS, 1), lambda i: (0, 0))],
        out_specs=[pl.BlockSpec((MOE_ROWS, D + MOE_EXTRA), lambda i: (i, 0)),
                   pl.BlockSpec((1, SUBLANES, MOE_TOK), lambda i: (i, 0, 0)),
                   pl.BlockSpec((1, N_EXPERTS, LANES), lambda i: (i, 0, 0))],
        out_shape=[jax.ShapeDtypeStruct((nt * MOE_ROWS, D + MOE_EXTRA), BF16),
                   jax.ShapeDtypeStruct((nt, SUBLANES, MOE_TOK), jnp.int32),
                   jax.ShapeDtypeStruct((nt, N_EXPERTS, LANES), jnp.int32)],
        compiler_params=_cparams("parallel"),
        name="moe_sort",
    )(h1, g, wr_hi, wr_lo, br)


def _moe_tables(cnt, n_mm_tiles):
    nt = cnt.shape[0]
    pc = (cnt + (MOE_CHUNK - 1)) // MOE_CHUNK
    run_end = jnp.cumsum(pc, axis=1)
    run_first = run_end - pc
    per_expert = jnp.sum(pc, axis=0)
    mm_tiles = (per_expert + (MOE_CPM - 1)) // MOE_CPM
    mm_end = jnp.cumsum(mm_tiles)
    slot_base = (mm_end - mm_tiles)[None, :] * MOE_CPM + (jnp.cumsum(pc, axis=0) - pc)
    k = jnp.arange(MOE_CPT, dtype=jnp.int32)[None, :, None]
    expert_of = jnp.sum((run_end[:, None, :] <= k).astype(jnp.int32), axis=2)
    valid = expert_of < N_EXPERTS
    pick = expert_of[:, :, None] == jnp.arange(N_EXPERTS, dtype=jnp.int32)
    take = lambda a: jnp.sum(jnp.where(pick, a[:, None, :], 0), axis=2)
    slot = take(slot_base) + (k[:, :, 0] - take(run_first))
    src = (jnp.arange(nt, dtype=jnp.int32)[:, None] * MOE_CPT + k[:, :, 0]).reshape(-1)
    slot = jnp.where(valid, slot, n_mm_tiles * MOE_CPM).reshape(-1)
    gather_tbl = jnp.zeros((n_mm_tiles * MOE_CPM,), jnp.int32).at[slot].set(src, mode="drop")
    return_tbl = jnp.where(valid.reshape(-1), slot, 0).astype(jnp.int32)
    g = jnp.arange(n_mm_tiles, dtype=jnp.int32)
    n_used = mm_end[-1].astype(jnp.int32)
    tile_expert = jnp.sum((mm_end[None, :] <= jnp.minimum(g, n_used - 1)[:, None]).astype(jnp.int32), axis=1)
    return gather_tbl, return_tbl, tile_expert.astype(jnp.int32), n_used.reshape(1)


def _chunk_gather(tbl_ref, src_hbm, buf, sem, chunks_per_tile):
    def chunk_copy(tile, c, slot):
        src = pl.multiple_of(tbl_ref[tile * chunks_per_tile + c] * MOE_CHUNK, MOE_CHUNK)
        dst = pl.multiple_of(c * MOE_CHUNK, MOE_CHUNK)
        return pltpu.make_async_copy(src_hbm.at[pl.ds(src, MOE_CHUNK), :],
                                     buf.at[slot, pl.ds(dst, MOE_CHUNK), :], sem.at[slot])

    def for_each_chunk(tile, slot, act):
        def body(c, carry):
            act(chunk_copy(tile, c, slot))
            return carry
        lax.fori_loop(0, chunks_per_tile, body, 0)

    def start(tile, slot):
        for_each_chunk(tile, slot, lambda copy: copy.start())

    def wait(tile, slot):
        for_each_chunk(tile, slot, lambda copy: copy.wait())

    return start, wait


def _moe_expert_kernel(gt_ref, te_ref, nu_ref, xs_hbm, wg_ref, wu_ref, wd_ref, o_ref,
                       wg_bf, wu_bf, wd_bf, x_buf, x_sem):
    D = o_ref.shape[1]
    g = pl.program_id(0)
    n_used = nu_ref[0]
    used = g < n_used

    start_gather, wait_gather = _chunk_gather(gt_ref, xs_hbm, x_buf, x_sem, MOE_CPM)

    @pl.when((g == 0) & (n_used > 0))
    def _():
        start_gather(0, 0)

    @pl.when(g + 1 < n_used)
    def _():
        start_gather(g + 1, (g + 1) % 2)

    @pl.when(used & ((g == 0) | (te_ref[g] != te_ref[jnp.maximum(g - 1, 0)])))
    def _():
        wg_bf[...] = wg_ref[0].astype(BF16)
        wu_bf[...] = wu_ref[0].astype(BF16)
        wd_bf[...] = wd_ref[0].astype(BF16)

    @pl.when(used)
    def _():
        slot = g % 2
        wait_gather(g, slot)
        xs = x_buf[slot]
        x = xs[:, :D]
        extra = xs[:, D:].astype(F32)
        c = extra[:, 0:1] + extra[:, 1:2]
        gate = jnp.dot(x, wg_bf[...], preferred_element_type=F32)
        up = jnp.dot(x, wu_bf[...], preferred_element_type=F32)
        hh = (gate * jax.nn.sigmoid(gate)) * up * c
        o_ref[...] = jnp.dot(hh.astype(BF16), wd_bf[...], preferred_element_type=F32).astype(o_ref.dtype)

    @pl.when(jnp.logical_not(used))
    def _():
        o_ref[...] = jnp.zeros(o_ref.shape, o_ref.dtype)


def _moe_experts(xs, gather_tbl, tile_expert, n_used, wg, wu, wd, n_mm_tiles):
    W = xs.shape[1]
    E, D, FF = wg.shape

    grid_spec = pltpu.PrefetchScalarGridSpec(
        num_scalar_prefetch=3,
        grid=(n_mm_tiles,),
        in_specs=[pl.BlockSpec(memory_space=pl.ANY),
                  pl.BlockSpec((1, D, FF), lambda g, gt, te, nu: (te[g], 0, 0)),
                  pl.BlockSpec((1, D, FF), lambda g, gt, te, nu: (te[g], 0, 0)),
                  pl.BlockSpec((1, FF, D), lambda g, gt, te, nu: (te[g], 0, 0))],
        out_specs=pl.BlockSpec((MOE_TM, D), lambda g, gt, te, nu: (g, 0)),
        scratch_shapes=[pltpu.VMEM((D, FF), BF16), pltpu.VMEM((D, FF), BF16), pltpu.VMEM((FF, D), BF16),
                        pltpu.VMEM((2, MOE_TM, W), BF16), pltpu.SemaphoreType.DMA((2,))],
    )
    return pl.pallas_call(
        _moe_expert_kernel,
        grid_spec=grid_spec,
        out_shape=jax.ShapeDtypeStruct((n_mm_tiles * MOE_TM, D), BF16),
        compiler_params=_cparams("arbitrary"),
        name="moe_experts",
    )(gather_tbl, tile_expert, n_used, xs, wg, wu, wd)


def _finish_kernel(rt_ref, ys_hbm, dest_ref, h_ref, p_ref, gp_ref, gf_ref, wg_ref, wp_ref, o_ref, y_buf, y_sem):
    tok = h_ref.shape[0]
    i = pl.program_id(0)
    slot = i % 2
    start_gather, wait_gather = _chunk_gather(rt_ref, ys_hbm, y_buf, y_sem, MOE_CPT)

    @pl.when(i == 0)
    def _():
        start_gather(0, 0)

    @pl.when(i + 1 < pl.num_programs(0))
    def _():
        start_gather(i + 1, (i + 1) % 2)

    wait_gather(i, slot)
    dcol = dest_ref[0].astype(F32).T
    lane = lax.broadcasted_iota(jnp.int32, (tok, MOE_ROWS), 1).astype(F32)
    unsort = ((lane == dcol[:, 0:1]) | (lane == dcol[:, 1:2])).astype(BF16)
    h = h_ref[...] + jnp.dot(unsort, y_buf[slot], preferred_element_type=F32)
    t = _rms(h, gp_ref[...]).astype(BF16)
    gate = jax.nn.sigmoid(jnp.dot(t, wg_ref[...], preferred_element_type=F32))
    emb = jnp.dot(p_ref[...].astype(BF16), wp_ref[...], preferred_element_type=F32)
    o_ref[...] = _rms(h + emb * gate, gf_ref[...])


def _finish(ys, return_tbl, dest, h1, p2, g_ple, g_final, wg, wp):
    T, D = h1.shape
    P = p2.shape[1]
    nt = T // MOE_TOK

    vec = pl.BlockSpec((1, D), lambda i, rt: (0, 0))
    grid_spec = pltpu.PrefetchScalarGridSpec(
        num_scalar_prefetch=1,
        grid=(nt,),
        in_specs=[
            pl.BlockSpec(memory_space=pl.ANY),
            pl.BlockSpec((1, SUBLANES, MOE_TOK), lambda i, rt: (i, 0, 0)),
            pl.BlockSpec((MOE_TOK, D), lambda i, rt: (i, 0)),
            pl.BlockSpec((MOE_TOK, P), lambda i, rt: (i, 0)),
            vec, vec,
            pl.BlockSpec((D, D), lambda i, rt: (0, 0), pipeline_mode=pl.Buffered(1)),
            pl.BlockSpec((P, D), lambda i, rt: (0, 0), pipeline_mode=pl.Buffered(1))],
        out_specs=pl.BlockSpec((MOE_TOK, D), lambda i, rt: (i, 0)),
        scratch_shapes=[pltpu.VMEM((2, MOE_ROWS, D), BF16), pltpu.SemaphoreType.DMA((2,))],
    )
    return pl.pallas_call(
        _finish_kernel,
        grid_spec=grid_spec,
        out_shape=jax.ShapeDtypeStruct((T, D), F32),
        compiler_params=_cparams("arbitrary"),
        name="finish",
    )(return_tbl, ys, dest, h1, p2, g_ple, g_final, wg, wp)


def kernel(x, p, g_mix, w_in, rel_bias, w_attn_br, conv_w, conv_b, ln_g, ln_b, w_conv_br, w_o, g_ffn,
           w_router_g, b_router_g, w_router_e, b_router_e, w_e_gate, w_e_up, w_e_down, g_ple, w_ple_gate,
           w_ple_proj, g_final):
    B, S, D = x.shape
    T = B * S
    depth = w_in.shape[0]
    width = N_HEADS * HEAD_DIM
    assert D == width and conv_w.shape[2] == D, "column-block indexing assumes all branch widths equal d_model"
    assert S % MOBA_BLOCK == 0
    assert depth == 1, "the final RMSNorm is fused into the last stage of a single layer"
    a_blk = 3 * width // D
    gate_blk = a_blk + 2

    h = x.reshape(T, D)
    for i in range(depth):
        proj = _in_proj(h, g_mix[i][None], w_in[i])
        proj3 = proj.reshape(B, S, -1)
        attn = _moba(proj3, rel_bias)
        conv = _conv(proj3, conv_w[i], conv_b[i][None], ln_g[i][None], ln_b[i][None], a_blk)
        h = _mix(h, attn.reshape(T, D), conv.reshape(T, D), proj,
                 w_attn_br[i].astype(BF16), w_conv_br[i].astype(BF16), w_o[i].astype(BF16), gate_blk)
        xs, dest, cnt = _moe_sort(h, g_ffn[i][None], w_router_e[i], b_router_e[i], w_router_g[i], b_router_g[i])
        n_tok_tiles = T // MOE_TOK
        worst_chunks = (MOE_TOPK * T + n_tok_tiles * N_EXPERTS * (MOE_CHUNK - 1)) // MOE_CHUNK
        n_mm_tiles = -(-worst_chunks // MOE_CPM) + N_EXPERTS
        gather_tbl, return_tbl, tile_expert, n_used = _moe_tables(cnt[:, :, 0], n_mm_tiles)
        ys = _moe_experts(xs, gather_tbl, tile_expert, n_used,
                          w_e_gate[i], w_e_up[i], w_e_down[i], n_mm_tiles)
        out = _finish(ys, return_tbl, dest, h, p[i].reshape(T, -1), g_ple[i][None], g_final[None],
                      w_ple_gate[i].astype(BF16), w_ple_proj[i].astype(BF16))
    return out.reshape(B, S, D)
```

```python
import functools
import math

import numpy as np
import jax
import jax.numpy as jnp
from jax import lax
from jax.experimental import pallas as pl
from jax.experimental.pallas import tpu as pltpu

F32 = jnp.float32
BF16 = jnp.bfloat16

N_HEADS = 16
HEAD_DIM = 128
MOBA_BLOCK = 256
MOBA_TOPK = 3
N_BUCKETS = 32
MAX_DISTANCE = 128
CONV_KERNEL = 31
N_GROUPS = 4
EXPERTS_PER_GROUP = 4
N_EXPERTS = N_GROUPS * EXPERTS_PER_GROUP
EPS = 1e-6
NEG = -1e30
LOG2E = 1.4426950408889634

V7X_VMEM_LIMIT_BYTES = 56 * 1024 * 1024
LANES = 128
SUBLANES = 8

_NT = (((1,), (1,)), ((), ()))


def _cparams(*sem):
    return pltpu.CompilerParams(dimension_semantics=sem, vmem_limit_bytes=V7X_VMEM_LIMIT_BYTES)


def _resident(shape):
    return pl.BlockSpec(shape, lambda *_: (0,) * len(shape), pipeline_mode=pl.Buffered(1))


def _sigmoid(x):
    return 0.5 * jnp.tanh(0.5 * x) + 0.5


def _rms(x, g):
    return x * lax.rsqrt(jnp.mean(x * x, axis=-1, keepdims=True) + EPS) * g


IN_PROJ_NORM_SPLIT = 4


def _in_proj_kernel(x_ref, g_ref, w_ref, o_ref, xn_ref):
    first = pl.program_id(1) == 0

    @pl.when(first)
    def _():
        sub = x_ref.shape[0] // IN_PROJ_NORM_SPLIT
        for r in range(IN_PROJ_NORM_SPLIT):
            rows = slice(r * sub, (r + 1) * sub)
            xn = _rms(x_ref[rows, :], g_ref[...]).astype(BF16)
            xn_ref[rows, :] = xn
            o_ref[rows, :] = jnp.dot(xn, w_ref[...], preferred_element_type=F32).astype(o_ref.dtype)

    @pl.when(jnp.logical_not(first))
    def _():
        o_ref[...] = jnp.dot(xn_ref[...], w_ref[...], preferred_element_type=F32).astype(o_ref.dtype)


def _in_proj(x2, g, w, tm=1024, tn=2048):
    T, D = x2.shape
    N = w.shape[1]
    return pl.pallas_call(
        _in_proj_kernel,
        grid=(T // tm, N // tn),
        in_specs=[pl.BlockSpec((tm, D), lambda i, j: (i, 0)),
                  pl.BlockSpec((1, D), lambda i, j: (0, 0)),
                  pl.BlockSpec((D, tn), lambda i, j: (0, j))],
        out_specs=pl.BlockSpec((tm, tn), lambda i, j: (i, j)),
        out_shape=jax.ShapeDtypeStruct((T, N), BF16),
        scratch_shapes=[pltpu.VMEM((tm, D), BF16)],
        compiler_params=_cparams("parallel", "arbitrary"),
        name="in_proj",
    )(x2, g, w)


MOBA_PV_ROWS = HEAD_DIM + 16
MOBA_Q_LANES = MOBA_BLOCK
MOBA_HEADS_PER_STEP = 4


def _rel_buckets_t():
    k = np.arange(MOBA_BLOCK)[:, None]
    q = np.arange(MOBA_BLOCK)[None, :]
    dist = np.stack([q - k, MOBA_BLOCK + q - k]).astype(np.int32)
    n = np.maximum(dist, 0)
    max_exact = N_BUCKETS // 2
    nf = np.maximum(n, 1).astype(np.float32)
    large = max_exact + (np.log(nf / np.float32(max_exact)) / np.float32(math.log(MAX_DISTANCE / max_exact))
                         * np.float32(N_BUCKETS - max_exact)).astype(np.int32)
    large = np.minimum(large, N_BUCKETS - 1)
    return np.where(n < max_exact, n, large).astype(np.int32)


def _moba_kernel(tbl_ref, bkt_ref, q_ref, k_ref, v_ref, o_ref,
                 kmean_ref, vt_ref, bias_ref, qt_ref, sel_ref, s_ref, p_ref, m_ref, a_ref, acc_ref,
                 *, n_blocks):
    hg = pl.program_id(0)
    b = pl.program_id(1)
    BS = MOBA_BLOCK
    HPS = MOBA_HEADS_PER_STEP
    lanes = [slice(hh * HEAD_DIM, (hh + 1) * HEAD_DIM) for hh in range(HPS)]

    @pl.when(b == 0)
    def _build_bias():
        krow = lax.broadcasted_iota(jnp.int32, (BS, BS), 0)
        qcol = lax.broadcasted_iota(jnp.int32, (BS, BS), 1)
        for hh in range(HPS):
            bias_ref[hh] = jnp.zeros((2, BS, BS), F32)

            def fill(n, carry):
                bias_ref[hh] = jnp.where(bkt_ref[...] == n, tbl_ref[hg * HPS + hh, n] * LOG2E, bias_ref[hh])
                return carry

            lax.fori_loop(0, N_BUCKETS, fill, 0)
            bias_ref[hh, 0] = jnp.where(qcol >= krow, bias_ref[hh, 0], NEG)

    ones_row = (lax.broadcasted_iota(jnp.int32, (MOBA_PV_ROWS - HEAD_DIM, BS), 0) == 0).astype(BF16)

    def prep(n, carry):
        rows = pl.ds(pl.multiple_of(n * BS, BS), BS)
        for hh in range(HPS):
            kmean_ref[hh, pl.ds(n, 1), :] = jnp.mean(k_ref[0, rows, lanes[hh]].astype(F32), axis=0, keepdims=True)
            vt_ref[hh, n, :HEAD_DIM] = v_ref[0, rows, lanes[hh]].astype(F32).T.astype(BF16)
            vt_ref[hh, n, HEAD_DIM:] = ones_row
        return carry

    lax.fori_loop(0, n_blocks, prep, 0)

    q_block = functools.partial(_moba_q_block, hg=hg, tbl_ref=tbl_ref, q_ref=q_ref, k_ref=k_ref, o_ref=o_ref,
                                kmean_ref=kmean_ref, vt_ref=vt_ref, bias_ref=bias_ref, qt_ref=qt_ref,
                                sel_ref=sel_ref, s_ref=s_ref, p_ref=p_ref, m_ref=m_ref, a_ref=a_ref,
                                acc_ref=acc_ref, n_blocks=n_blocks)

    q_block(jnp.int32(0), finish_previous=False)

    def next_block(qi, carry):
        q_block(qi, finish_previous=True)
        return carry

    lax.fori_loop(1, n_blocks, next_block, 0)
    _moba_close(jnp.int32(n_blocks - 1), vt_ref=vt_ref, p_ref=p_ref, a_ref=a_ref, acc_ref=acc_ref, o_ref=o_ref)


def _moba_close(qi, *, vt_ref, p_ref, a_ref, acc_ref, o_ref):
    lanes = [slice(hh * HEAD_DIM, (hh + 1) * HEAD_DIM) for hh in range(MOBA_HEADS_PER_STEP)]
    q_rows = pl.ds(pl.multiple_of(qi * MOBA_BLOCK, MOBA_BLOCK), MOBA_BLOCK)
    for hh in range(MOBA_HEADS_PER_STEP):
        acc = a_ref[hh] * acc_ref[hh] + jnp.dot(vt_ref[hh, 0], p_ref[hh], preferred_element_type=F32)
        denom = acc[HEAD_DIM:HEAD_DIM + 1, :]
        out_t = acc[:HEAD_DIM, :] * (1.0 / denom)
        o_ref[0, q_rows, lanes[hh]] = out_t.T.astype(o_ref.dtype)


def _moba_q_block(qi, *, finish_previous, hg, tbl_ref, q_ref, k_ref, o_ref, kmean_ref, vt_ref, bias_ref, qt_ref,
                  sel_ref, s_ref, p_ref, m_ref, a_ref, acc_ref, n_blocks):
    BS = MOBA_BLOCK
    HPS = MOBA_HEADS_PER_STEP
    scale = HEAD_DIM ** -0.5
    lanes = [slice(hh * HEAD_DIM, (hh + 1) * HEAD_DIM) for hh in range(HPS)]
    q_rows = pl.ds(pl.multiple_of(qi * BS, BS), BS)

    def qk_stage(hh, t):
        j0 = pl.multiple_of(jnp.maximum(qi - t, 0) * BS, BS)
        s_ref[hh] = jnp.dot(k_ref[0, pl.ds(j0, BS), lanes[hh]], qt_ref[hh], preferred_element_type=F32)

    def softmax_stage(hh, t, bias_of=None, far_bias=None):
        for c in range(BS // MOBA_Q_LANES):
            qs = slice(c * MOBA_Q_LANES, (c + 1) * MOBA_Q_LANES)
            chosen = (lax.shift_right_logical(sel_ref[hh, :, qs], qi - t) & 1) == 1
            c2 = scale * LOG2E
            if bias_of is not None:
                st = s_ref[hh, :, qs] * c2 + bias_of(qs)
                shift = 0.0
                top = jnp.max(st, axis=0, keepdims=True)
            else:
                st = None
                shift = far_bias
                top = jnp.max(s_ref[hh, :, qs], axis=0, keepdims=True) * c2 + shift
            m_prev = m_ref[hh, :, qs]
            m_new = jnp.where(chosen, jnp.maximum(m_prev, top), m_prev)
            a_ref[hh, :, qs] = jnp.exp2(m_prev - m_new)
            m_ref[hh, :, qs] = m_new
            cut = jnp.where(chosen, m_new - shift, -NEG)
            if st is None:
                st = s_ref[hh, :, qs] * c2
            p_ref[hh, :, qs] = jnp.exp2(st - cut).astype(BF16)

    def pv_stage(hh, t):
        acc_ref[hh] = a_ref[hh] * acc_ref[hh] + jnp.dot(vt_ref[hh, qi - t], p_ref[hh],
                                                        preferred_element_type=F32)

    if finish_previous:
        _moba_close(qi - 1, vt_ref=vt_ref, p_ref=p_ref, a_ref=a_ref, acc_ref=acc_ref, o_ref=o_ref)

    for hh in range(HPS):
        qt = q_ref[0, q_rows, lanes[hh]].astype(F32).T.astype(BF16)
        qt_ref[hh] = qt
        gate = jnp.dot(kmean_ref[hh].astype(BF16), qt, preferred_element_type=F32)
        blk = lax.broadcasted_iota(jnp.int32, gate.shape, 0)
        blkf = blk.astype(F32)
        g = jnp.where(blk < qi, gate, NEG)
        bits = jnp.full((1, BS), lax.shift_left(jnp.int32(1), qi), jnp.int32)
        for _ in range(MOBA_TOPK):
            mx = jnp.max(g, axis=0, keepdims=True)
            ix = jnp.min(jnp.where(g == mx, blkf, float(n_blocks)), axis=0, keepdims=True).astype(jnp.int32)
            bits = bits | jnp.where(ix < qi, lax.shift_left(jnp.int32(1), ix), 0)
            g = jnp.where(blk == ix, -jnp.inf, g)
        sel_ref[hh] = bits
        m_ref[hh] = jnp.full((1, BS), NEG, F32)
        acc_ref[hh] = jnp.zeros((MOBA_PV_ROWS, BS), F32)
        qk_stage(hh, 0)

    for hh in range(HPS):
        softmax_stage(hh, 0, lambda qs, hh=hh: bias_ref[hh, 0, :, qs])
    if not finish_previous:
        return

    for hh in range(HPS):
        qk_stage(hh, 1)
    for hh in range(HPS):
        pv_stage(hh, 0)
    for hh in range(HPS):
        softmax_stage(hh, 1, lambda qs, hh=hh: bias_ref[hh, 1, :, qs])
    for hh in range(HPS):
        qk_stage(hh, 2)

    def sweep(t):
        for hh in range(HPS):
            pv_stage(hh, t - 1)
        for hh in range(HPS):
            softmax_stage(hh, t, far_bias=tbl_ref[hg * HPS + hh, N_BUCKETS - 1] * LOG2E)
        for hh in range(HPS):
            qk_stage(hh, t + 1)

    n_far = qi - 1

    def sweep_pair(i, carry):
        sweep(2 + 2 * i)
        sweep(3 + 2 * i)
        return carry

    lax.fori_loop(0, n_far // 2, sweep_pair, 0)

    @pl.when(n_far % 2 == 1)
    def _odd_block():
        sweep(qi)


def _moba(proj3, rel_bias):
    B, S, _ = proj3.shape
    BS = MOBA_BLOCK
    HPS = MOBA_HEADS_PER_STEP
    W = HPS * HEAD_DIM
    nb = S // BS
    n_groups = N_HEADS // HPS
    bkt = jnp.asarray(_rel_buckets_t())
    tbl = rel_bias.astype(F32).T
    grid_spec = pltpu.PrefetchScalarGridSpec(
        num_scalar_prefetch=1,
        grid=(n_groups, B),
        in_specs=[pl.BlockSpec((2, BS, BS), lambda h, b, t: (0, 0, 0)),
                  pl.BlockSpec((1, S, W), lambda h, b, t: (b, 0, h)),
                  pl.BlockSpec((1, S, W), lambda h, b, t: (b, 0, n_groups + h)),
                  pl.BlockSpec((1, S, W), lambda h, b, t: (b, 0, 2 * n_groups + h))],
        out_specs=pl.BlockSpec((1, S, W), lambda h, b, t: (b, 0, h)),
        scratch_shapes=[pltpu.VMEM((HPS, nb, HEAD_DIM), F32),
                        pltpu.VMEM((HPS, nb, MOBA_PV_ROWS, BS), BF16),
                        pltpu.VMEM((HPS, 2, BS, BS), F32),
                        pltpu.VMEM((HPS, HEAD_DIM, BS), BF16),
                        pltpu.VMEM((HPS, 1, BS), jnp.int32),
                        pltpu.VMEM((HPS, BS, BS), F32),
                        pltpu.VMEM((HPS, BS, BS), BF16),
                        pltpu.VMEM((HPS, 1, BS), F32),
                        pltpu.VMEM((HPS, 1, BS), F32),
                        pltpu.VMEM((HPS, MOBA_PV_ROWS, BS), F32)],
    )
    return pl.pallas_call(
        functools.partial(_moba_kernel, n_blocks=nb),
        grid_spec=grid_spec,
        out_shape=jax.ShapeDtypeStruct((B, S, N_HEADS * HEAD_DIM), BF16),
        compiler_params=_cparams("arbitrary", "arbitrary"),
        name="moba",
    )(tbl, bkt, proj3, proj3, proj3)


CONV_HALO = 32
CONV_ROWS = 64
CONV_COLS = 256


def _conv_kernel(a_ref, g_ref, ah_ref, gh_ref, w_ref, cb_ref, lng_ref, lnb_ref, *refs, ts, n_side):
    side_in, (o_ref,), side_out = refs[:n_side], refs[n_side:n_side + 1], refs[n_side + 1:2 * n_side + 1]
    u_ref, sh_ref, c_ref = refs[2 * n_side + 1:]
    for src, dst in zip(side_in, side_out):
        dst[...] = src[...].astype(dst.dtype)

    i = pl.program_id(1)
    D = u_ref.shape[1]
    u_ref[CONV_HALO:, :] = a_ref[0].astype(F32) * _sigmoid(g_ref[0].astype(F32))
    uh = ah_ref[0].astype(F32) * _sigmoid(gh_ref[0].astype(F32))
    u_ref[:CONV_HALO, :] = jnp.where(i > 0, uh, 0.0)

    first = CONV_HALO - (CONV_KERNEL - 1)
    sh_rows = sh_ref.shape[1]

    def col_body(c, carry):
        cols = pl.ds(pl.multiple_of(c * CONV_COLS, CONV_COLS), CONV_COLS)
        for s in range(1, SUBLANES):
            sh_ref[s] = u_ref[s:s + sh_rows, cols]
        for r in range(ts // CONV_ROWS):
            acc = jnp.zeros((CONV_ROWS, CONV_COLS), F32)
            for k in range(CONV_KERNEL):
                s = (first + k) % SUBLANES
                r0 = r * CONV_ROWS + first + k - s
                tap = u_ref[r0:r0 + CONV_ROWS, cols] if s == 0 else sh_ref[s, r0:r0 + CONV_ROWS, :]
                acc = acc + jnp.tile(w_ref[k, :, cols], (CONV_ROWS // SUBLANES, 1)) * tap
            c_ref[r * CONV_ROWS:(r + 1) * CONV_ROWS, cols] = acc + cb_ref[:, cols]
        return carry

    lax.fori_loop(0, D // CONV_COLS, col_body, 0)

    y = c_ref[...]
    mu = jnp.mean(y, axis=-1, keepdims=True)
    yc = y - mu
    var = jnp.mean(yc * yc, axis=-1, keepdims=True)
    z = yc * lax.rsqrt(var + EPS) * lng_ref[...] + lnb_ref[...]
    o_ref[0] = (z * _sigmoid(z)).astype(o_ref.dtype)


def _conv(proj3, conv_w, conv_b, ln_g, ln_b, a_blk, side_weights, ts=512):
    B, S, _ = proj3.shape
    D = conv_w.shape[1]
    hpb = ts // CONV_HALO
    steps_per_seq = S // ts
    n_steps = B * steps_per_seq
    side_specs = [pl.BlockSpec((w.shape[0] // n_steps, w.shape[1]), lambda b, i: (b * steps_per_seq + i, 0))
                  for w in side_weights]
    w_rows = jnp.broadcast_to(conv_w[:, None, :], (CONV_KERNEL, SUBLANES, D))

    def halo(col):
        return pl.BlockSpec((1, CONV_HALO, D), lambda b, i: (b, jnp.maximum(i * hpb - 1, 0), col))

    vec = pl.BlockSpec((1, D), lambda b, i: (0, 0))
    conv_act, *side_bf16 = pl.pallas_call(
        functools.partial(_conv_kernel, ts=ts, n_side=len(side_weights)),
        grid=(B, steps_per_seq),
        in_specs=[pl.BlockSpec((1, ts, D), lambda b, i: (b, i, a_blk)),
                  pl.BlockSpec((1, ts, D), lambda b, i: (b, i, a_blk + 1)),
                  halo(a_blk), halo(a_blk + 1),
                  pl.BlockSpec((CONV_KERNEL, SUBLANES, D), lambda b, i: (0, 0, 0)),
                  vec, vec, vec] + side_specs,
        out_specs=[pl.BlockSpec((1, ts, D), lambda b, i: (b, i, 0))] + side_specs,
        out_shape=[jax.ShapeDtypeStruct((B, S, D), BF16)]
        + [jax.ShapeDtypeStruct(w.shape, BF16) for w in side_weights],
        scratch_shapes=[pltpu.VMEM((CONV_HALO + ts, D), F32),
                        pltpu.VMEM((SUBLANES, CONV_HALO + ts - SUBLANES, CONV_COLS), F32),
                        pltpu.VMEM((ts, D), F32)],
        compiler_params=_cparams("parallel", "arbitrary"),
        name="conv",
    )(proj3, proj3, proj3, proj3, w_rows, conv_b, ln_g, ln_b, *side_weights)
    return conv_act, side_bf16


def _mix_kernel(x_ref, at_ref, cv_ref, ga_ref, gc_ref, wa_ref, wc_ref, wo_ref, o_ref):
    ya = jnp.dot(at_ref[...], wa_ref[...], preferred_element_type=F32)
    yc = jnp.dot(cv_ref[...], wc_ref[...], preferred_element_type=F32)
    mixed = (jax.nn.sigmoid(ga_ref[...].astype(F32)) * ya + jax.nn.sigmoid(gc_ref[...].astype(F32)) * yc)
    o_ref[...] = x_ref[...] + jnp.dot(mixed.astype(BF16), wo_ref[...], preferred_element_type=F32)


def _mix(x2, attn2, conv2, proj2, wa, wc, wo, gate_blk, tm=256):
    T, D = x2.shape
    row = lambda col: pl.BlockSpec((tm, D), lambda i: (i, col))
    return pl.pallas_call(
        _mix_kernel,
        grid=(T // tm,),
        in_specs=[row(0), row(0), row(0), row(gate_blk), row(gate_blk + 1),
                  _resident((D, D)), _resident((D, D)), _resident((D, D))],
        out_specs=row(0),
        out_shape=jax.ShapeDtypeStruct((T, D), F32),
        compiler_params=_cparams("parallel"),
        name="mix",
    )(x2, attn2, conv2, proj2, proj2, wa, wc, wo)


MOE_TOK = 512
MOE_CHUNK = 16
MOE_TM = 512
MOE_TOPK = 2
_MOE_WORST_ROWS = MOE_TOPK * MOE_TOK + N_EXPERTS * (MOE_CHUNK - 1)
MOE_ROWS = -(-_MOE_WORST_ROWS // LANES) * LANES
MOE_CPT = MOE_ROWS // MOE_CHUNK
MOE_CPM = MOE_TM // MOE_CHUNK
MOE_EXTRA = LANES


def _route_t(lt):
    row = lax.broadcasted_iota(jnp.int32, lt.shape, 0)
    rowf = row.astype(F32)
    big = float(lt.shape[0])
    is_grp = (row >= N_EXPERTS) & (row < N_EXPERTS + N_GROUPS)
    cl = jnp.where(is_grp, lt, -jnp.inf)
    mg = jnp.max(cl, axis=0, keepdims=True)
    g_star = jnp.min(jnp.where(cl == mg, rowf, big), axis=0, keepdims=True).astype(jnp.int32) - N_EXPERTS
    p_g = 1.0 / jnp.sum(jnp.where(is_grp, jnp.exp(cl - mg), 0.0), axis=0, keepdims=True)
    in_grp = (row < N_EXPERTS) & ((row // EXPERTS_PER_GROUP) == g_star)
    f1 = jnp.where(in_grp, lt, -jnp.inf)
    v1 = jnp.max(f1, axis=0, keepdims=True)
    i1 = jnp.min(jnp.where(f1 == v1, rowf, big), axis=0, keepdims=True).astype(jnp.int32)
    f2 = jnp.where(row == i1, -jnp.inf, f1)
    v2 = jnp.max(f2, axis=0, keepdims=True)
    i2 = jnp.min(jnp.where(f2 == v2, rowf, big), axis=0, keepdims=True).astype(jnp.int32)
    e2 = jnp.exp(v2 - v1)
    return i1, i2, p_g / (1.0 + e2), p_g * e2 / (1.0 + e2)


def _split_bf16(x):
    hi = x.astype(BF16)
    return hi, (x - hi.astype(F32)).astype(BF16)


def _moe_sort_kernel(h_ref, g_ref, wh_ref, wl_ref, br_ref, xs_ref, dest_ref, cnt_ref):
    tok = h_ref.shape[0]
    D = h_ref.shape[1]
    hn = _rms(h_ref[...], g_ref[...])
    hn_hi, hn_lo = _split_bf16(hn)
    nt_dot = lambda a, b: lax.dot_general(a, b, _NT, preferred_element_type=F32)
    w_hi = wh_ref[...]
    both = nt_dot(jnp.concatenate([w_hi, wl_ref[...]], axis=0), hn_hi)
    lt = both[:MOE_ROUTE_ROWS] + both[MOE_ROUTE_ROWS:] + nt_dot(w_hi, hn_lo) + br_ref[...]
    i1, i2, c1, c2 = _route_t(lt)

    erow = lax.broadcasted_iota(jnp.int32, (N_EXPERTS, tok), 0)
    oh1 = erow == i1
    oh2 = erow == i2
    oh = (oh1 | oh2).astype(F32)
    earlier = (lax.broadcasted_iota(jnp.int32, (tok, tok), 0) < lax.broadcasted_iota(jnp.int32, (tok, tok), 1))
    cum = jnp.dot(oh.astype(BF16), earlier.astype(BF16), preferred_element_type=F32)
    cnt = jnp.sum(oh, axis=1, keepdims=True)
    n_chunks = jnp.floor((cnt + (MOE_CHUNK - 1)) * (1.0 / MOE_CHUNK))
    ncb = jnp.broadcast_to(n_chunks, (N_EXPERTS, LANES))
    erow_l = lax.broadcasted_iota(jnp.int32, (N_EXPERTS, LANES), 0)
    first_chunk = jnp.zeros((N_EXPERTS, LANES), F32)
    for e in range(N_EXPERTS - 1):
        first_chunk = first_chunk + jnp.where(erow_l > e, ncb[e:e + 1, :], 0.0)
    base = jnp.tile(first_chunk, (1, tok // LANES)) * MOE_CHUNK + cum
    d1 = jnp.sum(jnp.where(oh1, base, 0.0), axis=0, keepdims=True).astype(jnp.int32)
    d2 = jnp.sum(jnp.where(oh2, base, 0.0), axis=0, keepdims=True).astype(jnp.int32)

    srow = lax.broadcasted_iota(jnp.int32, (MOE_ROWS, tok), 0)
    p1 = srow == d1
    p2 = srow == d2
    perm1 = p1.astype(BF16)
    perm2 = p2.astype(BF16)
    xs_ref[:, :D] = jnp.dot(perm1 + perm2, hn_hi, preferred_element_type=F32).astype(BF16)
    xrow = lax.broadcasted_iota(jnp.int32, (MOE_EXTRA, tok), 0)

    def weight_rows(c):
        hi = c.astype(BF16).astype(F32)
        return jnp.where(xrow == 0, hi, jnp.where(xrow == 1, c - hi, 0.0)).astype(BF16)

    xs_ref[:, D:] = (nt_dot(perm1, weight_rows(c1)) + nt_dot(perm2, weight_rows(c2))).astype(BF16)

    r8 = lax.broadcasted_iota(jnp.int32, (SUBLANES, tok), 0)
    dest_ref[0] = jnp.where(r8 == 0, d1, jnp.where(r8 == 1, d2, 0))
    cnt_ref[0] = jnp.broadcast_to(cnt, (N_EXPERTS, LANES)).astype(jnp.int32)


MOE_ROUTE_ROWS = 32


def _moe_sort(h1, g, w_router_e, b_router_e, w_router_g, b_router_g):
    T, D = h1.shape
    nt = T // MOE_TOK
    pad = MOE_ROUTE_ROWS - N_EXPERTS - N_GROUPS
    wr = jnp.concatenate([w_router_e.T, w_router_g.T, jnp.zeros((pad, D), F32)], axis=0)
    wr_hi = wr.astype(BF16)
    wr_lo = (wr - wr_hi.astype(F32)).astype(BF16)
    br = jnp.concatenate([b_router_e, b_router_g, jnp.zeros((pad,), F32)])[:, None]
    return pl.pallas_call(
        _moe_sort_kernel,
        grid=(nt,),
        in_specs=[pl.BlockSpec((MOE_TOK, D), lambda i: (i, 0)),
                  pl.BlockSpec((1, D), lambda i: (0, 0)),
                  pl.BlockSpec((MOE_ROUTE_ROWS, D), lambda i: (0, 0)),
                  pl.BlockSpec((MOE_ROUTE_ROWS, D), lambda i: (0, 0)),
                  pl.BlockSpec((MOE_ROUTE_ROWS, 1), lambda i: (0, 0))],
        out_specs=[pl.BlockSpec((MOE_ROWS, D + MOE_EXTRA), lambda i: (i, 0)),
                   pl.BlockSpec((1, SUBLANES, MOE_TOK), lambda i: (i, 0, 0)),
                   pl.BlockSpec((1, N_EXPERTS, LANES), lambda i: (i, 0, 0))],
        out_shape=[jax.ShapeDtypeStruct((nt * MOE_ROWS, D + MOE_EXTRA), BF16),
                   jax.ShapeDtypeStruct((nt, SUBLANES, MOE_TOK), jnp.int32),
                   jax.ShapeDtypeStruct((nt, N_EXPERTS, LANES), jnp.int32)],
        compiler_params=_cparams("parallel"),
        name="moe_sort",
    )(h1, g, wr_hi, wr_lo, br)


def _moe_tables(cnt, n_mm_tiles):
    nt = cnt.shape[0]
    pc = (cnt + (MOE_CHUNK - 1)) // MOE_CHUNK
    run_end = jnp.cumsum(pc, axis=1)
    run_first = run_end - pc
    per_expert = jnp.sum(pc, axis=0)
    mm_tiles = (per_expert + (MOE_CPM - 1)) // MOE_CPM
    mm_end = jnp.cumsum(mm_tiles)
    slot_base = (mm_end - mm_tiles)[None, :] * MOE_CPM + (jnp.cumsum(pc, axis=0) - pc)
    k = jnp.arange(MOE_CPT, dtype=jnp.int32)[None, :, None]
    expert_of = jnp.sum((run_end[:, None, :] <= k).astype(jnp.int32), axis=2)
    valid = expert_of < N_EXPERTS
    pick = expert_of[:, :, None] == jnp.arange(N_EXPERTS, dtype=jnp.int32)
    take = lambda a: jnp.sum(jnp.where(pick, a[:, None, :], 0), axis=2)
    slot = take(slot_base) + (k[:, :, 0] - take(run_first))
    src = (jnp.arange(nt, dtype=jnp.int32)[:, None] * MOE_CPT + k[:, :, 0]).reshape(-1)
    slot = jnp.where(valid, slot, n_mm_tiles * MOE_CPM).reshape(-1)
    gather_tbl = jnp.zeros((n_mm_tiles * MOE_CPM,), jnp.int32).at[slot].set(src, mode="drop")
    return_tbl = jnp.where(valid.reshape(-1), slot, 0).astype(jnp.int32)
    g = jnp.arange(n_mm_tiles, dtype=jnp.int32)
    n_used = mm_end[-1].astype(jnp.int32)
    tile_expert = jnp.sum((mm_end[None, :] <= jnp.minimum(g, n_used - 1)[:, None]).astype(jnp.int32), axis=1)
    return gather_tbl, return_tbl, tile_expert.astype(jnp.int32), n_used.reshape(1)


def _chunk_gather(tbl_ref, src_hbm, buf, sem, chunks_per_tile):
    def chunk_copy(tile, c, slot):
        src = pl.multiple_of(tbl_ref[tile * chunks_per_tile + c] * MOE_CHUNK, MOE_CHUNK)
        dst = pl.multiple_of(c * MOE_CHUNK, MOE_CHUNK)
        return pltpu.make_async_copy(src_hbm.at[pl.ds(src, MOE_CHUNK), :],
                                     buf.at[slot, pl.ds(dst, MOE_CHUNK), :], sem.at[slot])

    def for_each_chunk(tile, slot, act):
        def body(c, carry):
            act(chunk_copy(tile, c, slot))
            return carry
        lax.fori_loop(0, chunks_per_tile, body, 0)

    def start(tile, slot):
        for_each_chunk(tile, slot, lambda copy: copy.start())

    def wait(tile, slot):
        for_each_chunk(tile, slot, lambda copy: copy.wait())

    return start, wait


def _moe_expert_kernel(gt_ref, te_ref, nu_ref, xs_hbm, wg_ref, wu_ref, wd_ref, o_ref,
                       wg_bf, wu_bf, wd_bf, x_buf, x_sem):
    D = o_ref.shape[1]
    g = pl.program_id(0)
    n_used = nu_ref[0]
    used = g < n_used

    start_gather, wait_gather = _chunk_gather(gt_ref, xs_hbm, x_buf, x_sem, MOE_CPM)

    @pl.when((g == 0) & (n_used > 0))
    def _():
        start_gather(0, 0)

    @pl.when(g + 1 < n_used)
    def _():
        start_gather(g + 1, (g + 1) % 2)

    @pl.when(used & ((g == 0) | (te_ref[g] != te_ref[jnp.maximum(g - 1, 0)])))
    def _():
        wg_bf[...] = wg_ref[0].astype(BF16)
        wu_bf[...] = wu_ref[0].astype(BF16)
        wd_bf[...] = wd_ref[0].astype(BF16)

    @pl.when(used)
    def _():
        slot = g % 2
        wait_gather(g, slot)
        xs = x_buf[slot]
        x = xs[:, :D]
        extra = xs[:, D:].astype(F32)
        c = extra[:, 0:1] + extra[:, 1:2]
        gate = jnp.dot(x, wg_bf[...], preferred_element_type=F32)
        up = jnp.dot(x, wu_bf[...], preferred_element_type=F32)
        hh = (gate * jax.nn.sigmoid(gate)) * up * c
        o_ref[...] = jnp.dot(hh.astype(BF16), wd_bf[...], preferred_element_type=F32).astype(o_ref.dtype)

    @pl.when(jnp.logical_not(used))
    def _():
        o_ref[...] = jnp.zeros(o_ref.shape, o_ref.dtype)


def _moe_experts(xs, gather_tbl, tile_expert, n_used, wg, wu, wd, n_mm_tiles):
    W = xs.shape[1]
    E, D, FF = wg.shape

    grid_spec = pltpu.PrefetchScalarGridSpec(
        num_scalar_prefetch=3,
        grid=(n_mm_tiles,),
        in_specs=[pl.BlockSpec(memory_space=pl.ANY),
                  pl.BlockSpec((1, D, FF), lambda g, gt, te, nu: (te[g], 0, 0)),
                  pl.BlockSpec((1, D, FF), lambda g, gt, te, nu: (te[g], 0, 0)),
                  pl.BlockSpec((1, FF, D), lambda g, gt, te, nu: (te[g], 0, 0))],
        out_specs=pl.BlockSpec((MOE_TM, D), lambda g, gt, te, nu: (g, 0)),
        scratch_shapes=[pltpu.VMEM((D, FF), BF16), pltpu.VMEM((D, FF), BF16), pltpu.VMEM((FF, D), BF16),
                        pltpu.VMEM((2, MOE_TM, W), BF16), pltpu.SemaphoreType.DMA((2,))],
    )
    return pl.pallas_call(
        _moe_expert_kernel,
        grid_spec=grid_spec,
        out_shape=jax.ShapeDtypeStruct((n_mm_tiles * MOE_TM, D), BF16),
        compiler_params=_cparams("arbitrary"),
        name="moe_experts",
    )(gather_tbl, tile_expert, n_used, xs, wg, wu, wd)


def _finish_kernel(rt_ref, ys_hbm, dest_ref, h_ref, p_ref, gp_ref, gf_ref, wg_ref, wp_ref, o_ref, y_buf, y_sem):
    tok = h_ref.shape[0]
    i = pl.program_id(0)
    slot = i % 2
    start_gather, wait_gather = _chunk_gather(rt_ref, ys_hbm, y_buf, y_sem, MOE_CPT)

    @pl.when(i == 0)
    def _():
        start_gather(0, 0)

    @pl.when(i + 1 < pl.num_programs(0))
    def _():
        start_gather(i + 1, (i + 1) % 2)

    wait_gather(i, slot)
    dcol = dest_ref[0].astype(F32).T
    lane = lax.broadcasted_iota(jnp.int32, (tok, MOE_ROWS), 1).astype(F32)
    unsort = ((lane == dcol[:, 0:1]) | (lane == dcol[:, 1:2])).astype(BF16)
    h = h_ref[...] + jnp.dot(unsort, y_buf[slot], preferred_element_type=F32)
    t = _rms(h, gp_ref[...]).astype(BF16)
    gate = jax.nn.sigmoid(jnp.dot(t, wg_ref[...], preferred_element_type=F32))
    emb = jnp.dot(p_ref[...].astype(BF16), wp_ref[...], preferred_element_type=F32)
    o_ref[...] = _rms(h + emb * gate, gf_ref[...])


def _finish(ys, return_tbl, dest, h1, p2, g_ple, g_final, wg, wp):
    T, D = h1.shape
    P = p2.shape[1]
    nt = T // MOE_TOK

    vec = pl.BlockSpec((1, D), lambda i, rt: (0, 0))
    grid_spec = pltpu.PrefetchScalarGridSpec(
        num_scalar_prefetch=1,
        grid=(nt,),
        in_specs=[
            pl.BlockSpec(memory_space=pl.ANY),
            pl.BlockSpec((1, SUBLANES, MOE_TOK), lambda i, rt: (i, 0, 0)),
            pl.BlockSpec((MOE_TOK, D), lambda i, rt: (i, 0)),
            pl.BlockSpec((MOE_TOK, P), lambda i, rt: (i, 0)),
            vec, vec,
            pl.BlockSpec((D, D), lambda i, rt: (0, 0), pipeline_mode=pl.Buffered(1)),
            pl.BlockSpec((P, D), lambda i, rt: (0, 0), pipeline_mode=pl.Buffered(1))],
        out_specs=pl.BlockSpec((MOE_TOK, D), lambda i, rt: (i, 0)),
        scratch_shapes=[pltpu.VMEM((2, MOE_ROWS, D), BF16), pltpu.SemaphoreType.DMA((2,))],
    )
    return pl.pallas_call(
        _finish_kernel,
        grid_spec=grid_spec,
        out_shape=jax.ShapeDtypeStruct((T, D), F32),
        compiler_params=_cparams("arbitrary"),
        name="finish",
    )(return_tbl, ys, dest, h1, p2, g_ple, g_final, wg, wp)


def kernel(x, p, g_mix, w_in, rel_bias, w_attn_br, conv_w, conv_b, ln_g, ln_b, w_conv_br, w_o, g_ffn,
           w_router_g, b_router_g, w_router_e, b_router_e, w_e_gate, w_e_up, w_e_down, g_ple, w_ple_gate,
           w_ple_proj, g_final):
    B, S, D = x.shape
    T = B * S
    depth = w_in.shape[0]
    width = N_HEADS * HEAD_DIM
    assert D == width and conv_w.shape[2] == D, "column-block indexing assumes all branch widths equal d_model"
    assert S % MOBA_BLOCK == 0
    assert depth == 1, "the final RMSNorm is fused into the last stage of a single layer"
    a_blk = 3 * width // D
    gate_blk = a_blk + 2

    h = x.reshape(T, D)
    for i in range(depth):
        proj = _in_proj(h, g_mix[i][None], w_in[i].astype(BF16))
        proj3 = proj.reshape(B, S, -1)
        attn = _moba(proj3, rel_bias)
        conv, (wa, wc, wo, w_gate) = _conv(proj3, conv_w[i], conv_b[i][None], ln_g[i][None], ln_b[i][None], a_blk,
                                           side_weights=[w_attn_br[i], w_conv_br[i], w_o[i], w_ple_gate[i]])
        h = _mix(h, attn.reshape(T, D), conv.reshape(T, D), proj, wa, wc, wo, gate_blk)
        xs, dest, cnt = _moe_sort(h, g_ffn[i][None], w_router_e[i], b_router_e[i], w_router_g[i], b_router_g[i])
        n_tok_tiles = T // MOE_TOK
        worst_chunks = (MOE_TOPK * T + n_tok_tiles * N_EXPERTS * (MOE_CHUNK - 1)) // MOE_CHUNK
        n_mm_tiles = -(-worst_chunks // MOE_CPM) + N_EXPERTS
        gather_tbl, return_tbl, tile_expert, n_used = _moe_tables(cnt[:, :, 0], n_mm_tiles)
        ys = _moe_experts(xs, gather_tbl, tile_expert, n_used,
                          w_e_gate[i], w_e_up[i], w_e_down[i], n_mm_tiles)
        out = _finish(ys, return_tbl, dest, h, p[i].reshape(T, -1), g_ple[i][None], g_final[None],
                      w_gate, w_ple_proj[i].astype(BF16))
    return out.reshape(B, S, D)
```

```python
import functools
import math

import numpy as np
import jax
import jax.numpy as jnp
from jax import lax
from jax.experimental import pallas as pl
from jax.experimental.pallas import tpu as pltpu

F32 = jnp.float32
BF16 = jnp.bfloat16

N_HEADS = 16
HEAD_DIM = 128
MOBA_BLOCK = 256
MOBA_TOPK = 3
N_BUCKETS = 32
MAX_DISTANCE = 128
CONV_KERNEL = 31
N_GROUPS = 4
EXPERTS_PER_GROUP = 4
N_EXPERTS = N_GROUPS * EXPERTS_PER_GROUP
EPS = 1e-6
NEG = -1e30
LOG2E = 1.4426950408889634

V7X_VMEM_LIMIT_BYTES = 56 * 1024 * 1024
LANES = 128
SUBLANES = 8

_NT = (((1,), (1,)), ((), ()))


def _cparams(*sem):
    return pltpu.CompilerParams(dimension_semantics=sem, vmem_limit_bytes=V7X_VMEM_LIMIT_BYTES)


def _resident(shape):
    return pl.BlockSpec(shape, lambda *_: (0,) * len(shape), pipeline_mode=pl.Buffered(1))


def _sigmoid(x):
    return 0.5 * jnp.tanh(0.5 * x) + 0.5


def _rms(x, g):
    return x * lax.rsqrt(jnp.mean(x * x, axis=-1, keepdims=True) + EPS) * g


IN_PROJ_NORM_SPLIT = 4


def _in_proj_kernel(x_ref, g_ref, w_ref, o_ref, xn_ref):
    first = pl.program_id(1) == 0

    @pl.when(first)
    def _():
        sub = x_ref.shape[0] // IN_PROJ_NORM_SPLIT
        for r in range(IN_PROJ_NORM_SPLIT):
            rows = slice(r * sub, (r + 1) * sub)
            xn = _rms(x_ref[rows, :], g_ref[...]).astype(BF16)
            xn_ref[rows, :] = xn
            o_ref[rows, :] = jnp.dot(xn, w_ref[...], preferred_element_type=F32).astype(o_ref.dtype)

    @pl.when(jnp.logical_not(first))
    def _():
        o_ref[...] = jnp.dot(xn_ref[...], w_ref[...], preferred_element_type=F32).astype(o_ref.dtype)


def _in_proj(x2, g, w, tm=1024, tn=2048):
    T, D = x2.shape
    N = w.shape[1]
    return pl.pallas_call(
        _in_proj_kernel,
        grid=(T // tm, N // tn),
        in_specs=[pl.BlockSpec((tm, D), lambda i, j: (i, 0)),
                  pl.BlockSpec((1, D), lambda i, j: (0, 0)),
                  pl.BlockSpec((D, tn), lambda i, j: (0, j))],
        out_specs=pl.BlockSpec((tm, tn), lambda i, j: (i, j)),
        out_shape=jax.ShapeDtypeStruct((T, N), BF16),
        scratch_shapes=[pltpu.VMEM((tm, D), BF16)],
        compiler_params=_cparams("parallel", "arbitrary"),
        name="in_proj",
    )(x2, g, w)


MOBA_PV_ROWS = HEAD_DIM + 16
MOBA_Q_LANES = MOBA_BLOCK
MOBA_HEADS_PER_STEP = 4


def _rel_buckets_t():
    k = np.arange(MOBA_BLOCK)[:, None]
    q = np.arange(MOBA_BLOCK)[None, :]
    dist = np.stack([q - k, MOBA_BLOCK + q - k]).astype(np.int32)
    n = np.maximum(dist, 0)
    max_exact = N_BUCKETS // 2
    nf = np.maximum(n, 1).astype(np.float32)
    large = max_exact + (np.log(nf / np.float32(max_exact)) / np.float32(math.log(MAX_DISTANCE / max_exact))
                         * np.float32(N_BUCKETS - max_exact)).astype(np.int32)
    large = np.minimum(large, N_BUCKETS - 1)
    return np.where(n < max_exact, n, large).astype(np.int32)


def _moba_kernel(tbl_ref, bkt_ref, q_ref, k_ref, v_ref, o_ref,
                 kmean_ref, vt_ref, bias_ref, qt_ref, sel_ref, s_ref, p_ref, m_ref, a_ref, acc_ref,
                 *, n_blocks):
    hg = pl.program_id(0)
    b = pl.program_id(1)
    BS = MOBA_BLOCK
    HPS = MOBA_HEADS_PER_STEP
    lanes = [slice(hh * HEAD_DIM, (hh + 1) * HEAD_DIM) for hh in range(HPS)]

    @pl.when(b == 0)
    def _build_bias():
        krow = lax.broadcasted_iota(jnp.int32, (BS, BS), 0)
        qcol = lax.broadcasted_iota(jnp.int32, (BS, BS), 1)
        for hh in range(HPS):
            bias_ref[hh] = jnp.zeros((2, BS, BS), F32)

            def fill(n, carry):
                bias_ref[hh] = jnp.where(bkt_ref[...] == n, tbl_ref[hg * HPS + hh, n] * LOG2E, bias_ref[hh])
                return carry

            lax.fori_loop(0, N_BUCKETS, fill, 0)
            bias_ref[hh, 0] = jnp.where(qcol >= krow, bias_ref[hh, 0], NEG)

    ones_row = (lax.broadcasted_iota(jnp.int32, (MOBA_PV_ROWS - HEAD_DIM, BS), 0) == 0).astype(BF16)

    def prep(n, carry):
        rows = pl.ds(pl.multiple_of(n * BS, BS), BS)
        for hh in range(HPS):
            kmean_ref[hh, pl.ds(n, 1), :] = jnp.mean(k_ref[0, rows, lanes[hh]].astype(F32), axis=0, keepdims=True)
            vt_ref[hh, n, :HEAD_DIM] = v_ref[0, rows, lanes[hh]].astype(F32).T.astype(BF16)
            vt_ref[hh, n, HEAD_DIM:] = ones_row
        return carry

    lax.fori_loop(0, n_blocks, prep, 0)

    q_block = functools.partial(_moba_q_block, hg=hg, tbl_ref=tbl_ref, q_ref=q_ref, k_ref=k_ref, o_ref=o_ref,
                                kmean_ref=kmean_ref, vt_ref=vt_ref, bias_ref=bias_ref, qt_ref=qt_ref,
                                sel_ref=sel_ref, s_ref=s_ref, p_ref=p_ref, m_ref=m_ref, a_ref=a_ref,
                                acc_ref=acc_ref, n_blocks=n_blocks)

    q_block(jnp.int32(0), finish_previous=False)

    def next_block(qi, carry):
        q_block(qi, finish_previous=True)
        return carry

    lax.fori_loop(1, n_blocks, next_block, 0)
    _moba_close(jnp.int32(n_blocks - 1), vt_ref=vt_ref, p_ref=p_ref, a_ref=a_ref, acc_ref=acc_ref, o_ref=o_ref)


def _moba_close(qi, *, vt_ref, p_ref, a_ref, acc_ref, o_ref):
    lanes = [slice(hh * HEAD_DIM, (hh + 1) * HEAD_DIM) for hh in range(MOBA_HEADS_PER_STEP)]
    q_rows = pl.ds(pl.multiple_of(qi * MOBA_BLOCK, MOBA_BLOCK), MOBA_BLOCK)
    for hh in range(MOBA_HEADS_PER_STEP):
        acc = a_ref[hh] * acc_ref[hh] + jnp.dot(vt_ref[hh, 0], p_ref[hh], preferred_element_type=F32)
        denom = acc[HEAD_DIM:HEAD_DIM + 1, :]
        out_t = acc[:HEAD_DIM, :] * (1.0 / denom)
        o_ref[0, q_rows, lanes[hh]] = out_t.T.astype(o_ref.dtype)


def _moba_q_block(qi, *, finish_previous, hg, tbl_ref, q_ref, k_ref, o_ref, kmean_ref, vt_ref, bias_ref, qt_ref,
                  sel_ref, s_ref, p_ref, m_ref, a_ref, acc_ref, n_blocks):
    BS = MOBA_BLOCK
    HPS = MOBA_HEADS_PER_STEP
    scale = HEAD_DIM ** -0.5
    lanes = [slice(hh * HEAD_DIM, (hh + 1) * HEAD_DIM) for hh in range(HPS)]
    q_rows = pl.ds(pl.multiple_of(qi * BS, BS), BS)

    def qk_stage(hh, t):
        j0 = pl.multiple_of(jnp.maximum(qi - t, 0) * BS, BS)
        s_ref[hh] = jnp.dot(k_ref[0, pl.ds(j0, BS), lanes[hh]], qt_ref[hh], preferred_element_type=F32)

    def softmax_stage(hh, t, bias_of=None, far_bias=None):
        for c in range(BS // MOBA_Q_LANES):
            qs = slice(c * MOBA_Q_LANES, (c + 1) * MOBA_Q_LANES)
            chosen = (lax.shift_right_logical(sel_ref[hh, :, qs], qi - t) & 1) == 1
            c2 = scale * LOG2E
            if bias_of is not None:
                st = s_ref[hh, :, qs] * c2 + bias_of(qs)
                shift = 0.0
                top = jnp.max(st, axis=0, keepdims=True)
            else:
                st = None
                shift = far_bias
                top = jnp.max(s_ref[hh, :, qs], axis=0, keepdims=True) * c2 + shift
            m_prev = m_ref[hh, :, qs]
            m_new = jnp.where(chosen, jnp.maximum(m_prev, top), m_prev)
            a_ref[hh, :, qs] = jnp.exp2(m_prev - m_new)
            m_ref[hh, :, qs] = m_new
            cut = jnp.where(chosen, m_new - shift, -NEG)
            if st is None:
                st = s_ref[hh, :, qs] * c2
            p_ref[hh, :, qs] = jnp.exp2(st - cut).astype(BF16)

    def pv_stage(hh, t):
        acc_ref[hh] = a_ref[hh] * acc_ref[hh] + jnp.dot(vt_ref[hh, qi - t], p_ref[hh],
                                                        preferred_element_type=F32)

    if finish_previous:
        _moba_close(qi - 1, vt_ref=vt_ref, p_ref=p_ref, a_ref=a_ref, acc_ref=acc_ref, o_ref=o_ref)

    for hh in range(HPS):
        qt = q_ref[0, q_rows, lanes[hh]].astype(F32).T.astype(BF16)
        qt_ref[hh] = qt
        gate = jnp.dot(kmean_ref[hh].astype(BF16), qt, preferred_element_type=F32)
        blk = lax.broadcasted_iota(jnp.int32, gate.shape, 0)
        blkf = blk.astype(F32)
        g = jnp.where(blk < qi, gate, NEG)
        bits = jnp.full((1, BS), lax.shift_left(jnp.int32(1), qi), jnp.int32)
        for _ in range(MOBA_TOPK):
            mx = jnp.max(g, axis=0, keepdims=True)
            ix = jnp.min(jnp.where(g == mx, blkf, float(n_blocks)), axis=0, keepdims=True).astype(jnp.int32)
            bits = bits | jnp.where(ix < qi, lax.shift_left(jnp.int32(1), ix), 0)
            g = jnp.where(blk == ix, -jnp.inf, g)
        sel_ref[hh] = bits
        m_ref[hh] = jnp.full((1, BS), NEG, F32)
        acc_ref[hh] = jnp.zeros((MOBA_PV_ROWS, BS), F32)
        qk_stage(hh, 0)

    for hh in range(HPS):
        softmax_stage(hh, 0, lambda qs, hh=hh: bias_ref[hh, 0, :, qs])
    if not finish_previous:
        return

    for hh in range(HPS):
        qk_stage(hh, 1)
    for hh in range(HPS):
        pv_stage(hh, 0)
        softmax_stage(hh, 1, lambda qs, hh=hh: bias_ref[hh, 1, :, qs])
        qk_stage(hh, 2)

    def sweep(t):
        for hh in range(HPS):
            pv_stage(hh, t - 1)
            softmax_stage(hh, t, far_bias=tbl_ref[hg * HPS + hh, N_BUCKETS - 1] * LOG2E)
            qk_stage(hh, t + 1)

    n_far = qi - 1

    def sweep_pair(i, carry):
        sweep(2 + 2 * i)
        sweep(3 + 2 * i)
        return carry

    lax.fori_loop(0, n_far // 2, sweep_pair, 0)

    @pl.when(n_far % 2 == 1)
    def _odd_block():
        sweep(qi)


def _moba(proj3, rel_bias):
    B, S, _ = proj3.shape
    BS = MOBA_BLOCK
    HPS = MOBA_HEADS_PER_STEP
    W = HPS * HEAD_DIM
    nb = S // BS
    n_groups = N_HEADS // HPS
    bkt = jnp.asarray(_rel_buckets_t())
    tbl = rel_bias.astype(F32).T
    grid_spec = pltpu.PrefetchScalarGridSpec(
        num_scalar_prefetch=1,
        grid=(n_groups, B),
        in_specs=[pl.BlockSpec((2, BS, BS), lambda h, b, t: (0, 0, 0)),
                  pl.BlockSpec((1, S, W), lambda h, b, t: (b, 0, h)),
                  pl.BlockSpec((1, S, W), lambda h, b, t: (b, 0, n_groups + h)),
                  pl.BlockSpec((1, S, W), lambda h, b, t: (b, 0, 2 * n_groups + h))],
        out_specs=pl.BlockSpec((1, S, W), lambda h, b, t: (b, 0, h)),
        scratch_shapes=[pltpu.VMEM((HPS, nb, HEAD_DIM), F32),
                        pltpu.VMEM((HPS, nb, MOBA_PV_ROWS, BS), BF16),
                        pltpu.VMEM((HPS, 2, BS, BS), F32),
                        pltpu.VMEM((HPS, HEAD_DIM, BS), BF16),
                        pltpu.VMEM((HPS, 1, BS), jnp.int32),
                        pltpu.VMEM((HPS, BS, BS), F32),
                        pltpu.VMEM((HPS, BS, BS), BF16),
                        pltpu.VMEM((HPS, 1, BS), F32),
                        pltpu.VMEM((HPS, 1, BS), F32),
                        pltpu.VMEM((HPS, MOBA_PV_ROWS, BS), F32)],
    )
    return pl.pallas_call(
        functools.partial(_moba_kernel, n_blocks=nb),
        grid_spec=grid_spec,
        out_shape=jax.ShapeDtypeStruct((B, S, N_HEADS * HEAD_DIM), BF16),
        compiler_params=_cparams("arbitrary", "arbitrary"),
        name="moba",
    )(tbl, bkt, proj3, proj3, proj3)


CONV_HALO = 32
CONV_ROWS = 64
CONV_COLS = 256


def _conv_kernel(a_ref, g_ref, ah_ref, gh_ref, w_ref, cb_ref, lng_ref, lnb_ref, *refs, ts, n_side):
    side_in, (o_ref,), side_out = refs[:n_side], refs[n_side:n_side + 1], refs[n_side + 1:2 * n_side + 1]
    u_ref, sh_ref, c_ref = refs[2 * n_side + 1:]
    for src, dst in zip(side_in, side_out):
        dst[...] = src[...].astype(dst.dtype)

    i = pl.program_id(1)
    D = u_ref.shape[1]
    u_ref[CONV_HALO:, :] = a_ref[0].astype(F32) * _sigmoid(g_ref[0].astype(F32))
    uh = ah_ref[0].astype(F32) * _sigmoid(gh_ref[0].astype(F32))
    u_ref[:CONV_HALO, :] = jnp.where(i > 0, uh, 0.0)

    first = CONV_HALO - (CONV_KERNEL - 1)
    sh_rows = sh_ref.shape[1]

    def col_body(c, carry):
        cols = pl.ds(pl.multiple_of(c * CONV_COLS, CONV_COLS), CONV_COLS)
        for s in range(1, SUBLANES):
            sh_ref[s] = u_ref[s:s + sh_rows, cols]
        for r in range(ts // CONV_ROWS):
            acc = jnp.zeros((CONV_ROWS, CONV_COLS), F32)
            for k in range(CONV_KERNEL):
                s = (first + k) % SUBLANES
                r0 = r * CONV_ROWS + first + k - s
                tap = u_ref[r0:r0 + CONV_ROWS, cols] if s == 0 else sh_ref[s, r0:r0 + CONV_ROWS, :]
                acc = acc + jnp.tile(w_ref[k, :, cols], (CONV_ROWS // SUBLANES, 1)) * tap
            c_ref[r * CONV_ROWS:(r + 1) * CONV_ROWS, cols] = acc + cb_ref[:, cols]
        return carry

    lax.fori_loop(0, D // CONV_COLS, col_body, 0)

    y = c_ref[...]
    mu = jnp.mean(y, axis=-1, keepdims=True)
    yc = y - mu
    var = jnp.mean(yc * yc, axis=-1, keepdims=True)
    z = yc * lax.rsqrt(var + EPS) * lng_ref[...] + lnb_ref[...]
    o_ref[0] = (z * _sigmoid(z)).astype(o_ref.dtype)


def _conv(proj3, conv_w, conv_b, ln_g, ln_b, a_blk, side_weights, ts=512):
    B, S, _ = proj3.shape
    D = conv_w.shape[1]
    hpb = ts // CONV_HALO
    steps_per_seq = S // ts
    n_steps = B * steps_per_seq
    side_specs = [pl.BlockSpec((w.shape[0] // n_steps, w.shape[1]), lambda b, i: (b * steps_per_seq + i, 0))
                  for w in side_weights]
    w_rows = jnp.broadcast_to(conv_w[:, None, :], (CONV_KERNEL, SUBLANES, D))

    def halo(col):
        return pl.BlockSpec((1, CONV_HALO, D), lambda b, i: (b, jnp.maximum(i * hpb - 1, 0), col))

    vec = pl.BlockSpec((1, D), lambda b, i: (0, 0))
    conv_act, *side_bf16 = pl.pallas_call(
        functools.partial(_conv_kernel, ts=ts, n_side=len(side_weights)),
        grid=(B, steps_per_seq),
        in_specs=[pl.BlockSpec((1, ts, D), lambda b, i: (b, i, a_blk)),
                  pl.BlockSpec((1, ts, D), lambda b, i: (b, i, a_blk + 1)),
                  halo(a_blk), halo(a_blk + 1),
                  pl.BlockSpec((CONV_KERNEL, SUBLANES, D), lambda b, i: (0, 0, 0)),
                  vec, vec, vec] + side_specs,
        out_specs=[pl.BlockSpec((1, ts, D), lambda b, i: (b, i, 0))] + side_specs,
        out_shape=[jax.ShapeDtypeStruct((B, S, D), BF16)]
        + [jax.ShapeDtypeStruct(w.shape, BF16) for w in side_weights],
        scratch_shapes=[pltpu.VMEM((CONV_HALO + ts, D), F32),
                        pltpu.VMEM((SUBLANES, CONV_HALO + ts - SUBLANES, CONV_COLS), F32),
                        pltpu.VMEM((ts, D), F32)],
        compiler_params=_cparams("parallel", "arbitrary"),
        name="conv",
    )(proj3, proj3, proj3, proj3, w_rows, conv_b, ln_g, ln_b, *side_weights)
    return conv_act, side_bf16


def _mix_kernel(x_ref, at_ref, cv_ref, ga_ref, gc_ref, wa_ref, wc_ref, wo_ref, o_ref):
    ya = jnp.dot(at_ref[...], wa_ref[...], preferred_element_type=F32)
    yc = jnp.dot(cv_ref[...], wc_ref[...], preferred_element_type=F32)
    mixed = (jax.nn.sigmoid(ga_ref[...].astype(F32)) * ya + jax.nn.sigmoid(gc_ref[...].astype(F32)) * yc)
    o_ref[...] = x_ref[...] + jnp.dot(mixed.astype(BF16), wo_ref[...], preferred_element_type=F32)


def _mix(x2, attn2, conv2, proj2, wa, wc, wo, gate_blk, tm=256):
    T, D = x2.shape
    row = lambda col: pl.BlockSpec((tm, D), lambda i: (i, col))
    return pl.pallas_call(
        _mix_kernel,
        grid=(T // tm,),
        in_specs=[row(0), row(0), row(0), row(gate_blk), row(gate_blk + 1),
                  _resident((D, D)), _resident((D, D)), _resident((D, D))],
        out_specs=row(0),
        out_shape=jax.ShapeDtypeStruct((T, D), F32),
        compiler_params=_cparams("parallel"),
        name="mix",
    )(x2, attn2, conv2, proj2, proj2, wa, wc, wo)


MOE_TOK = 512
MOE_CHUNK = 16
MOE_TM = 512
MOE_TOPK = 2
_MOE_WORST_ROWS = MOE_TOPK * MOE_TOK + N_EXPERTS * (MOE_CHUNK - 1)
MOE_ROWS = -(-_MOE_WORST_ROWS // LANES) * LANES
MOE_CPT = MOE_ROWS // MOE_CHUNK
MOE_CPM = MOE_TM // MOE_CHUNK
MOE_EXTRA = LANES


def _route_t(lt):
    row = lax.broadcasted_iota(jnp.int32, lt.shape, 0)
    rowf = row.astype(F32)
    big = float(lt.shape[0])
    is_grp = (row >= N_EXPERTS) & (row < N_EXPERTS + N_GROUPS)
    cl = jnp.where(is_grp, lt, -jnp.inf)
    mg = jnp.max(cl, axis=0, keepdims=True)
    g_star = jnp.min(jnp.where(cl == mg, rowf, big), axis=0, keepdims=True).astype(jnp.int32) - N_EXPERTS
    p_g = 1.0 / jnp.sum(jnp.where(is_grp, jnp.exp(cl - mg), 0.0), axis=0, keepdims=True)
    in_grp = (row < N_EXPERTS) & ((row // EXPERTS_PER_GROUP) == g_star)
    f1 = jnp.where(in_grp, lt, -jnp.inf)
    v1 = jnp.max(f1, axis=0, keepdims=True)
    i1 = jnp.min(jnp.where(f1 == v1, rowf, big), axis=0, keepdims=True).astype(jnp.int32)
    f2 = jnp.where(row == i1, -jnp.inf, f1)
    v2 = jnp.max(f2, axis=0, keepdims=True)
    i2 = jnp.min(jnp.where(f2 == v2, rowf, big), axis=0, keepdims=True).astype(jnp.int32)
    e2 = jnp.exp(v2 - v1)
    return i1, i2, p_g / (1.0 + e2), p_g * e2 / (1.0 + e2)


def _split_bf16(x):
    hi = x.astype(BF16)
    return hi, (x - hi.astype(F32)).astype(BF16)


def _moe_sort_kernel(h_ref, g_ref, wh_ref, wl_ref, br_ref, xs_ref, dest_ref, cnt_ref):
    tok = h_ref.shape[0]
    D = h_ref.shape[1]
    hn = _rms(h_ref[...], g_ref[...])
    hn_hi, hn_lo = _split_bf16(hn)
    nt_dot = lambda a, b: lax.dot_general(a, b, _NT, preferred_element_type=F32)
    w_hi = wh_ref[...]
    both = nt_dot(jnp.concatenate([w_hi, wl_ref[...]], axis=0), hn_hi)
    lt = both[:MOE_ROUTE_ROWS] + both[MOE_ROUTE_ROWS:] + nt_dot(w_hi, hn_lo) + br_ref[...]
    i1, i2, c1, c2 = _route_t(lt)

    erow = lax.broadcasted_iota(jnp.int32, (N_EXPERTS, tok), 0)
    oh1 = erow == i1
    oh2 = erow == i2
    oh = (oh1 | oh2).astype(F32)
    earlier = (lax.broadcasted_iota(jnp.int32, (tok, tok), 0) < lax.broadcasted_iota(jnp.int32, (tok, tok), 1))
    cum = jnp.dot(oh.astype(BF16), earlier.astype(BF16), preferred_element_type=F32)
    cnt = jnp.sum(oh, axis=1, keepdims=True)
    n_chunks = jnp.floor((cnt + (MOE_CHUNK - 1)) * (1.0 / MOE_CHUNK))
    ncb = jnp.broadcast_to(n_chunks, (N_EXPERTS, LANES))
    erow_l = lax.broadcasted_iota(jnp.int32, (N_EXPERTS, LANES), 0)
    first_chunk = jnp.zeros((N_EXPERTS, LANES), F32)
    for e in range(N_EXPERTS - 1):
        first_chunk = first_chunk + jnp.where(erow_l > e, ncb[e:e + 1, :], 0.0)
    base = jnp.tile(first_chunk, (1, tok // LANES)) * MOE_CHUNK + cum
    d1 = jnp.sum(jnp.where(oh1, base, 0.0), axis=0, keepdims=True).astype(jnp.int32)
    d2 = jnp.sum(jnp.where(oh2, base, 0.0), axis=0, keepdims=True).astype(jnp.int32)

    srow = lax.broadcasted_iota(jnp.int32, (MOE_ROWS, tok), 0)
    p1 = srow == d1
    p2 = srow == d2
    perm1 = p1.astype(BF16)
    perm2 = p2.astype(BF16)
    xs_ref[:, :D] = jnp.dot(perm1 + perm2, hn_hi, preferred_element_type=F32).astype(BF16)
    xrow = lax.broadcasted_iota(jnp.int32, (MOE_EXTRA, tok), 0)

    def weight_rows(c):
        hi = c.astype(BF16).astype(F32)
        return jnp.where(xrow == 0, hi, jnp.where(xrow == 1, c - hi, 0.0)).astype(BF16)

    xs_ref[:, D:] = (nt_dot(perm1, weight_rows(c1)) + nt_dot(perm2, weight_rows(c2))).astype(BF16)

    r8 = lax.broadcasted_iota(jnp.int32, (SUBLANES, tok), 0)
    dest_ref[0] = jnp.where(r8 == 0, d1, jnp.where(r8 == 1, d2, 0))
    cnt_ref[0] = jnp.broadcast_to(cnt, (N_EXPERTS, LANES)).astype(jnp.int32)


MOE_ROUTE_ROWS = 32


def _moe_sort(h1, g, w_router_e, b_router_e, w_router_g, b_router_g):
    T, D = h1.shape
    nt = T // MOE_TOK
    pad = MOE_ROUTE_ROWS - N_EXPERTS - N_GROUPS
    wr = jnp.concatenate([w_router_e.T, w_router_g.T, jnp.zeros((pad, D), F32)], axis=0)
    wr_hi = wr.astype(BF16)
    wr_lo = (wr - wr_hi.astype(F32)).astype(BF16)
    br = jnp.concatenate([b_router_e, b_router_g, jnp.zeros((pad,), F32)])[:, None]
    return pl.pallas_call(
        _moe_sort_kernel,
        grid=(nt,),
        in_specs=[pl.BlockSpec((MOE_TOK, D), lambda i: (i, 0)),
                  pl.BlockSpec((1, D), lambda i: (0, 0)),
                  pl.BlockSpec((MOE_ROUTE_ROWS, D), lambda i: (0, 0)),
                  pl.BlockSpec((MOE_ROUTE_ROWS, D), lambda i: (0, 0)),
                  pl.BlockSpec((MOE_ROUTE_ROWS, 1), lambda i: (0, 0))],
        out_specs=[pl.BlockSpec((MOE_ROWS, D + MOE_EXTRA), lambda i: (i, 0)),
                   pl.BlockSpec((1, SUBLANES, MOE_TOK), lambda i: (i, 0, 0)),
                   pl.BlockSpec((1, N_EXPERTS, LANES), lambda i: (i, 0, 0))],
        out_shape=[jax.ShapeDtypeStruct((nt * MOE_ROWS, D + MOE_EXTRA), BF16),
                   jax.ShapeDtypeStruct((nt, SUBLANES, MOE_TOK), jnp.int32),
                   jax.ShapeDtypeStruct((nt, N_EXPERTS, LANES), jnp.int32)],
        compiler_params=_cparams("parallel"),
        name="moe_sort",
    )(h1, g, wr_hi, wr_lo, br)


def _moe_tables(cnt, n_mm_tiles):
    nt = cnt.shape[0]
    pc = (cnt + (MOE_CHUNK - 1)) // MOE_CHUNK
    run_end = jnp.cumsum(pc, axis=1)
    run_first = run_end - pc
    per_expert = jnp.sum(pc, axis=0)
    mm_tiles = (per_expert + (MOE_CPM - 1)) // MOE_CPM
    mm_end = jnp.cumsum(mm_tiles)
    slot_base = (mm_end - mm_tiles)[None, :] * MOE_CPM + (jnp.cumsum(pc, axis=0) - pc)
    k = jnp.arange(MOE_CPT, dtype=jnp.int32)[None, :, None]
    expert_of = jnp.sum((run_end[:, None, :] <= k).astype(jnp.int32), axis=2)
    valid = expert_of < N_EXPERTS
    pick = expert_of[:, :, None] == jnp.arange(N_EXPERTS, dtype=jnp.int32)
    take = lambda a: jnp.sum(jnp.where(pick, a[:, None, :], 0), axis=2)
    slot = take(slot_base) + (k[:, :, 0] - take(run_first))
    src = (jnp.arange(nt, dtype=jnp.int32)[:, None] * MOE_CPT + k[:, :, 0]).reshape(-1)
    slot = jnp.where(valid, slot, n_mm_tiles * MOE_CPM).reshape(-1)
    gather_tbl = jnp.zeros((n_mm_tiles * MOE_CPM,), jnp.int32).at[slot].set(src, mode="drop")
    return_tbl = jnp.where(valid.reshape(-1), slot, 0).astype(jnp.int32)
    g = jnp.arange(n_mm_tiles, dtype=jnp.int32)
    n_used = mm_end[-1].astype(jnp.int32)
    tile_expert = jnp.sum((mm_end[None, :] <= jnp.minimum(g, n_used - 1)[:, None]).astype(jnp.int32), axis=1)
    return gather_tbl, return_tbl, tile_expert.astype(jnp.int32), n_used.reshape(1)


def _chunk_gather(tbl_ref, src_hbm, buf, sem, chunks_per_tile):
    def chunk_copy(tile, c, slot):
        src = pl.multiple_of(tbl_ref[tile * chunks_per_tile + c] * MOE_CHUNK, MOE_CHUNK)
        dst = pl.multiple_of(c * MOE_CHUNK, MOE_CHUNK)
        return pltpu.make_async_copy(src_hbm.at[pl.ds(src, MOE_CHUNK), :],
                                     buf.at[slot, pl.ds(dst, MOE_CHUNK), :], sem.at[slot])

    def for_each_chunk(tile, slot, act):
        def body(c, carry):
            act(chunk_copy(tile, c, slot))
            return carry
        lax.fori_loop(0, chunks_per_tile, body, 0)

    def start(tile, slot):
        for_each_chunk(tile, slot, lambda copy: copy.start())

    def wait(tile, slot):
        for_each_chunk(tile, slot, lambda copy: copy.wait())

    return start, wait


def _moe_expert_kernel(gt_ref, te_ref, nu_ref, xs_hbm, wg_ref, wu_ref, wd_ref, o_ref,
                       wg_bf, wu_bf, wd_bf, x_buf, x_sem):
    D = o_ref.shape[1]
    g = pl.program_id(0)
    n_used = nu_ref[0]
    used = g < n_used

    start_gather, wait_gather = _chunk_gather(gt_ref, xs_hbm, x_buf, x_sem, MOE_CPM)

    @pl.when((g == 0) & (n_used > 0))
    def _():
        start_gather(0, 0)

    @pl.when(g + 1 < n_used)
    def _():
        start_gather(g + 1, (g + 1) % 2)

    @pl.when(used & ((g == 0) | (te_ref[g] != te_ref[jnp.maximum(g - 1, 0)])))
    def _():
        wg_bf[...] = wg_ref[0].astype(BF16)
        wu_bf[...] = wu_ref[0].astype(BF16)
        wd_bf[...] = wd_ref[0].astype(BF16)

    @pl.when(used)
    def _():
        slot = g % 2
        wait_gather(g, slot)
        xs = x_buf[slot]
        x = xs[:, :D]
        extra = xs[:, D:].astype(F32)
        c = extra[:, 0:1] + extra[:, 1:2]
        gate = jnp.dot(x, wg_bf[...], preferred_element_type=F32)
        up = jnp.dot(x, wu_bf[...], preferred_element_type=F32)
        hh = (gate * jax.nn.sigmoid(gate)) * up * c
        o_ref[...] = jnp.dot(hh.astype(BF16), wd_bf[...], preferred_element_type=F32).astype(o_ref.dtype)

    @pl.when(jnp.logical_not(used))
    def _():
        o_ref[...] = jnp.zeros(o_ref.shape, o_ref.dtype)


def _moe_experts(xs, gather_tbl, tile_expert, n_used, wg, wu, wd, n_mm_tiles):
    W = xs.shape[1]
    E, D, FF = wg.shape

    grid_spec = pltpu.PrefetchScalarGridSpec(
        num_scalar_prefetch=3,
        grid=(n_mm_tiles,),
        in_specs=[pl.BlockSpec(memory_space=pl.ANY),
                  pl.BlockSpec((1, D, FF), lambda g, gt, te, nu: (te[g], 0, 0)),
                  pl.BlockSpec((1, D, FF), lambda g, gt, te, nu: (te[g], 0, 0)),
                  pl.BlockSpec((1, FF, D), lambda g, gt, te, nu: (te[g], 0, 0))],
        out_specs=pl.BlockSpec((MOE_TM, D), lambda g, gt, te, nu: (g, 0)),
        scratch_shapes=[pltpu.VMEM((D, FF), BF16), pltpu.VMEM((D, FF), BF16), pltpu.VMEM((FF, D), BF16),
                        pltpu.VMEM((2, MOE_TM, W), BF16), pltpu.SemaphoreType.DMA((2,))],
    )
    return pl.pallas_call(
        _moe_expert_kernel,
        grid_spec=grid_spec,
        out_shape=jax.ShapeDtypeStruct((n_mm_tiles * MOE_TM, D), BF16),
        compiler_params=_cparams("arbitrary"),
        name="moe_experts",
    )(gather_tbl, tile_expert, n_used, xs, wg, wu, wd)


def _finish_kernel(rt_ref, ys_hbm, dest_ref, h_ref, p_ref, gp_ref, gf_ref, wg_ref, wp_ref, o_ref, y_buf, y_sem):
    tok = h_ref.shape[0]
    i = pl.program_id(0)
    slot = i % 2
    start_gather, wait_gather = _chunk_gather(rt_ref, ys_hbm, y_buf, y_sem, MOE_CPT)

    @pl.when(i == 0)
    def _():
        start_gather(0, 0)

    @pl.when(i + 1 < pl.num_programs(0))
    def _():
        start_gather(i + 1, (i + 1) % 2)

    wait_gather(i, slot)
    dcol = dest_ref[0].astype(F32).T
    lane = lax.broadcasted_iota(jnp.int32, (tok, MOE_ROWS), 1).astype(F32)
    unsort = ((lane == dcol[:, 0:1]) | (lane == dcol[:, 1:2])).astype(BF16)
    h = h_ref[...] + jnp.dot(unsort, y_buf[slot], preferred_element_type=F32)
    t = _rms(h, gp_ref[...]).astype(BF16)
    gate = jax.nn.sigmoid(jnp.dot(t, wg_ref[...], preferred_element_type=F32))
    emb = jnp.dot(p_ref[...].astype(BF16), wp_ref[...], preferred_element_type=F32)
    o_ref[...] = _rms(h + emb * gate, gf_ref[...])


def _finish(ys, return_tbl, dest, h1, p2, g_ple, g_final, wg, wp):
    T, D = h1.shape
    P = p2.shape[1]
    nt = T // MOE_TOK

    vec = pl.BlockSpec((1, D), lambda i, rt: (0, 0))
    grid_spec = pltpu.PrefetchScalarGridSpec(
        num_scalar_prefetch=1,
        grid=(nt,),
        in_specs=[
            pl.BlockSpec(memory_space=pl.ANY),
            pl.BlockSpec((1, SUBLANES, MOE_TOK), lambda i, rt: (i, 0, 0)),
            pl.BlockSpec((MOE_TOK, D), lambda i, rt: (i, 0)),
            pl.BlockSpec((MOE_TOK, P), lambda i, rt: (i, 0)),
            vec, vec,
            pl.BlockSpec((D, D), lambda i, rt: (0, 0), pipeline_mode=pl.Buffered(1)),
            pl.BlockSpec((P, D), lambda i, rt: (0, 0), pipeline_mode=pl.Buffered(1))],
        out_specs=pl.BlockSpec((MOE_TOK, D), lambda i, rt: (i, 0)),
        scratch_shapes=[pltpu.VMEM((2, MOE_ROWS, D), BF16), pltpu.SemaphoreType.DMA((2,))],
    )
    return pl.pallas_call(
        _finish_kernel,
        grid_spec=grid_spec,
        out_shape=jax.ShapeDtypeStruct((T, D), F32),
        compiler_params=_cparams("arbitrary"),
        name="finish",
    )(return_tbl, ys, dest, h1, p2, g_ple, g_final, wg, wp)


def kernel(x, p, g_mix, w_in, rel_bias, w_attn_br, conv_w, conv_b, ln_g, ln_b, w_conv_br, w_o, g_ffn,
           w_router_g, b_router_g, w_router_e, b_router_e, w_e_gate, w_e_up, w_e_down, g_ple, w_ple_gate,
           w_ple_proj, g_final):
    B, S, D = x.shape
    T = B * S
    depth = w_in.shape[0]
    width = N_HEADS * HEAD_DIM
    assert D == width and conv_w.shape[2] == D, "column-block indexing assumes all branch widths equal d_model"
    assert S % MOBA_BLOCK == 0
    assert depth == 1, "the final RMSNorm is fused into the last stage of a single layer"
    a_blk = 3 * width // D
    gate_blk = a_blk + 2

    h = x.reshape(T, D)
    for i in range(depth):
        proj = _in_proj(h, g_mix[i][None], w_in[i].astype(BF16))
        proj3 = proj.reshape(B, S, -1)
        attn = _moba(proj3, rel_bias)
        conv, (wa, wc, wo, w_gate) = _conv(proj3, conv_w[i], conv_b[i][None], ln_g[i][None], ln_b[i][None], a_blk,
                                           side_weights=[w_attn_br[i], w_conv_br[i], w_o[i], w_ple_gate[i]])
        h = _mix(h, attn.reshape(T, D), conv.reshape(T, D), proj, wa, wc, wo, gate_blk)
        xs, dest, cnt = _moe_sort(h, g_ffn[i][None], w_router_e[i], b_router_e[i], w_router_g[i], b_router_g[i])
        n_tok_tiles = T // MOE_TOK
        worst_chunks = (MOE_TOPK * T + n_tok_tiles * N_EXPERTS * (MOE_CHUNK - 1)) // MOE_CHUNK
        n_mm_tiles = -(-worst_chunks // MOE_CPM) + N_EXPERTS
        gather_tbl, return_tbl, tile_expert, n_used = _moe_tables(cnt[:, :, 0], n_mm_tiles)
        ys = _moe_experts(xs, gather_tbl, tile_expert, n_used,
                          w_e_gate[i], w_e_up[i], w_e_down[i], n_mm_tiles)
        out = _finish(ys, return_tbl, dest, h, p[i].reshape(T, -1), g_ple[i][None], g_final[None],
                      w_gate, w_ple_proj[i].astype(BF16))
    return out.reshape(B, S, D)
```

```python
import functools
import math

import numpy as np
import jax
import jax.numpy as jnp
from jax import lax
from jax.experimental import pallas as pl
from jax.experimental.pallas import tpu as pltpu

F32 = jnp.float32
BF16 = jnp.bfloat16

N_HEADS = 16
HEAD_DIM = 128
MOBA_BLOCK = 256
MOBA_TOPK = 3
N_BUCKETS = 32
MAX_DISTANCE = 128
CONV_KERNEL = 31
N_GROUPS = 4
EXPERTS_PER_GROUP = 4
N_EXPERTS = N_GROUPS * EXPERTS_PER_GROUP
EPS = 1e-6
NEG = -1e30
LOG2E = 1.4426950408889634

V7X_VMEM_LIMIT_BYTES = 56 * 1024 * 1024
LANES = 128
SUBLANES = 8

_NT = (((1,), (1,)), ((), ()))


def _cparams(*sem):
    return pltpu.CompilerParams(dimension_semantics=sem, vmem_limit_bytes=V7X_VMEM_LIMIT_BYTES)


def _resident(shape):
    return pl.BlockSpec(shape, lambda *_: (0,) * len(shape), pipeline_mode=pl.Buffered(1))


def _sigmoid(x):
    return 0.5 * jnp.tanh(0.5 * x) + 0.5


def _rms(x, g):
    return x * lax.rsqrt(jnp.mean(x * x, axis=-1, keepdims=True) + EPS) * g


IN_PROJ_NORM_SPLIT = 4


def _in_proj_kernel(x_ref, g_ref, w_ref, o_ref, xn_ref):
    first = pl.program_id(1) == 0

    @pl.when(first)
    def _():
        sub = x_ref.shape[0] // IN_PROJ_NORM_SPLIT
        for r in range(IN_PROJ_NORM_SPLIT):
            rows = slice(r * sub, (r + 1) * sub)
            xn = _rms(x_ref[rows, :], g_ref[...]).astype(BF16)
            xn_ref[rows, :] = xn
            o_ref[rows, :] = jnp.dot(xn, w_ref[...], preferred_element_type=F32).astype(o_ref.dtype)

    @pl.when(jnp.logical_not(first))
    def _():
        o_ref[...] = jnp.dot(xn_ref[...], w_ref[...], preferred_element_type=F32).astype(o_ref.dtype)


def _in_proj(x2, g, w, tm=1024, tn=2048):
    T, D = x2.shape
    N = w.shape[1]
    return pl.pallas_call(
        _in_proj_kernel,
        grid=(T // tm, N // tn),
        in_specs=[pl.BlockSpec((tm, D), lambda i, j: (i, 0)),
                  pl.BlockSpec((1, D), lambda i, j: (0, 0)),
                  pl.BlockSpec((D, tn), lambda i, j: (0, j))],
        out_specs=pl.BlockSpec((tm, tn), lambda i, j: (i, j)),
        out_shape=jax.ShapeDtypeStruct((T, N), BF16),
        scratch_shapes=[pltpu.VMEM((tm, D), BF16)],
        compiler_params=_cparams("parallel", "arbitrary"),
        name="in_proj",
    )(x2, g, w)


MOBA_PV_ROWS = HEAD_DIM + 16
MOBA_HEADS_PER_STEP = 4


def _rel_buckets_t():
    k = np.arange(MOBA_BLOCK)[:, None]
    q = np.arange(MOBA_BLOCK)[None, :]
    dist = np.stack([q - k, MOBA_BLOCK + q - k]).astype(np.int32)
    n = np.maximum(dist, 0)
    max_exact = N_BUCKETS // 2
    nf = np.maximum(n, 1).astype(np.float32)
    large = max_exact + (np.log(nf / np.float32(max_exact)) / np.float32(math.log(MAX_DISTANCE / max_exact))
                         * np.float32(N_BUCKETS - max_exact)).astype(np.int32)
    large = np.minimum(large, N_BUCKETS - 1)
    return np.where(n < max_exact, n, large).astype(np.int32)


def _moba_kernel(tbl_ref, bkt_ref, q_ref, k_ref, v_ref, o_ref,
                 kmean_ref, vt_ref, bias_ref, qt_ref, sel_ref, s_ref, p_ref, m_ref, a_ref, acc_ref,
                 *, n_blocks):
    hg = pl.program_id(0)
    b = pl.program_id(1)
    BS = MOBA_BLOCK
    HPS = MOBA_HEADS_PER_STEP
    lanes = [slice(hh * HEAD_DIM, (hh + 1) * HEAD_DIM) for hh in range(HPS)]

    @pl.when(b == 0)
    def _build_bias():
        krow = lax.broadcasted_iota(jnp.int32, (BS, BS), 0)
        qcol = lax.broadcasted_iota(jnp.int32, (BS, BS), 1)
        for hh in range(HPS):
            bias_ref[hh] = jnp.zeros((2, BS, BS), F32)

            def fill(n, carry):
                bias_ref[hh] = jnp.where(bkt_ref[...] == n, tbl_ref[hg * HPS + hh, n] * LOG2E, bias_ref[hh])
                return carry

            lax.fori_loop(0, N_BUCKETS, fill, 0)
            bias_ref[hh, 0] = jnp.where(qcol >= krow, bias_ref[hh, 0], NEG)

    ones_row = (lax.broadcasted_iota(jnp.int32, (MOBA_PV_ROWS - HEAD_DIM, BS), 0) == 0).astype(BF16)

    def prep(n, carry):
        rows = pl.ds(pl.multiple_of(n * BS, BS), BS)
        for hh in range(HPS):
            kmean_ref[hh, pl.ds(n, 1), :] = jnp.mean(k_ref[0, rows, lanes[hh]].astype(F32), axis=0, keepdims=True)
            vt_ref[hh, n, :HEAD_DIM] = v_ref[0, rows, lanes[hh]].astype(F32).T.astype(BF16)
            vt_ref[hh, n, HEAD_DIM:] = ones_row
        return carry

    lax.fori_loop(0, n_blocks, prep, 0)

    q_block = functools.partial(_moba_q_block, hg=hg, tbl_ref=tbl_ref, q_ref=q_ref, k_ref=k_ref, o_ref=o_ref,
                                kmean_ref=kmean_ref, vt_ref=vt_ref, bias_ref=bias_ref, qt_ref=qt_ref,
                                sel_ref=sel_ref, s_ref=s_ref, p_ref=p_ref, m_ref=m_ref, a_ref=a_ref,
                                acc_ref=acc_ref, n_blocks=n_blocks)

    q_block(jnp.int32(0), finish_previous=False)

    def next_block(qi, carry):
        q_block(qi, finish_previous=True)
        return carry

    lax.fori_loop(1, n_blocks, next_block, 0)
    _moba_close(jnp.int32(n_blocks - 1), vt_ref=vt_ref, p_ref=p_ref, a_ref=a_ref, acc_ref=acc_ref, o_ref=o_ref)


def _moba_close(qi, *, vt_ref, p_ref, a_ref, acc_ref, o_ref):
    lanes = [slice(hh * HEAD_DIM, (hh + 1) * HEAD_DIM) for hh in range(MOBA_HEADS_PER_STEP)]
    q_rows = pl.ds(pl.multiple_of(qi * MOBA_BLOCK, MOBA_BLOCK), MOBA_BLOCK)
    for hh in range(MOBA_HEADS_PER_STEP):
        acc = a_ref[hh] * acc_ref[hh] + jnp.dot(vt_ref[hh, 0], p_ref[hh], preferred_element_type=F32)
        denom = acc[HEAD_DIM:HEAD_DIM + 1, :]
        out_t = acc[:HEAD_DIM, :] * (1.0 / denom)
        o_ref[0, q_rows, lanes[hh]] = out_t.T.astype(o_ref.dtype)


def _moba_q_block(qi, *, finish_previous, hg, tbl_ref, q_ref, k_ref, o_ref, kmean_ref, vt_ref, bias_ref, qt_ref,
                  sel_ref, s_ref, p_ref, m_ref, a_ref, acc_ref, n_blocks):
    BS = MOBA_BLOCK
    HPS = MOBA_HEADS_PER_STEP
    scale = HEAD_DIM ** -0.5
    lanes = [slice(hh * HEAD_DIM, (hh + 1) * HEAD_DIM) for hh in range(HPS)]
    q_rows = pl.ds(pl.multiple_of(qi * BS, BS), BS)

    def qk_stage(hh, t):
        j0 = pl.multiple_of(jnp.maximum(qi - t, 0) * BS, BS)
        s_ref[hh] = jnp.dot(k_ref[0, pl.ds(j0, BS), lanes[hh]], qt_ref[hh], preferred_element_type=F32)

    def softmax_stage(hh, t, bias_of=None, far_bias=None):
        chosen = (lax.shift_right_logical(sel_ref[hh], qi - t) & 1) == 1
        c2 = scale * LOG2E
        if bias_of is not None:
            st = s_ref[hh] * c2 + bias_of[...]
            shift = 0.0
            top = jnp.max(st, axis=0, keepdims=True)
        else:
            st = None
            shift = far_bias
            top = jnp.max(s_ref[hh], axis=0, keepdims=True) * c2 + shift
        m_prev = m_ref[hh]
        m_new = jnp.where(chosen, jnp.maximum(m_prev, top), m_prev)
        a_ref[hh] = jnp.exp2(m_prev - m_new)
        m_ref[hh] = m_new
        cut = jnp.where(chosen, m_new - shift, -NEG)
        if st is None:
            st = s_ref[hh] * c2
        p_ref[hh] = jnp.exp2(st - cut).astype(BF16)

    def pv_stage(hh, t):
        acc_ref[hh] = a_ref[hh] * acc_ref[hh] + jnp.dot(vt_ref[hh, qi - t], p_ref[hh],
                                                        preferred_element_type=F32)

    if finish_previous:
        _moba_close(qi - 1, vt_ref=vt_ref, p_ref=p_ref, a_ref=a_ref, acc_ref=acc_ref, o_ref=o_ref)

    for hh in range(HPS):
        qt = q_ref[0, q_rows, lanes[hh]].astype(F32).T.astype(BF16)
        qt_ref[hh] = qt
        gate = jnp.dot(kmean_ref[hh].astype(BF16), qt, preferred_element_type=F32)
        blk = lax.broadcasted_iota(jnp.int32, gate.shape, 0)
        blkf = blk.astype(F32)
        g = jnp.where(blk < qi, gate, NEG)
        bits = jnp.full((1, BS), lax.shift_left(jnp.int32(1), qi), jnp.int32)
        for _ in range(MOBA_TOPK):
            mx = jnp.max(g, axis=0, keepdims=True)
            ix = jnp.min(jnp.where(g == mx, blkf, float(n_blocks)), axis=0, keepdims=True).astype(jnp.int32)
            bits = bits | jnp.where(ix < qi, lax.shift_left(jnp.int32(1), ix), 0)
            g = jnp.where(blk == ix, -jnp.inf, g)
        sel_ref[hh] = bits
        m_ref[hh] = jnp.full((1, BS), NEG, F32)
        acc_ref[hh] = jnp.zeros((MOBA_PV_ROWS, BS), F32)
        qk_stage(hh, 0)

    for hh in range(HPS):
        softmax_stage(hh, 0, bias_ref.at[hh, 0])
    if not finish_previous:
        return

    for hh in range(HPS):
        qk_stage(hh, 1)
    for hh in range(HPS):
        pv_stage(hh, 0)
        softmax_stage(hh, 1, bias_ref.at[hh, 1])
        qk_stage(hh, 2)

    def sweep(t):
        for hh in range(HPS):
            pv_stage(hh, t - 1)
            softmax_stage(hh, t, far_bias=tbl_ref[hg * HPS + hh, N_BUCKETS - 1] * LOG2E)
            qk_stage(hh, t + 1)

    n_far = qi - 1

    def sweep_pair(i, carry):
        sweep(2 + 2 * i)
        sweep(3 + 2 * i)
        return carry

    lax.fori_loop(0, n_far // 2, sweep_pair, 0)

    @pl.when(n_far % 2 == 1)
    def _odd_block():
        sweep(qi)


def _moba(proj3, rel_bias):
    B, S, _ = proj3.shape
    BS = MOBA_BLOCK
    HPS = MOBA_HEADS_PER_STEP
    W = HPS * HEAD_DIM
    nb = S // BS
    n_groups = N_HEADS // HPS
    bkt = jnp.asarray(_rel_buckets_t())
    tbl = rel_bias.astype(F32).T
    grid_spec = pltpu.PrefetchScalarGridSpec(
        num_scalar_prefetch=1,
        grid=(n_groups, B),
        in_specs=[pl.BlockSpec((2, BS, BS), lambda h, b, t: (0, 0, 0)),
                  pl.BlockSpec((1, S, W), lambda h, b, t: (b, 0, h)),
                  pl.BlockSpec((1, S, W), lambda h, b, t: (b, 0, n_groups + h)),
                  pl.BlockSpec((1, S, W), lambda h, b, t: (b, 0, 2 * n_groups + h))],
        out_specs=pl.BlockSpec((1, S, W), lambda h, b, t: (b, 0, h)),
        scratch_shapes=[pltpu.VMEM((HPS, nb, HEAD_DIM), F32),
                        pltpu.VMEM((HPS, nb, MOBA_PV_ROWS, BS), BF16),
                        pltpu.VMEM((HPS, 2, BS, BS), F32),
                        pltpu.VMEM((HPS, HEAD_DIM, BS), BF16),
                        pltpu.VMEM((HPS, 1, BS), jnp.int32),
                        pltpu.VMEM((HPS, BS, BS), F32),
                        pltpu.VMEM((HPS, BS, BS), BF16),
                        pltpu.VMEM((HPS, 1, BS), F32),
                        pltpu.VMEM((HPS, 1, BS), F32),
                        pltpu.VMEM((HPS, MOBA_PV_ROWS, BS), F32)],
    )
    return pl.pallas_call(
        functools.partial(_moba_kernel, n_blocks=nb),
        grid_spec=grid_spec,
        out_shape=jax.ShapeDtypeStruct((B, S, N_HEADS * HEAD_DIM), BF16),
        compiler_params=_cparams("arbitrary", "arbitrary"),
        name="moba",
    )(tbl, bkt, proj3, proj3, proj3)


CONV_HALO = 32
CONV_ROWS = 64
CONV_COLS = 256


def _conv_kernel(a_ref, g_ref, ah_ref, gh_ref, w_ref, cb_ref, lng_ref, lnb_ref, *refs, ts, n_side):
    side_in, (o_ref,), side_out = refs[:n_side], refs[n_side:n_side + 1], refs[n_side + 1:2 * n_side + 1]
    u_ref, sh_ref, c_ref = refs[2 * n_side + 1:]
    for src, dst in zip(side_in, side_out):
        dst[...] = src[...].astype(dst.dtype)

    i = pl.program_id(1)
    D = u_ref.shape[1]
    u_ref[CONV_HALO:, :] = a_ref[0].astype(F32) * _sigmoid(g_ref[0].astype(F32))
    uh = ah_ref[0].astype(F32) * _sigmoid(gh_ref[0].astype(F32))
    u_ref[:CONV_HALO, :] = jnp.where(i > 0, uh, 0.0)

    first = CONV_HALO - (CONV_KERNEL - 1)
    sh_rows = sh_ref.shape[1]

    def col_body(c, carry):
        cols = pl.ds(pl.multiple_of(c * CONV_COLS, CONV_COLS), CONV_COLS)
        for s in range(1, SUBLANES):
            sh_ref[s] = u_ref[s:s + sh_rows, cols]
        for r in range(ts // CONV_ROWS):
            acc = jnp.zeros((CONV_ROWS, CONV_COLS), F32)
            for k in range(CONV_KERNEL):
                s = (first + k) % SUBLANES
                r0 = r * CONV_ROWS + first + k - s
                tap = u_ref[r0:r0 + CONV_ROWS, cols] if s == 0 else sh_ref[s, r0:r0 + CONV_ROWS, :]
                acc = acc + jnp.tile(w_ref[k, :, cols], (CONV_ROWS // SUBLANES, 1)) * tap
            c_ref[r * CONV_ROWS:(r + 1) * CONV_ROWS, cols] = acc + cb_ref[:, cols]
        return carry

    lax.fori_loop(0, D // CONV_COLS, col_body, 0)

    y = c_ref[...]
    mu = jnp.mean(y, axis=-1, keepdims=True)
    yc = y - mu
    var = jnp.mean(yc * yc, axis=-1, keepdims=True)
    z = yc * lax.rsqrt(var + EPS) * lng_ref[...] + lnb_ref[...]
    o_ref[0] = (z * _sigmoid(z)).astype(o_ref.dtype)


def _conv(proj3, conv_w, conv_b, ln_g, ln_b, a_blk, side_weights, ts=512):
    B, S, _ = proj3.shape
    D = conv_w.shape[1]
    hpb = ts // CONV_HALO
    steps_per_seq = S // ts
    n_steps = B * steps_per_seq
    side_specs = [pl.BlockSpec((w.shape[0] // n_steps, w.shape[1]), lambda b, i: (b * steps_per_seq + i, 0))
                  for w in side_weights]
    w_rows = jnp.broadcast_to(conv_w[:, None, :], (CONV_KERNEL, SUBLANES, D))

    def halo(col):
        return pl.BlockSpec((1, CONV_HALO, D), lambda b, i: (b, jnp.maximum(i * hpb - 1, 0), col))

    vec = pl.BlockSpec((1, D), lambda b, i: (0, 0))
    conv_act, *side_bf16 = pl.pallas_call(
        functools.partial(_conv_kernel, ts=ts, n_side=len(side_weights)),
        grid=(B, steps_per_seq),
        in_specs=[pl.BlockSpec((1, ts, D), lambda b, i: (b, i, a_blk)),
                  pl.BlockSpec((1, ts, D), lambda b, i: (b, i, a_blk + 1)),
                  halo(a_blk), halo(a_blk + 1),
                  pl.BlockSpec((CONV_KERNEL, SUBLANES, D), lambda b, i: (0, 0, 0)),
                  vec, vec, vec] + side_specs,
        out_specs=[pl.BlockSpec((1, ts, D), lambda b, i: (b, i, 0))] + side_specs,
        out_shape=[jax.ShapeDtypeStruct((B, S, D), BF16)]
        + [jax.ShapeDtypeStruct(w.shape, BF16) for w in side_weights],
        scratch_shapes=[pltpu.VMEM((CONV_HALO + ts, D), F32),
                        pltpu.VMEM((SUBLANES, CONV_HALO + ts - SUBLANES, CONV_COLS), F32),
                        pltpu.VMEM((ts, D), F32)],
        compiler_params=_cparams("parallel", "arbitrary"),
        name="conv",
    )(proj3, proj3, proj3, proj3, w_rows, conv_b, ln_g, ln_b, *side_weights)
    return conv_act, side_bf16


def _mix_kernel(x_ref, at_ref, cv_ref, ga_ref, gc_ref, wa_ref, wc_ref, wo_ref, o_ref):
    ya = jnp.dot(at_ref[...], wa_ref[...], preferred_element_type=F32)
    yc = jnp.dot(cv_ref[...], wc_ref[...], preferred_element_type=F32)
    mixed = (jax.nn.sigmoid(ga_ref[...].astype(F32)) * ya + jax.nn.sigmoid(gc_ref[...].astype(F32)) * yc)
    o_ref[...] = x_ref[...] + jnp.dot(mixed.astype(BF16), wo_ref[...], preferred_element_type=F32)


def _mix(x2, attn2, conv2, proj2, wa, wc, wo, gate_blk, tm=256):
    T, D = x2.shape
    row = lambda col: pl.BlockSpec((tm, D), lambda i: (i, col))
    return pl.pallas_call(
        _mix_kernel,
        grid=(T // tm,),
        in_specs=[row(0), row(0), row(0), row(gate_blk), row(gate_blk + 1),
                  _resident((D, D)), _resident((D, D)), _resident((D, D))],
        out_specs=row(0),
        out_shape=jax.ShapeDtypeStruct((T, D), F32),
        compiler_params=_cparams("parallel"),
        name="mix",
    )(x2, attn2, conv2, proj2, proj2, wa, wc, wo)


MOE_TOK = 512
MOE_CHUNK = 16
MOE_TM = 512
MOE_TOPK = 2
_MOE_WORST_ROWS = MOE_TOPK * MOE_TOK + N_EXPERTS * (MOE_CHUNK - 1)
MOE_ROWS = -(-_MOE_WORST_ROWS // LANES) * LANES
MOE_CPT = MOE_ROWS // MOE_CHUNK
MOE_CPM = MOE_TM // MOE_CHUNK
MOE_EXTRA = LANES


def _route_t(lt):
    row = lax.broadcasted_iota(jnp.int32, lt.shape, 0)
    rowf = row.astype(F32)
    big = float(lt.shape[0])
    is_grp = (row >= N_EXPERTS) & (row < N_EXPERTS + N_GROUPS)
    cl = jnp.where(is_grp, lt, -jnp.inf)
    mg = jnp.max(cl, axis=0, keepdims=True)
    g_star = jnp.min(jnp.where(cl == mg, rowf, big), axis=0, keepdims=True).astype(jnp.int32) - N_EXPERTS
    p_g = 1.0 / jnp.sum(jnp.where(is_grp, jnp.exp(cl - mg), 0.0), axis=0, keepdims=True)
    in_grp = (row < N_EXPERTS) & ((row // EXPERTS_PER_GROUP) == g_star)
    f1 = jnp.where(in_grp, lt, -jnp.inf)
    v1 = jnp.max(f1, axis=0, keepdims=True)
    i1 = jnp.min(jnp.where(f1 == v1, rowf, big), axis=0, keepdims=True).astype(jnp.int32)
    f2 = jnp.where(row == i1, -jnp.inf, f1)
    v2 = jnp.max(f2, axis=0, keepdims=True)
    i2 = jnp.min(jnp.where(f2 == v2, rowf, big), axis=0, keepdims=True).astype(jnp.int32)
    e2 = jnp.exp(v2 - v1)
    return i1, i2, p_g / (1.0 + e2), p_g * e2 / (1.0 + e2)


def _split_bf16(x):
    hi = x.astype(BF16)
    return hi, (x - hi.astype(F32)).astype(BF16)


def _moe_sort_kernel(h_ref, g_ref, wh_ref, wl_ref, br_ref, xs_ref, dest_ref, cnt_ref):
    tok = h_ref.shape[0]
    D = h_ref.shape[1]
    hn = _rms(h_ref[...], g_ref[...])
    hn_hi, hn_lo = _split_bf16(hn)
    nt_dot = lambda a, b: lax.dot_general(a, b, _NT, preferred_element_type=F32)
    w_hi = wh_ref[...]
    both = nt_dot(jnp.concatenate([w_hi, wl_ref[...]], axis=0), hn_hi)
    lt = both[:MOE_ROUTE_ROWS] + both[MOE_ROUTE_ROWS:] + nt_dot(w_hi, hn_lo) + br_ref[...]
    i1, i2, c1, c2 = _route_t(lt)

    erow = lax.broadcasted_iota(jnp.int32, (N_EXPERTS, tok), 0)
    oh1 = erow == i1
    oh2 = erow == i2
    oh = (oh1 | oh2).astype(F32)
    earlier = (lax.broadcasted_iota(jnp.int32, (tok, tok), 0) < lax.broadcasted_iota(jnp.int32, (tok, tok), 1))
    cum = jnp.dot(oh.astype(BF16), earlier.astype(BF16), preferred_element_type=F32)
    cnt = jnp.sum(oh, axis=1, keepdims=True)
    n_chunks = jnp.floor((cnt + (MOE_CHUNK - 1)) * (1.0 / MOE_CHUNK))
    ncb = jnp.broadcast_to(n_chunks, (N_EXPERTS, LANES))
    erow_l = lax.broadcasted_iota(jnp.int32, (N_EXPERTS, LANES), 0)
    first_chunk = jnp.zeros((N_EXPERTS, LANES), F32)
    for e in range(N_EXPERTS - 1):
        first_chunk = first_chunk + jnp.where(erow_l > e, ncb[e:e + 1, :], 0.0)
    base = jnp.tile(first_chunk, (1, tok // LANES)) * MOE_CHUNK + cum
    d1 = jnp.sum(jnp.where(oh1, base, 0.0), axis=0, keepdims=True).astype(jnp.int32)
    d2 = jnp.sum(jnp.where(oh2, base, 0.0), axis=0, keepdims=True).astype(jnp.int32)

    srow = lax.broadcasted_iota(jnp.int32, (MOE_ROWS, tok), 0)
    p1 = srow == d1
    p2 = srow == d2
    perm1 = p1.astype(BF16)
    perm2 = p2.astype(BF16)
    xs_ref[:, :D] = jnp.dot(perm1 + perm2, hn_hi, preferred_element_type=F32).astype(BF16)
    xrow = lax.broadcasted_iota(jnp.int32, (MOE_EXTRA, tok), 0)

    def weight_rows(c):
        hi = c.astype(BF16).astype(F32)
        return jnp.where(xrow == 0, hi, jnp.where(xrow == 1, c - hi, 0.0)).astype(BF16)

    xs_ref[:, D:] = (nt_dot(perm1, weight_rows(c1)) + nt_dot(perm2, weight_rows(c2))).astype(BF16)

    r8 = lax.broadcasted_iota(jnp.int32, (SUBLANES, tok), 0)
    dest_ref[0] = jnp.where(r8 == 0, d1, jnp.where(r8 == 1, d2, 0))
    cnt_ref[0] = jnp.broadcast_to(cnt, (N_EXPERTS, LANES)).astype(jnp.int32)


MOE_ROUTE_ROWS = 32


def _moe_sort(h1, g, w_router_e, b_router_e, w_router_g, b_router_g):
    T, D = h1.shape
    nt = T // MOE_TOK
    pad = MOE_ROUTE_ROWS - N_EXPERTS - N_GROUPS
    wr = jnp.concatenate([w_router_e.T, w_router_g.T, jnp.zeros((pad, D), F32)], axis=0)
    wr_hi = wr.astype(BF16)
    wr_lo = (wr - wr_hi.astype(F32)).astype(BF16)
    br = jnp.concatenate([b_router_e, b_router_g, jnp.zeros((pad,), F32)])[:, None]
    return pl.pallas_call(
        _moe_sort_kernel,
        grid=(nt,),
        in_specs=[pl.BlockSpec((MOE_TOK, D), lambda i: (i, 0)),
                  pl.BlockSpec((1, D), lambda i: (0, 0)),
                  pl.BlockSpec((MOE_ROUTE_ROWS, D), lambda i: (0, 0)),
                  pl.BlockSpec((MOE_ROUTE_ROWS, D), lambda i: (0, 0)),
                  pl.BlockSpec((MOE_ROUTE_ROWS, 1), lambda i: (0, 0))],
        out_specs=[pl.BlockSpec((MOE_ROWS, D + MOE_EXTRA), lambda i: (i, 0)),
                   pl.BlockSpec((1, SUBLANES, MOE_TOK), lambda i: (i, 0, 0)),
                   pl.BlockSpec((1, N_EXPERTS, LANES), lambda i: (i, 0, 0))],
        out_shape=[jax.ShapeDtypeStruct((nt * MOE_ROWS, D + MOE_EXTRA), BF16),
                   jax.ShapeDtypeStruct((nt, SUBLANES, MOE_TOK), jnp.int32),
                   jax.ShapeDtypeStruct((nt, N_EXPERTS, LANES), jnp.int32)],
        compiler_params=_cparams("parallel"),
        name="moe_sort",
    )(h1, g, wr_hi, wr_lo, br)


def _moe_tables(cnt, n_mm_tiles):
    nt = cnt.shape[0]
    pc = (cnt + (MOE_CHUNK - 1)) // MOE_CHUNK
    run_end = jnp.cumsum(pc, axis=1)
    run_first = run_end - pc
    per_expert = jnp.sum(pc, axis=0)
    mm_tiles = (per_expert + (MOE_CPM - 1)) // MOE_CPM
    mm_end = jnp.cumsum(mm_tiles)
    slot_base = (mm_end - mm_tiles)[None, :] * MOE_CPM + (jnp.cumsum(pc, axis=0) - pc)
    k = jnp.arange(MOE_CPT, dtype=jnp.int32)[None, :, None]
    expert_of = jnp.sum((run_end[:, None, :] <= k).astype(jnp.int32), axis=2)
    valid = expert_of < N_EXPERTS
    pick = expert_of[:, :, None] == jnp.arange(N_EXPERTS, dtype=jnp.int32)
    take = lambda a: jnp.sum(jnp.where(pick, a[:, None, :], 0), axis=2)
    slot = take(slot_base) + (k[:, :, 0] - take(run_first))
    src = (jnp.arange(nt, dtype=jnp.int32)[:, None] * MOE_CPT + k[:, :, 0]).reshape(-1)
    slot = jnp.where(valid, slot, n_mm_tiles * MOE_CPM).reshape(-1)
    gather_tbl = jnp.zeros((n_mm_tiles * MOE_CPM,), jnp.int32).at[slot].set(src, mode="drop")
    return_tbl = jnp.where(valid.reshape(-1), slot, 0).astype(jnp.int32)
    g = jnp.arange(n_mm_tiles, dtype=jnp.int32)
    n_used = mm_end[-1].astype(jnp.int32)
    tile_expert = jnp.sum((mm_end[None, :] <= jnp.minimum(g, n_used - 1)[:, None]).astype(jnp.int32), axis=1)
    return gather_tbl, return_tbl, tile_expert.astype(jnp.int32), n_used.reshape(1)


MOE_GATHER_UNROLL = 8


def _chunk_gather(tbl_ref, src_hbm, buf, sem, chunks_per_tile):
    def chunk_copy(tile, c, slot):
        src = pl.multiple_of(tbl_ref[tile * chunks_per_tile + c] * MOE_CHUNK, MOE_CHUNK)
        dst = pl.multiple_of(c * MOE_CHUNK, MOE_CHUNK)
        return pltpu.make_async_copy(src_hbm.at[pl.ds(src, MOE_CHUNK), :],
                                     buf.at[slot, pl.ds(dst, MOE_CHUNK), :], sem.at[slot])

    def for_each_chunk(tile, slot, act):
        def body(c, carry):
            act(chunk_copy(tile, c, slot))
            return carry
        lax.fori_loop(0, chunks_per_tile, body, 0, unroll=MOE_GATHER_UNROLL)

    def start(tile, slot):
        for_each_chunk(tile, slot, lambda copy: copy.start())

    def wait(tile, slot):
        for_each_chunk(tile, slot, lambda copy: copy.wait())

    return start, wait


def _moe_expert_kernel(gt_ref, te_ref, nu_ref, xs_hbm, wg_ref, wu_ref, wd_ref, o_ref,
                       wg_bf, wu_bf, wd_bf, x_buf, x_sem):
    D = o_ref.shape[1]
    g = pl.program_id(0)
    n_used = nu_ref[0]
    used = g < n_used

    start_gather, wait_gather = _chunk_gather(gt_ref, xs_hbm, x_buf, x_sem, MOE_CPM)

    @pl.when((g == 0) & (n_used > 0))
    def _():
        start_gather(0, 0)

    @pl.when(g + 1 < n_used)
    def _():
        start_gather(g + 1, (g + 1) % 2)

    @pl.when(used & ((g == 0) | (te_ref[g] != te_ref[jnp.maximum(g - 1, 0)])))
    def _():
        wg_bf[...] = wg_ref[0].astype(BF16)
        wu_bf[...] = wu_ref[0].astype(BF16)
        wd_bf[...] = wd_ref[0].astype(BF16)

    @pl.when(used)
    def _():
        slot = g % 2
        wait_gather(g, slot)
        xs = x_buf[slot]
        x = xs[:, :D]
        extra = xs[:, D:].astype(F32)
        c = extra[:, 0:1] + extra[:, 1:2]
        gate = jnp.dot(x, wg_bf[...], preferred_element_type=F32)
        up = jnp.dot(x, wu_bf[...], preferred_element_type=F32)
        hh = (gate * jax.nn.sigmoid(gate)) * up * c
        o_ref[...] = jnp.dot(hh.astype(BF16), wd_bf[...], preferred_element_type=F32).astype(o_ref.dtype)

    @pl.when(jnp.logical_not(used))
    def _():
        o_ref[...] = jnp.zeros(o_ref.shape, o_ref.dtype)


def _moe_experts(xs, gather_tbl, tile_expert, n_used, wg, wu, wd, n_mm_tiles):
    W = xs.shape[1]
    E, D, FF = wg.shape

    grid_spec = pltpu.PrefetchScalarGridSpec(
        num_scalar_prefetch=3,
        grid=(n_mm_tiles,),
        in_specs=[pl.BlockSpec(memory_space=pl.ANY),
                  pl.BlockSpec((1, D, FF), lambda g, gt, te, nu: (te[g], 0, 0)),
                  pl.BlockSpec((1, D, FF), lambda g, gt, te, nu: (te[g], 0, 0)),
                  pl.BlockSpec((1, FF, D), lambda g, gt, te, nu: (te[g], 0, 0))],
        out_specs=pl.BlockSpec((MOE_TM, D), lambda g, gt, te, nu: (g, 0)),
        scratch_shapes=[pltpu.VMEM((D, FF), BF16), pltpu.VMEM((D, FF), BF16), pltpu.VMEM((FF, D), BF16),
                        pltpu.VMEM((2, MOE_TM, W), BF16), pltpu.SemaphoreType.DMA((2,))],
    )
    return pl.pallas_call(
        _moe_expert_kernel,
        grid_spec=grid_spec,
        out_shape=jax.ShapeDtypeStruct((n_mm_tiles * MOE_TM, D), BF16),
        compiler_params=_cparams("arbitrary"),
        name="moe_experts",
    )(gather_tbl, tile_expert, n_used, xs, wg, wu, wd)


def _finish_kernel(rt_ref, ys_hbm, dest_ref, h_ref, p_ref, gp_ref, gf_ref, wg_ref, wp_ref, o_ref, y_buf, y_sem):
    tok = h_ref.shape[0]
    i = pl.program_id(0)
    slot = i % 2
    start_gather, wait_gather = _chunk_gather(rt_ref, ys_hbm, y_buf, y_sem, MOE_CPT)

    @pl.when(i == 0)
    def _():
        start_gather(0, 0)

    @pl.when(i + 1 < pl.num_programs(0))
    def _():
        start_gather(i + 1, (i + 1) % 2)

    wait_gather(i, slot)
    dcol = dest_ref[0].astype(F32).T
    lane = lax.broadcasted_iota(jnp.int32, (tok, MOE_ROWS), 1).astype(F32)
    unsort = ((lane == dcol[:, 0:1]) | (lane == dcol[:, 1:2])).astype(BF16)
    h = h_ref[...] + jnp.dot(unsort, y_buf[slot], preferred_element_type=F32)
    t = _rms(h, gp_ref[...]).astype(BF16)
    gate = jax.nn.sigmoid(jnp.dot(t, wg_ref[...], preferred_element_type=F32))
    emb = jnp.dot(p_ref[...].astype(BF16), wp_ref[...], preferred_element_type=F32)
    o_ref[...] = _rms(h + emb * gate, gf_ref[...])


def _finish(ys, return_tbl, dest, h1, p2, g_ple, g_final, wg, wp):
    T, D = h1.shape
    P = p2.shape[1]
    nt = T // MOE_TOK

    vec = pl.BlockSpec((1, D), lambda i, rt: (0, 0))
    grid_spec = pltpu.PrefetchScalarGridSpec(
        num_scalar_prefetch=1,
        grid=(nt,),
        in_specs=[
            pl.BlockSpec(memory_space=pl.ANY),
            pl.BlockSpec((1, SUBLANES, MOE_TOK), lambda i, rt: (i, 0, 0)),
            pl.BlockSpec((MOE_TOK, D), lambda i, rt: (i, 0)),
            pl.BlockSpec((MOE_TOK, P), lambda i, rt: (i, 0)),
            vec, vec,
            pl.BlockSpec((D, D), lambda i, rt: (0, 0), pipeline_mode=pl.Buffered(1)),
            pl.BlockSpec((P, D), lambda i, rt: (0, 0), pipeline_mode=pl.Buffered(1))],
        out_specs=pl.BlockSpec((MOE_TOK, D), lambda i, rt: (i, 0)),
        scratch_shapes=[pltpu.VMEM((2, MOE_ROWS, D), BF16), pltpu.SemaphoreType.DMA((2,))],
    )
    return pl.pallas_call(
        _finish_kernel,
        grid_spec=grid_spec,
        out_shape=jax.ShapeDtypeStruct((T, D), F32),
        compiler_params=_cparams("arbitrary"),
        name="finish",
    )(return_tbl, ys, dest, h1, p2, g_ple, g_final, wg, wp)


def kernel(x, p, g_mix, w_in, rel_bias, w_attn_br, conv_w, conv_b, ln_g, ln_b, w_conv_br, w_o, g_ffn,
           w_router_g, b_router_g, w_router_e, b_router_e, w_e_gate, w_e_up, w_e_down, g_ple, w_ple_gate,
           w_ple_proj, g_final):
    B, S, D = x.shape
    T = B * S
    depth = w_in.shape[0]
    width = N_HEADS * HEAD_DIM
    assert D == width and conv_w.shape[2] == D, "column-block indexing assumes all branch widths equal d_model"
    assert S % MOBA_BLOCK == 0
    assert depth == 1, "the final RMSNorm is fused into the last stage of a single layer"
    a_blk = 3 * width // D
    gate_blk = a_blk + 2

    h = x.reshape(T, D)
    for i in range(depth):
        proj = _in_proj(h, g_mix[i][None], w_in[i].astype(BF16))
        proj3 = proj.reshape(B, S, -1)
        attn = _moba(proj3, rel_bias)
        conv, (wa, wc, wo, w_gate) = _conv(proj3, conv_w[i], conv_b[i][None], ln_g[i][None], ln_b[i][None], a_blk,
                                           side_weights=[w_attn_br[i], w_conv_br[i], w_o[i], w_ple_gate[i]])
        h = _mix(h, attn.reshape(T, D), conv.reshape(T, D), proj, wa, wc, wo, gate_blk)
        xs, dest, cnt = _moe_sort(h, g_ffn[i][None], w_router_e[i], b_router_e[i], w_router_g[i], b_router_g[i])
        n_tok_tiles = T // MOE_TOK
        worst_chunks = (MOE_TOPK * T + n_tok_tiles * N_EXPERTS * (MOE_CHUNK - 1)) // MOE_CHUNK
        n_mm_tiles = -(-worst_chunks // MOE_CPM) + N_EXPERTS
        gather_tbl, return_tbl, tile_expert, n_used = _moe_tables(cnt[:, :, 0], n_mm_tiles)
        ys = _moe_experts(xs, gather_tbl, tile_expert, n_used,
                          w_e_gate[i], w_e_up[i], w_e_down[i], n_mm_tiles)
        out = _finish(ys, return_tbl, dest, h, p[i].reshape(T, -1), g_ple[i][None], g_final[None],
                      w_gate, w_ple_proj[i].astype(BF16))
    return out.reshape(B, S, D)
```

```python
import functools
import math

import numpy as np
import jax
import jax.numpy as jnp
from jax import lax
from jax.experimental import pallas as pl
from jax.experimental.pallas import tpu as pltpu

F32 = jnp.float32
BF16 = jnp.bfloat16

N_HEADS = 16
HEAD_DIM = 128
MOBA_BLOCK = 256
MOBA_TOPK = 3
N_BUCKETS = 32
MAX_DISTANCE = 128
CONV_KERNEL = 31
N_GROUPS = 4
EXPERTS_PER_GROUP = 4
N_EXPERTS = N_GROUPS * EXPERTS_PER_GROUP
EPS = 1e-6
NEG = -1e30
LOG2E = 1.4426950408889634

V7X_VMEM_LIMIT_BYTES = 56 * 1024 * 1024
LANES = 128
SUBLANES = 8
BF16_TILE_ROWS = 2 * SUBLANES

_NT = (((1,), (1,)), ((), ()))


def _cparams(*sem):
    return pltpu.CompilerParams(dimension_semantics=sem, vmem_limit_bytes=V7X_VMEM_LIMIT_BYTES)


def _resident(shape):
    return pl.BlockSpec(shape, lambda *_: (0,) * len(shape), pipeline_mode=pl.Buffered(1))


def _sigmoid(x):
    return 0.5 * jnp.tanh(0.5 * x) + 0.5


def _rms(x, g):
    return x * lax.rsqrt(jnp.mean(x * x, axis=-1, keepdims=True) + EPS) * g


IN_PROJ_NORM_SPLIT = 4


def _in_proj_kernel(x_ref, g_ref, w_ref, o_ref, xn_ref):
    first = pl.program_id(1) == 0

    @pl.when(first)
    def _():
        sub = x_ref.shape[0] // IN_PROJ_NORM_SPLIT
        for r in range(IN_PROJ_NORM_SPLIT):
            rows = slice(r * sub, (r + 1) * sub)
            xn = _rms(x_ref[rows, :], g_ref[...]).astype(BF16)
            xn_ref[rows, :] = xn
            o_ref[rows, :] = jnp.dot(xn, w_ref[...], preferred_element_type=F32).astype(o_ref.dtype)

    @pl.when(jnp.logical_not(first))
    def _():
        o_ref[...] = jnp.dot(xn_ref[...], w_ref[...], preferred_element_type=F32).astype(o_ref.dtype)


def _in_proj(x2, g, w, tm=1024, tn=2048):
    T, D = x2.shape
    N = w.shape[1]
    return pl.pallas_call(
        _in_proj_kernel,
        grid=(T // tm, N // tn),
        in_specs=[pl.BlockSpec((tm, D), lambda i, j: (i, 0)),
                  pl.BlockSpec((1, D), lambda i, j: (0, 0)),
                  pl.BlockSpec((D, tn), lambda i, j: (0, j))],
        out_specs=pl.BlockSpec((tm, tn), lambda i, j: (i, j)),
        out_shape=jax.ShapeDtypeStruct((T, N), BF16),
        scratch_shapes=[pltpu.VMEM((tm, D), BF16)],
        compiler_params=_cparams("parallel", "arbitrary"),
        name="in_proj",
    )(x2, g, w)


MOBA_PV_ROWS = HEAD_DIM + BF16_TILE_ROWS
MOBA_HEADS_PER_STEP = 4


def _rel_buckets_t():
    k = np.arange(MOBA_BLOCK)[:, None]
    q = np.arange(MOBA_BLOCK)[None, :]
    dist = np.stack([q - k, MOBA_BLOCK + q - k]).astype(np.int32)
    n = np.maximum(dist, 0)
    max_exact = N_BUCKETS // 2
    nf = np.maximum(n, 1).astype(np.float32)
    large = max_exact + (np.log(nf / np.float32(max_exact)) / np.float32(math.log(MAX_DISTANCE / max_exact))
                         * np.float32(N_BUCKETS - max_exact)).astype(np.int32)
    large = np.minimum(large, N_BUCKETS - 1)
    return np.where(n < max_exact, n, large).astype(np.int32)


def _moba_kernel(tbl_ref, bkt_ref, q_ref, k_ref, v_ref, o_ref,
                 kmean_ref, vt_ref, bias_ref, qt_ref, sel_ref, s_ref, top_ref, p_ref, m_ref, a_ref, acc_ref,
                 *, n_blocks):
    hg = pl.program_id(0)
    b = pl.program_id(1)
    BS = MOBA_BLOCK
    HPS = MOBA_HEADS_PER_STEP
    lanes = [slice(hh * HEAD_DIM, (hh + 1) * HEAD_DIM) for hh in range(HPS)]

    @pl.when(b == 0)
    def _build_bias():
        krow = lax.broadcasted_iota(jnp.int32, (BS, BS), 0)
        qcol = lax.broadcasted_iota(jnp.int32, (BS, BS), 1)
        for hh in range(HPS):
            bias_ref[hh] = jnp.zeros((2, BS, BS), F32)

            def fill(n, carry):
                bias_ref[hh] = jnp.where(bkt_ref[...] == n, tbl_ref[hg * HPS + hh, n] * LOG2E, bias_ref[hh])
                return carry

            lax.fori_loop(0, N_BUCKETS, fill, 0)
            bias_ref[hh, 0] = jnp.where(qcol >= krow, bias_ref[hh, 0], NEG)

    ones_row = (lax.broadcasted_iota(jnp.int32, (MOBA_PV_ROWS - HEAD_DIM, BS), 0) == 0).astype(BF16)

    def prep(n, carry):
        rows = pl.ds(pl.multiple_of(n * BS, BS), BS)
        for hh in range(HPS):
            kmean_ref[hh, pl.ds(n, 1), :] = jnp.mean(k_ref[0, rows, lanes[hh]].astype(F32), axis=0, keepdims=True)
            vt_ref[hh, n, :HEAD_DIM] = v_ref[0, rows, lanes[hh]].astype(F32).T.astype(BF16)
            vt_ref[hh, n, HEAD_DIM:] = ones_row
        return carry

    lax.fori_loop(0, n_blocks, prep, 0)

    q_block = functools.partial(_moba_q_block, hg=hg, tbl_ref=tbl_ref, q_ref=q_ref, k_ref=k_ref, o_ref=o_ref,
                                kmean_ref=kmean_ref, vt_ref=vt_ref, bias_ref=bias_ref, qt_ref=qt_ref,
                                sel_ref=sel_ref, s_ref=s_ref, top_ref=top_ref, p_ref=p_ref, m_ref=m_ref, a_ref=a_ref,
                                acc_ref=acc_ref, n_blocks=n_blocks)

    q_block(jnp.int32(0), finish_previous=False)

    def next_block(qi, carry):
        q_block(qi, finish_previous=True)
        return carry

    lax.fori_loop(1, n_blocks, next_block, 0)
    _moba_close(jnp.int32(n_blocks - 1), vt_ref=vt_ref, p_ref=p_ref, a_ref=a_ref, acc_ref=acc_ref, o_ref=o_ref)


def _moba_close(qi, *, vt_ref, p_ref, a_ref, acc_ref, o_ref):
    lanes = [slice(hh * HEAD_DIM, (hh + 1) * HEAD_DIM) for hh in range(MOBA_HEADS_PER_STEP)]
    q_rows = pl.ds(pl.multiple_of(qi * MOBA_BLOCK, MOBA_BLOCK), MOBA_BLOCK)
    for hh in range(MOBA_HEADS_PER_STEP):
        acc = a_ref[hh] * acc_ref[hh] + jnp.dot(vt_ref[hh, 0], p_ref[hh], preferred_element_type=F32)
        denom = acc[HEAD_DIM:HEAD_DIM + 1, :]
        out_t = acc[:HEAD_DIM, :] * (1.0 / denom)
        o_ref[0, q_rows, lanes[hh]] = out_t.T.astype(o_ref.dtype)


def _moba_q_block(qi, *, finish_previous, hg, tbl_ref, q_ref, k_ref, o_ref, kmean_ref, vt_ref, bias_ref, qt_ref,
                  sel_ref, s_ref, top_ref, p_ref, m_ref, a_ref, acc_ref, n_blocks):
    BS = MOBA_BLOCK
    HPS = MOBA_HEADS_PER_STEP
    scale = HEAD_DIM ** -0.5
    lanes = [slice(hh * HEAD_DIM, (hh + 1) * HEAD_DIM) for hh in range(HPS)]
    q_rows = pl.ds(pl.multiple_of(qi * BS, BS), BS)

    def qk_stage(hh, t, far=True):
        j0 = pl.multiple_of(jnp.maximum(qi - t, 0) * BS, BS)
        s = jnp.dot(k_ref[0, pl.ds(j0, BS), lanes[hh]], qt_ref[hh], preferred_element_type=F32)
        s_ref[hh] = s
        if far:
            top_ref[hh] = jnp.max(s, axis=0, keepdims=True)

    def softmax_stage(hh, t, bias_of=None, far_bias=None):
        chosen = (lax.shift_right_logical(sel_ref[hh], qi - t) & 1) == 1
        c2 = scale * LOG2E
        if bias_of is not None:
            st = s_ref[hh] * c2 + bias_of[...]
            shift = 0.0
            top = jnp.max(st, axis=0, keepdims=True)
        else:
            st = None
            shift = far_bias
            top = top_ref[hh] * c2 + shift
        m_prev = m_ref[hh]
        m_new = jnp.where(chosen, jnp.maximum(m_prev, top), m_prev)
        a_ref[hh] = jnp.exp2(m_prev - m_new)
        m_ref[hh] = m_new
        cut = jnp.where(chosen, m_new - shift, -NEG)
        if st is None:
            st = s_ref[hh] * c2
        p_ref[hh] = jnp.exp2(st - cut).astype(BF16)

    def pv_stage(hh, t):
        acc_ref[hh] = a_ref[hh] * acc_ref[hh] + jnp.dot(vt_ref[hh, qi - t], p_ref[hh],
                                                        preferred_element_type=F32)

    if finish_previous:
        _moba_close(qi - 1, vt_ref=vt_ref, p_ref=p_ref, a_ref=a_ref, acc_ref=acc_ref, o_ref=o_ref)

    for hh in range(HPS):
        qt = q_ref[0, q_rows, lanes[hh]].astype(F32).T.astype(BF16)
        qt_ref[hh] = qt
        gate = jnp.dot(kmean_ref[hh].astype(BF16), qt, preferred_element_type=F32)
        blk = lax.broadcasted_iota(jnp.int32, gate.shape, 0)
        blkf = blk.astype(F32)
        g = jnp.where(blk < qi, gate, NEG)
        bits = jnp.full((1, BS), lax.shift_left(jnp.int32(1), qi), jnp.int32)
        for _ in range(MOBA_TOPK):
            mx = jnp.max(g, axis=0, keepdims=True)
            ix = jnp.min(jnp.where(g == mx, blkf, float(n_blocks)), axis=0, keepdims=True).astype(jnp.int32)
            bits = bits | jnp.where(ix < qi, lax.shift_left(jnp.int32(1), ix), 0)
            g = jnp.where(blk == ix, -jnp.inf, g)
        sel_ref[hh] = bits
        m_ref[hh] = jnp.full((1, BS), NEG, F32)
        acc_ref[hh] = jnp.zeros((MOBA_PV_ROWS, BS), F32)
        qk_stage(hh, 0, far=False)

    for hh in range(HPS):
        softmax_stage(hh, 0, bias_ref.at[hh, 0])
    if not finish_previous:
        return

    for hh in range(HPS):
        qk_stage(hh, 1, far=False)
    for hh in range(HPS):
        pv_stage(hh, 0)
        softmax_stage(hh, 1, bias_ref.at[hh, 1])
        qk_stage(hh, 2)

    def sweep(t):
        for hh in range(HPS):
            pv_stage(hh, t - 1)
            softmax_stage(hh, t, far_bias=tbl_ref[hg * HPS + hh, N_BUCKETS - 1] * LOG2E)
            qk_stage(hh, t + 1)

    n_far = qi - 1

    def sweep_pair(i, carry):
        sweep(2 + 2 * i)
        sweep(3 + 2 * i)
        return carry

    lax.fori_loop(0, n_far // 2, sweep_pair, 0)

    @pl.when(n_far % 2 == 1)
    def _odd_block():
        sweep(qi)


def _moba(proj3, rel_bias):
    B, S, _ = proj3.shape
    BS = MOBA_BLOCK
    HPS = MOBA_HEADS_PER_STEP
    W = HPS * HEAD_DIM
    nb = S // BS
    n_groups = N_HEADS // HPS
    bkt = jnp.asarray(_rel_buckets_t())
    tbl = rel_bias.astype(F32).T
    grid_spec = pltpu.PrefetchScalarGridSpec(
        num_scalar_prefetch=1,
        grid=(n_groups, B),
        in_specs=[pl.BlockSpec((2, BS, BS), lambda h, b, t: (0, 0, 0)),
                  pl.BlockSpec((1, S, W), lambda h, b, t: (b, 0, h)),
                  pl.BlockSpec((1, S, W), lambda h, b, t: (b, 0, n_groups + h)),
                  pl.BlockSpec((1, S, W), lambda h, b, t: (b, 0, 2 * n_groups + h))],
        out_specs=pl.BlockSpec((1, S, W), lambda h, b, t: (b, 0, h)),
        scratch_shapes=[pltpu.VMEM((HPS, nb, HEAD_DIM), F32),
                        pltpu.VMEM((HPS, nb, MOBA_PV_ROWS, BS), BF16),
                        pltpu.VMEM((HPS, 2, BS, BS), F32),
                        pltpu.VMEM((HPS, HEAD_DIM, BS), BF16),
                        pltpu.VMEM((HPS, 1, BS), jnp.int32),
                        pltpu.VMEM((HPS, BS, BS), F32),
                        pltpu.VMEM((HPS, 1, BS), F32),
                        pltpu.VMEM((HPS, BS, BS), BF16),
                        pltpu.VMEM((HPS, 1, BS), F32),
                        pltpu.VMEM((HPS, 1, BS), F32),
                        pltpu.VMEM((HPS, MOBA_PV_ROWS, BS), F32)],
    )
    return pl.pallas_call(
        functools.partial(_moba_kernel, n_blocks=nb),
        grid_spec=grid_spec,
        out_shape=jax.ShapeDtypeStruct((B, S, N_HEADS * HEAD_DIM), BF16),
        compiler_params=_cparams("arbitrary", "arbitrary"),
        name="moba",
    )(tbl, bkt, proj3, proj3, proj3)


CONV_HALO = 32
CONV_ROWS = 64
CONV_COLS = 256


def _conv_kernel(a_ref, g_ref, ah_ref, gh_ref, w_ref, cb_ref, lng_ref, lnb_ref, *refs, ts, n_side):
    side_in, (o_ref,), side_out = refs[:n_side], refs[n_side:n_side + 1], refs[n_side + 1:2 * n_side + 1]
    u_ref, sh_ref, c_ref = refs[2 * n_side + 1:]
    for src, dst in zip(side_in, side_out):
        dst[...] = src[...].astype(dst.dtype)

    i = pl.program_id(1)
    D = u_ref.shape[1]
    u_ref[CONV_HALO:, :] = a_ref[0].astype(F32) * _sigmoid(g_ref[0].astype(F32))
    uh = ah_ref[0].astype(F32) * _sigmoid(gh_ref[0].astype(F32))
    u_ref[:CONV_HALO, :] = jnp.where(i > 0, uh, 0.0)

    first = CONV_HALO - (CONV_KERNEL - 1)
    sh_rows = sh_ref.shape[1]

    def col_body(c, carry):
        cols = pl.ds(pl.multiple_of(c * CONV_COLS, CONV_COLS), CONV_COLS)
        for s in range(1, SUBLANES):
            sh_ref[s] = u_ref[s:s + sh_rows, cols]
        for r in range(ts // CONV_ROWS):
            acc = jnp.zeros((CONV_ROWS, CONV_COLS), F32)
            for k in range(CONV_KERNEL):
                s = (first + k) % SUBLANES
                r0 = r * CONV_ROWS + first + k - s
                tap = u_ref[r0:r0 + CONV_ROWS, cols] if s == 0 else sh_ref[s, r0:r0 + CONV_ROWS, :]
                acc = acc + jnp.tile(w_ref[k, :, cols], (CONV_ROWS // SUBLANES, 1)) * tap
            c_ref[r * CONV_ROWS:(r + 1) * CONV_ROWS, cols] = acc + cb_ref[:, cols]
        return carry

    lax.fori_loop(0, D // CONV_COLS, col_body, 0)

    y = c_ref[...]
    mu = jnp.mean(y, axis=-1, keepdims=True)
    yc = y - mu
    var = jnp.mean(yc * yc, axis=-1, keepdims=True)
    z = yc * lax.rsqrt(var + EPS) * lng_ref[...] + lnb_ref[...]
    o_ref[0] = (z * _sigmoid(z)).astype(o_ref.dtype)


def _conv(proj3, conv_w, conv_b, ln_g, ln_b, a_blk, side_weights, ts=512):
    B, S, _ = proj3.shape
    D = conv_w.shape[1]
    hpb = ts // CONV_HALO
    steps_per_seq = S // ts
    n_steps = B * steps_per_seq
    side_specs = [pl.BlockSpec((w.shape[0] // n_steps, w.shape[1]), lambda b, i: (b * steps_per_seq + i, 0))
                  for w in side_weights]
    w_rows = jnp.broadcast_to(conv_w[:, None, :], (CONV_KERNEL, SUBLANES, D))

    def halo(col):
        return pl.BlockSpec((1, CONV_HALO, D), lambda b, i: (b, jnp.maximum(i * hpb - 1, 0), col))

    vec = pl.BlockSpec((1, D), lambda b, i: (0, 0))
    conv_act, *side_bf16 = pl.pallas_call(
        functools.partial(_conv_kernel, ts=ts, n_side=len(side_weights)),
        grid=(B, steps_per_seq),
        in_specs=[pl.BlockSpec((1, ts, D), lambda b, i: (b, i, a_blk)),
                  pl.BlockSpec((1, ts, D), lambda b, i: (b, i, a_blk + 1)),
                  halo(a_blk), halo(a_blk + 1),
                  pl.BlockSpec((CONV_KERNEL, SUBLANES, D), lambda b, i: (0, 0, 0)),
                  vec, vec, vec] + side_specs,
        out_specs=[pl.BlockSpec((1, ts, D), lambda b, i: (b, i, 0))] + side_specs,
        out_shape=[jax.ShapeDtypeStruct((B, S, D), BF16)]
        + [jax.ShapeDtypeStruct(w.shape, BF16) for w in side_weights],
        scratch_shapes=[pltpu.VMEM((CONV_HALO + ts, D), F32),
                        pltpu.VMEM((SUBLANES, CONV_HALO + ts - SUBLANES, CONV_COLS), F32),
                        pltpu.VMEM((ts, D), F32)],
        compiler_params=_cparams("parallel", "arbitrary"),
        name="conv",
    )(proj3, proj3, proj3, proj3, w_rows, conv_b, ln_g, ln_b, *side_weights)
    return conv_act, side_bf16


def _mix_kernel(x_ref, at_ref, cv_ref, ga_ref, gc_ref, wa_ref, wc_ref, wo_ref, o_ref):
    ya = jnp.dot(at_ref[...], wa_ref[...], preferred_element_type=F32)
    yc = jnp.dot(cv_ref[...], wc_ref[...], preferred_element_type=F32)
    mixed = (jax.nn.sigmoid(ga_ref[...].astype(F32)) * ya + jax.nn.sigmoid(gc_ref[...].astype(F32)) * yc)
    o_ref[...] = x_ref[...] + jnp.dot(mixed.astype(BF16), wo_ref[...], preferred_element_type=F32)


def _mix(x2, attn2, conv2, proj2, wa, wc, wo, gate_blk, tm=256):
    T, D = x2.shape
    row = lambda col: pl.BlockSpec((tm, D), lambda i: (i, col))
    return pl.pallas_call(
        _mix_kernel,
        grid=(T // tm,),
        in_specs=[row(0), row(0), row(0), row(gate_blk), row(gate_blk + 1),
                  _resident((D, D)), _resident((D, D)), _resident((D, D))],
        out_specs=row(0),
        out_shape=jax.ShapeDtypeStruct((T, D), F32),
        compiler_params=_cparams("parallel"),
        name="mix",
    )(x2, attn2, conv2, proj2, proj2, wa, wc, wo)


MOE_TOK = 512
MOE_CHUNK = BF16_TILE_ROWS
MOE_TM = 512
MOE_TOPK = 2
_MOE_WORST_ROWS = MOE_TOPK * MOE_TOK + N_EXPERTS * (MOE_CHUNK - 1)
MOE_ROWS = -(-_MOE_WORST_ROWS // LANES) * LANES
MOE_CPT = MOE_ROWS // MOE_CHUNK
MOE_CPM = MOE_TM // MOE_CHUNK
MOE_EXTRA = LANES


def _route_t(lt):
    row = lax.broadcasted_iota(jnp.int32, lt.shape, 0)
    rowf = row.astype(F32)
    big = float(lt.shape[0])
    is_grp = (row >= N_EXPERTS) & (row < N_EXPERTS + N_GROUPS)
    cl = jnp.where(is_grp, lt, -jnp.inf)
    mg = jnp.max(cl, axis=0, keepdims=True)
    g_star = jnp.min(jnp.where(cl == mg, rowf, big), axis=0, keepdims=True).astype(jnp.int32) - N_EXPERTS
    p_g = 1.0 / jnp.sum(jnp.where(is_grp, jnp.exp(cl - mg), 0.0), axis=0, keepdims=True)
    in_grp = (row < N_EXPERTS) & ((row // EXPERTS_PER_GROUP) == g_star)
    f1 = jnp.where(in_grp, lt, -jnp.inf)
    v1 = jnp.max(f1, axis=0, keepdims=True)
    i1 = jnp.min(jnp.where(f1 == v1, rowf, big), axis=0, keepdims=True).astype(jnp.int32)
    f2 = jnp.where(row == i1, -jnp.inf, f1)
    v2 = jnp.max(f2, axis=0, keepdims=True)
    i2 = jnp.min(jnp.where(f2 == v2, rowf, big), axis=0, keepdims=True).astype(jnp.int32)
    e2 = jnp.exp(v2 - v1)
    return i1, i2, p_g / (1.0 + e2), p_g * e2 / (1.0 + e2)


def _split_bf16(x):
    hi = x.astype(BF16)
    return hi, (x - hi.astype(F32)).astype(BF16)


def _moe_sort_kernel(h_ref, g_ref, wh_ref, wl_ref, br_ref, xs_ref, dest_ref, cnt_ref):
    tok = h_ref.shape[0]
    D = h_ref.shape[1]
    hn = _rms(h_ref[...], g_ref[...])
    hn_hi, hn_lo = _split_bf16(hn)
    nt_dot = lambda a, b: lax.dot_general(a, b, _NT, preferred_element_type=F32)
    w_hi = wh_ref[...]
    both = nt_dot(jnp.concatenate([w_hi, wl_ref[...]], axis=0), hn_hi)
    lt = both[:MOE_ROUTE_ROWS] + both[MOE_ROUTE_ROWS:] + nt_dot(w_hi, hn_lo) + br_ref[...]
    i1, i2, c1, c2 = _route_t(lt)

    erow = lax.broadcasted_iota(jnp.int32, (N_EXPERTS, tok), 0)
    oh1 = erow == i1
    oh2 = erow == i2
    oh = (oh1 | oh2).astype(F32)
    earlier = (lax.broadcasted_iota(jnp.int32, (tok, tok), 0) < lax.broadcasted_iota(jnp.int32, (tok, tok), 1))
    cum = jnp.dot(oh.astype(BF16), earlier.astype(BF16), preferred_element_type=F32)
    cnt = jnp.sum(oh, axis=1, keepdims=True)
    n_chunks = jnp.floor((cnt + (MOE_CHUNK - 1)) * (1.0 / MOE_CHUNK))
    ncb = jnp.broadcast_to(n_chunks, (N_EXPERTS, LANES))
    erow_l = lax.broadcasted_iota(jnp.int32, (N_EXPERTS, LANES), 0)
    first_chunk = jnp.zeros((N_EXPERTS, LANES), F32)
    for e in range(N_EXPERTS - 1):
        first_chunk = first_chunk + jnp.where(erow_l > e, ncb[e:e + 1, :], 0.0)
    base = jnp.tile(first_chunk, (1, tok // LANES)) * MOE_CHUNK + cum
    d1 = jnp.sum(jnp.where(oh1, base, 0.0), axis=0, keepdims=True).astype(jnp.int32)
    d2 = jnp.sum(jnp.where(oh2, base, 0.0), axis=0, keepdims=True).astype(jnp.int32)

    srow = lax.broadcasted_iota(jnp.int32, (MOE_ROWS, tok), 0)
    p1 = srow == d1
    p2 = srow == d2
    perm1 = p1.astype(BF16)
    perm2 = p2.astype(BF16)
    xs_ref[:, :D] = jnp.dot(perm1 + perm2, hn_hi, preferred_element_type=F32).astype(BF16)
    xrow = lax.broadcasted_iota(jnp.int32, (MOE_EXTRA, tok), 0)

    def weight_rows(c):
        hi = c.astype(BF16).astype(F32)
        return jnp.where(xrow == 0, hi, jnp.where(xrow == 1, c - hi, 0.0)).astype(BF16)

    xs_ref[:, D:] = (nt_dot(perm1, weight_rows(c1)) + nt_dot(perm2, weight_rows(c2))).astype(BF16)

    r8 = lax.broadcasted_iota(jnp.int32, (SUBLANES, tok), 0)
    dest_ref[0] = jnp.where(r8 == 0, d1, jnp.where(r8 == 1, d2, 0))
    cnt_ref[0] = jnp.broadcast_to(cnt, (N_EXPERTS, LANES)).astype(jnp.int32)


MOE_ROUTE_ROWS = -(-(N_EXPERTS + N_GROUPS) // BF16_TILE_ROWS) * BF16_TILE_ROWS


def _moe_sort(h1, g, w_router_e, b_router_e, w_router_g, b_router_g):
    T, D = h1.shape
    nt = T // MOE_TOK
    pad = MOE_ROUTE_ROWS - N_EXPERTS - N_GROUPS
    wr = jnp.concatenate([w_router_e.T, w_router_g.T, jnp.zeros((pad, D), F32)], axis=0)
    wr_hi = wr.astype(BF16)
    wr_lo = (wr - wr_hi.astype(F32)).astype(BF16)
    br = jnp.concatenate([b_router_e, b_router_g, jnp.zeros((pad,), F32)])[:, None]
    return pl.pallas_call(
        _moe_sort_kernel,
        grid=(nt,),
        in_specs=[pl.BlockSpec((MOE_TOK, D), lambda i: (i, 0)),
                  pl.BlockSpec((1, D), lambda i: (0, 0)),
                  pl.BlockSpec((MOE_ROUTE_ROWS, D), lambda i: (0, 0)),
                  pl.BlockSpec((MOE_ROUTE_ROWS, D), lambda i: (0, 0)),
                  pl.BlockSpec((MOE_ROUTE_ROWS, 1), lambda i: (0, 0))],
        out_specs=[pl.BlockSpec((MOE_ROWS, D + MOE_EXTRA), lambda i: (i, 0)),
                   pl.BlockSpec((1, SUBLANES, MOE_TOK), lambda i: (i, 0, 0)),
                   pl.BlockSpec((1, N_EXPERTS, LANES), lambda i: (i, 0, 0))],
        out_shape=[jax.ShapeDtypeStruct((nt * MOE_ROWS, D + MOE_EXTRA), BF16),
                   jax.ShapeDtypeStruct((nt, SUBLANES, MOE_TOK), jnp.int32),
                   jax.ShapeDtypeStruct((nt, N_EXPERTS, LANES), jnp.int32)],
        compiler_params=_cparams("parallel"),
        name="moe_sort",
    )(h1, g, wr_hi, wr_lo, br)


def _moe_tables(cnt, n_mm_tiles):
    nt = cnt.shape[0]
    pc = (cnt + (MOE_CHUNK - 1)) // MOE_CHUNK
    run_end = jnp.cumsum(pc, axis=1)
    run_first = run_end - pc
    per_expert = jnp.sum(pc, axis=0)
    mm_tiles = (per_expert + (MOE_CPM - 1)) // MOE_CPM
    mm_end = jnp.cumsum(mm_tiles)
    slot_base = (mm_end - mm_tiles)[None, :] * MOE_CPM + (jnp.cumsum(pc, axis=0) - pc)
    k = jnp.arange(MOE_CPT, dtype=jnp.int32)[None, :, None]
    expert_of = jnp.sum((run_end[:, None, :] <= k).astype(jnp.int32), axis=2)
    valid = expert_of < N_EXPERTS
    pick = expert_of[:, :, None] == jnp.arange(N_EXPERTS, dtype=jnp.int32)
    take = lambda a: jnp.sum(jnp.where(pick, a[:, None, :], 0), axis=2)
    slot = take(slot_base) + (k[:, :, 0] - take(run_first))
    src = (jnp.arange(nt, dtype=jnp.int32)[:, None] * MOE_CPT + k[:, :, 0]).reshape(-1)
    slot = jnp.where(valid, slot, n_mm_tiles * MOE_CPM).reshape(-1)
    gather_tbl = jnp.zeros((n_mm_tiles * MOE_CPM,), jnp.int32).at[slot].set(src, mode="drop")
    return_tbl = jnp.where(valid.reshape(-1), slot, 0).astype(jnp.int32)
    g = jnp.arange(n_mm_tiles, dtype=jnp.int32)
    n_used = mm_end[-1].astype(jnp.int32)
    tile_expert = jnp.sum((mm_end[None, :] <= jnp.minimum(g, n_used - 1)[:, None]).astype(jnp.int32), axis=1)
    return gather_tbl, return_tbl, tile_expert.astype(jnp.int32), n_used.reshape(1)


MOE_GATHER_UNROLL = 8


def _chunk_gather(tbl_ref, src_hbm, buf, sem, chunks_per_tile):
    def chunk_copy(tile, c, slot):
        src = pl.multiple_of(tbl_ref[tile * chunks_per_tile + c] * MOE_CHUNK, MOE_CHUNK)
        dst = pl.multiple_of(c * MOE_CHUNK, MOE_CHUNK)
        return pltpu.make_async_copy(src_hbm.at[pl.ds(src, MOE_CHUNK), :],
                                     buf.at[slot, pl.ds(dst, MOE_CHUNK), :], sem.at[slot])

    def for_each_chunk(tile, slot, act):
        def body(c, carry):
            act(chunk_copy(tile, c, slot))
            return carry
        lax.fori_loop(0, chunks_per_tile, body, 0, unroll=MOE_GATHER_UNROLL)

    def start(tile, slot):
        for_each_chunk(tile, slot, lambda copy: copy.start())

    def wait(tile, slot):
        for_each_chunk(tile, slot, lambda copy: copy.wait())

    return start, wait


def _moe_expert_kernel(gt_ref, te_ref, nu_ref, xs_hbm, wg_ref, wu_ref, wd_ref, o_ref,
                       wg_bf, wu_bf, wd_bf, x_buf, x_sem):
    D = o_ref.shape[1]
    g = pl.program_id(0)
    n_used = nu_ref[0]
    used = g < n_used

    start_gather, wait_gather = _chunk_gather(gt_ref, xs_hbm, x_buf, x_sem, MOE_CPM)

    @pl.when((g == 0) & (n_used > 0))
    def _():
        start_gather(0, 0)

    @pl.when(g + 1 < n_used)
    def _():
        start_gather(g + 1, (g + 1) % 2)

    @pl.when(used & ((g == 0) | (te_ref[g] != te_ref[jnp.maximum(g - 1, 0)])))
    def _():
        wg_bf[...] = wg_ref[0].astype(BF16)
        wu_bf[...] = wu_ref[0].astype(BF16)
        wd_bf[...] = wd_ref[0].astype(BF16)

    @pl.when(used)
    def _():
        slot = g % 2
        wait_gather(g, slot)
        xs = x_buf[slot]
        x = xs[:, :D]
        extra = xs[:, D:].astype(F32)
        c = extra[:, 0:1] + extra[:, 1:2]
        gate = jnp.dot(x, wg_bf[...], preferred_element_type=F32)
        up = jnp.dot(x, wu_bf[...], preferred_element_type=F32)
        hh = (gate * jax.nn.sigmoid(gate)) * up * c
        o_ref[...] = jnp.dot(hh.astype(BF16), wd_bf[...], preferred_element_type=F32).astype(o_ref.dtype)

    @pl.when(jnp.logical_not(used))
    def _():
        o_ref[...] = jnp.zeros(o_ref.shape, o_ref.dtype)


def _moe_experts(xs, gather_tbl, tile_expert, n_used, wg, wu, wd, n_mm_tiles):
    W = xs.shape[1]
    E, D, FF = wg.shape

    grid_spec = pltpu.PrefetchScalarGridSpec(
        num_scalar_prefetch=3,
        grid=(n_mm_tiles,),
        in_specs=[pl.BlockSpec(memory_space=pl.ANY),
                  pl.BlockSpec((1, D, FF), lambda g, gt, te, nu: (te[g], 0, 0)),
                  pl.BlockSpec((1, D, FF), lambda g, gt, te, nu: (te[g], 0, 0)),
                  pl.BlockSpec((1, FF, D), lambda g, gt, te, nu: (te[g], 0, 0))],
        out_specs=pl.BlockSpec((MOE_TM, D), lambda g, gt, te, nu: (g, 0)),
        scratch_shapes=[pltpu.VMEM((D, FF), BF16), pltpu.VMEM((D, FF), BF16), pltpu.VMEM((FF, D), BF16),
                        pltpu.VMEM((2, MOE_TM, W), BF16), pltpu.SemaphoreType.DMA((2,))],
    )
    return pl.pallas_call(
        _moe_expert_kernel,
        grid_spec=grid_spec,
        out_shape=jax.ShapeDtypeStruct((n_mm_tiles * MOE_TM, D), BF16),
        compiler_params=_cparams("arbitrary"),
        name="moe_experts",
    )(gather_tbl, tile_expert, n_used, xs, wg, wu, wd)


def _finish_kernel(rt_ref, ys_hbm, dest_ref, h_ref, p_ref, gp_ref, gf_ref, wg_ref, wp_ref, o_ref, y_buf, y_sem):
    tok = h_ref.shape[0]
    i = pl.program_id(0)
    slot = i % 2
    start_gather, wait_gather = _chunk_gather(rt_ref, ys_hbm, y_buf, y_sem, MOE_CPT)

    @pl.when(i == 0)
    def _():
        start_gather(0, 0)

    @pl.when(i + 1 < pl.num_programs(0))
    def _():
        start_gather(i + 1, (i + 1) % 2)

    wait_gather(i, slot)
    dcol = dest_ref[0].astype(F32).T
    lane = lax.broadcasted_iota(jnp.int32, (tok, MOE_ROWS), 1).astype(F32)
    unsort = ((lane == dcol[:, 0:1]) | (lane == dcol[:, 1:2])).astype(BF16)
    h = h_ref[...] + jnp.dot(unsort, y_buf[slot], preferred_element_type=F32)
    t = _rms(h, gp_ref[...]).astype(BF16)
    gate = jax.nn.sigmoid(jnp.dot(t, wg_ref[...], preferred_element_type=F32))
    emb = jnp.dot(p_ref[...].astype(BF16), wp_ref[...], preferred_element_type=F32)
    o_ref[...] = _rms(h + emb * gate, gf_ref[...])


def _finish(ys, return_tbl, dest, h1, p2, g_ple, g_final, wg, wp):
    T, D = h1.shape
    P = p2.shape[1]
    nt = T // MOE_TOK

    vec = pl.BlockSpec((1, D), lambda i, rt: (0, 0))
    grid_spec = pltpu.PrefetchScalarGridSpec(
        num_scalar_prefetch=1,
        grid=(nt,),
        in_specs=[
            pl.BlockSpec(memory_space=pl.ANY),
            pl.BlockSpec((1, SUBLANES, MOE_TOK), lambda i, rt: (i, 0, 0)),
            pl.BlockSpec((MOE_TOK, D), lambda i, rt: (i, 0)),
            pl.BlockSpec((MOE_TOK, P), lambda i, rt: (i, 0)),
            vec, vec,
            pl.BlockSpec((D, D), lambda i, rt: (0, 0), pipeline_mode=pl.Buffered(1)),
            pl.BlockSpec((P, D), lambda i, rt: (0, 0), pipeline_mode=pl.Buffered(1))],
        out_specs=pl.BlockSpec((MOE_TOK, D), lambda i, rt: (i, 0)),
        scratch_shapes=[pltpu.VMEM((2, MOE_ROWS, D), BF16), pltpu.SemaphoreType.DMA((2,))],
    )
    return pl.pallas_call(
        _finish_kernel,
        grid_spec=grid_spec,
        out_shape=jax.ShapeDtypeStruct((T, D), F32),
        compiler_params=_cparams("arbitrary"),
        name="finish",
    )(return_tbl, ys, dest, h1, p2, g_ple, g_final, wg, wp)


def kernel(x, p, g_mix, w_in, rel_bias, w_attn_br, conv_w, conv_b, ln_g, ln_b, w_conv_br, w_o, g_ffn,
           w_router_g, b_router_g, w_router_e, b_router_e, w_e_gate, w_e_up, w_e_down, g_ple, w_ple_gate,
           w_ple_proj, g_final):
    B, S, D = x.shape
    T = B * S
    depth = w_in.shape[0]
    width = N_HEADS * HEAD_DIM
    assert D == width and conv_w.shape[2] == D, "column-block indexing assumes all branch widths equal d_model"
    assert S % MOBA_BLOCK == 0
    assert depth == 1, "the final RMSNorm is fused into the last stage of a single layer"
    a_blk = 3 * width // D
    gate_blk = a_blk + 2

    h = x.reshape(T, D)
    for i in range(depth):
        proj = _in_proj(h, g_mix[i][None], w_in[i].astype(BF16))
        proj3 = proj.reshape(B, S, -1)
        attn = _moba(proj3, rel_bias)
        conv, (wa, wc, wo, w_gate) = _conv(proj3, conv_w[i], conv_b[i][None], ln_g[i][None], ln_b[i][None], a_blk,
                                           side_weights=[w_attn_br[i], w_conv_br[i], w_o[i], w_ple_gate[i]])
        h = _mix(h, attn.reshape(T, D), conv.reshape(T, D), proj, wa, wc, wo, gate_blk)
        xs, dest, cnt = _moe_sort(h, g_ffn[i][None], w_router_e[i], b_router_e[i], w_router_g[i], b_router_g[i])
        n_tok_tiles = T // MOE_TOK
        worst_chunks = (MOE_TOPK * T + n_tok_tiles * N_EXPERTS * (MOE_CHUNK - 1)) // MOE_CHUNK
        n_mm_tiles = -(-worst_chunks // MOE_CPM) + N_EXPERTS
        gather_tbl, return_tbl, tile_expert, n_used = _moe_tables(cnt[:, :, 0], n_mm_tiles)
        ys = _moe_experts(xs, gather_tbl, tile_expert, n_used,
                          w_e_gate[i], w_e_up[i], w_e_down[i], n_mm_tiles)
        out = _finish(ys, return_tbl, dest, h, p[i].reshape(T, -1), g_ple[i][None], g_final[None],
                      w_gate, w_ple_proj[i].astype(BF16))
    return out.reshape(B, S, D)
```

```python
import functools
import math

import numpy as np
import jax
import jax.numpy as jnp
from jax import lax
from jax.experimental import pallas as pl
from jax.experimental.pallas import tpu as pltpu

F32 = jnp.float32
BF16 = jnp.bfloat16

N_HEADS = 16
HEAD_DIM = 128
MOBA_BLOCK = 256
MOBA_TOPK = 3
N_BUCKETS = 32
MAX_DISTANCE = 128
CONV_KERNEL = 31
N_GROUPS = 4
EXPERTS_PER_GROUP = 4
N_EXPERTS = N_GROUPS * EXPERTS_PER_GROUP
EPS = 1e-6
NEG = -1e30
LOG2E = 1.4426950408889634

V7X_VMEM_LIMIT_BYTES = 56 * 1024 * 1024
LANES = 128
SUBLANES = 8
BF16_TILE_ROWS = 2 * SUBLANES

_NT = (((1,), (1,)), ((), ()))


def _cparams(*sem):
    return pltpu.CompilerParams(dimension_semantics=sem, vmem_limit_bytes=V7X_VMEM_LIMIT_BYTES)


def _resident(shape):
    return pl.BlockSpec(shape, lambda *_: (0,) * len(shape), pipeline_mode=pl.Buffered(1))


def _sigmoid(x):
    return 0.5 * jnp.tanh(0.5 * x) + 0.5


def _rms(x, g):
    return x * lax.rsqrt(jnp.mean(x * x, axis=-1, keepdims=True) + EPS) * g


IN_PROJ_NORM_SPLIT = 4


def _in_proj_kernel(x_ref, g_ref, w_ref, o_ref, xn_ref):
    first = pl.program_id(1) == 0

    @pl.when(first)
    def _():
        sub = x_ref.shape[0] // IN_PROJ_NORM_SPLIT
        for r in range(IN_PROJ_NORM_SPLIT):
            rows = slice(r * sub, (r + 1) * sub)
            xn = _rms(x_ref[rows, :], g_ref[...]).astype(BF16)
            xn_ref[rows, :] = xn
            o_ref[rows, :] = jnp.dot(xn, w_ref[...], preferred_element_type=F32).astype(o_ref.dtype)

    @pl.when(jnp.logical_not(first))
    def _():
        o_ref[...] = jnp.dot(xn_ref[...], w_ref[...], preferred_element_type=F32).astype(o_ref.dtype)


def _in_proj(x2, g, w, tm=1024, tn=2048):
    T, D = x2.shape
    N = w.shape[1]
    return pl.pallas_call(
        _in_proj_kernel,
        grid=(T // tm, N // tn),
        in_specs=[pl.BlockSpec((tm, D), lambda i, j: (i, 0)),
                  pl.BlockSpec((1, D), lambda i, j: (0, 0)),
                  pl.BlockSpec((D, tn), lambda i, j: (0, j))],
        out_specs=pl.BlockSpec((tm, tn), lambda i, j: (i, j)),
        out_shape=jax.ShapeDtypeStruct((T, N), BF16),
        scratch_shapes=[pltpu.VMEM((tm, D), BF16)],
        compiler_params=_cparams("parallel", "arbitrary"),
        name="in_proj",
    )(x2, g, w)


MOBA_PV_ROWS = HEAD_DIM + BF16_TILE_ROWS
MOBA_HEADS_PER_STEP = 4


def _rel_buckets_t():
    k = np.arange(MOBA_BLOCK)[:, None]
    q = np.arange(MOBA_BLOCK)[None, :]
    dist = np.stack([q - k, MOBA_BLOCK + q - k]).astype(np.int32)
    n = np.maximum(dist, 0)
    max_exact = N_BUCKETS // 2
    nf = np.maximum(n, 1).astype(np.float32)
    large = max_exact + (np.log(nf / np.float32(max_exact)) / np.float32(math.log(MAX_DISTANCE / max_exact))
                         * np.float32(N_BUCKETS - max_exact)).astype(np.int32)
    large = np.minimum(large, N_BUCKETS - 1)
    return np.where(n < max_exact, n, large).astype(np.int32)


def _moba_kernel(tbl_ref, bkt_ref, q_ref, k_ref, v_ref, o_ref,
                 kmean_ref, vt_ref, bias_ref, qt_ref, sel_ref, s_ref, top_ref, p_ref, m_ref, a_ref, acc_ref,
                 *, n_blocks):
    hg = pl.program_id(0)
    b = pl.program_id(1)
    BS = MOBA_BLOCK
    HPS = MOBA_HEADS_PER_STEP
    lanes = [slice(hh * HEAD_DIM, (hh + 1) * HEAD_DIM) for hh in range(HPS)]

    @pl.when(b == 0)
    def _build_bias():
        krow = lax.broadcasted_iota(jnp.int32, (BS, BS), 0)
        qcol = lax.broadcasted_iota(jnp.int32, (BS, BS), 1)
        for hh in range(HPS):
            bias_ref[hh] = jnp.zeros((2, BS, BS), F32)

            def fill(n, carry):
                bias_ref[hh] = jnp.where(bkt_ref[...] == n, tbl_ref[hg * HPS + hh, n] * LOG2E, bias_ref[hh])
                return carry

            lax.fori_loop(0, N_BUCKETS, fill, 0)
            bias_ref[hh, 0] = jnp.where(qcol >= krow, bias_ref[hh, 0], NEG)

    ones_row = (lax.broadcasted_iota(jnp.int32, (MOBA_PV_ROWS - HEAD_DIM, BS), 0) == 0).astype(BF16)

    def prep(n, carry):
        rows = pl.ds(pl.multiple_of(n * BS, BS), BS)
        for hh in range(HPS):
            kmean_ref[hh, pl.ds(n, 1), :] = jnp.mean(k_ref[0, rows, lanes[hh]].astype(F32), axis=0, keepdims=True)
            vt_ref[hh, n, :HEAD_DIM] = v_ref[0, rows, lanes[hh]].astype(F32).T.astype(BF16)
            vt_ref[hh, n, HEAD_DIM:] = ones_row
        return carry

    lax.fori_loop(0, n_blocks, prep, 0)

    q_block = functools.partial(_moba_q_block, hg=hg, tbl_ref=tbl_ref, q_ref=q_ref, k_ref=k_ref, o_ref=o_ref,
                                kmean_ref=kmean_ref, vt_ref=vt_ref, bias_ref=bias_ref, qt_ref=qt_ref,
                                sel_ref=sel_ref, s_ref=s_ref, top_ref=top_ref, p_ref=p_ref, m_ref=m_ref, a_ref=a_ref,
                                acc_ref=acc_ref, n_blocks=n_blocks)

    q_block(jnp.int32(0), finish_previous=False)

    def next_block(qi, carry):
        q_block(qi, finish_previous=True)
        return carry

    lax.fori_loop(1, n_blocks, next_block, 0)
    _moba_close(jnp.int32(n_blocks - 1), vt_ref=vt_ref, p_ref=p_ref, a_ref=a_ref, acc_ref=acc_ref, o_ref=o_ref)


def _moba_close(qi, *, vt_ref, p_ref, a_ref, acc_ref, o_ref):
    lanes = [slice(hh * HEAD_DIM, (hh + 1) * HEAD_DIM) for hh in range(MOBA_HEADS_PER_STEP)]
    q_rows = pl.ds(pl.multiple_of(qi * MOBA_BLOCK, MOBA_BLOCK), MOBA_BLOCK)
    for hh in range(MOBA_HEADS_PER_STEP):
        acc = a_ref[hh] * acc_ref[hh] + jnp.dot(vt_ref[hh, 0], p_ref[hh], preferred_element_type=F32)
        denom = acc[HEAD_DIM:HEAD_DIM + 1, :]
        out_t = acc[:HEAD_DIM, :] * (1.0 / denom)
        o_ref[0, q_rows, lanes[hh]] = out_t.T.astype(o_ref.dtype)


def _moba_q_block(qi, *, finish_previous, hg, tbl_ref, q_ref, k_ref, o_ref, kmean_ref, vt_ref, bias_ref, qt_ref,
                  sel_ref, s_ref, top_ref, p_ref, m_ref, a_ref, acc_ref, n_blocks):
    BS = MOBA_BLOCK
    HPS = MOBA_HEADS_PER_STEP
    scale = HEAD_DIM ** -0.5
    lanes = [slice(hh * HEAD_DIM, (hh + 1) * HEAD_DIM) for hh in range(HPS)]
    q_rows = pl.ds(pl.multiple_of(qi * BS, BS), BS)

    def qk_stage(hh, t, far=True):
        j0 = pl.multiple_of(jnp.maximum(qi - t, 0) * BS, BS)
        s = jnp.dot(k_ref[0, pl.ds(j0, BS), lanes[hh]], qt_ref[hh], preferred_element_type=F32)
        s_ref[hh] = s
        if far:
            top_ref[hh] = jnp.max(s, axis=0, keepdims=True)

    def softmax_stage(hh, t, bias_of=None, far_bias=None):
        chosen = (lax.shift_right_logical(sel_ref[hh], qi - t) & 1) == 1
        c2 = scale * LOG2E
        if bias_of is not None:
            st = s_ref[hh] * c2 + bias_of[...]
            shift = 0.0
            top = jnp.max(st, axis=0, keepdims=True)
        else:
            st = None
            shift = far_bias
            top = top_ref[hh] * c2 + shift
        m_prev = m_ref[hh]
        m_new = jnp.where(chosen, jnp.maximum(m_prev, top), m_prev)
        a_ref[hh] = jnp.exp2(m_prev - m_new)
        m_ref[hh] = m_new
        cut = jnp.where(chosen, m_new - shift, -NEG)
        if st is None:
            st = s_ref[hh] * c2
        p_ref[hh] = jnp.exp2(st - cut).astype(BF16)

    def pv_stage(hh, t):
        acc_ref[hh] = a_ref[hh] * acc_ref[hh] + jnp.dot(vt_ref[hh, qi - t], p_ref[hh],
                                                        preferred_element_type=F32)

    if finish_previous:
        _moba_close(qi - 1, vt_ref=vt_ref, p_ref=p_ref, a_ref=a_ref, acc_ref=acc_ref, o_ref=o_ref)

    for hh in range(HPS):
        qt = q_ref[0, q_rows, lanes[hh]].astype(F32).T.astype(BF16)
        qt_ref[hh] = qt
        gate = jnp.dot(kmean_ref[hh].astype(BF16), qt, preferred_element_type=F32)
        blk = lax.broadcasted_iota(jnp.int32, gate.shape, 0)
        blkf = blk.astype(F32)
        g = jnp.where(blk < qi, gate, NEG)
        bits = jnp.full((1, BS), lax.shift_left(jnp.int32(1), qi), jnp.int32)
        for _ in range(MOBA_TOPK):
            mx = jnp.max(g, axis=0, keepdims=True)
            ix = jnp.min(jnp.where(g == mx, blkf, float(n_blocks)), axis=0, keepdims=True).astype(jnp.int32)
            bits = bits | jnp.where(ix < qi, lax.shift_left(jnp.int32(1), ix), 0)
            g = jnp.where(blk == ix, -jnp.inf, g)
        sel_ref[hh] = bits
        m_ref[hh] = jnp.full((1, BS), NEG, F32)
        acc_ref[hh] = jnp.zeros((MOBA_PV_ROWS, BS), F32)
        qk_stage(hh, 0, far=False)

    for hh in range(HPS):
        softmax_stage(hh, 0, bias_ref.at[hh, 0])
    if not finish_previous:
        return

    for hh in range(HPS):
        qk_stage(hh, 1, far=False)
    for hh in range(HPS):
        pv_stage(hh, 0)
        softmax_stage(hh, 1, bias_ref.at[hh, 1])
        qk_stage(hh, 2)

    def sweep(t):
        for hh in range(HPS):
            pv_stage(hh, t - 1)
            softmax_stage(hh, t, far_bias=tbl_ref[hg * HPS + hh, N_BUCKETS - 1] * LOG2E)
            qk_stage(hh, t + 1)

    n_far = qi - 1

    def sweep_pair(i, carry):
        sweep(2 + 2 * i)
        sweep(3 + 2 * i)
        return carry

    lax.fori_loop(0, n_far // 2, sweep_pair, 0)

    @pl.when(n_far % 2 == 1)
    def _odd_block():
        sweep(qi)


def _moba(proj3, rel_bias):
    B, S, _ = proj3.shape
    BS = MOBA_BLOCK
    HPS = MOBA_HEADS_PER_STEP
    W = HPS * HEAD_DIM
    nb = S // BS
    n_groups = N_HEADS // HPS
    bkt = jnp.asarray(_rel_buckets_t())
    tbl = rel_bias.astype(F32).T
    grid_spec = pltpu.PrefetchScalarGridSpec(
        num_scalar_prefetch=1,
        grid=(n_groups, B),
        in_specs=[pl.BlockSpec((2, BS, BS), lambda h, b, t: (0, 0, 0)),
                  pl.BlockSpec((1, S, W), lambda h, b, t: (b, 0, h)),
                  pl.BlockSpec((1, S, W), lambda h, b, t: (b, 0, n_groups + h)),
                  pl.BlockSpec((1, S, W), lambda h, b, t: (b, 0, 2 * n_groups + h))],
        out_specs=pl.BlockSpec((1, S, W), lambda h, b, t: (b, 0, h)),
        scratch_shapes=[pltpu.VMEM((HPS, nb, HEAD_DIM), F32),
                        pltpu.VMEM((HPS, nb, MOBA_PV_ROWS, BS), BF16),
                        pltpu.VMEM((HPS, 2, BS, BS), F32),
                        pltpu.VMEM((HPS, HEAD_DIM, BS), BF16),
                        pltpu.VMEM((HPS, 1, BS), jnp.int32),
                        pltpu.VMEM((HPS, BS, BS), F32),
                        pltpu.VMEM((HPS, 1, BS), F32),
                        pltpu.VMEM((HPS, BS, BS), BF16),
                        pltpu.VMEM((HPS, 1, BS), F32),
                        pltpu.VMEM((HPS, 1, BS), F32),
                        pltpu.VMEM((HPS, MOBA_PV_ROWS, BS), F32)],
    )
    return pl.pallas_call(
        functools.partial(_moba_kernel, n_blocks=nb),
        grid_spec=grid_spec,
        out_shape=jax.ShapeDtypeStruct((B, S, N_HEADS * HEAD_DIM), BF16),
        compiler_params=_cparams("arbitrary", "arbitrary"),
        name="moba",
    )(tbl, bkt, proj3, proj3, proj3)


CONV_HALO = 32
CONV_ROWS = 64
CONV_COLS = 256


def _conv_kernel(a_ref, g_ref, ah_ref, gh_ref, w_ref, cb_ref, lng_ref, lnb_ref, *refs, ts, n_side):
    side_in, (o_ref,), side_out = refs[:n_side], refs[n_side:n_side + 1], refs[n_side + 1:2 * n_side + 1]
    u_ref, sh_ref, c_ref = refs[2 * n_side + 1:]
    for src, dst in zip(side_in, side_out):
        dst[...] = src[...].astype(dst.dtype)

    i = pl.program_id(1)
    D = u_ref.shape[1]
    u_ref[CONV_HALO:, :] = a_ref[0].astype(F32) * _sigmoid(g_ref[0].astype(F32))
    uh = ah_ref[0].astype(F32) * _sigmoid(gh_ref[0].astype(F32))
    u_ref[:CONV_HALO, :] = jnp.where(i > 0, uh, 0.0)

    first = CONV_HALO - (CONV_KERNEL - 1)
    sh_rows = sh_ref.shape[1]

    def col_body(c, carry):
        cols = pl.ds(pl.multiple_of(c * CONV_COLS, CONV_COLS), CONV_COLS)
        for s in range(1, SUBLANES):
            sh_ref[s] = u_ref[s:s + sh_rows, cols]
        for r in range(ts // CONV_ROWS):
            acc = jnp.zeros((CONV_ROWS, CONV_COLS), F32)
            for k in range(CONV_KERNEL):
                s = (first + k) % SUBLANES
                r0 = r * CONV_ROWS + first + k - s
                tap = u_ref[r0:r0 + CONV_ROWS, cols] if s == 0 else sh_ref[s, r0:r0 + CONV_ROWS, :]
                acc = acc + jnp.tile(w_ref[k, :, cols], (CONV_ROWS // SUBLANES, 1)) * tap
            c_ref[r * CONV_ROWS:(r + 1) * CONV_ROWS, cols] = acc + cb_ref[:, cols]
        return carry

    lax.fori_loop(0, D // CONV_COLS, col_body, 0)

    y = c_ref[...]
    mu = jnp.mean(y, axis=-1, keepdims=True)
    yc = y - mu
    var = jnp.mean(yc * yc, axis=-1, keepdims=True)
    z = yc * lax.rsqrt(var + EPS) * lng_ref[...] + lnb_ref[...]
    o_ref[0] = (z * _sigmoid(z)).astype(o_ref.dtype)


def _conv(proj3, conv_w, conv_b, ln_g, ln_b, a_blk, side_weights, ts=512):
    B, S, _ = proj3.shape
    D = conv_w.shape[1]
    hpb = ts // CONV_HALO
    steps_per_seq = S // ts
    n_steps = B * steps_per_seq
    side_specs = [pl.BlockSpec((w.shape[0] // n_steps, w.shape[1]), lambda b, i: (b * steps_per_seq + i, 0))
                  for w in side_weights]
    w_rows = jnp.broadcast_to(conv_w[:, None, :], (CONV_KERNEL, SUBLANES, D))

    def halo(col):
        return pl.BlockSpec((1, CONV_HALO, D), lambda b, i: (b, jnp.maximum(i * hpb - 1, 0), col))

    vec = pl.BlockSpec((1, D), lambda b, i: (0, 0))
    conv_act, *side_bf16 = pl.pallas_call(
        functools.partial(_conv_kernel, ts=ts, n_side=len(side_weights)),
        grid=(B, steps_per_seq),
        in_specs=[pl.BlockSpec((1, ts, D), lambda b, i: (b, i, a_blk)),
                  pl.BlockSpec((1, ts, D), lambda b, i: (b, i, a_blk + 1)),
                  halo(a_blk), halo(a_blk + 1),
                  pl.BlockSpec((CONV_KERNEL, SUBLANES, D), lambda b, i: (0, 0, 0)),
                  vec, vec, vec] + side_specs,
        out_specs=[pl.BlockSpec((1, ts, D), lambda b, i: (b, i, 0))] + side_specs,
        out_shape=[jax.ShapeDtypeStruct((B, S, D), BF16)]
        + [jax.ShapeDtypeStruct(w.shape, BF16) for w in side_weights],
        scratch_shapes=[pltpu.VMEM((CONV_HALO + ts, D), F32),
                        pltpu.VMEM((SUBLANES, CONV_HALO + ts - SUBLANES, CONV_COLS), F32),
                        pltpu.VMEM((ts, D), F32)],
        compiler_params=_cparams("parallel", "arbitrary"),
        name="conv",
    )(proj3, proj3, proj3, proj3, w_rows, conv_b, ln_g, ln_b, *side_weights)
    return conv_act, side_bf16


def _mix_kernel(x_ref, at_ref, cv_ref, ga_ref, gc_ref, wa_ref, wc_ref, wo_ref, o_ref):
    ya = jnp.dot(at_ref[...], wa_ref[...], preferred_element_type=F32)
    yc = jnp.dot(cv_ref[...], wc_ref[...], preferred_element_type=F32)
    mixed = (jax.nn.sigmoid(ga_ref[...].astype(F32)) * ya + jax.nn.sigmoid(gc_ref[...].astype(F32)) * yc)
    o_ref[...] = x_ref[...] + jnp.dot(mixed.astype(BF16), wo_ref[...], preferred_element_type=F32)


def _mix(x2, attn2, conv2, proj2, wa, wc, wo, gate_blk, tm=256):
    T, D = x2.shape
    row = lambda col: pl.BlockSpec((tm, D), lambda i: (i, col))
    return pl.pallas_call(
        _mix_kernel,
        grid=(T // tm,),
        in_specs=[row(0), row(0), row(0), row(gate_blk), row(gate_blk + 1),
                  _resident((D, D)), _resident((D, D)), _resident((D, D))],
        out_specs=row(0),
        out_shape=jax.ShapeDtypeStruct((T, D), F32),
        compiler_params=_cparams("parallel"),
        name="mix",
    )(x2, attn2, conv2, proj2, proj2, wa, wc, wo)


MOE_TOK = 512
MOE_CHUNK = BF16_TILE_ROWS
MOE_TM = 512
MOE_TOPK = 2
_MOE_WORST_ROWS = MOE_TOPK * MOE_TOK + N_EXPERTS * (MOE_CHUNK - 1)
MOE_ROWS = -(-_MOE_WORST_ROWS // LANES) * LANES
MOE_CPT = MOE_ROWS // MOE_CHUNK
MOE_CPM = MOE_TM // MOE_CHUNK
MOE_EXTRA = LANES


def _route_t(lt):
    row = lax.broadcasted_iota(jnp.int32, lt.shape, 0)
    rowf = row.astype(F32)
    big = float(lt.shape[0])
    is_grp = (row >= N_EXPERTS) & (row < N_EXPERTS + N_GROUPS)
    cl = jnp.where(is_grp, lt, -jnp.inf)
    mg = jnp.max(cl, axis=0, keepdims=True)
    g_star = jnp.min(jnp.where(cl == mg, rowf, big), axis=0, keepdims=True).astype(jnp.int32) - N_EXPERTS
    p_g = 1.0 / jnp.sum(jnp.where(is_grp, jnp.exp(cl - mg), 0.0), axis=0, keepdims=True)
    in_grp = (row < N_EXPERTS) & ((row // EXPERTS_PER_GROUP) == g_star)
    f1 = jnp.where(in_grp, lt, -jnp.inf)
    v1 = jnp.max(f1, axis=0, keepdims=True)
    i1 = jnp.min(jnp.where(f1 == v1, rowf, big), axis=0, keepdims=True).astype(jnp.int32)
    f2 = jnp.where(row == i1, -jnp.inf, f1)
    v2 = jnp.max(f2, axis=0, keepdims=True)
    i2 = jnp.min(jnp.where(f2 == v2, rowf, big), axis=0, keepdims=True).astype(jnp.int32)
    e2 = jnp.exp(v2 - v1)
    return i1, i2, p_g / (1.0 + e2), p_g * e2 / (1.0 + e2)


def _split_bf16(x):
    hi = x.astype(BF16)
    return hi, (x - hi.astype(F32)).astype(BF16)


def _moe_sort_kernel(h_ref, g_ref, wh_ref, wl_ref, br_ref, xs_ref, dest_ref, cnt_ref):
    tok = h_ref.shape[0]
    D = h_ref.shape[1]
    hn = _rms(h_ref[...], g_ref[...])
    hn_hi, hn_lo = _split_bf16(hn)
    nt_dot = lambda a, b: lax.dot_general(a, b, _NT, preferred_element_type=F32)
    w_hi = wh_ref[...]
    both = nt_dot(jnp.concatenate([w_hi, wl_ref[...]], axis=0), hn_hi)
    lt = both[:MOE_ROUTE_ROWS] + both[MOE_ROUTE_ROWS:] + nt_dot(w_hi, hn_lo) + br_ref[...]
    i1, i2, c1, c2 = _route_t(lt)

    erow = lax.broadcasted_iota(jnp.int32, (N_EXPERTS, tok), 0)
    oh1 = erow == i1
    oh2 = erow == i2
    oh = (oh1 | oh2).astype(F32)
    earlier = (lax.broadcasted_iota(jnp.int32, (tok, tok), 0) < lax.broadcasted_iota(jnp.int32, (tok, tok), 1))
    cum = jnp.dot(oh.astype(BF16), earlier.astype(BF16), preferred_element_type=F32)
    cnt = jnp.sum(oh, axis=1, keepdims=True)
    n_chunks = jnp.floor((cnt + (MOE_CHUNK - 1)) * (1.0 / MOE_CHUNK))
    ncb = jnp.broadcast_to(n_chunks, (N_EXPERTS, LANES))
    erow_l = lax.broadcasted_iota(jnp.int32, (N_EXPERTS, LANES), 0)
    first_chunk = jnp.zeros((N_EXPERTS, LANES), F32)
    for e in range(N_EXPERTS - 1):
        first_chunk = first_chunk + jnp.where(erow_l > e, ncb[e:e + 1, :], 0.0)
    base = jnp.tile(first_chunk, (1, tok // LANES)) * MOE_CHUNK + cum
    d1 = jnp.sum(jnp.where(oh1, base, 0.0), axis=0, keepdims=True).astype(jnp.int32)
    d2 = jnp.sum(jnp.where(oh2, base, 0.0), axis=0, keepdims=True).astype(jnp.int32)

    srow = lax.broadcasted_iota(jnp.int32, (MOE_ROWS, tok), 0)
    p1 = srow == d1
    p2 = srow == d2
    perm1 = p1.astype(BF16)
    perm2 = p2.astype(BF16)
    xs_ref[:, :D] = jnp.dot(perm1 + perm2, hn_hi, preferred_element_type=F32).astype(BF16)
    xrow = lax.broadcasted_iota(jnp.int32, (MOE_EXTRA, tok), 0)

    def weight_rows(c):
        hi = c.astype(BF16).astype(F32)
        return jnp.where(xrow == 0, hi, jnp.where(xrow == 1, c - hi, 0.0)).astype(BF16)

    xs_ref[:, D:] = (nt_dot(perm1, weight_rows(c1)) + nt_dot(perm2, weight_rows(c2))).astype(BF16)

    r8 = lax.broadcasted_iota(jnp.int32, (SUBLANES, tok), 0)
    dest_ref[0] = jnp.where(r8 == 0, d1, jnp.where(r8 == 1, d2, 0))
    cnt_ref[0] = jnp.broadcast_to(cnt, (N_EXPERTS, LANES)).astype(jnp.int32)


MOE_ROUTE_ROWS = -(-(N_EXPERTS + N_GROUPS) // BF16_TILE_ROWS) * BF16_TILE_ROWS


def _moe_sort(h1, g, w_router_e, b_router_e, w_router_g, b_router_g):
    T, D = h1.shape
    nt = T // MOE_TOK
    pad = MOE_ROUTE_ROWS - N_EXPERTS - N_GROUPS
    wr = jnp.concatenate([w_router_e.T, w_router_g.T, jnp.zeros((pad, D), F32)], axis=0)
    wr_hi = wr.astype(BF16)
    wr_lo = (wr - wr_hi.astype(F32)).astype(BF16)
    br = jnp.concatenate([b_router_e, b_router_g, jnp.zeros((pad,), F32)])[:, None]
    return pl.pallas_call(
        _moe_sort_kernel,
        grid=(nt,),
        in_specs=[pl.BlockSpec((MOE_TOK, D), lambda i: (i, 0)),
                  pl.BlockSpec((1, D), lambda i: (0, 0)),
                  pl.BlockSpec((MOE_ROUTE_ROWS, D), lambda i: (0, 0)),
                  pl.BlockSpec((MOE_ROUTE_ROWS, D), lambda i: (0, 0)),
                  pl.BlockSpec((MOE_ROUTE_ROWS, 1), lambda i: (0, 0))],
        out_specs=[pl.BlockSpec((MOE_ROWS, D + MOE_EXTRA), lambda i: (i, 0)),
                   pl.BlockSpec((1, SUBLANES, MOE_TOK), lambda i: (i, 0, 0)),
                   pl.BlockSpec((1, N_EXPERTS, LANES), lambda i: (i, 0, 0))],
        out_shape=[jax.ShapeDtypeStruct((nt * MOE_ROWS, D + MOE_EXTRA), BF16),
                   jax.ShapeDtypeStruct((nt, SUBLANES, MOE_TOK), jnp.int32),
                   jax.ShapeDtypeStruct((nt, N_EXPERTS, LANES), jnp.int32)],
        compiler_params=_cparams("parallel"),
        name="moe_sort",
    )(h1, g, wr_hi, wr_lo, br)


def _moe_tables(cnt, n_mm_tiles):
    nt = cnt.shape[0]
    pc = (cnt + (MOE_CHUNK - 1)) // MOE_CHUNK
    run_end = jnp.cumsum(pc, axis=1)
    run_first = run_end - pc
    per_expert = jnp.sum(pc, axis=0)
    mm_tiles = (per_expert + (MOE_CPM - 1)) // MOE_CPM
    mm_end = jnp.cumsum(mm_tiles)
    slot_base = (mm_end - mm_tiles)[None, :] * MOE_CPM + (jnp.cumsum(pc, axis=0) - pc)
    k = jnp.arange(MOE_CPT, dtype=jnp.int32)[None, :, None]
    expert_of = jnp.sum((run_end[:, None, :] <= k).astype(jnp.int32), axis=2)
    valid = expert_of < N_EXPERTS
    pick = expert_of[:, :, None] == jnp.arange(N_EXPERTS, dtype=jnp.int32)
    take = lambda a: jnp.sum(jnp.where(pick, a[:, None, :], 0), axis=2)
    slot = take(slot_base) + (k[:, :, 0] - take(run_first))
    src = (jnp.arange(nt, dtype=jnp.int32)[:, None] * MOE_CPT + k[:, :, 0]).reshape(-1)
    slot = jnp.where(valid, slot, n_mm_tiles * MOE_CPM).reshape(-1)
    gather_tbl = jnp.zeros((n_mm_tiles * MOE_CPM,), jnp.int32).at[slot].set(src, mode="drop")
    return_tbl = jnp.where(valid.reshape(-1), slot, 0).astype(jnp.int32)
    g = jnp.arange(n_mm_tiles, dtype=jnp.int32)
    n_used = mm_end[-1].astype(jnp.int32)
    tile_expert = jnp.sum((mm_end[None, :] <= jnp.minimum(g, n_used - 1)[:, None]).astype(jnp.int32), axis=1)
    return gather_tbl, return_tbl, tile_expert.astype(jnp.int32), n_used.reshape(1)


MOE_GATHER_UNROLL = 8


def _chunk_gather(tbl_ref, src_hbm, buf, sem, chunks_per_tile):
    def chunk_copy(tile, c, slot):
        src = pl.multiple_of(tbl_ref[tile * chunks_per_tile + c] * MOE_CHUNK, MOE_CHUNK)
        dst = pl.multiple_of(c * MOE_CHUNK, MOE_CHUNK)
        return pltpu.make_async_copy(src_hbm.at[pl.ds(src, MOE_CHUNK), :],
                                     buf.at[slot, pl.ds(dst, MOE_CHUNK), :], sem.at[slot])

    def for_each_chunk(tile, slot, act):
        def body(c, carry):
            act(chunk_copy(tile, c, slot))
            return carry
        lax.fori_loop(0, chunks_per_tile, body, 0, unroll=MOE_GATHER_UNROLL)

    def start(tile, slot):
        for_each_chunk(tile, slot, lambda copy: copy.start())

    def wait(tile, slot):
        for_each_chunk(tile, slot, lambda copy: copy.wait())

    return start, wait


def _moe_expert_kernel(gt_ref, te_ref, nu_ref, xs_hbm, wg_ref, wu_ref, wd_ref, o_ref,
                       wg_bf, wu_bf, wd_bf, x_buf, x_sem):
    D = o_ref.shape[1]
    g = pl.program_id(0)
    n_used = nu_ref[0]
    used = g < n_used

    start_gather, wait_gather = _chunk_gather(gt_ref, xs_hbm, x_buf, x_sem, MOE_CPM)

    @pl.when((g == 0) & (n_used > 0))
    def _():
        start_gather(0, 0)

    @pl.when(g + 1 < n_used)
    def _():
        start_gather(g + 1, (g + 1) % 2)

    @pl.when(used & ((g == 0) | (te_ref[g] != te_ref[jnp.maximum(g - 1, 0)])))
    def _():
        wg_bf[...] = wg_ref[0].astype(BF16)
        wu_bf[...] = wu_ref[0].astype(BF16)
        wd_bf[...] = wd_ref[0].astype(BF16)

    @pl.when(used)
    def _():
        slot = g % 2
        wait_gather(g, slot)
        xs = x_buf[slot]
        x = xs[:, :D]
        extra = xs[:, D:].astype(F32)
        c = extra[:, 0:1] + extra[:, 1:2]
        gate = jnp.dot(x, wg_bf[...], preferred_element_type=F32)
        up = jnp.dot(x, wu_bf[...], preferred_element_type=F32)
        hh = (gate * jax.nn.sigmoid(gate)) * up * c
        o_ref[...] = jnp.dot(hh.astype(BF16), wd_bf[...], preferred_element_type=F32).astype(o_ref.dtype)

    @pl.when(jnp.logical_not(used))
    def _():
        o_ref[...] = jnp.zeros(o_ref.shape, o_ref.dtype)


def _moe_experts(xs, gather_tbl, tile_expert, n_used, wg, wu, wd, n_mm_tiles):
    W = xs.shape[1]
    E, D, FF = wg.shape

    grid_spec = pltpu.PrefetchScalarGridSpec(
        num_scalar_prefetch=3,
        grid=(n_mm_tiles,),
        in_specs=[pl.BlockSpec(memory_space=pl.ANY),
                  pl.BlockSpec((1, D, FF), lambda g, gt, te, nu: (te[g], 0, 0)),
                  pl.BlockSpec((1, D, FF), lambda g, gt, te, nu: (te[g], 0, 0)),
                  pl.BlockSpec((1, FF, D), lambda g, gt, te, nu: (te[g], 0, 0))],
        out_specs=pl.BlockSpec((MOE_TM, D), lambda g, gt, te, nu: (g, 0)),
        scratch_shapes=[pltpu.VMEM((D, FF), BF16), pltpu.VMEM((D, FF), BF16), pltpu.VMEM((FF, D), BF16),
                        pltpu.VMEM((2, MOE_TM, W), BF16), pltpu.SemaphoreType.DMA((2,))],
    )
    return pl.pallas_call(
        _moe_expert_kernel,
        grid_spec=grid_spec,
        out_shape=jax.ShapeDtypeStruct((n_mm_tiles * MOE_TM, D), BF16),
        compiler_params=_cparams("arbitrary"),
        name="moe_experts",
    )(gather_tbl, tile_expert, n_used, xs, wg, wu, wd)


FINISH_COLS = 512


def _finish_kernel(rt_ref, ys_hbm, dest_ref, h_ref, p_ref, gp_ref, gf_ref, wg_ref, wp_ref, o_ref, y_buf, y_sem):
    tok = h_ref.shape[0]
    i = pl.program_id(0)
    slot = i % 2
    start_gather, wait_gather = _chunk_gather(rt_ref, ys_hbm, y_buf, y_sem, MOE_CPT)

    @pl.when(i == 0)
    def _():
        start_gather(0, 0)

    @pl.when(i + 1 < pl.num_programs(0))
    def _():
        start_gather(i + 1, (i + 1) % 2)

    wait_gather(i, slot)
    dcol = dest_ref[0].astype(F32).T
    lane = lax.broadcasted_iota(jnp.int32, (tok, MOE_ROWS), 1).astype(F32)
    unsort = ((lane == dcol[:, 0:1]) | (lane == dcol[:, 1:2])).astype(BF16)
    o_ref[...] = h_ref[...] + jnp.dot(unsort, y_buf[slot], preferred_element_type=F32)
    t = _rms(o_ref[...], gp_ref[...]).astype(BF16)
    pe = p_ref[...].astype(BF16)
    D = o_ref.shape[1]
    for c in range(D // FINISH_COLS):
        cols = slice(c * FINISH_COLS, (c + 1) * FINISH_COLS)
        gate = jax.nn.sigmoid(jnp.dot(t, wg_ref[:, cols], preferred_element_type=F32))
        emb = jnp.dot(pe, wp_ref[:, cols], preferred_element_type=F32)
        o_ref[:, cols] += emb * gate
    o_ref[...] = _rms(o_ref[...], gf_ref[...])


def _finish(ys, return_tbl, dest, h1, p2, g_ple, g_final, wg, wp):
    T, D = h1.shape
    P = p2.shape[1]
    nt = T // MOE_TOK

    vec = pl.BlockSpec((1, D), lambda i, rt: (0, 0))
    grid_spec = pltpu.PrefetchScalarGridSpec(
        num_scalar_prefetch=1,
        grid=(nt,),
        in_specs=[
            pl.BlockSpec(memory_space=pl.ANY),
            pl.BlockSpec((1, SUBLANES, MOE_TOK), lambda i, rt: (i, 0, 0)),
            pl.BlockSpec((MOE_TOK, D), lambda i, rt: (i, 0)),
            pl.BlockSpec((MOE_TOK, P), lambda i, rt: (i, 0)),
            vec, vec,
            pl.BlockSpec((D, D), lambda i, rt: (0, 0), pipeline_mode=pl.Buffered(1)),
            pl.BlockSpec((P, D), lambda i, rt: (0, 0), pipeline_mode=pl.Buffered(1))],
        out_specs=pl.BlockSpec((MOE_TOK, D), lambda i, rt: (i, 0)),
        scratch_shapes=[pltpu.VMEM((2, MOE_ROWS, D), BF16), pltpu.SemaphoreType.DMA((2,))],
    )
    return pl.pallas_call(
        _finish_kernel,
        grid_spec=grid_spec,
        out_shape=jax.ShapeDtypeStruct((T, D), F32),
        compiler_params=_cparams("arbitrary"),
        name="finish",
    )(return_tbl, ys, dest, h1, p2, g_ple, g_final, wg, wp)


def kernel(x, p, g_mix, w_in, rel_bias, w_attn_br, conv_w, conv_b, ln_g, ln_b, w_conv_br, w_o, g_ffn,
           w_router_g, b_router_g, w_router_e, b_router_e, w_e_gate, w_e_up, w_e_down, g_ple, w_ple_gate,
           w_ple_proj, g_final):
    B, S, D = x.shape
    T = B * S
    depth = w_in.shape[0]
    width = N_HEADS * HEAD_DIM
    assert D == width and conv_w.shape[2] == D, "column-block indexing assumes all branch widths equal d_model"
    assert S % MOBA_BLOCK == 0
    assert depth == 1, "the final RMSNorm is fused into the last stage of a single layer"
    a_blk = 3 * width // D
    gate_blk = a_blk + 2

    h = x.reshape(T, D)
    for i in range(depth):
        proj = _in_proj(h, g_mix[i][None], w_in[i].astype(BF16))
        proj3 = proj.reshape(B, S, -1)
        attn = _moba(proj3, rel_bias)
        conv, (wa, wc, wo, w_gate) = _conv(proj3, conv_w[i], conv_b[i][None], ln_g[i][None], ln_b[i][None], a_blk,
                                           side_weights=[w_attn_br[i], w_conv_br[i], w_o[i], w_ple_gate[i]])
        h = _mix(h, attn.reshape(T, D), conv.reshape(T, D), proj, wa, wc, wo, gate_blk)
        xs, dest, cnt = _moe_sort(h, g_ffn[i][None], w_router_e[i], b_router_e[i], w_router_g[i], b_router_g[i])
        n_tok_tiles = T // MOE_TOK
        worst_chunks = (MOE_TOPK * T + n_tok_tiles * N_EXPERTS * (MOE_CHUNK - 1)) // MOE_CHUNK
        n_mm_tiles = -(-worst_chunks // MOE_CPM) + N_EXPERTS
        gather_tbl, return_tbl, tile_expert, n_used = _moe_tables(cnt[:, :, 0], n_mm_tiles)
        ys = _moe_experts(xs, gather_tbl, tile_expert, n_used,
                          w_e_gate[i], w_e_up[i], w_e_down[i], n_mm_tiles)
        out = _finish(ys, return_tbl, dest, h, p[i].reshape(T, -1), g_ple[i][None], g_final[None],
                      w_gate, w_ple_proj[i].astype(BF16))
    return out.reshape(B, S, D)
```

```python
import functools
import math

import numpy as np
import jax
import jax.numpy as jnp
from jax import lax
from jax.experimental import pallas as pl
from jax.experimental.pallas import tpu as pltpu

F32 = jnp.float32
BF16 = jnp.bfloat16

N_HEADS = 16
HEAD_DIM = 128
MOBA_BLOCK = 256
MOBA_TOPK = 3
N_BUCKETS = 32
MAX_DISTANCE = 128
CONV_KERNEL = 31
N_GROUPS = 4
EXPERTS_PER_GROUP = 4
N_EXPERTS = N_GROUPS * EXPERTS_PER_GROUP
EPS = 1e-6
NEG = -1e30
LOG2E = 1.4426950408889634

V7X_VMEM_LIMIT_BYTES = 56 * 1024 * 1024
LANES = 128
SUBLANES = 8
BF16_TILE_ROWS = 2 * SUBLANES

_NT = (((1,), (1,)), ((), ()))


def _cparams(*sem):
    return pltpu.CompilerParams(dimension_semantics=sem, vmem_limit_bytes=V7X_VMEM_LIMIT_BYTES)


def _resident(shape):
    return pl.BlockSpec(shape, lambda *_: (0,) * len(shape), pipeline_mode=pl.Buffered(1))


def _sigmoid(x):
    return 0.5 * jnp.tanh(0.5 * x) + 0.5


def _rms(x, g):
    return x * lax.rsqrt(jnp.mean(x * x, axis=-1, keepdims=True) + EPS) * g


IN_PROJ_NORM_SPLIT = 4


def _in_proj_kernel(x_ref, g_ref, w_ref, o_ref, xn_ref):
    first = pl.program_id(1) == 0

    @pl.when(first)
    def _():
        sub = x_ref.shape[0] // IN_PROJ_NORM_SPLIT
        for r in range(IN_PROJ_NORM_SPLIT):
            rows = slice(r * sub, (r + 1) * sub)
            xn = _rms(x_ref[rows, :], g_ref[...]).astype(BF16)
            xn_ref[rows, :] = xn
            o_ref[rows, :] = jnp.dot(xn, w_ref[...], preferred_element_type=F32).astype(o_ref.dtype)

    @pl.when(jnp.logical_not(first))
    def _():
        o_ref[...] = jnp.dot(xn_ref[...], w_ref[...], preferred_element_type=F32).astype(o_ref.dtype)


def _in_proj(x2, g, w, tm=1024, tn=2048):
    T, D = x2.shape
    N = w.shape[1]
    return pl.pallas_call(
        _in_proj_kernel,
        grid=(T // tm, N // tn),
        in_specs=[pl.BlockSpec((tm, D), lambda i, j: (i, 0)),
                  pl.BlockSpec((1, D), lambda i, j: (0, 0)),
                  pl.BlockSpec((D, tn), lambda i, j: (0, j))],
        out_specs=pl.BlockSpec((tm, tn), lambda i, j: (i, j)),
        out_shape=jax.ShapeDtypeStruct((T, N), BF16),
        scratch_shapes=[pltpu.VMEM((tm, D), BF16)],
        compiler_params=_cparams("parallel", "arbitrary"),
        name="in_proj",
    )(x2, g, w)


MOBA_PV_ROWS = HEAD_DIM + BF16_TILE_ROWS
MOBA_HEADS_PER_STEP = 4


def _rel_buckets_t():
    k = np.arange(MOBA_BLOCK)[:, None]
    q = np.arange(MOBA_BLOCK)[None, :]
    dist = np.stack([q - k, MOBA_BLOCK + q - k]).astype(np.int32)
    n = np.maximum(dist, 0)
    max_exact = N_BUCKETS // 2
    nf = np.maximum(n, 1).astype(np.float32)
    large = max_exact + (np.log(nf / np.float32(max_exact)) / np.float32(math.log(MAX_DISTANCE / max_exact))
                         * np.float32(N_BUCKETS - max_exact)).astype(np.int32)
    large = np.minimum(large, N_BUCKETS - 1)
    return np.where(n < max_exact, n, large).astype(np.int32)


def _moba_kernel(tbl_ref, bkt_ref, q_ref, k_ref, v_ref, o_ref,
                 kmean_ref, vt_ref, bias_ref, qt_ref, sel_ref, s_ref, top_ref, p_ref, m_ref, a_ref, acc_ref,
                 *, n_blocks):
    hg = pl.program_id(0)
    b = pl.program_id(1)
    BS = MOBA_BLOCK
    HPS = MOBA_HEADS_PER_STEP
    lanes = [slice(hh * HEAD_DIM, (hh + 1) * HEAD_DIM) for hh in range(HPS)]

    @pl.when(b == 0)
    def _build_bias():
        krow = lax.broadcasted_iota(jnp.int32, (BS, BS), 0)
        qcol = lax.broadcasted_iota(jnp.int32, (BS, BS), 1)
        for hh in range(HPS):
            bias_ref[hh] = jnp.zeros((2, BS, BS), F32)

            def fill(n, carry):
                bias_ref[hh] = jnp.where(bkt_ref[...] == n, tbl_ref[hg * HPS + hh, n] * LOG2E, bias_ref[hh])
                return carry

            lax.fori_loop(0, N_BUCKETS, fill, 0)
            bias_ref[hh, 0] = jnp.where(qcol >= krow, bias_ref[hh, 0], NEG)

    ones_row = (lax.broadcasted_iota(jnp.int32, (MOBA_PV_ROWS - HEAD_DIM, BS), 0) == 0).astype(BF16)

    def prep(n, carry):
        rows = pl.ds(pl.multiple_of(n * BS, BS), BS)
        for hh in range(HPS):
            kmean_ref[hh, pl.ds(n, 1), :] = jnp.mean(k_ref[0, rows, lanes[hh]].astype(F32), axis=0, keepdims=True)
            vt_ref[hh, n, :HEAD_DIM] = v_ref[0, rows, lanes[hh]].astype(F32).T.astype(BF16)
            vt_ref[hh, n, HEAD_DIM:] = ones_row
        return carry

    lax.fori_loop(0, n_blocks, prep, 0)

    q_block = functools.partial(_moba_q_block, hg=hg, tbl_ref=tbl_ref, q_ref=q_ref, k_ref=k_ref, o_ref=o_ref,
                                kmean_ref=kmean_ref, vt_ref=vt_ref, bias_ref=bias_ref, qt_ref=qt_ref,
                                sel_ref=sel_ref, s_ref=s_ref, top_ref=top_ref, p_ref=p_ref, m_ref=m_ref, a_ref=a_ref,
                                acc_ref=acc_ref, n_blocks=n_blocks)

    q_block(jnp.int32(0), finish_previous=False)

    def next_block(qi, carry):
        q_block(qi, finish_previous=True)
        return carry

    lax.fori_loop(1, n_blocks, next_block, 0)
    _moba_close(jnp.int32(n_blocks - 1), vt_ref=vt_ref, p_ref=p_ref, a_ref=a_ref, acc_ref=acc_ref, o_ref=o_ref)


def _moba_close(qi, *, vt_ref, p_ref, a_ref, acc_ref, o_ref):
    lanes = [slice(hh * HEAD_DIM, (hh + 1) * HEAD_DIM) for hh in range(MOBA_HEADS_PER_STEP)]
    q_rows = pl.ds(pl.multiple_of(qi * MOBA_BLOCK, MOBA_BLOCK), MOBA_BLOCK)
    for hh in range(MOBA_HEADS_PER_STEP):
        acc = a_ref[hh] * acc_ref[hh] + jnp.dot(vt_ref[hh, 0], p_ref[hh], preferred_element_type=F32)
        denom = acc[HEAD_DIM:HEAD_DIM + 1, :]
        out_t = acc[:HEAD_DIM, :] * (1.0 / denom)
        o_ref[0, q_rows, lanes[hh]] = out_t.T.astype(o_ref.dtype)


def _moba_q_block(qi, *, finish_previous, hg, tbl_ref, q_ref, k_ref, o_ref, kmean_ref, vt_ref, bias_ref, qt_ref,
                  sel_ref, s_ref, top_ref, p_ref, m_ref, a_ref, acc_ref, n_blocks):
    BS = MOBA_BLOCK
    HPS = MOBA_HEADS_PER_STEP
    scale = HEAD_DIM ** -0.5
    lanes = [slice(hh * HEAD_DIM, (hh + 1) * HEAD_DIM) for hh in range(HPS)]
    q_rows = pl.ds(pl.multiple_of(qi * BS, BS), BS)

    def qk_stage(hh, t, far=True):
        j0 = pl.multiple_of(jnp.maximum(qi - t, 0) * BS, BS)
        s = jnp.dot(k_ref[0, pl.ds(j0, BS), lanes[hh]], qt_ref[hh], preferred_element_type=F32)
        s_ref[hh] = s
        if far:
            top_ref[hh] = jnp.max(s, axis=0, keepdims=True)

    def softmax_stage(hh, t, bias_of=None, far_bias=None, slot=0):
        chosen = (lax.shift_right_logical(sel_ref[hh], qi - t) & 1) == 1
        c2 = scale * LOG2E
        if bias_of is not None:
            st = s_ref[hh] * c2 + bias_of[...]
            shift = 0.0
            top = jnp.max(st, axis=0, keepdims=True)
        else:
            st = None
            shift = far_bias
            top = top_ref[hh] * c2 + shift
        m_prev = m_ref[hh]
        m_new = jnp.where(chosen, jnp.maximum(m_prev, top), m_prev)
        a_ref[slot * HPS + hh] = jnp.exp2(m_prev - m_new)
        m_ref[hh] = m_new
        cut = jnp.where(chosen, m_new - shift, -NEG)
        if st is None:
            st = s_ref[hh] * c2
        p_ref[slot * HPS + hh] = jnp.exp2(st - cut).astype(BF16)

    def pv_stage(hh, t):
        acc_ref[hh] = a_ref[hh] * acc_ref[hh] + jnp.dot(vt_ref[hh, qi - t], p_ref[hh],
                                                        preferred_element_type=F32)

    if finish_previous:
        _moba_close(qi - 1, vt_ref=vt_ref, p_ref=p_ref, a_ref=a_ref, acc_ref=acc_ref, o_ref=o_ref)

    for hh in range(HPS):
        qt = q_ref[0, q_rows, lanes[hh]].astype(F32).T.astype(BF16)
        qt_ref[hh] = qt
        gate = jnp.dot(kmean_ref[hh].astype(BF16), qt, preferred_element_type=F32)
        blk = lax.broadcasted_iota(jnp.int32, gate.shape, 0)
        blkf = blk.astype(F32)
        g = jnp.where(blk < qi, gate, NEG)
        bits = jnp.full((1, BS), lax.shift_left(jnp.int32(1), qi), jnp.int32)
        for _ in range(MOBA_TOPK):
            mx = jnp.max(g, axis=0, keepdims=True)
            ix = jnp.min(jnp.where(g == mx, blkf, float(n_blocks)), axis=0, keepdims=True).astype(jnp.int32)
            bits = bits | jnp.where(ix < qi, lax.shift_left(jnp.int32(1), ix), 0)
            g = jnp.where(blk == ix, -jnp.inf, g)
        sel_ref[hh] = bits
        m_ref[hh] = jnp.full((1, BS), NEG, F32)
        acc_ref[hh] = jnp.zeros((MOBA_PV_ROWS, BS), F32)
        qk_stage(hh, 0, far=False)

    for hh in range(HPS):
        softmax_stage(hh, 0, bias_ref.at[hh, 0])
    if not finish_previous:
        return

    for hh in range(HPS):
        qk_stage(hh, 1, far=False)
    for hh in range(HPS):
        pv_stage(hh, 0)
        softmax_stage(hh, 1, bias_ref.at[hh, 1])
        qk_stage(hh, 2)

    def sweep(t):
        for hh in range(HPS):
            pv_stage(hh, t - 1)
            softmax_stage(hh, t, far_bias=tbl_ref[hg * HPS + hh, N_BUCKETS - 1] * LOG2E)
            qk_stage(hh, t + 1)

    n_far = qi - 1

    def sweep_pair(i, carry):
        t = 2 + 2 * i
        for hh in range(HPS):
            far_bias = tbl_ref[hg * HPS + hh, N_BUCKETS - 1] * LOG2E
            softmax_stage(hh, t, far_bias=far_bias, slot=1)
            qk_stage(hh, t + 1)
            older = a_ref[hh] * acc_ref[hh] + jnp.dot(vt_ref[hh, qi - (t - 1)], p_ref[hh],
                                                      preferred_element_type=F32)
            acc_ref[hh] = a_ref[HPS + hh] * older + jnp.dot(vt_ref[hh, qi - t], p_ref[HPS + hh],
                                                            preferred_element_type=F32)
            softmax_stage(hh, t + 1, far_bias=far_bias)
            qk_stage(hh, t + 2)
        return carry

    lax.fori_loop(0, n_far // 2, sweep_pair, 0)

    @pl.when(n_far % 2 == 1)
    def _odd_block():
        sweep(qi)


def _moba(proj3, rel_bias):
    B, S, _ = proj3.shape
    BS = MOBA_BLOCK
    HPS = MOBA_HEADS_PER_STEP
    W = HPS * HEAD_DIM
    nb = S // BS
    n_groups = N_HEADS // HPS
    bkt = jnp.asarray(_rel_buckets_t())
    tbl = rel_bias.astype(F32).T
    grid_spec = pltpu.PrefetchScalarGridSpec(
        num_scalar_prefetch=1,
        grid=(n_groups, B),
        in_specs=[pl.BlockSpec((2, BS, BS), lambda h, b, t: (0, 0, 0)),
                  pl.BlockSpec((1, S, W), lambda h, b, t: (b, 0, h)),
                  pl.BlockSpec((1, S, W), lambda h, b, t: (b, 0, n_groups + h)),
                  pl.BlockSpec((1, S, W), lambda h, b, t: (b, 0, 2 * n_groups + h))],
        out_specs=pl.BlockSpec((1, S, W), lambda h, b, t: (b, 0, h)),
        scratch_shapes=[pltpu.VMEM((HPS, nb, HEAD_DIM), F32),
                        pltpu.VMEM((HPS, nb, MOBA_PV_ROWS, BS), BF16),
                        pltpu.VMEM((HPS, 2, BS, BS), F32),
                        pltpu.VMEM((HPS, HEAD_DIM, BS), BF16),
                        pltpu.VMEM((HPS, 1, BS), jnp.int32),
                        pltpu.VMEM((HPS, BS, BS), F32),
                        pltpu.VMEM((HPS, 1, BS), F32),
                        pltpu.VMEM((2 * HPS, BS, BS), BF16),
                        pltpu.VMEM((HPS, 1, BS), F32),
                        pltpu.VMEM((2 * HPS, 1, BS), F32),
                        pltpu.VMEM((HPS, MOBA_PV_ROWS, BS), F32)],
    )
    return pl.pallas_call(
        functools.partial(_moba_kernel, n_blocks=nb),
        grid_spec=grid_spec,
        out_shape=jax.ShapeDtypeStruct((B, S, N_HEADS * HEAD_DIM), BF16),
        compiler_params=_cparams("arbitrary", "arbitrary"),
        name="moba",
    )(tbl, bkt, proj3, proj3, proj3)


CONV_HALO = 32
CONV_ROWS = 64
CONV_COLS = 256


def _conv_kernel(a_ref, g_ref, ah_ref, gh_ref, w_ref, cb_ref, lng_ref, lnb_ref, *refs, ts, n_side):
    side_in, (o_ref,), side_out = refs[:n_side], refs[n_side:n_side + 1], refs[n_side + 1:2 * n_side + 1]
    u_ref, sh_ref, c_ref = refs[2 * n_side + 1:]
    for src, dst in zip(side_in, side_out):
        dst[...] = src[...].astype(dst.dtype)

    i = pl.program_id(1)
    D = u_ref.shape[1]
    u_ref[CONV_HALO:, :] = a_ref[0].astype(F32) * _sigmoid(g_ref[0].astype(F32))
    uh = ah_ref[0].astype(F32) * _sigmoid(gh_ref[0].astype(F32))
    u_ref[:CONV_HALO, :] = jnp.where(i > 0, uh, 0.0)

    first = CONV_HALO - (CONV_KERNEL - 1)
    sh_rows = sh_ref.shape[1]

    def col_body(c, carry):
        cols = pl.ds(pl.multiple_of(c * CONV_COLS, CONV_COLS), CONV_COLS)
        for s in range(1, SUBLANES):
            sh_ref[s] = u_ref[s:s + sh_rows, cols]
        for r in range(ts // CONV_ROWS):
            acc = jnp.zeros((CONV_ROWS, CONV_COLS), F32)
            for k in range(CONV_KERNEL):
                s = (first + k) % SUBLANES
                r0 = r * CONV_ROWS + first + k - s
                tap = u_ref[r0:r0 + CONV_ROWS, cols] if s == 0 else sh_ref[s, r0:r0 + CONV_ROWS, :]
                acc = acc + jnp.tile(w_ref[k, :, cols], (CONV_ROWS // SUBLANES, 1)) * tap
            c_ref[r * CONV_ROWS:(r + 1) * CONV_ROWS, cols] = acc + cb_ref[:, cols]
        return carry

    lax.fori_loop(0, D // CONV_COLS, col_body, 0)

    y = c_ref[...]
    mu = jnp.mean(y, axis=-1, keepdims=True)
    yc = y - mu
    var = jnp.mean(yc * yc, axis=-1, keepdims=True)
    z = yc * lax.rsqrt(var + EPS) * lng_ref[...] + lnb_ref[...]
    o_ref[0] = (z * _sigmoid(z)).astype(o_ref.dtype)


def _conv(proj3, conv_w, conv_b, ln_g, ln_b, a_blk, side_weights, ts=512):
    B, S, _ = proj3.shape
    D = conv_w.shape[1]
    hpb = ts // CONV_HALO
    steps_per_seq = S // ts
    n_steps = B * steps_per_seq
    side_specs = [pl.BlockSpec((w.shape[0] // n_steps, w.shape[1]), lambda b, i: (b * steps_per_seq + i, 0))
                  for w in side_weights]
    w_rows = jnp.broadcast_to(conv_w[:, None, :], (CONV_KERNEL, SUBLANES, D))

    def halo(col):
        return pl.BlockSpec((1, CONV_HALO, D), lambda b, i: (b, jnp.maximum(i * hpb - 1, 0), col))

    vec = pl.BlockSpec((1, D), lambda b, i: (0, 0))
    conv_act, *side_bf16 = pl.pallas_call(
        functools.partial(_conv_kernel, ts=ts, n_side=len(side_weights)),
        grid=(B, steps_per_seq),
        in_specs=[pl.BlockSpec((1, ts, D), lambda b, i: (b, i, a_blk)),
                  pl.BlockSpec((1, ts, D), lambda b, i: (b, i, a_blk + 1)),
                  halo(a_blk), halo(a_blk + 1),
                  pl.BlockSpec((CONV_KERNEL, SUBLANES, D), lambda b, i: (0, 0, 0)),
                  vec, vec, vec] + side_specs,
        out_specs=[pl.BlockSpec((1, ts, D), lambda b, i: (b, i, 0))] + side_specs,
        out_shape=[jax.ShapeDtypeStruct((B, S, D), BF16)]
        + [jax.ShapeDtypeStruct(w.shape, BF16) for w in side_weights],
        scratch_shapes=[pltpu.VMEM((CONV_HALO + ts, D), F32),
                        pltpu.VMEM((SUBLANES, CONV_HALO + ts - SUBLANES, CONV_COLS), F32),
                        pltpu.VMEM((ts, D), F32)],
        compiler_params=_cparams("parallel", "arbitrary"),
        name="conv",
    )(proj3, proj3, proj3, proj3, w_rows, conv_b, ln_g, ln_b, *side_weights)
    return conv_act, side_bf16


def _mix_kernel(x_ref, at_ref, cv_ref, ga_ref, gc_ref, wa_ref, wc_ref, wo_ref, o_ref):
    ya = jnp.dot(at_ref[...], wa_ref[...], preferred_element_type=F32)
    yc = jnp.dot(cv_ref[...], wc_ref[...], preferred_element_type=F32)
    mixed = (jax.nn.sigmoid(ga_ref[...].astype(F32)) * ya + jax.nn.sigmoid(gc_ref[...].astype(F32)) * yc)
    o_ref[...] = x_ref[...] + jnp.dot(mixed.astype(BF16), wo_ref[...], preferred_element_type=F32)


def _mix(x2, attn2, conv2, proj2, wa, wc, wo, gate_blk, tm=256):
    T, D = x2.shape
    row = lambda col: pl.BlockSpec((tm, D), lambda i: (i, col))
    return pl.pallas_call(
        _mix_kernel,
        grid=(T // tm,),
        in_specs=[row(0), row(0), row(0), row(gate_blk), row(gate_blk + 1),
                  _resident((D, D)), _resident((D, D)), _resident((D, D))],
        out_specs=row(0),
        out_shape=jax.ShapeDtypeStruct((T, D), F32),
        compiler_params=_cparams("parallel"),
        name="mix",
    )(x2, attn2, conv2, proj2, proj2, wa, wc, wo)


MOE_TOK = 512
MOE_CHUNK = BF16_TILE_ROWS
MOE_TM = 512
MOE_TOPK = 2
_MOE_WORST_ROWS = MOE_TOPK * MOE_TOK + N_EXPERTS * (MOE_CHUNK - 1)
MOE_ROWS = -(-_MOE_WORST_ROWS // LANES) * LANES
MOE_CPT = MOE_ROWS // MOE_CHUNK
MOE_CPM = MOE_TM // MOE_CHUNK
MOE_EXTRA = LANES


def _route_t(lt):
    row = lax.broadcasted_iota(jnp.int32, lt.shape, 0)
    rowf = row.astype(F32)
    big = float(lt.shape[0])
    is_grp = (row >= N_EXPERTS) & (row < N_EXPERTS + N_GROUPS)
    cl = jnp.where(is_grp, lt, -jnp.inf)
    mg = jnp.max(cl, axis=0, keepdims=True)
    g_star = jnp.min(jnp.where(cl == mg, rowf, big), axis=0, keepdims=True).astype(jnp.int32) - N_EXPERTS
    p_g = 1.0 / jnp.sum(jnp.where(is_grp, jnp.exp(cl - mg), 0.0), axis=0, keepdims=True)
    in_grp = (row < N_EXPERTS) & ((row // EXPERTS_PER_GROUP) == g_star)
    f1 = jnp.where(in_grp, lt, -jnp.inf)
    v1 = jnp.max(f1, axis=0, keepdims=True)
    i1 = jnp.min(jnp.where(f1 == v1, rowf, big), axis=0, keepdims=True).astype(jnp.int32)
    f2 = jnp.where(row == i1, -jnp.inf, f1)
    v2 = jnp.max(f2, axis=0, keepdims=True)
    i2 = jnp.min(jnp.where(f2 == v2, rowf, big), axis=0, keepdims=True).astype(jnp.int32)
    e2 = jnp.exp(v2 - v1)
    return i1, i2, p_g / (1.0 + e2), p_g * e2 / (1.0 + e2)


def _split_bf16(x):
    hi = x.astype(BF16)
    return hi, (x - hi.astype(F32)).astype(BF16)


def _moe_sort_kernel(h_ref, g_ref, wh_ref, wl_ref, br_ref, xs_ref, dest_ref, cnt_ref):
    tok = h_ref.shape[0]
    D = h_ref.shape[1]
    hn = _rms(h_ref[...], g_ref[...])
    hn_hi, hn_lo = _split_bf16(hn)
    nt_dot = lambda a, b: lax.dot_general(a, b, _NT, preferred_element_type=F32)
    w_hi = wh_ref[...]
    both = nt_dot(jnp.concatenate([w_hi, wl_ref[...]], axis=0), hn_hi)
    lt = both[:MOE_ROUTE_ROWS] + both[MOE_ROUTE_ROWS:] + nt_dot(w_hi, hn_lo) + br_ref[...]
    i1, i2, c1, c2 = _route_t(lt)

    erow = lax.broadcasted_iota(jnp.int32, (N_EXPERTS, tok), 0)
    oh1 = erow == i1
    oh2 = erow == i2
    oh = (oh1 | oh2).astype(F32)
    earlier = (lax.broadcasted_iota(jnp.int32, (tok, tok), 0) < lax.broadcasted_iota(jnp.int32, (tok, tok), 1))
    cum = jnp.dot(oh.astype(BF16), earlier.astype(BF16), preferred_element_type=F32)
    cnt = jnp.sum(oh, axis=1, keepdims=True)
    n_chunks = jnp.floor((cnt + (MOE_CHUNK - 1)) * (1.0 / MOE_CHUNK))
    ncb = jnp.broadcast_to(n_chunks, (N_EXPERTS, LANES))
    erow_l = lax.broadcasted_iota(jnp.int32, (N_EXPERTS, LANES), 0)
    first_chunk = jnp.zeros((N_EXPERTS, LANES), F32)
    for e in range(N_EXPERTS - 1):
        first_chunk = first_chunk + jnp.where(erow_l > e, ncb[e:e + 1, :], 0.0)
    base = jnp.tile(first_chunk, (1, tok // LANES)) * MOE_CHUNK + cum
    d1 = jnp.sum(jnp.where(oh1, base, 0.0), axis=0, keepdims=True).astype(jnp.int32)
    d2 = jnp.sum(jnp.where(oh2, base, 0.0), axis=0, keepdims=True).astype(jnp.int32)

    srow = lax.broadcasted_iota(jnp.int32, (MOE_ROWS, tok), 0)
    p1 = srow == d1
    p2 = srow == d2
    perm1 = p1.astype(BF16)
    perm2 = p2.astype(BF16)
    xs_ref[:, :D] = jnp.dot(perm1 + perm2, hn_hi, preferred_element_type=F32).astype(BF16)
    xrow = lax.broadcasted_iota(jnp.int32, (MOE_EXTRA, tok), 0)

    def weight_rows(c):
        hi = c.astype(BF16).astype(F32)
        return jnp.where(xrow == 0, hi, jnp.where(xrow == 1, c - hi, 0.0)).astype(BF16)

    xs_ref[:, D:] = (nt_dot(perm1, weight_rows(c1)) + nt_dot(perm2, weight_rows(c2))).astype(BF16)

    r8 = lax.broadcasted_iota(jnp.int32, (SUBLANES, tok), 0)
    dest_ref[0] = jnp.where(r8 == 0, d1, jnp.where(r8 == 1, d2, 0))
    cnt_ref[0] = jnp.broadcast_to(cnt, (N_EXPERTS, LANES)).astype(jnp.int32)


MOE_ROUTE_ROWS = -(-(N_EXPERTS + N_GROUPS) // BF16_TILE_ROWS) * BF16_TILE_ROWS


def _moe_sort(h1, g, w_router_e, b_router_e, w_router_g, b_router_g):
    T, D = h1.shape
    nt = T // MOE_TOK
    pad = MOE_ROUTE_ROWS - N_EXPERTS - N_GROUPS
    wr = jnp.concatenate([w_router_e.T, w_router_g.T, jnp.zeros((pad, D), F32)], axis=0)
    wr_hi = wr.astype(BF16)
    wr_lo = (wr - wr_hi.astype(F32)).astype(BF16)
    br = jnp.concatenate([b_router_e, b_router_g, jnp.zeros((pad,), F32)])[:, None]
    return pl.pallas_call(
        _moe_sort_kernel,
        grid=(nt,),
        in_specs=[pl.BlockSpec((MOE_TOK, D), lambda i: (i, 0)),
                  pl.BlockSpec((1, D), lambda i: (0, 0)),
                  pl.BlockSpec((MOE_ROUTE_ROWS, D), lambda i: (0, 0)),
                  pl.BlockSpec((MOE_ROUTE_ROWS, D), lambda i: (0, 0)),
                  pl.BlockSpec((MOE_ROUTE_ROWS, 1), lambda i: (0, 0))],
        out_specs=[pl.BlockSpec((MOE_ROWS, D + MOE_EXTRA), lambda i: (i, 0)),
                   pl.BlockSpec((1, SUBLANES, MOE_TOK), lambda i: (i, 0, 0)),
                   pl.BlockSpec((1, N_EXPERTS, LANES), lambda i: (i, 0, 0))],
        out_shape=[jax.ShapeDtypeStruct((nt * MOE_ROWS, D + MOE_EXTRA), BF16),
                   jax.ShapeDtypeStruct((nt, SUBLANES, MOE_TOK), jnp.int32),
                   jax.ShapeDtypeStruct((nt, N_EXPERTS, LANES), jnp.int32)],
        compiler_params=_cparams("parallel"),
        name="moe_sort",
    )(h1, g, wr_hi, wr_lo, br)


def _moe_tables(cnt, n_mm_tiles):
    nt = cnt.shape[0]
    pc = (cnt + (MOE_CHUNK - 1)) // MOE_CHUNK
    run_end = jnp.cumsum(pc, axis=1)
    run_first = run_end - pc
    per_expert = jnp.sum(pc, axis=0)
    mm_tiles = (per_expert + (MOE_CPM - 1)) // MOE_CPM
    mm_end = jnp.cumsum(mm_tiles)
    slot_base = (mm_end - mm_tiles)[None, :] * MOE_CPM + (jnp.cumsum(pc, axis=0) - pc)
    k = jnp.arange(MOE_CPT, dtype=jnp.int32)[None, :, None]
    expert_of = jnp.sum((run_end[:, None, :] <= k).astype(jnp.int32), axis=2)
    valid = expert_of < N_EXPERTS
    pick = expert_of[:, :, None] == jnp.arange(N_EXPERTS, dtype=jnp.int32)
    take = lambda a: jnp.sum(jnp.where(pick, a[:, None, :], 0), axis=2)
    slot = take(slot_base) + (k[:, :, 0] - take(run_first))
    src = (jnp.arange(nt, dtype=jnp.int32)[:, None] * MOE_CPT + k[:, :, 0]).reshape(-1)
    slot = jnp.where(valid, slot, n_mm_tiles * MOE_CPM).reshape(-1)
    gather_tbl = jnp.zeros((n_mm_tiles * MOE_CPM,), jnp.int32).at[slot].set(src, mode="drop")
    return_tbl = jnp.where(valid.reshape(-1), slot, 0).astype(jnp.int32)
    g = jnp.arange(n_mm_tiles, dtype=jnp.int32)
    n_used = mm_end[-1].astype(jnp.int32)
    tile_expert = jnp.sum((mm_end[None, :] <= jnp.minimum(g, n_used - 1)[:, None]).astype(jnp.int32), axis=1)
    return gather_tbl, return_tbl, tile_expert.astype(jnp.int32), n_used.reshape(1)


MOE_GATHER_UNROLL = 8


def _chunk_gather(tbl_ref, src_hbm, buf, sem, chunks_per_tile):
    def chunk_copy(tile, c, slot):
        src = pl.multiple_of(tbl_ref[tile * chunks_per_tile + c] * MOE_CHUNK, MOE_CHUNK)
        dst = pl.multiple_of(c * MOE_CHUNK, MOE_CHUNK)
        return pltpu.make_async_copy(src_hbm.at[pl.ds(src, MOE_CHUNK), :],
                                     buf.at[slot, pl.ds(dst, MOE_CHUNK), :], sem.at[slot])

    def for_each_chunk(tile, slot, act):
        def body(c, carry):
            act(chunk_copy(tile, c, slot))
            return carry
        lax.fori_loop(0, chunks_per_tile, body, 0, unroll=MOE_GATHER_UNROLL)

    def start(tile, slot):
        for_each_chunk(tile, slot, lambda copy: copy.start())

    def wait(tile, slot):
        for_each_chunk(tile, slot, lambda copy: copy.wait())

    return start, wait


def _moe_expert_kernel(gt_ref, te_ref, nu_ref, xs_hbm, wg_ref, wu_ref, wd_ref, o_ref,
                       wg_bf, wu_bf, wd_bf, x_buf, x_sem):
    D = o_ref.shape[1]
    g = pl.program_id(0)
    n_used = nu_ref[0]
    used = g < n_used

    start_gather, wait_gather = _chunk_gather(gt_ref, xs_hbm, x_buf, x_sem, MOE_CPM)

    @pl.when((g == 0) & (n_used > 0))
    def _():
        start_gather(0, 0)

    @pl.when(g + 1 < n_used)
    def _():
        start_gather(g + 1, (g + 1) % 2)

    @pl.when(used & ((g == 0) | (te_ref[g] != te_ref[jnp.maximum(g - 1, 0)])))
    def _():
        wg_bf[...] = wg_ref[0].astype(BF16)
        wu_bf[...] = wu_ref[0].astype(BF16)
        wd_bf[...] = wd_ref[0].astype(BF16)

    @pl.when(used)
    def _():
        slot = g % 2
        wait_gather(g, slot)
        xs = x_buf[slot]
        x = xs[:, :D]
        extra = xs[:, D:].astype(F32)
        c = extra[:, 0:1] + extra[:, 1:2]
        gate = jnp.dot(x, wg_bf[...], preferred_element_type=F32)
        up = jnp.dot(x, wu_bf[...], preferred_element_type=F32)
        hh = (gate * jax.nn.sigmoid(gate)) * up * c
        o_ref[...] = jnp.dot(hh.astype(BF16), wd_bf[...], preferred_element_type=F32).astype(o_ref.dtype)

    @pl.when(jnp.logical_not(used))
    def _():
        o_ref[...] = jnp.zeros(o_ref.shape, o_ref.dtype)


def _moe_experts(xs, gather_tbl, tile_expert, n_used, wg, wu, wd, n_mm_tiles):
    W = xs.shape[1]
    E, D, FF = wg.shape

    grid_spec = pltpu.PrefetchScalarGridSpec(
        num_scalar_prefetch=3,
        grid=(n_mm_tiles,),
        in_specs=[pl.BlockSpec(memory_space=pl.ANY),
                  pl.BlockSpec((1, D, FF), lambda g, gt, te, nu: (te[g], 0, 0)),
                  pl.BlockSpec((1, D, FF), lambda g, gt, te, nu: (te[g], 0, 0)),
                  pl.BlockSpec((1, FF, D), lambda g, gt, te, nu: (te[g], 0, 0))],
        out_specs=pl.BlockSpec((MOE_TM, D), lambda g, gt, te, nu: (g, 0)),
        scratch_shapes=[pltpu.VMEM((D, FF), BF16), pltpu.VMEM((D, FF), BF16), pltpu.VMEM((FF, D), BF16),
                        pltpu.VMEM((2, MOE_TM, W), BF16), pltpu.SemaphoreType.DMA((2,))],
    )
    return pl.pallas_call(
        _moe_expert_kernel,
        grid_spec=grid_spec,
        out_shape=jax.ShapeDtypeStruct((n_mm_tiles * MOE_TM, D), BF16),
        compiler_params=_cparams("arbitrary"),
        name="moe_experts",
    )(gather_tbl, tile_expert, n_used, xs, wg, wu, wd)


FINISH_COLS = 512


def _finish_kernel(rt_ref, ys_hbm, dest_ref, h_ref, p_ref, gp_ref, gf_ref, wg_ref, wp_ref, o_ref, y_buf, y_sem):
    tok = h_ref.shape[0]
    i = pl.program_id(0)
    slot = i % 2
    start_gather, wait_gather = _chunk_gather(rt_ref, ys_hbm, y_buf, y_sem, MOE_CPT)

    @pl.when(i == 0)
    def _():
        start_gather(0, 0)

    @pl.when(i + 1 < pl.num_programs(0))
    def _():
        start_gather(i + 1, (i + 1) % 2)

    wait_gather(i, slot)
    dcol = dest_ref[0].astype(F32).T
    lane = lax.broadcasted_iota(jnp.int32, (tok, MOE_ROWS), 1).astype(F32)
    unsort = ((lane == dcol[:, 0:1]) | (lane == dcol[:, 1:2])).astype(BF16)
    o_ref[...] = h_ref[...] + jnp.dot(unsort, y_buf[slot], preferred_element_type=F32)
    t = _rms(o_ref[...], gp_ref[...]).astype(BF16)
    pe = p_ref[...].astype(BF16)
    D = o_ref.shape[1]
    for c in range(D // FINISH_COLS):
        cols = slice(c * FINISH_COLS, (c + 1) * FINISH_COLS)
        gate = jax.nn.sigmoid(jnp.dot(t, wg_ref[:, cols], preferred_element_type=F32))
        emb = jnp.dot(pe, wp_ref[:, cols], preferred_element_type=F32)
        o_ref[:, cols] += emb * gate
    o_ref[...] = _rms(o_ref[...], gf_ref[...])


def _finish(ys, return_tbl, dest, h1, p2, g_ple, g_final, wg, wp):
    T, D = h1.shape
    P = p2.shape[1]
    nt = T // MOE_TOK

    vec = pl.BlockSpec((1, D), lambda i, rt: (0, 0))
    grid_spec = pltpu.PrefetchScalarGridSpec(
        num_scalar_prefetch=1,
        grid=(nt,),
        in_specs=[
            pl.BlockSpec(memory_space=pl.ANY),
            pl.BlockSpec((1, SUBLANES, MOE_TOK), lambda i, rt: (i, 0, 0)),
            pl.BlockSpec((MOE_TOK, D), lambda i, rt: (i, 0)),
            pl.BlockSpec((MOE_TOK, P), lambda i, rt: (i, 0)),
            vec, vec,
            pl.BlockSpec((D, D), lambda i, rt: (0, 0), pipeline_mode=pl.Buffered(1)),
            pl.BlockSpec((P, D), lambda i, rt: (0, 0), pipeline_mode=pl.Buffered(1))],
        out_specs=pl.BlockSpec((MOE_TOK, D), lambda i, rt: (i, 0)),
        scratch_shapes=[pltpu.VMEM((2, MOE_ROWS, D), BF16), pltpu.SemaphoreType.DMA((2,))],
    )
    return pl.pallas_call(
        _finish_kernel,
        grid_spec=grid_spec,
        out_shape=jax.ShapeDtypeStruct((T, D), F32),
        compiler_params=_cparams("arbitrary"),
        name="finish",
    )(return_tbl, ys, dest, h1, p2, g_ple, g_final, wg, wp)


def kernel(x, p, g_mix, w_in, rel_bias, w_attn_br, conv_w, conv_b, ln_g, ln_b, w_conv_br, w_o, g_ffn,
           w_router_g, b_router_g, w_router_e, b_router_e, w_e_gate, w_e_up, w_e_down, g_ple, w_ple_gate,
           w_ple_proj, g_final):
    B, S, D = x.shape
    T = B * S
    depth = w_in.shape[0]
    width = N_HEADS * HEAD_DIM
    assert D == width and conv_w.shape[2] == D, "column-block indexing assumes all branch widths equal d_model"
    assert S % MOBA_BLOCK == 0
    assert depth == 1, "the final RMSNorm is fused into the last stage of a single layer"
    a_blk = 3 * width // D
    gate_blk = a_blk + 2

    h = x.reshape(T, D)
    for i in range(depth):
        proj = _in_proj(h, g_mix[i][None], w_in[i].astype(BF16))
        proj3 = proj.reshape(B, S, -1)
        attn = _moba(proj3, rel_bias)
        conv, (wa, wc, wo, w_gate) = _conv(proj3, conv_w[i], conv_b[i][None], ln_g[i][None], ln_b[i][None], a_blk,
                                           side_weights=[w_attn_br[i], w_conv_br[i], w_o[i], w_ple_gate[i]])
        h = _mix(h, attn.reshape(T, D), conv.reshape(T, D), proj, wa, wc, wo, gate_blk)
        xs, dest, cnt = _moe_sort(h, g_ffn[i][None], w_router_e[i], b_router_e[i], w_router_g[i], b_router_g[i])
        n_tok_tiles = T // MOE_TOK
        worst_chunks = (MOE_TOPK * T + n_tok_tiles * N_EXPERTS * (MOE_CHUNK - 1)) // MOE_CHUNK
        n_mm_tiles = -(-worst_chunks // MOE_CPM) + N_EXPERTS
        gather_tbl, return_tbl, tile_expert, n_used = _moe_tables(cnt[:, :, 0], n_mm_tiles)
        ys = _moe_experts(xs, gather_tbl, tile_expert, n_used,
                          w_e_gate[i], w_e_up[i], w_e_down[i], n_mm_tiles)
        out = _finish(ys, return_tbl, dest, h, p[i].reshape(T, -1), g_ple[i][None], g_final[None],
                      w_gate, w_ple_proj[i].astype(BF16))
    return out.reshape(B, S, D)
```
